```python
import math
import jax
import jax.numpy as jnp
from jax import lax
import numpy as np

D_MODEL = 1024
BATCH = 8
SEQ = 4096
DEPTH = 4

CTX_LEN = 256
GRID_W = 64
N_MIXERS = 3
EPS = 1e-6

NA_HEADS = 16
NA_HEAD_DIM = D_MODEL // NA_HEADS
NA_WIN_H = 8
NA_WIN_W = 16
NA_QBLOCK_W = NA_WIN_W
NA_KBLOCK_W = 2 * NA_WIN_W

ML_HEADS = 4
ML_INNER = 2 * D_MODEL
ML_HEAD_DIM = ML_INNER // ML_HEADS
ML_BLOCK = 4
ML_CONV_K = 5
ML_CHUNK = 64

DA_HEADS = 8
DA_HEAD_DIM = D_MODEL // (2 * DA_HEADS)
DA_QBLOCK = 128
ROPE_BASE = 10000.0

MOE_GROUPS = 4
MOE_EXPERTS_PER_GROUP = 8
MOE_N_EXPERTS = MOE_GROUPS * MOE_EXPERTS_PER_GROUP
MOE_TOP_K = 2
MOE_HIDDEN = 512

kernel_name = 'hybrid_natten_mlstm_diffattn_hmoe_dit'


def rms_norm(x, g):
    xf = x.astype(jnp.float32)
    y = xf * lax.rsqrt(jnp.mean(xf * xf, axis=-1, keepdims=True) + EPS)
    return (y * g.astype(jnp.float32)).astype(x.dtype)


def modulate(x, g, shift, scale):
    return rms_norm(x, g) * (1.0 + scale) + shift


def _col_tables():
    n_cb = GRID_W // NA_QBLOCK_W
    q_col = np.arange(GRID_W).reshape(n_cb, NA_QBLOCK_W)
    k_start = np.clip(q_col[:, 0] - NA_WIN_W // 2, 0, GRID_W - NA_KBLOCK_W)
    k_col = k_start[:, None] + np.arange(NA_KBLOCK_W)
    w_start = np.clip(q_col - NA_WIN_W // 2, 0, GRID_W - NA_WIN_W)
    valid = (k_col[:, None, :] >= w_start[..., None]) & (k_col[:, None, :] < w_start[..., None] + NA_WIN_W)
    dx = np.clip(k_col[:, None, :] - q_col[..., None], -(NA_WIN_W - 1), NA_WIN_W - 1) + NA_WIN_W - 1
    return k_start, valid, dx


def neighbourhood_attention(hx, hc, w_qkv, w_o, rpb, with_ctx_out):
    b, s, d = hx.shape
    n_ctx = hc.shape[1]
    rows = s // GRID_W
    kh = min(NA_WIN_H, rows)
    n_cb = GRID_W // NA_QBLOCK_W
    nw = kh * NA_KBLOCK_W
    scale = NA_HEAD_DIM ** -0.5
    qkv = (hx @ w_qkv).reshape(b, s, 3, NA_HEADS, NA_HEAD_DIM)
    qg = qkv[:, :, 0].reshape(b, rows, n_cb, NA_QBLOCK_W, NA_HEADS, NA_HEAD_DIM)
    kg = qkv[:, :, 1].reshape(b, rows, GRID_W, NA_HEADS, NA_HEAD_DIM)
    vg = qkv[:, :, 2].reshape(b, rows, GRID_W, NA_HEADS, NA_HEAD_DIM)
    kvc = (hc @ w_qkv[:, d:]).reshape(b, n_ctx, 2, NA_HEADS, NA_HEAD_DIM)
    kc, vc = kvc[:, :, 0], kvc[:, :, 1]
    k_start, valid, dx = _col_tables()
    mask = jnp.asarray(np.broadcast_to(valid[:, :, None, :], (n_cb, NA_QBLOCK_W, kh, NA_KBLOCK_W)).reshape(n_cb, NA_QBLOCK_W, nw))

    def row_block(r):
        rs = jnp.clip(r - kh // 2, 0, rows - kh)
        kr = lax.dynamic_slice_in_dim(kg, rs, kh, axis=1)
        vr = lax.dynamic_slice_in_dim(vg, rs, kh, axis=1)
        kb = jnp.stack([kr[:, :, int(ks):int(ks) + NA_KBLOCK_W] for ks in k_start], axis=1).reshape(b, n_cb, nw, NA_HEADS, NA_HEAD_DIM)
        vb = jnp.stack([vr[:, :, int(ks):int(ks) + NA_KBLOCK_W] for ks in k_start], axis=1).reshape(b, n_cb, nw, NA_HEADS, NA_HEAD_DIM)
        qr = lax.dynamic_index_in_dim(qg, r, axis=1, keepdims=False)
        dy = rs - r + jnp.arange(kh) + NA_WIN_H - 1
        bias = rpb[:, dy][:, :, dx]
        bias = bias.transpose(0, 2, 3, 1, 4).reshape(NA_HEADS, n_cb, NA_QBLOCK_W, nw).astype(jnp.float32)
        s_win = jnp.einsum('bnqhd,bnkhd->bhnqk', qr, kb).astype(jnp.float32) * scale + bias
        s_win = jnp.where(mask, s_win, -jnp.inf)
        s_ctx = jnp.einsum('bnqhd,bkhd->bhnqk', qr, kc).astype(jnp.float32) * scale
        p = jax.nn.softmax(jnp.concatenate([s_win, s_ctx], axis=-1), axis=-1).astype(hx.dtype)
        return (jnp.einsum('bhnqk,bnkhd->bnqhd', p[..., :nw], vb)
                + jnp.einsum('bhnqk,bkhd->bnqhd', p[..., nw:], vc))

    o = lax.map(row_block, jnp.arange(rows))
    ox = jnp.moveaxis(o, 0, 1).reshape(b, s, d) @ w_o
    oc = None
    if with_ctx_out:
        qc = (hc @ w_qkv[:, :d]).reshape(b, n_ctx, NA_HEADS, NA_HEAD_DIM)
        sc = jnp.einsum('bqhd,bkhd->bhqk', qc, kc).astype(jnp.float32) * scale
        pc = jax.nn.softmax(sc, axis=-1).astype(hc.dtype)
        oc = jnp.einsum('bhqk,bkhd->bqhd', pc, vc).reshape(b, n_ctx, d) @ w_o
    return ox, oc


def _centred_dwconv(x, w, bias):
    k, ch = w.shape
    y = lax.conv_general_dilated(x, w[:, None, :].astype(x.dtype), window_strides=(1,),
                                 padding=[(k // 2, k // 2)], dimension_numbers=('NWC', 'WIO', 'NWC'),
                                 feature_group_count=ch)
    return y + bias


def _headwise(x, w):
    b, t, _ = x.shape
    return jnp.einsum('btgi,gio->btgo', x.reshape(b, t, -1, ML_BLOCK), w).reshape(b, t, -1)


def _to_heads(a):
    b, t, _ = a.shape
    return a.reshape(b, t, ML_HEADS, ML_HEAD_DIM).transpose(0, 2, 1, 3).astype(jnp.float32)


def _mlstm_scan(q, k, v, i_pre, log_f, state):
    b, h, t, dh = q.shape
    nc = t // ML_CHUNK
    causal = jnp.tril(jnp.ones((ML_CHUNK, ML_CHUNK), dtype=bool))

    def to_chunks(a):
        return jnp.moveaxis(a.reshape(b, h, nc, ML_CHUNK, *a.shape[3:]), 2, 0)

    def step(carry, xs):
        c_prev, n_prev, m_prev = carry
        qc, kc, vc, ic, fc = xs
        bcum = jnp.cumsum(fc, axis=-1)
        logd = bcum[..., :, None] - bcum[..., None, :] + ic[..., None, :]
        logd = jnp.where(causal, logd, -jnp.inf)
        log_inter = bcum + m_prev[..., None]
        m_j = jnp.maximum(log_inter, jnp.max(logd, axis=-1))
        dmat = jnp.exp(logd - m_j[..., None])
        inter = jnp.exp(log_inter - m_j)
        sc = jnp.einsum('bhjd,bhsd->bhjs', qc, kc) * dmat
        num = inter[..., None] * jnp.einsum('bhjd,bhde->bhje', qc, c_prev) + jnp.einsum('bhjs,bhse->bhje', sc, vc)
        den = inter * jnp.einsum('bhjd,bhd->bhj', qc, n_prev) + jnp.sum(sc, axis=-1)
        h_out = num / jnp.maximum(jnp.abs(den), jnp.exp(-m_j))[..., None]
        b_last = bcum[..., -1]
        log_state = b_last[..., None] - bcum + ic
        m_new = jnp.maximum(b_last + m_prev, jnp.max(log_state, axis=-1))
        w_s = jnp.exp(log_state - m_new[..., None])
        decay = jnp.exp(b_last + m_prev - m_new)
        c_new = decay[..., None, None] * c_prev + jnp.einsum('bhs,bhsd,bhse->bhde', w_s, kc, vc)
        n_new = decay[..., None] * n_prev + jnp.einsum('bhs,bhsd->bhd', w_s, kc)
        return (c_new, n_new, m_new), h_out

    state, hs = lax.scan(step, state, (to_chunks(q), to_chunks(k), to_chunks(v), to_chunks(i_pre), to_chunks(log_f)))
    return jnp.moveaxis(hs, 0, 2).reshape(b, h, t, dh), state


def mlstm_mixer(hx, hc, w_up, conv_w, conv_b, w_q, w_k, w_v, w_gate, b_gate, gn_w, skip, w_down, with_ctx_out):
    def prep(h):
        bsz, t, _ = h.shape
        xm, z = jnp.split(h @ w_up, 2, axis=-1)
        xc = jax.nn.silu(_centred_dwconv(xm, conv_w, conv_b))
        q = _headwise(xc, w_q)
        k = _headwise(xc, w_k)
        v = _headwise(xm, w_v)
        g = (jnp.concatenate([q, k, v], axis=-1) @ w_gate + b_gate).astype(jnp.float32)
        g = g.reshape(bsz, t, 2, 2, ML_HEADS)
        i_pre = jnp.moveaxis(g[:, :, :, 0], 1, 3)
        log_f = jax.nn.log_sigmoid(jnp.moveaxis(g[:, :, :, 1], 1, 3))
        return xc, z, _to_heads(q) * ML_HEAD_DIM ** -0.5, _to_heads(k), _to_heads(v), i_pre, log_f

    def bidir(qh, kh, vh, i_pre, log_f, st_fwd, st_bwd):
        h_f, st_f = _mlstm_scan(qh, kh, vh, i_pre[:, 0], log_f[:, 0], st_fwd)
        rev = lambda a: jnp.flip(a, axis=2)
        h_b, st_b = _mlstm_scan(rev(qh), rev(kh), rev(vh), rev(i_pre[:, 1]), rev(log_f[:, 1]), st_bwd)
        return h_f + rev(h_b), st_f, st_b

    def finish(hh, xc, z):
        bsz, _, t, _ = hh.shape
        mu = jnp.mean(hh, axis=-1, keepdims=True)
        var = jnp.mean(jnp.square(hh - mu), axis=-1, keepdims=True)
        hn = ((hh - mu) * lax.rsqrt(var + EPS)).transpose(0, 2, 1, 3).reshape(bsz, t, ML_INNER).astype(xc.dtype) * gn_w
        return ((hn + skip * xc) * jax.nn.silu(z)) @ w_down

    b = hx.shape[0]
    zero = (jnp.zeros((b, ML_HEADS, ML_HEAD_DIM, ML_HEAD_DIM), jnp.float32),
            jnp.zeros((b, ML_HEADS, ML_HEAD_DIM), jnp.float32),
            jnp.full((b, ML_HEADS), -jnp.inf, jnp.float32))
    xc_c, z_c, q_c, k_c, v_c, i_c, f_c = prep(hc)
    h_c, st_f, st_b = bidir(q_c, k_c, v_c, i_c, f_c, zero, zero)
    xc_x, z_x, q_x, k_x, v_x, i_x, f_x = prep(hx)
    h_x, _, _ = bidir(q_x, k_x, v_x, i_x, f_x, st_f, st_b)
    ox = finish(h_x, xc_x, z_x)
    oc = finish(h_c, xc_c, z_c) if with_ctx_out else None
    return ox, oc


def rope_2d(x, row, col):
    half = x.shape[-1] // 2
    freqs = ROPE_BASE ** (-jnp.arange(0, half, 2, dtype=jnp.float32) / half)

    def rot(xa, pos):
        ang = pos.astype(jnp.float32)[:, None] * freqs
        cos = jnp.cos(ang)[:, None, None, :].astype(x.dtype)
        sin = jnp.sin(ang)[:, None, None, :].astype(x.dtype)
        x1, x2 = xa[..., :half // 2], xa[..., half // 2:]
        return jnp.concatenate([x1 * cos - x2 * sin, x2 * cos + x1 * sin], axis=-1)

    return jnp.concatenate([rot(x[..., :half], row), rot(x[..., half:], col)], axis=-1)


def diff_attention(hx, hc, w_qkv, lam, subln_g, w_o, lambda_init, with_ctx_out):
    b, s, d = hx.shape
    n_ctx = hc.shape[1]
    t = jnp.arange(s)
    row, col = t // GRID_W, t % GRID_W
    qkv = (hx @ w_qkv).reshape(b, s, 3, d)
    q = rope_2d(qkv[:, :, 0].reshape(b, s, DA_HEADS, 2, DA_HEAD_DIM), row, col)
    k = rope_2d(qkv[:, :, 1].reshape(b, s, DA_HEADS, 2, DA_HEAD_DIM), row, col)
    v = qkv[:, :, 2].reshape(b, s, DA_HEADS, 2 * DA_HEAD_DIM)
    kvc = (hc @ w_qkv[:, d:]).reshape(b, n_ctx, 2, d)
    kc = kvc[:, :, 0].reshape(b, n_ctx, DA_HEADS, 2, DA_HEAD_DIM)
    vc = kvc[:, :, 1].reshape(b, n_ctx, DA_HEADS, 2 * DA_HEAD_DIM)
    lf = lam.astype(jnp.float32)
    lam_full = jnp.exp(jnp.sum(lf[0] * lf[1])) - jnp.exp(jnp.sum(lf[2] * lf[3])) + lambda_init
    scale = DA_HEAD_DIM ** -0.5

    def attend(qb, keys, vals):
        sc = jnp.einsum('bqhmd,bkhmd->bhmqk', qb, keys).astype(jnp.float32) * scale
        p = jax.nn.softmax(sc, axis=-1)
        a = (p[:, :, 0] - lam_full * p[:, :, 1]).astype(qb.dtype)
        o = jnp.einsum('bhqk,bkhe->bqhe', a, vals)
        return rms_norm(o, subln_g) * (1.0 - lambda_init)

    k_all = jnp.concatenate([kc, k], axis=1)
    v_all = jnp.concatenate([vc, v], axis=1)
    qblocks = jnp.moveaxis(q.reshape(b, s // DA_QBLOCK, DA_QBLOCK, DA_HEADS, 2, DA_HEAD_DIM), 1, 0)
    o = lax.map(lambda qb: attend(qb, k_all, v_all), qblocks)
    ox = jnp.moveaxis(o, 0, 1).reshape(b, s, d) @ w_o
    oc = None
    if with_ctx_out:
        qc = (hc @ w_qkv[:, :d]).reshape(b, n_ctx, DA_HEADS, 2, DA_HEAD_DIM)
        oc = attend(qc, kc, vc).reshape(b, n_ctx, d) @ w_o
    return ox, oc


def hier_moe(h, w_group, b_group, w_router, b_router, w1, w3, w2):
    shape = h.shape
    xt = h.reshape(-1, shape[-1])
    n = xt.shape[0]
    g_prob = jax.nn.softmax((xt @ w_group).astype(jnp.float32) + b_group.astype(jnp.float32), axis=-1)
    g_val, g_idx = lax.top_k(g_prob, 1)
    e_logits = ((xt @ w_router).astype(jnp.float32) + b_router.astype(jnp.float32)).reshape(n, MOE_GROUPS, MOE_EXPERTS_PER_GROUP)
    e_logits = jnp.take_along_axis(e_logits, g_idx[:, :, None], axis=1)[:, 0]
    e_val, e_idx = lax.top_k(jax.nn.softmax(e_logits, axis=-1), MOE_TOP_K)
    w_tok = g_val * e_val / jnp.sum(e_val, axis=-1, keepdims=True)
    ids = g_idx * MOE_EXPERTS_PER_GROUP + e_idx
    gates = jnp.sum(jax.nn.one_hot(ids, MOE_N_EXPERTS, dtype=jnp.float32) * w_tok[..., None], axis=1).astype(h.dtype)
    y = jnp.zeros_like(xt)
    for e in range(MOE_N_EXPERTS):
        he = jax.nn.silu(xt @ w1[e]) * (xt @ w3[e])
        y = y + (he @ w2[e]) * gates[:, e:e + 1]
    return y.reshape(shape)


def setup_inputs(seed: int = 0) -> dict:
    key = jax.random.key(seed)
    keys = iter(jax.random.split(key, 40))
    f32 = jnp.float32
    D = D_MODEL

    def normal(shape, scale):
        return jax.random.normal(next(keys), shape, f32) * scale

    def gain(shape):
        return 1.0 + normal(shape, 0.05)

    n_a = len(range(0, DEPTH, N_MIXERS))
    n_b = len(range(1, DEPTH, N_MIXERS))
    n_c = len(range(2, DEPTH, N_MIXERS))
    bg = normal((n_b, 2, 2, ML_HEADS), 0.1)
    bg = bg.at[:, :, 1, :].add(jnp.linspace(3.0, 6.0, ML_HEADS, dtype=f32))
    return {
        'x': normal((BATCH, SEQ, D), 1.0),
        'c': normal((BATCH, D), 1.0),
        'ctx': normal((BATCH, CTX_LEN, D), 1.0),
        'c_ctx': normal((D,), 1.0),
        'mod_w': normal((DEPTH, D, 6 * D), 0.5 * D ** -0.5),
        'mod_b': normal((DEPTH, 6 * D), 0.02),
        'norm_g': gain((DEPTH, 2, D)),
        'final_g': gain((D,)),
        'na_w_qkv': normal((n_a, D, 3 * D), D ** -0.5),
        'na_w_o': normal((n_a, D, D), D ** -0.5),
        'na_rpb': normal((n_a, NA_HEADS, 2 * NA_WIN_H - 1, 2 * NA_WIN_W - 1), 0.1),
        'ml_w_up': normal((n_b, D, 2 * ML_INNER), D ** -0.5),
        'ml_conv_w': normal((n_b, ML_CONV_K, ML_INNER), ML_CONV_K ** -0.5),
        'ml_conv_b': normal((n_b, ML_INNER), 0.02),
        'ml_w_q': normal((n_b, ML_INNER // ML_BLOCK, ML_BLOCK, ML_BLOCK), ML_BLOCK ** -0.5),
        'ml_w_k': normal((n_b, ML_INNER // ML_BLOCK, ML_BLOCK, ML_BLOCK), ML_BLOCK ** -0.5),
        'ml_w_v': normal((n_b, ML_INNER // ML_BLOCK, ML_BLOCK, ML_BLOCK), ML_BLOCK ** -0.5),
        'ml_w_gate': normal((n_b, 3 * ML_INNER, 4 * ML_HEADS), (3 * ML_INNER) ** -0.5),
        'ml_b_gate': bg.reshape(n_b, 4 * ML_HEADS),
        'ml_gn_w': gain((n_b, ML_INNER)),
        'ml_skip': gain((n_b, ML_INNER)),
        'ml_w_down': normal((n_b, ML_INNER, D), ML_INNER ** -0.5),
        'da_w_qkv': normal((n_c, D, 3 * D), D ** -0.5),
        'da_lambda': normal((n_c, 4, DA_HEAD_DIM), 0.1),
        'da_subln_g': gain((n_c, 2 * DA_HEAD_DIM)),
        'da_w_o': normal((n_c, D, D), D ** -0.5),
        'moe_w_group': normal((DEPTH, D, MOE_GROUPS), D ** -0.5),
        'moe_b_group': normal((DEPTH, MOE_GROUPS), 0.01),
        'moe_w_router': normal((DEPTH, D, MOE_N_EXPERTS), D ** -0.5),
        'moe_b_router': normal((DEPTH, MOE_N_EXPERTS), 0.01),
        'moe_w1': normal((DEPTH, MOE_N_EXPERTS, D, MOE_HIDDEN), D ** -0.5),
        'moe_w3': normal((DEPTH, MOE_N_EXPERTS, D, MOE_HIDDEN), D ** -0.5),
        'moe_w2': normal((DEPTH, MOE_N_EXPERTS, MOE_HIDDEN, D), MOE_HIDDEN ** -0.5),
    }


def reference(x, c, ctx, c_ctx, mod_w, mod_b, norm_g, final_g,
              na_w_qkv, na_w_o, na_rpb,
              ml_w_up, ml_conv_w, ml_conv_b, ml_w_q, ml_w_k, ml_w_v, ml_w_gate, ml_b_gate,
              ml_gn_w, ml_skip, ml_w_down,
              da_w_qkv, da_lambda, da_subln_g, da_w_o,
              moe_w_group, moe_b_group, moe_w_router, moe_b_router, moe_w1, moe_w3, moe_w2):
    b, s, d = x.shape
    n_ctx = ctx.shape[1]
    xs, cs = x, ctx
    silu_c = jax.nn.silu(c)
    silu_cc = jax.nn.silu(c_ctx)
    for i in range(DEPTH):
        last = i == DEPTH - 1
        kind, j = i % N_MIXERS, i // N_MIXERS
        sh1, sc1, g1, sh2, sc2, g2 = jnp.split((silu_c @ mod_w[i] + mod_b[i])[:, None, :], 6, axis=-1)
        csh1, csc1, cg1, csh2, csc2, cg2 = jnp.split(silu_cc @ mod_w[i] + mod_b[i], 6)
        hx = modulate(xs, norm_g[i, 0], sh1, sc1)
        hc = modulate(cs, norm_g[i, 0], csh1, csc1)
        if kind == 0:
            ox, oc = neighbourhood_attention(hx, hc, na_w_qkv[j], na_w_o[j], na_rpb[j], not last)
        elif kind == 1:
            ox, oc = mlstm_mixer(hx, hc, ml_w_up[j], ml_conv_w[j], ml_conv_b[j], ml_w_q[j], ml_w_k[j], ml_w_v[j],
                                 ml_w_gate[j], ml_b_gate[j], ml_gn_w[j], ml_skip[j], ml_w_down[j], not last)
        else:
            lambda_init = 0.8 - 0.6 * math.exp(-0.3 * i)
            ox, oc = diff_attention(hx, hc, da_w_qkv[j], da_lambda[j], da_subln_g[j], da_w_o[j], lambda_init, not last)
        xs = xs + g1 * ox
        hx = modulate(xs, norm_g[i, 1], sh2, sc2)
        moe_args = (moe_w_group[i], moe_b_group[i], moe_w_router[i], moe_b_router[i], moe_w1[i], moe_w3[i], moe_w2[i])
        if last:
            xs = xs + g2 * hier_moe(hx, *moe_args)
        else:
            cs = cs + cg1 * oc
            hc = modulate(cs, norm_g[i, 1], csh2, csc2)
            y = hier_moe(jnp.concatenate([hc, hx], axis=1), *moe_args)
            cs = cs + cg2 * y[:, :n_ctx]
            xs = xs + g2 * y[:, n_ctx:]
    return rms_norm(xs, final_g)
```

```python
import functools
import math

import numpy as np
import jax
import jax.numpy as jnp
from jax import lax
from jax.experimental import pallas as pl
from jax.experimental.pallas import tpu as pltpu

F32 = jnp.float32
BF16 = jnp.bfloat16

D_MODEL = 1024
SEQ = 4096
CTX_LEN = 256
GRID_W = 64
N_MIXERS = 3
EPS = 1e-6

NA_HEADS = 16
NA_WIN_H = 8
NA_WIN_W = 16

ML_HEADS = 4
ML_INNER = 2 * D_MODEL
ML_HEAD_DIM = ML_INNER // ML_HEADS
ML_BLOCK = 4
ML_CONV_K = 5

DA_HEADS = 8
DA_HEAD_DIM = 64
ROPE_BASE = 10000.0

MOE_GROUPS = 4
MOE_EPG = 8
MOE_E = MOE_GROUPS * MOE_EPG
MOE_HIDDEN = 512

LANES = 128
CHUNK = 256
X_CHUNKS = SEQ // CHUNK
NEG = -1e30
VMEM_LIMIT = 56 * 1024 * 1024


def _cp(sem, vmem=VMEM_LIMIT):
    return pltpu.CompilerParams(dimension_semantics=sem, vmem_limit_bytes=vmem)


def _tile(n, pref):
    tm = pref
    while n % tm:
        tm //= 2
    assert tm >= CHUNK
    return tm


def _mod_row(i, tm, nb):
    return jnp.where(i < nb * (SEQ // tm), i // (SEQ // tm), nb)


def _nt(a, b):
    return lax.dot_general(a, b, (((1,), (1,)), ((), ())), preferred_element_type=F32)


def _silu(x):
    return x * jax.nn.sigmoid(x)


def _mod_kernel(c_ref, w_ref, b_ref, o_ref):
    a = _silu(c_ref[...]).astype(BF16)
    o_ref[0] = jnp.dot(a, w_ref[0].astype(BF16), preferred_element_type=F32) + b_ref[0]


def mod_vectors(cvec, mod_w, mod_b):
    depth, d, n6 = mod_w.shape
    tn = 1024
    return pl.pallas_call(
        _mod_kernel,
        grid=(depth, n6 // tn),
        in_specs=[pl.BlockSpec((16, d), lambda l, j: (0, 0)),
                  pl.BlockSpec((1, d, tn), lambda l, j: (l, 0, j)),
                  pl.BlockSpec((1, 1, tn), lambda l, j: (l, 0, j))],
        out_specs=pl.BlockSpec((1, 16, tn), lambda l, j: (l, 0, j)),
        out_shape=jax.ShapeDtypeStruct((depth, 16, n6), F32),
        compiler_params=_cp(("parallel", "parallel")),
        name="mod_vectors",
    )(cvec, mod_w, mod_b.reshape(depth, 1, n6))


def _norm_mod_kernel(x_ref, g_ref, sh_ref, sc_ref, o_ref):
    x = x_ref[...]
    y = x * lax.rsqrt(jnp.mean(x * x, axis=-1, keepdims=True) + EPS) * g_ref[...]
    o_ref[...] = (y * (1.0 + sc_ref[0, 0]) + sh_ref[0, 0]).astype(o_ref.dtype)


def norm_mod(s, g, mod, k_shift, k_scale, nb):
    n, d = s.shape
    tm = _tile(n, 512)
    return pl.pallas_call(
        _norm_mod_kernel,
        grid=(n // tm,),
        in_specs=[pl.BlockSpec((tm, d), lambda i: (i, 0)),
                  pl.BlockSpec((1, d), lambda i: (0, 0)),
                  pl.BlockSpec((1, 1, 1, d), lambda i: (_mod_row(i, tm, nb), k_shift, 0, 0)),
                  pl.BlockSpec((1, 1, 1, d), lambda i: (_mod_row(i, tm, nb), k_scale, 0, 0))],
        out_specs=pl.BlockSpec((tm, d), lambda i: (i, 0)),
        out_shape=jax.ShapeDtypeStruct((n, d), BF16),
        compiler_params=_cp(("parallel",)),
        name="norm_mod",
    )(s, g.reshape(1, d), mod, mod)


def _final_norm_kernel(x_ref, g_ref, o_ref):
    x = x_ref[...]
    o_ref[...] = x * lax.rsqrt(jnp.mean(x * x, axis=-1, keepdims=True) + EPS) * g_ref[...]


def final_norm(s, g, n_rows):
    d = s.shape[1]
    tm = _tile(n_rows, 512)
    return pl.pallas_call(
        _final_norm_kernel,
        grid=(n_rows // tm,),
        in_specs=[pl.BlockSpec((tm, d), lambda i: (i, 0)),
                  pl.BlockSpec((1, d), lambda i: (0, 0))],
        out_specs=pl.BlockSpec((tm, d), lambda i: (i, 0)),
        out_shape=jax.ShapeDtypeStruct((n_rows, d), F32),
        compiler_params=_cp(("parallel",)),
        name="final_norm",
    )(s, g.reshape(1, d))


def _mm_kernel(a_ref, w_ref, o_ref):
    o_ref[...] = jnp.dot(a_ref[...], w_ref[...], preferred_element_type=F32).astype(o_ref.dtype)


def matmul(a, w, out_dtype, tm=512, tn=1024):
    n, k = a.shape
    tm = _tile(n, tm)
    nout = w.shape[1]
    tn = min(tn, nout)
    return pl.pallas_call(
        _mm_kernel,
        grid=(nout // tn, n // tm),
        in_specs=[pl.BlockSpec((tm, k), lambda j, i: (i, 0)),
                  pl.BlockSpec((k, tn), lambda j, i: (0, j))],
        out_specs=pl.BlockSpec((tm, tn), lambda j, i: (i, j)),
        out_shape=jax.ShapeDtypeStruct((n, nout), out_dtype),
        compiler_params=_cp(("parallel", "parallel")),
        name="matmul",
    )(a, w)


def _mm_res_kernel(a_ref, w_ref, r_ref, g_ref, o_ref):
    acc = jnp.dot(a_ref[...], w_ref[...], preferred_element_type=F32)
    o_ref[...] = r_ref[...] + g_ref[0, 0] * acc


def matmul_residual(a, w, s, mod, k_gate, nb):
    n, k = a.shape
    d = w.shape[1]
    tm = _tile(n, 512)
    return pl.pallas_call(
        _mm_res_kernel,
        grid=(n // tm,),
        in_specs=[pl.BlockSpec((tm, k), lambda i: (i, 0)),
                  pl.BlockSpec((k, d), lambda i: (0, 0)),
                  pl.BlockSpec((tm, d), lambda i: (i, 0)),
                  pl.BlockSpec((1, 1, 1, d), lambda i: (_mod_row(i, tm, nb), k_gate, 0, 0))],
        out_specs=pl.BlockSpec((tm, d), lambda i: (i, 0)),
        out_shape=jax.ShapeDtypeStruct((n, d), F32),
        input_output_aliases={2: 0},
        compiler_params=_cp(("parallel",)),
        name="matmul_residual",
    )(a, w, s, mod)


def _na_bias_tables(rpb):
    o = np.arange(NA_WIN_H)[:, None]
    j = np.arange(NA_WIN_H)[None, :]
    dy = j - o + NA_WIN_H - 1
    qc = np.arange(GRID_W)[:, None]
    kc = np.arange(GRID_W)[None, :]
    dx = np.clip(kc - qc, -(NA_WIN_W - 1), NA_WIN_W - 1) + NA_WIN_W - 1
    w_start = np.clip(qc - NA_WIN_W // 2, 0, GRID_W - NA_WIN_W)
    valid = (kc >= w_start) & (kc < w_start + NA_WIN_W)
    tbl = rpb.astype(F32)[:, dy][:, :, :, dx]
    tbl = jnp.where(jnp.asarray(valid)[None, None, None], tbl, NEG)
    tbl = tbl.transpose(1, 0, 3, 2, 4)
    return tbl.reshape(NA_WIN_H, NA_HEADS, GRID_W, NA_WIN_H * GRID_W)


def _na_kernel(q_ref, kx_ref, vx_ref, kc_ref, vc_ref, bias_ref, o_ref):
    t = pl.program_id(2)
    lane = lax.broadcasted_iota(jnp.int32, (1, LANES), 1)
    kc = kc_ref[...]
    vc = vc_ref[...]
    nwin = NA_WIN_H * GRID_W

    def attend(qm, kw, vw, bias):
        s_c = _nt(qm, kc)
        m = jnp.max(s_c, axis=1, keepdims=True)
        if kw is not None:
            s_w = _nt(qm, kw) + bias
            m = jnp.maximum(m, jnp.max(s_w, axis=1, keepdims=True))
        p_c = jnp.exp(s_c - m)
        l = jnp.sum(p_c, axis=1, keepdims=True)
        o = jnp.dot(p_c.astype(BF16), vc, preferred_element_type=F32)
        if kw is not None:
            p_w = jnp.exp(s_w - m)
            l = l + jnp.sum(p_w, axis=1, keepdims=True)
            o = o + jnp.dot(p_w.astype(BF16), vw, preferred_element_type=F32)
        return o / l

    def two_heads(q, kw, vw, off):
        outs = []
        for hh in range(2):
            qm = jnp.where((lane // 64) == hh, q, jnp.zeros_like(q))
            bias = None if kw is None else bias_ref[off, hh]
            outs.append(attend(qm, kw, vw, bias))
        return jnp.where(lane < 64, outs[0], outs[1])

    @pl.when(t < X_CHUNKS)
    def _():
        rows = CHUNK // GRID_W
        for i in range(rows):
            r = t * rows + i
            rs = jnp.clip(r - NA_WIN_H // 2, 0, SEQ // GRID_W - NA_WIN_H)
            start = pl.multiple_of(rs * GRID_W, GRID_W)
            kw = kx_ref[pl.ds(start, nwin), :]
            vw = vx_ref[pl.ds(start, nwin), :]
            q = q_ref[i * GRID_W:(i + 1) * GRID_W, :] * 0.125
            o = two_heads(q, kw, vw, r - rs)
            o_ref[i * GRID_W:(i + 1) * GRID_W, :] = o.astype(o_ref.dtype)

    @pl.when(t == X_CHUNKS)
    def _():
        q = q_ref[...] * 0.125
        o_ref[...] = two_heads(q, None, None, None).astype(o_ref.dtype)


def na_attention(qkv, bias, nb):
    n = qkv.shape[0]
    d = D_MODEL
    ncb = d // LANES
    xc = nb * X_CHUNKS

    def qtile(b, t):
        return jnp.where(t < X_CHUNKS, b * X_CHUNKS + t, xc + b)

    return pl.pallas_call(
        _na_kernel,
        grid=(nb, ncb, X_CHUNKS + 1),
        in_specs=[pl.BlockSpec((CHUNK, LANES), lambda b, h, t: (qtile(b, t), h)),
                  pl.BlockSpec((SEQ, LANES), lambda b, h, t: (b, ncb + h)),
                  pl.BlockSpec((SEQ, LANES), lambda b, h, t: (b, 2 * ncb + h)),
                  pl.BlockSpec((CHUNK, LANES), lambda b, h, t: (xc + b, ncb + h)),
                  pl.BlockSpec((CHUNK, LANES), lambda b, h, t: (xc + b, 2 * ncb + h)),
                  pl.BlockSpec((NA_WIN_H, 2, GRID_W, NA_WIN_H * GRID_W), lambda b, h, t: (0, h, 0, 0))],
        out_specs=pl.BlockSpec((CHUNK, LANES), lambda b, h, t: (qtile(b, t), h)),
        out_shape=jax.ShapeDtypeStruct((n, d), BF16),
        compiler_params=_cp(("parallel", "parallel", "arbitrary")),
        name="na_attention",
    )(qkv, qkv, qkv, qkv, qkv, bias)


def _rope_tables(nb):
    half = DA_HEAD_DIM // 2
    freqs = ROPE_BASE ** (-np.arange(0, half, 2, dtype=np.float32) / half)
    t = np.arange(SEQ)
    row, col = t // GRID_W, t % GRID_W
    lane = np.arange(LANES)
    l64 = lane % DA_HEAD_DIM
    use_col = (l64 // half) == 1
    l32 = l64 % half
    fi = l32 % (half // 2)
    second = l32 >= half // 2
    pos = np.where(use_col[None, :], col[:, None], row[:, None]).astype(np.float32)
    ang = pos * freqs[fi][None, :]
    cos = np.cos(ang).astype(np.float32)
    sin = np.sin(ang).astype(np.float32)
    sa = np.where(second[None, :], 0.0, -sin).astype(np.float32)
    sb = np.where(second[None, :], sin, 0.0).astype(np.float32)
    ident = np.ones((CHUNK, LANES), np.float32)
    zero = np.zeros((CHUNK, LANES), np.float32)
    return (jnp.asarray(np.concatenate([cos, ident])),
            jnp.asarray(np.concatenate([sa, zero])),
            jnp.asarray(np.concatenate([sb, zero])))


def _rope_kernel(x_ref, c_ref, sa_ref, sb_ref, o_ref, *, n_rot):
    j = pl.program_id(1)
    x = x_ref[...]

    @pl.when(j < n_rot)
    def _():
        y = x * c_ref[...] + pltpu.roll(x, LANES - 16, 1) * sa_ref[...] + pltpu.roll(x, 16, 1) * sb_ref[...]
        o_ref[...] = y.astype(o_ref.dtype)

    @pl.when(j >= n_rot)
    def _():
        o_ref[...] = x.astype(o_ref.dtype)


def rope_cast(qkv, tables, nb):
    n, w = qkv.shape
    cos, sa, sb = tables
    xc = nb * X_CHUNKS

    def tab(i):
        return jnp.where(i < xc, i % X_CHUNKS, X_CHUNKS)

    tspec = pl.BlockSpec((CHUNK, LANES), lambda i, j: (tab(i), 0))
    return pl.pallas_call(
        functools.partial(_rope_kernel, n_rot=2 * D_MODEL // LANES),
        grid=(n // CHUNK, w // LANES),
        in_specs=[pl.BlockSpec((CHUNK, LANES), lambda i, j: (i, j)), tspec, tspec, tspec],
        out_specs=pl.BlockSpec((CHUNK, LANES), lambda i, j: (i, j)),
        out_shape=jax.ShapeDtypeStruct((n, w), BF16),
        compiler_params=_cp(("parallel", "parallel")),
        name="rope_cast",
    )(qkv, cos, sa, sb)


def _da_kernel(q_ref, kx_ref, vx_ref, kc_ref, vc_ref, lam_ref, g_ref, o_ref, *, lambda_init):
    t = pl.program_id(2)
    lane = lax.broadcasted_iota(jnp.int32, (1, LANES), 1)
    q = q_ref[...] * 0.125
    zero = jnp.zeros_like(q)
    qq = jnp.concatenate([jnp.where(lane < DA_HEAD_DIM, q, zero),
                          jnp.where(lane >= DA_HEAD_DIM, q, zero)], axis=0)

    def chunk(k, v, carry):
        m, l, acc = carry
        s = _nt(qq, k)
        m_new = jnp.maximum(m, jnp.max(s, axis=1, keepdims=True))
        alpha = jnp.exp(m - m_new)
        p = jnp.exp(s - m_new)
        l = alpha * l + jnp.sum(p, axis=1, keepdims=True)
        acc = alpha * acc + jnp.dot(p.astype(BF16), v, preferred_element_type=F32)
        return m_new, l, acc

    init = (jnp.full((2 * CHUNK, 1), NEG, F32), jnp.zeros((2 * CHUNK, 1), F32),
            jnp.zeros((2 * CHUNK, LANES), F32))
    carry = chunk(kc_ref[...], vc_ref[...], init)

    def body(c, carry):
        start = pl.multiple_of(c * CHUNK, CHUNK)
        return chunk(kx_ref[pl.ds(start, CHUNK), :], vx_ref[pl.ds(start, CHUNK), :], carry)

    n_x = jnp.where(t < X_CHUNKS, X_CHUNKS, 0)
    _, l, acc = lax.fori_loop(0, n_x, body, carry)
    o = acc / l
    lam = lam_ref[...]
    lam_full = (jnp.exp(jnp.sum(lam[0:1] * lam[1:2], axis=1, keepdims=True))
                - jnp.exp(jnp.sum(lam[2:3] * lam[3:4], axis=1, keepdims=True)) + lambda_init)
    od = o[:CHUNK] - lam_full * o[CHUNK:]
    y = od * lax.rsqrt(jnp.mean(od * od, axis=-1, keepdims=True) + EPS) * g_ref[...]
    o_ref[...] = (y * (1.0 - lambda_init)).astype(o_ref.dtype)


def da_attention(qkv, lam, subln_g, lambda_init, nb):
    n = qkv.shape[0]
    d = D_MODEL
    ncb = d // LANES
    xc = nb * X_CHUNKS

    def qtile(b, t):
        return jnp.where(t < X_CHUNKS, b * X_CHUNKS + t, xc + b)

    return pl.pallas_call(
        functools.partial(_da_kernel, lambda_init=lambda_init),
        grid=(nb, ncb, X_CHUNKS + 1),
        in_specs=[pl.BlockSpec((CHUNK, LANES), lambda b, h, t: (qtile(b, t), h)),
                  pl.BlockSpec((SEQ, LANES), lambda b, h, t: (b, ncb + h)),
                  pl.BlockSpec((SEQ, LANES), lambda b, h, t: (b, 2 * ncb + h)),
                  pl.BlockSpec((CHUNK, LANES), lambda b, h, t: (xc + b, ncb + h)),
                  pl.BlockSpec((CHUNK, LANES), lambda b, h, t: (xc + b, 2 * ncb + h)),
                  pl.BlockSpec((4, DA_HEAD_DIM), lambda b, h, t: (0, 0)),
                  pl.BlockSpec((1, LANES), lambda b, h, t: (0, 0))],
        out_specs=pl.BlockSpec((CHUNK, LANES), lambda b, h, t: (qtile(b, t), h)),
        out_shape=jax.ShapeDtypeStruct((n, d), BF16),
        compiler_params=_cp(("parallel", "parallel", "arbitrary")),
        name="da_attention",
    )(qkv, qkv, qkv, qkv, qkv, lam.astype(F32), subln_g.astype(F32).reshape(1, LANES))


def _block_diag(w):
    per = LANES // ML_BLOCK
    wr = w.reshape(ML_INNER // LANES, per, ML_BLOCK, ML_BLOCK)
    eye = jnp.eye(per, dtype=w.dtype)
    return jnp.einsum('cgio,gh->cgiho', wr, eye).reshape(ML_INNER // LANES, LANES, LANES).astype(BF16)


def _ml_conv_kernel(p_ref, c_ref, n_ref, cw_ref, cb_ref, wq_ref, wk_ref, wv_ref,
                    xc_ref, q_ref, k_ref, v_ref, *, n_xchunks):
    i = pl.program_id(0)
    is_x = i < n_xchunks
    j = i % X_CHUNKS
    halo = 16
    cur = c_ref[...].astype(F32)
    prev = p_ref[CHUNK - halo:CHUNK, :].astype(F32)
    nxt = n_ref[0:halo, :].astype(F32)
    prev = jnp.where(jnp.logical_and(is_x, j > 0), prev, jnp.zeros_like(prev))
    nxt = jnp.where(jnp.logical_and(is_x, j < X_CHUNKS - 1), nxt, jnp.zeros_like(nxt))
    xp = jnp.concatenate([prev, cur, nxt], axis=0)
    rows = CHUNK + 2 * halo
    y = cb_ref[...] + cw_ref[ML_CONV_K // 2:ML_CONV_K // 2 + 1, :] * cur
    for tap in range(ML_CONV_K):
        dlt = tap - ML_CONV_K // 2
        if dlt == 0:
            continue
        shifted = pltpu.roll(xp, (-dlt) % rows, 0)[halo:halo + CHUNK]
        y = y + cw_ref[tap:tap + 1, :] * shifted
    xcb = _silu(y).astype(BF16)
    xc_ref[...] = xcb
    xm = c_ref[...]
    for s in range(xcb.shape[1] // LANES):
        sl = slice(s * LANES, (s + 1) * LANES)
        q_ref[:, sl] = jnp.dot(xcb[:, sl], wq_ref[s], preferred_element_type=F32).astype(BF16)
        k_ref[:, sl] = jnp.dot(xcb[:, sl], wk_ref[s], preferred_element_type=F32).astype(BF16)
        v_ref[:, sl] = jnp.dot(xm[:, sl], wv_ref[s], preferred_element_type=F32).astype(BF16)


def ml_conv_qkv(up, conv_w, conv_b, wq, wk, wv, nb):
    n = up.shape[0]
    nchunks = n // CHUNK
    cw = 512
    ncb = ML_INNER // cw
    sub = cw // LANES
    blk = pl.BlockSpec((CHUNK, cw), lambda i, c: (i, c))
    wspec = pl.BlockSpec((sub, LANES, LANES), lambda i, c: (c, 0, 0))
    out = jax.ShapeDtypeStruct((n, ML_INNER), BF16)
    return pl.pallas_call(
        functools.partial(_ml_conv_kernel, n_xchunks=nb * X_CHUNKS),
        grid=(nchunks, ncb),
        in_specs=[pl.BlockSpec((CHUNK, cw), lambda i, c: (jnp.maximum(i - 1, 0), c)),
                  blk,
                  pl.BlockSpec((CHUNK, cw), lambda i, c: (jnp.minimum(i + 1, nchunks - 1), c)),
                  pl.BlockSpec((ML_CONV_K, cw), lambda i, c: (0, c)),
                  pl.BlockSpec((1, cw), lambda i, c: (0, c)),
                  wspec, wspec, wspec],
        out_specs=[blk, blk, blk, blk],
        out_shape=[out, out, out, out],
        compiler_params=_cp(("parallel", "parallel")),
        name="ml_conv_qkv",
    )(up, up, up, conv_w.astype(F32), conv_b.astype(F32).reshape(1, ML_INNER), wq, wk, wv)


def _ml_gate_kernel(q_ref, k_ref, v_ref, w_ref, b_ref, o_ref):
    acc = jnp.dot(q_ref[...], w_ref[0], preferred_element_type=F32)
    acc = acc + jnp.dot(k_ref[...], w_ref[1], preferred_element_type=F32)
    acc = acc + jnp.dot(v_ref[...], w_ref[2], preferred_element_type=F32)
    o_ref[...] = acc + b_ref[...]


def ml_gates(q, k, v, wg, bg):
    n = q.shape[0]
    tm = _tile(n, 512)
    blk = pl.BlockSpec((tm, ML_INNER), lambda i: (i, 0))
    return pl.pallas_call(
        _ml_gate_kernel,
        grid=(n // tm,),
        in_specs=[blk, blk, blk,
                  pl.BlockSpec((3, ML_INNER, LANES), lambda i: (0, 0, 0)),
                  pl.BlockSpec((1, LANES), lambda i: (0, 0))],
        out_specs=pl.BlockSpec((tm, LANES), lambda i: (i, 0)),
        out_shape=jax.ShapeDtypeStruct((n, LANES), F32),
        compiler_params=_cp(("parallel",)),
        name="ml_gates",
    )(q, k, v, wg, bg)


def _log_sigmoid(x):
    return jnp.minimum(x, 0.0) - jnp.log(1.0 + jnp.exp(-jnp.abs(x)))


def _ml_scan_kernel(q_ref, k_ref, v_ref, g_ref, gt_ref, o_ref, c_sc, n_sc, m_sc):
    h = pl.program_id(1)
    d = pl.program_id(2)
    p = pl.program_id(3)
    ninf = -jnp.inf

    @pl.when(p == 0)
    def _():
        c_sc[...] = jnp.zeros_like(c_sc)
        n_sc[...] = jnp.zeros_like(n_sc)
        m_sc[...] = jnp.full_like(m_sc, ninf)

    L = CHUNK
    col_i = d * 2 * ML_HEADS + h
    col_f = col_i + ML_HEADS
    lane = lax.broadcasted_iota(jnp.int32, (1, LANES), 1)
    g = g_ref[...]
    i_col = jnp.sum(jnp.where(lane == col_i, g, 0.0), axis=1, keepdims=True)
    f_col = _log_sigmoid(jnp.sum(jnp.where(lane == col_f, g, 0.0), axis=1, keepdims=True))
    i_row = gt_ref[pl.ds(col_i, 1), :]
    f_row = _log_sigmoid(gt_ref[pl.ds(col_f, 1), :])

    jj = lax.broadcasted_iota(jnp.int32, (L, L), 0)
    ss = lax.broadcasted_iota(jnp.int32, (L, L), 1)
    fwd = d == 0
    bwd = d == 1
    valid = jnp.logical_or(jnp.logical_and(ss <= jj, fwd), jnp.logical_and(ss >= jj, bwd))
    valid_t = jnp.logical_or(jnp.logical_and(jj <= ss, fwd), jnp.logical_and(jj >= ss, bwd))
    bcum_col = jnp.sum(jnp.where(valid, f_row, 0.0), axis=1, keepdims=True)
    bcum_row = jnp.sum(jnp.where(valid_t, f_col, 0.0), axis=0, keepdims=True)
    b_last = jnp.sum(f_row, axis=1, keepdims=True)
    m_prev = m_sc[...]

    logd = jnp.where(valid, bcum_col - bcum_row + i_row, ninf)
    log_inter = bcum_col + m_prev
    m_j = jnp.maximum(log_inter, jnp.max(logd, axis=1, keepdims=True))
    dmat = jnp.exp(logd - m_j)
    inter = jnp.exp(log_inter - m_j)

    q = q_ref[...]
    k = k_ref[...]
    v = v_ref[...]
    scale = ML_HEAD_DIM ** -0.5
    sc = _nt(q, k) * scale * dmat
    c_prev = c_sc[...]
    n_prev = n_sc[...]
    qc = jnp.dot(q, c_prev.astype(BF16), preferred_element_type=F32) * scale
    num = inter * qc + jnp.dot(sc.astype(BF16), v, preferred_element_type=F32)
    qn = jnp.sum(q.astype(F32) * n_prev, axis=1, keepdims=True) * scale
    den = inter * qn + jnp.sum(sc, axis=1, keepdims=True)
    o_ref[0] = num / jnp.maximum(jnp.abs(den), jnp.exp(-m_j))

    ls = b_last - bcum_col + i_col
    m_new = jnp.maximum(b_last + m_prev, jnp.max(ls, axis=0, keepdims=True))
    w = jnp.exp(ls - m_new)
    decay = jnp.exp(b_last + m_prev - m_new)
    kw = k.astype(F32) * w
    c_sc[...] = decay * c_prev + lax.dot_general(kw.astype(BF16), v, (((0,), (0,)), ((), ())),
                                                 preferred_element_type=F32)
    n_sc[...] = decay * n_prev + jnp.sum(kw, axis=0, keepdims=True)
    m_sc[...] = m_new


def ml_scan(q, k, v, g, gt, nb):
    n = q.shape[0]
    xc = nb * X_CHUNKS

    def cidx(b, d, p):
        xi = jnp.where(d == 0, p - 1, X_CHUNKS - p)
        return jnp.where(p == 0, xc + b, b * X_CHUNKS + xi)

    blk = pl.BlockSpec((CHUNK, ML_HEAD_DIM), lambda b, h, d, p: (cidx(b, d, p), h))
    return pl.pallas_call(
        _ml_scan_kernel,
        grid=(nb, ML_HEADS, 2, X_CHUNKS + 1),
        in_specs=[blk, blk, blk,
                  pl.BlockSpec((CHUNK, LANES), lambda b, h, d, p: (cidx(b, d, p), 0)),
                  pl.BlockSpec((16, CHUNK), lambda b, h, d, p: (0, cidx(b, d, p)))],
        out_specs=pl.BlockSpec((1, CHUNK, ML_HEAD_DIM), lambda b, h, d, p: (d, cidx(b, d, p), h)),
        out_shape=jax.ShapeDtypeStruct((2, n, ML_INNER), F32),
        scratch_shapes=[pltpu.VMEM((ML_HEAD_DIM, ML_HEAD_DIM), F32),
                        pltpu.VMEM((1, ML_HEAD_DIM), F32),
                        pltpu.VMEM((1, 1), F32)],
        compiler_params=_cp(("parallel", "parallel", "parallel", "arbitrary")),
        name="ml_scan",
    )(q, k, v, g, gt)


def _ml_finish_kernel(h_ref, xc_ref, z_ref, gn_ref, sk_ref, o_ref):
    hh = h_ref[0] + h_ref[1]
    z = z_ref[...].astype(F32)
    gate = _silu(z)
    for hd in range(ML_HEADS):
        sl = slice(hd * ML_HEAD_DIM, (hd + 1) * ML_HEAD_DIM)
        seg = hh[:, sl]
        mu = jnp.mean(seg, axis=-1, keepdims=True)
        cen = seg - mu
        var = jnp.mean(cen * cen, axis=-1, keepdims=True)
        hn = cen * lax.rsqrt(var + EPS) * gn_ref[:, sl]
        a = (hn + sk_ref[:, sl] * xc_ref[:, sl].astype(F32)) * gate[:, sl]
        o_ref[:, sl] = a.astype(o_ref.dtype)


def ml_finish(hs, xc, up, gn_w, skip):
    n = xc.shape[0]
    tm = CHUNK
    vec = pl.BlockSpec((1, ML_INNER), lambda i: (0, 0))
    return pl.pallas_call(
        _ml_finish_kernel,
        grid=(n // tm,),
        in_specs=[pl.BlockSpec((2, tm, ML_INNER), lambda i: (0, i, 0)),
                  pl.BlockSpec((tm, ML_INNER), lambda i: (i, 0)),
                  pl.BlockSpec((tm, ML_INNER), lambda i: (i, 1)),
                  vec, vec],
        out_specs=pl.BlockSpec((tm, ML_INNER), lambda i: (i, 0)),
        out_shape=jax.ShapeDtypeStruct((n, ML_INNER), BF16),
        compiler_params=_cp(("parallel",)),
        name="ml_finish",
    )(hs, xc, up, gn_w.astype(F32).reshape(1, ML_INNER), skip.astype(F32).reshape(1, ML_INNER))


def _router_kernel(x_ref, g_ref, sh_ref, sc_ref, w_ref, b_ref, h_ref, o_ref):
    x = x_ref[...]
    y = x * lax.rsqrt(jnp.mean(x * x, axis=-1, keepdims=True) + EPS) * g_ref[...]
    hf = y * (1.0 + sc_ref[0, 0]) + sh_ref[0, 0]
    h_hi = hf.astype(BF16)
    h_ref[...] = h_hi
    h_lo = (hf - h_hi.astype(F32)).astype(BF16)
    logits = (jnp.dot(h_hi, w_ref[0], preferred_element_type=F32)
              + jnp.dot(h_hi, w_ref[1], preferred_element_type=F32)
              + jnp.dot(h_lo, w_ref[0], preferred_element_type=F32)) + b_ref[...]
    lane = lax.broadcasted_iota(jnp.int32, (1, LANES), 1).astype(F32)
    big = 1e9
    ninf = -jnp.inf
    is_g = jnp.logical_and(lane >= MOE_E, lane < MOE_E + MOE_GROUPS)
    gl = jnp.where(is_g, logits, ninf)
    gmax = jnp.max(gl, axis=1, keepdims=True)
    g_val = 1.0 / jnp.sum(jnp.exp(gl - gmax), axis=1, keepdims=True)
    g_idx = jnp.min(jnp.where(gl == gmax, lane, big), axis=1, keepdims=True) - MOE_E
    lo = g_idx * MOE_EPG
    sel = jnp.logical_and(lane >= lo, lane < lo + MOE_EPG)
    el = jnp.where(sel, logits, ninf)
    e1 = jnp.max(el, axis=1, keepdims=True)
    esum = jnp.sum(jnp.exp(el - e1), axis=1, keepdims=True)
    i1 = jnp.min(jnp.where(el == e1, lane, big), axis=1, keepdims=True)
    el2 = jnp.where(lane == i1, ninf, el)
    e2 = jnp.max(el2, axis=1, keepdims=True)
    i2 = jnp.min(jnp.where(el2 == e2, lane, big), axis=1, keepdims=True)
    p1 = 1.0 / esum
    p2 = jnp.exp(e2 - e1) / esum
    w1 = g_val * p1 / (p1 + p2)
    w2 = g_val * p2 / (p1 + p2)
    o_ref[...] = jnp.where(lane == i1, w1, 0.0) + jnp.where(lane == i2, w2, 0.0)


def moe_router(s, g, mod, k_shift, k_scale, wr, br, nb):
    n, d = s.shape
    tm = _tile(n, 512)
    return pl.pallas_call(
        _router_kernel,
        grid=(n // tm,),
        in_specs=[pl.BlockSpec((tm, d), lambda i: (i, 0)),
                  pl.BlockSpec((1, d), lambda i: (0, 0)),
                  pl.BlockSpec((1, 1, 1, d), lambda i: (_mod_row(i, tm, nb), k_shift, 0, 0)),
                  pl.BlockSpec((1, 1, 1, d), lambda i: (_mod_row(i, tm, nb), k_scale, 0, 0)),
                  pl.BlockSpec((2, d, LANES), lambda i: (0, 0, 0)),
                  pl.BlockSpec((1, LANES), lambda i: (0, 0))],
        out_specs=[pl.BlockSpec((tm, d), lambda i: (i, 0)),
                   pl.BlockSpec((tm, LANES), lambda i: (i, 0))],
        out_shape=[jax.ShapeDtypeStruct((n, d), BF16),
                   jax.ShapeDtypeStruct((n, LANES), F32)],
        compiler_params=_cp(("parallel",)),
        name="moe_router",
    )(s, g.reshape(1, d), mod, mod, wr, br)


def _moe_kernel(h_ref, gates_ref, w1_ref, w3_ref, w2_ref, r_ref, g_ref, o_ref, acc_ref):
    e = pl.program_id(1)

    @pl.when(e == 0)
    def _():
        acc_ref[...] = jnp.zeros_like(acc_ref)

    x = h_ref[...]
    a = jnp.dot(x, w1_ref[0], preferred_element_type=F32)
    b = jnp.dot(x, w3_ref[0], preferred_element_type=F32)
    he = (_silu(a) * b).astype(BF16)
    y = jnp.dot(he, w2_ref[0], preferred_element_type=F32)
    lane = lax.broadcasted_iota(jnp.int32, (1, LANES), 1)
    ge = jnp.sum(jnp.where(lane == e, gates_ref[...], 0.0), axis=1, keepdims=True)
    acc_ref[...] += y * ge

    @pl.when(e == MOE_E - 1)
    def _():
        o_ref[...] = r_ref[...] + g_ref[0, 0] * acc_ref[...]


def moe_dense(h, gates, w1, w3, w2, s, mod, k_gate, nb):
    n, d = h.shape
    tm = _tile(n, 1024)
    return pl.pallas_call(
        _moe_kernel,
        grid=(n // tm, MOE_E),
        in_specs=[pl.BlockSpec((tm, d), lambda i, e: (i, 0)),
                  pl.BlockSpec((tm, LANES), lambda i, e: (i, 0)),
                  pl.BlockSpec((1, d, MOE_HIDDEN), lambda i, e: (e, 0, 0)),
                  pl.BlockSpec((1, d, MOE_HIDDEN), lambda i, e: (e, 0, 0)),
                  pl.BlockSpec((1, MOE_HIDDEN, d), lambda i, e: (e, 0, 0)),
                  pl.BlockSpec((tm, d), lambda i, e: (i, 0)),
                  pl.BlockSpec((1, 1, 1, d), lambda i, e: (_mod_row(i, tm, nb), k_gate, 0, 0))],
        out_specs=pl.BlockSpec((tm, d), lambda i, e: (i, 0)),
        out_shape=jax.ShapeDtypeStruct((n, d), F32),
        scratch_shapes=[pltpu.VMEM((tm, d), F32)],
        input_output_aliases={5: 0},
        compiler_params=_cp(("parallel", "arbitrary")),
        name="moe_dense",
    )(h, gates, w1, w3, w2, s, mod)


def _na_layer(h, w_qkv, w_o, rpb, s, mod, nb):
    qkv = matmul(h, w_qkv.astype(BF16), BF16)
    o = na_attention(qkv, _na_bias_tables(rpb), nb)
    return matmul_residual(o, w_o.astype(BF16), s, mod, 2, nb)


def _da_layer(h, w_qkv, lam, subln_g, w_o, lambda_init, s, mod, nb):
    qkv = matmul(h, w_qkv.astype(BF16), F32)
    qkv = rope_cast(qkv, _rope_tables(nb), nb)
    o = da_attention(qkv, lam, subln_g, lambda_init, nb)
    return matmul_residual(o, w_o.astype(BF16), s, mod, 2, nb)


def _ml_layer(h, w_up, conv_w, conv_b, w_q, w_k, w_v, w_gate, b_gate, gn_w, skip, w_down, s, mod, nb):
    up = matmul(h, w_up.astype(BF16), BF16)
    xc, q, k, v = ml_conv_qkv(up, conv_w, conv_b, _block_diag(w_q), _block_diag(w_k), _block_diag(w_v), nb)
    ng = w_gate.shape[1]
    wg = jnp.pad(w_gate, ((0, 0), (0, LANES - ng))).reshape(3, ML_INNER, LANES).astype(BF16)
    bg = jnp.pad(b_gate.astype(F32), (0, LANES - ng)).reshape(1, LANES)
    g = ml_gates(q, k, v, wg, bg)
    gt = g[:, :ng].T
    hs = ml_scan(q, k, v, g, gt, nb)
    a = ml_finish(hs, xc, up, gn_w, skip)
    return matmul_residual(a, w_down.astype(BF16), s, mod, 2, nb)


def _moe_layer(norm_g, w_group, b_group, w_router, b_router, w1, w3, w2, s, mod, nb):
    d = s.shape[1]
    pad = LANES - MOE_E - MOE_GROUPS
    wr = jnp.concatenate([w_router, w_group, jnp.zeros((d, pad), w_router.dtype)], axis=1).astype(F32)
    wr_hi = wr.astype(BF16)
    wr_lo = (wr - wr_hi.astype(F32)).astype(BF16)
    br = jnp.concatenate([b_router, b_group, jnp.zeros((pad,), b_router.dtype)]).astype(F32).reshape(1, LANES)
    h, gates = moe_router(s, norm_g.astype(F32), mod, 3, 4, jnp.stack([wr_hi, wr_lo]), br, nb)
    return moe_dense(h, gates, w1.astype(BF16), w3.astype(BF16), w2.astype(BF16), s, mod, 5, nb)


def kernel(x, c, ctx, c_ctx, mod_w, mod_b, norm_g, final_g, na_w_qkv, na_w_o, na_rpb, ml_w_up, ml_conv_w, ml_conv_b, ml_w_q, ml_w_k, ml_w_v, ml_w_gate, ml_b_gate, ml_gn_w, ml_skip, ml_w_down, da_w_qkv, da_lambda, da_subln_g, da_w_o, moe_w_group, moe_b_group, moe_w_router, moe_b_router, moe_w1, moe_w3, moe_w2):
    nb, seq, d = x.shape
    assert (seq, d, ctx.shape[1]) == (SEQ, D_MODEL, CTX_LEN) and nb < 16
    depth = mod_w.shape[0]
    nx = nb * seq
    s = jnp.concatenate([x.reshape(nx, d), ctx.reshape(nb * CTX_LEN, d)], axis=0).astype(F32)
    cvec = jnp.concatenate([c, c_ctx[None, :], jnp.zeros((16 - nb - 1, d), c.dtype)], axis=0).astype(F32)
    mods = mod_vectors(cvec, mod_w, mod_b).reshape(depth, 16, 6, 1, d)

    for i in range(depth):
        kind, j = i % N_MIXERS, i // N_MIXERS
        mod = mods[i]
        h = norm_mod(s, norm_g[i, 0].astype(F32), mod, 0, 1, nb)
        if kind == 0:
            s = _na_layer(h, na_w_qkv[j], na_w_o[j], na_rpb[j], s, mod, nb)
        elif kind == 1:
            s = _ml_layer(h, ml_w_up[j], ml_conv_w[j], ml_conv_b[j], ml_w_q[j], ml_w_k[j], ml_w_v[j],
                          ml_w_gate[j], ml_b_gate[j], ml_gn_w[j], ml_skip[j], ml_w_down[j], s, mod, nb)
        else:
            lambda_init = 0.8 - 0.6 * math.exp(-0.3 * i)
            s = _da_layer(h, da_w_qkv[j], da_lambda[j], da_subln_g[j], da_w_o[j], lambda_init, s, mod, nb)
        s = _moe_layer(norm_g[i, 1], moe_w_group[i], moe_b_group[i], moe_w_router[i], moe_b_router[i],
                       moe_w1[i], moe_w3[i], moe_w2[i], s, mod, nb)
    return final_norm(s, final_g.astype(F32), nx).reshape(nb, seq, d)
```

```python
import functools
import math

import numpy as np
import jax
import jax.numpy as jnp
from jax import lax
from jax.experimental import pallas as pl
from jax.experimental.pallas import tpu as pltpu

F32 = jnp.float32
BF16 = jnp.bfloat16

D_MODEL = 1024
SEQ = 4096
CTX_LEN = 256
GRID_W = 64
N_MIXERS = 3
EPS = 1e-6

NA_HEADS = 16
NA_WIN_H = 8
NA_WIN_W = 16

ML_HEADS = 4
ML_INNER = 2 * D_MODEL
ML_HEAD_DIM = ML_INNER // ML_HEADS
ML_BLOCK = 4
ML_CONV_K = 5

DA_HEADS = 8
DA_HEAD_DIM = 64
ROPE_BASE = 10000.0

MOE_GROUPS = 4
MOE_EPG = 8
MOE_E = MOE_GROUPS * MOE_EPG
MOE_HIDDEN = 512
MOE_TM = 256
SEL_LANE = 64

LANES = 128
CHUNK = 256
X_CHUNKS = SEQ // CHUNK
NEG = -1e30
VMEM_LIMIT = 56 * 1024 * 1024


def _cp(sem, vmem=VMEM_LIMIT):
    return pltpu.CompilerParams(dimension_semantics=sem, vmem_limit_bytes=vmem)


def _tile(n, pref):
    tm = pref
    while n % tm:
        tm //= 2
    assert tm >= CHUNK
    return tm


def _mod_row(i, tm, nb):
    return jnp.where(i < nb * (SEQ // tm), i // (SEQ // tm), nb)


def _nt(a, b):
    return lax.dot_general(a, b, (((1,), (1,)), ((), ())), preferred_element_type=F32)


def _silu(x):
    return x * jax.nn.sigmoid(x)


def _mod_kernel(c_ref, w_ref, b_ref, o_ref):
    a = _silu(c_ref[...]).astype(BF16)
    o_ref[0] = jnp.dot(a, w_ref[0].astype(BF16), preferred_element_type=F32) + b_ref[0]


def mod_vectors(cvec, mod_w, mod_b):
    depth, d, n6 = mod_w.shape
    tn = 1024
    return pl.pallas_call(
        _mod_kernel,
        grid=(depth, n6 // tn),
        in_specs=[pl.BlockSpec((16, d), lambda l, j: (0, 0)),
                  pl.BlockSpec((1, d, tn), lambda l, j: (l, 0, j)),
                  pl.BlockSpec((1, 1, tn), lambda l, j: (l, 0, j))],
        out_specs=pl.BlockSpec((1, 16, tn), lambda l, j: (l, 0, j)),
        out_shape=jax.ShapeDtypeStruct((depth, 16, n6), F32),
        compiler_params=_cp(("parallel", "parallel")),
        name="mod_vectors",
    )(cvec, mod_w, mod_b.reshape(depth, 1, n6))


def _norm_mod_kernel(x_ref, g_ref, sh_ref, sc_ref, o_ref):
    x = x_ref[...]
    y = x * lax.rsqrt(jnp.mean(x * x, axis=-1, keepdims=True) + EPS) * g_ref[...]
    o_ref[...] = (y * (1.0 + sc_ref[0, 0]) + sh_ref[0, 0]).astype(o_ref.dtype)


def norm_mod(s, g, mod, k_shift, k_scale, nb):
    n, d = s.shape
    tm = _tile(n, 512)
    return pl.pallas_call(
        _norm_mod_kernel,
        grid=(n // tm,),
        in_specs=[pl.BlockSpec((tm, d), lambda i: (i, 0)),
                  pl.BlockSpec((1, d), lambda i: (0, 0)),
                  pl.BlockSpec((1, 1, 1, d), lambda i: (_mod_row(i, tm, nb), k_shift, 0, 0)),
                  pl.BlockSpec((1, 1, 1, d), lambda i: (_mod_row(i, tm, nb), k_scale, 0, 0))],
        out_specs=pl.BlockSpec((tm, d), lambda i: (i, 0)),
        out_shape=jax.ShapeDtypeStruct((n, d), BF16),
        compiler_params=_cp(("parallel",)),
        name="norm_mod",
    )(s, g.reshape(1, d), mod, mod)


def _final_norm_kernel(x_ref, g_ref, o_ref):
    x = x_ref[...]
    o_ref[...] = x * lax.rsqrt(jnp.mean(x * x, axis=-1, keepdims=True) + EPS) * g_ref[...]


def final_norm(s, g, n_rows):
    d = s.shape[1]
    tm = _tile(n_rows, 512)
    return pl.pallas_call(
        _final_norm_kernel,
        grid=(n_rows // tm,),
        in_specs=[pl.BlockSpec((tm, d), lambda i: (i, 0)),
                  pl.BlockSpec((1, d), lambda i: (0, 0))],
        out_specs=pl.BlockSpec((tm, d), lambda i: (i, 0)),
        out_shape=jax.ShapeDtypeStruct((n_rows, d), F32),
        compiler_params=_cp(("parallel",)),
        name="final_norm",
    )(s, g.reshape(1, d))


def _mm_kernel(a_ref, w_ref, o_ref):
    o_ref[...] = jnp.dot(a_ref[...], w_ref[...], preferred_element_type=F32).astype(o_ref.dtype)


def matmul(a, w, out_dtype, tm=512, tn=1024):
    n, k = a.shape
    tm = _tile(n, tm)
    nout = w.shape[1]
    tn = min(tn, nout)
    return pl.pallas_call(
        _mm_kernel,
        grid=(nout // tn, n // tm),
        in_specs=[pl.BlockSpec((tm, k), lambda j, i: (i, 0)),
                  pl.BlockSpec((k, tn), lambda j, i: (0, j))],
        out_specs=pl.BlockSpec((tm, tn), lambda j, i: (i, j)),
        out_shape=jax.ShapeDtypeStruct((n, nout), out_dtype),
        compiler_params=_cp(("parallel", "parallel")),
        name="matmul",
    )(a, w)


def _mm_res_kernel(a_ref, w_ref, r_ref, g_ref, o_ref):
    acc = jnp.dot(a_ref[...], w_ref[...], preferred_element_type=F32)
    o_ref[...] = r_ref[...] + g_ref[0, 0] * acc


def matmul_residual(a, w, s, mod, k_gate, nb):
    n, k = a.shape
    d = w.shape[1]
    tm = _tile(n, 512)
    return pl.pallas_call(
        _mm_res_kernel,
        grid=(n // tm,),
        in_specs=[pl.BlockSpec((tm, k), lambda i: (i, 0)),
                  pl.BlockSpec((k, d), lambda i: (0, 0)),
                  pl.BlockSpec((tm, d), lambda i: (i, 0)),
                  pl.BlockSpec((1, 1, 1, d), lambda i: (_mod_row(i, tm, nb), k_gate, 0, 0))],
        out_specs=pl.BlockSpec((tm, d), lambda i: (i, 0)),
        out_shape=jax.ShapeDtypeStruct((n, d), F32),
        input_output_aliases={2: 0},
        compiler_params=_cp(("parallel",)),
        name="matmul_residual",
    )(a, w, s, mod)


def _na_bias_tables(rpb):
    o = np.arange(NA_WIN_H)[:, None]
    j = np.arange(NA_WIN_H)[None, :]
    dy = j - o + NA_WIN_H - 1
    qc = np.arange(GRID_W)[:, None]
    kc = np.arange(GRID_W)[None, :]
    dx = np.clip(kc - qc, -(NA_WIN_W - 1), NA_WIN_W - 1) + NA_WIN_W - 1
    w_start = np.clip(qc - NA_WIN_W // 2, 0, GRID_W - NA_WIN_W)
    valid = (kc >= w_start) & (kc < w_start + NA_WIN_W)
    tbl = rpb.astype(F32)[:, dy][:, :, :, dx]
    tbl = jnp.where(jnp.asarray(valid)[None, None, None], tbl, NEG)
    tbl = tbl.transpose(1, 0, 3, 2, 4)
    return tbl.reshape(NA_WIN_H, NA_HEADS, GRID_W, NA_WIN_H * GRID_W)


def _na_kernel(q_ref, kx_ref, vx_ref, kc_ref, vc_ref, bias_ref, o_ref):
    t = pl.program_id(2)
    lane = lax.broadcasted_iota(jnp.int32, (1, LANES), 1)
    kc = kc_ref[...]
    vc = vc_ref[...]
    nwin = NA_WIN_H * GRID_W

    def attend(qm, kw, vw, bias):
        s_c = _nt(qm, kc)
        m = jnp.max(s_c, axis=1, keepdims=True)
        if kw is not None:
            s_w = _nt(qm, kw) + bias
            m = jnp.maximum(m, jnp.max(s_w, axis=1, keepdims=True))
        p_c = jnp.exp(s_c - m)
        l = jnp.sum(p_c, axis=1, keepdims=True)
        o = jnp.dot(p_c.astype(BF16), vc, preferred_element_type=F32)
        if kw is not None:
            p_w = jnp.exp(s_w - m)
            l = l + jnp.sum(p_w, axis=1, keepdims=True)
            o = o + jnp.dot(p_w.astype(BF16), vw, preferred_element_type=F32)
        return o / l

    def two_heads(q, kw, vw, off):
        outs = []
        for hh in range(2):
            qm = jnp.where((lane // 64) == hh, q, jnp.zeros_like(q))
            bias = None if kw is None else bias_ref[off, hh]
            outs.append(attend(qm, kw, vw, bias))
        return jnp.where(lane < 64, outs[0], outs[1])

    @pl.when(t < X_CHUNKS)
    def _():
        rows = CHUNK // GRID_W
        for i in range(rows):
            r = t * rows + i
            rs = jnp.clip(r - NA_WIN_H // 2, 0, SEQ // GRID_W - NA_WIN_H)
            start = pl.multiple_of(rs * GRID_W, GRID_W)
            kw = kx_ref[pl.ds(start, nwin), :]
            vw = vx_ref[pl.ds(start, nwin), :]
            q = q_ref[i * GRID_W:(i + 1) * GRID_W, :] * 0.125
            o = two_heads(q, kw, vw, r - rs)
            o_ref[i * GRID_W:(i + 1) * GRID_W, :] = o.astype(o_ref.dtype)

    @pl.when(t == X_CHUNKS)
    def _():
        q = q_ref[...] * 0.125
        o_ref[...] = two_heads(q, None, None, None).astype(o_ref.dtype)


def na_attention(qkv, bias, nb):
    n = qkv.shape[0]
    d = D_MODEL
    ncb = d // LANES
    xc = nb * X_CHUNKS

    def qtile(b, t):
        return jnp.where(t < X_CHUNKS, b * X_CHUNKS + t, xc + b)

    return pl.pallas_call(
        _na_kernel,
        grid=(nb, ncb, X_CHUNKS + 1),
        in_specs=[pl.BlockSpec((CHUNK, LANES), lambda b, h, t: (qtile(b, t), h)),
                  pl.BlockSpec((SEQ, LANES), lambda b, h, t: (b, ncb + h)),
                  pl.BlockSpec((SEQ, LANES), lambda b, h, t: (b, 2 * ncb + h)),
                  pl.BlockSpec((CHUNK, LANES), lambda b, h, t: (xc + b, ncb + h)),
                  pl.BlockSpec((CHUNK, LANES), lambda b, h, t: (xc + b, 2 * ncb + h)),
                  pl.BlockSpec((NA_WIN_H, 2, GRID_W, NA_WIN_H * GRID_W), lambda b, h, t: (0, h, 0, 0))],
        out_specs=pl.BlockSpec((CHUNK, LANES), lambda b, h, t: (qtile(b, t), h)),
        out_shape=jax.ShapeDtypeStruct((n, d), BF16),
        compiler_params=_cp(("parallel", "parallel", "arbitrary")),
        name="na_attention",
    )(qkv, qkv, qkv, qkv, qkv, bias)


def _rope_tables(nb):
    half = DA_HEAD_DIM // 2
    freqs = ROPE_BASE ** (-np.arange(0, half, 2, dtype=np.float32) / half)
    t = np.arange(SEQ)
    row, col = t // GRID_W, t % GRID_W
    lane = np.arange(LANES)
    l64 = lane % DA_HEAD_DIM
    use_col = (l64 // half) == 1
    l32 = l64 % half
    fi = l32 % (half // 2)
    second = l32 >= half // 2
    pos = np.where(use_col[None, :], col[:, None], row[:, None]).astype(np.float32)
    ang = pos * freqs[fi][None, :]
    cos = np.cos(ang).astype(np.float32)
    sin = np.sin(ang).astype(np.float32)
    sa = np.where(second[None, :], 0.0, -sin).astype(np.float32)
    sb = np.where(second[None, :], sin, 0.0).astype(np.float32)
    ident = np.ones((CHUNK, LANES), np.float32)
    zero = np.zeros((CHUNK, LANES), np.float32)
    return (jnp.asarray(np.concatenate([cos, ident])),
            jnp.asarray(np.concatenate([sa, zero])),
            jnp.asarray(np.concatenate([sb, zero])))


def _rope_kernel(x_ref, c_ref, sa_ref, sb_ref, o_ref, *, n_rot):
    j = pl.program_id(1)
    x = x_ref[...]

    @pl.when(j < n_rot)
    def _():
        y = x * c_ref[...] + pltpu.roll(x, LANES - 16, 1) * sa_ref[...] + pltpu.roll(x, 16, 1) * sb_ref[...]
        o_ref[...] = y.astype(o_ref.dtype)

    @pl.when(j >= n_rot)
    def _():
        o_ref[...] = x.astype(o_ref.dtype)


def rope_cast(qkv, tables, nb):
    n, w = qkv.shape
    cos, sa, sb = tables
    xc = nb * X_CHUNKS

    def tab(i):
        return jnp.where(i < xc, i % X_CHUNKS, X_CHUNKS)

    tspec = pl.BlockSpec((CHUNK, LANES), lambda i, j: (tab(i), 0))
    return pl.pallas_call(
        functools.partial(_rope_kernel, n_rot=2 * D_MODEL // LANES),
        grid=(n // CHUNK, w // LANES),
        in_specs=[pl.BlockSpec((CHUNK, LANES), lambda i, j: (i, j)), tspec, tspec, tspec],
        out_specs=pl.BlockSpec((CHUNK, LANES), lambda i, j: (i, j)),
        out_shape=jax.ShapeDtypeStruct((n, w), BF16),
        compiler_params=_cp(("parallel", "parallel")),
        name="rope_cast",
    )(qkv, cos, sa, sb)


def _da_kernel(q_ref, kx_ref, vx_ref, kc_ref, vc_ref, lam_ref, g_ref, o_ref, *, lambda_init):
    t = pl.program_id(2)
    lane = lax.broadcasted_iota(jnp.int32, (1, LANES), 1)
    q = q_ref[...] * 0.125
    zero = jnp.zeros_like(q)
    qq = jnp.concatenate([jnp.where(lane < DA_HEAD_DIM, q, zero),
                          jnp.where(lane >= DA_HEAD_DIM, q, zero)], axis=0)

    def chunk(k, v, carry):
        m, l, acc = carry
        s = _nt(qq, k)
        m_new = jnp.maximum(m, jnp.max(s, axis=1, keepdims=True))
        alpha = jnp.exp(m - m_new)
        p = jnp.exp(s - m_new)
        l = alpha * l + jnp.sum(p, axis=1, keepdims=True)
        acc = alpha * acc + jnp.dot(p.astype(BF16), v, preferred_element_type=F32)
        return m_new, l, acc

    init = (jnp.full((2 * CHUNK, 1), NEG, F32), jnp.zeros((2 * CHUNK, 1), F32),
            jnp.zeros((2 * CHUNK, LANES), F32))
    carry = chunk(kc_ref[...], vc_ref[...], init)

    def body(c, carry):
        start = pl.multiple_of(c * CHUNK, CHUNK)
        return chunk(kx_ref[pl.ds(start, CHUNK), :], vx_ref[pl.ds(start, CHUNK), :], carry)

    n_x = jnp.where(t < X_CHUNKS, X_CHUNKS, 0)
    _, l, acc = lax.fori_loop(0, n_x, body, carry)
    o = acc / l
    lam = lam_ref[...]
    lam_full = (jnp.exp(jnp.sum(lam[0:1] * lam[1:2], axis=1, keepdims=True))
                - jnp.exp(jnp.sum(lam[2:3] * lam[3:4], axis=1, keepdims=True)) + lambda_init)
    od = o[:CHUNK] - lam_full * o[CHUNK:]
    y = od * lax.rsqrt(jnp.mean(od * od, axis=-1, keepdims=True) + EPS) * g_ref[...]
    o_ref[...] = (y * (1.0 - lambda_init)).astype(o_ref.dtype)


def da_attention(qkv, lam, subln_g, lambda_init, nb):
    n = qkv.shape[0]
    d = D_MODEL
    ncb = d // LANES
    xc = nb * X_CHUNKS

    def qtile(b, t):
        return jnp.where(t < X_CHUNKS, b * X_CHUNKS + t, xc + b)

    return pl.pallas_call(
        functools.partial(_da_kernel, lambda_init=lambda_init),
        grid=(nb, ncb, X_CHUNKS + 1),
        in_specs=[pl.BlockSpec((CHUNK, LANES), lambda b, h, t: (qtile(b, t), h)),
                  pl.BlockSpec((SEQ, LANES), lambda b, h, t: (b, ncb + h)),
                  pl.BlockSpec((SEQ, LANES), lambda b, h, t: (b, 2 * ncb + h)),
                  pl.BlockSpec((CHUNK, LANES), lambda b, h, t: (xc + b, ncb + h)),
                  pl.BlockSpec((CHUNK, LANES), lambda b, h, t: (xc + b, 2 * ncb + h)),
                  pl.BlockSpec((4, DA_HEAD_DIM), lambda b, h, t: (0, 0)),
                  pl.BlockSpec((1, LANES), lambda b, h, t: (0, 0))],
        out_specs=pl.BlockSpec((CHUNK, LANES), lambda b, h, t: (qtile(b, t), h)),
        out_shape=jax.ShapeDtypeStruct((n, d), BF16),
        compiler_params=_cp(("parallel", "parallel", "arbitrary")),
        name="da_attention",
    )(qkv, qkv, qkv, qkv, qkv, lam.astype(F32), subln_g.astype(F32).reshape(1, LANES))


def _block_diag(w):
    per = LANES // ML_BLOCK
    wr = w.reshape(ML_INNER // LANES, per, ML_BLOCK, ML_BLOCK)
    eye = jnp.eye(per, dtype=w.dtype)
    return jnp.einsum('cgio,gh->cgiho', wr, eye).reshape(ML_INNER // LANES, LANES, LANES).astype(BF16)


def _ml_conv_kernel(p_ref, c_ref, n_ref, cw_ref, cb_ref, wq_ref, wk_ref, wv_ref,
                    xc_ref, q_ref, k_ref, v_ref, *, n_xchunks):
    i = pl.program_id(0)
    is_x = i < n_xchunks
    j = i % X_CHUNKS
    halo = 16
    cur = c_ref[...].astype(F32)
    prev = p_ref[CHUNK - halo:CHUNK, :].astype(F32)
    nxt = n_ref[0:halo, :].astype(F32)
    prev = jnp.where(jnp.logical_and(is_x, j > 0), prev, jnp.zeros_like(prev))
    nxt = jnp.where(jnp.logical_and(is_x, j < X_CHUNKS - 1), nxt, jnp.zeros_like(nxt))
    xp = jnp.concatenate([prev, cur, nxt], axis=0)
    rows = CHUNK + 2 * halo
    y = cb_ref[...] + cw_ref[ML_CONV_K // 2:ML_CONV_K // 2 + 1, :] * cur
    for tap in range(ML_CONV_K):
        dlt = tap - ML_CONV_K // 2
        if dlt == 0:
            continue
        shifted = pltpu.roll(xp, (-dlt) % rows, 0)[halo:halo + CHUNK]
        y = y + cw_ref[tap:tap + 1, :] * shifted
    xcb = _silu(y).astype(BF16)
    xc_ref[...] = xcb
    xm = c_ref[...]
    for s in range(xcb.shape[1] // LANES):
        sl = slice(s * LANES, (s + 1) * LANES)
        q_ref[:, sl] = jnp.dot(xcb[:, sl], wq_ref[s], preferred_element_type=F32).astype(BF16)
        k_ref[:, sl] = jnp.dot(xcb[:, sl], wk_ref[s], preferred_element_type=F32).astype(BF16)
        v_ref[:, sl] = jnp.dot(xm[:, sl], wv_ref[s], preferred_element_type=F32).astype(BF16)


def ml_conv_qkv(up, conv_w, conv_b, wq, wk, wv, nb):
    n = up.shape[0]
    nchunks = n // CHUNK
    cw = 512
    ncb = ML_INNER // cw
    sub = cw // LANES
    blk = pl.BlockSpec((CHUNK, cw), lambda i, c: (i, c))
    wspec = pl.BlockSpec((sub, LANES, LANES), lambda i, c: (c, 0, 0))
    out = jax.ShapeDtypeStruct((n, ML_INNER), BF16)
    return pl.pallas_call(
        functools.partial(_ml_conv_kernel, n_xchunks=nb * X_CHUNKS),
        grid=(nchunks, ncb),
        in_specs=[pl.BlockSpec((CHUNK, cw), lambda i, c: (jnp.maximum(i - 1, 0), c)),
                  blk,
                  pl.BlockSpec((CHUNK, cw), lambda i, c: (jnp.minimum(i + 1, nchunks - 1), c)),
                  pl.BlockSpec((ML_CONV_K, cw), lambda i, c: (0, c)),
                  pl.BlockSpec((1, cw), lambda i, c: (0, c)),
                  wspec, wspec, wspec],
        out_specs=[blk, blk, blk, blk],
        out_shape=[out, out, out, out],
        compiler_params=_cp(("parallel", "parallel")),
        name="ml_conv_qkv",
    )(up, up, up, conv_w.astype(F32), conv_b.astype(F32).reshape(1, ML_INNER), wq, wk, wv)


def _ml_gate_kernel(q_ref, k_ref, v_ref, w_ref, b_ref, o_ref):
    acc = jnp.dot(q_ref[...], w_ref[0], preferred_element_type=F32)
    acc = acc + jnp.dot(k_ref[...], w_ref[1], preferred_element_type=F32)
    acc = acc + jnp.dot(v_ref[...], w_ref[2], preferred_element_type=F32)
    o_ref[...] = acc + b_ref[...]


def ml_gates(q, k, v, wg, bg):
    n = q.shape[0]
    tm = _tile(n, 512)
    blk = pl.BlockSpec((tm, ML_INNER), lambda i: (i, 0))
    return pl.pallas_call(
        _ml_gate_kernel,
        grid=(n // tm,),
        in_specs=[blk, blk, blk,
                  pl.BlockSpec((3, ML_INNER, LANES), lambda i: (0, 0, 0)),
                  pl.BlockSpec((1, LANES), lambda i: (0, 0))],
        out_specs=pl.BlockSpec((tm, LANES), lambda i: (i, 0)),
        out_shape=jax.ShapeDtypeStruct((n, LANES), F32),
        compiler_params=_cp(("parallel",)),
        name="ml_gates",
    )(q, k, v, wg, bg)


def _log_sigmoid(x):
    return jnp.minimum(x, 0.0) - jnp.log(1.0 + jnp.exp(-jnp.abs(x)))


def _ml_scan_kernel(q_ref, k_ref, v_ref, g_ref, gt_ref, o_ref, c_sc, n_sc, m_sc):
    h = pl.program_id(1)
    d = pl.program_id(2)
    p = pl.program_id(3)
    ninf = -jnp.inf

    @pl.when(p == 0)
    def _():
        c_sc[...] = jnp.zeros_like(c_sc)
        n_sc[...] = jnp.zeros_like(n_sc)
        m_sc[...] = jnp.full_like(m_sc, ninf)

    L = CHUNK
    col_i = d * 2 * ML_HEADS + h
    col_f = col_i + ML_HEADS
    lane = lax.broadcasted_iota(jnp.int32, (1, LANES), 1)
    g = g_ref[...]
    i_col = jnp.sum(jnp.where(lane == col_i, g, 0.0), axis=1, keepdims=True)
    f_col = _log_sigmoid(jnp.sum(jnp.where(lane == col_f, g, 0.0), axis=1, keepdims=True))
    i_row = gt_ref[pl.ds(col_i, 1), :]
    f_row = _log_sigmoid(gt_ref[pl.ds(col_f, 1), :])

    jj = lax.broadcasted_iota(jnp.int32, (L, L), 0)
    ss = lax.broadcasted_iota(jnp.int32, (L, L), 1)
    fwd = d == 0
    bwd = d == 1
    valid = jnp.logical_or(jnp.logical_and(ss <= jj, fwd), jnp.logical_and(ss >= jj, bwd))
    valid_t = jnp.logical_or(jnp.logical_and(jj <= ss, fwd), jnp.logical_and(jj >= ss, bwd))
    bcum_col = jnp.sum(jnp.where(valid, f_row, 0.0), axis=1, keepdims=True)
    bcum_row = jnp.sum(jnp.where(valid_t, f_col, 0.0), axis=0, keepdims=True)
    b_last = jnp.sum(f_row, axis=1, keepdims=True)
    m_prev = m_sc[...]

    logd = jnp.where(valid, bcum_col - bcum_row + i_row, ninf)
    log_inter = bcum_col + m_prev
    m_j = jnp.maximum(log_inter, jnp.max(logd, axis=1, keepdims=True))
    dmat = jnp.exp(logd - m_j)
    inter = jnp.exp(log_inter - m_j)

    q = q_ref[...]
    k = k_ref[...]
    v = v_ref[...]
    scale = ML_HEAD_DIM ** -0.5
    sc = _nt(q, k) * scale * dmat
    c_prev = c_sc[...]
    n_prev = n_sc[...]
    qc = jnp.dot(q, c_prev.astype(BF16), preferred_element_type=F32) * scale
    num = inter * qc + jnp.dot(sc.astype(BF16), v, preferred_element_type=F32)
    qn = jnp.sum(q.astype(F32) * n_prev, axis=1, keepdims=True) * scale
    den = inter * qn + jnp.sum(sc, axis=1, keepdims=True)
    o_ref[0] = num / jnp.maximum(jnp.abs(den), jnp.exp(-m_j))

    ls = b_last - bcum_col + i_col
    m_new = jnp.maximum(b_last + m_prev, jnp.max(ls, axis=0, keepdims=True))
    w = jnp.exp(ls - m_new)
    decay = jnp.exp(b_last + m_prev - m_new)
    kw = k.astype(F32) * w
    c_sc[...] = decay * c_prev + lax.dot_general(kw.astype(BF16), v, (((0,), (0,)), ((), ())),
                                                 preferred_element_type=F32)
    n_sc[...] = decay * n_prev + jnp.sum(kw, axis=0, keepdims=True)
    m_sc[...] = m_new


def ml_scan(q, k, v, g, gt, nb):
    n = q.shape[0]
    xc = nb * X_CHUNKS

    def cidx(b, d, p):
        xi = jnp.where(d == 0, p - 1, X_CHUNKS - p)
        return jnp.where(p == 0, xc + b, b * X_CHUNKS + xi)

    blk = pl.BlockSpec((CHUNK, ML_HEAD_DIM), lambda b, h, d, p: (cidx(b, d, p), h))
    return pl.pallas_call(
        _ml_scan_kernel,
        grid=(nb, ML_HEADS, 2, X_CHUNKS + 1),
        in_specs=[blk, blk, blk,
                  pl.BlockSpec((CHUNK, LANES), lambda b, h, d, p: (cidx(b, d, p), 0)),
                  pl.BlockSpec((16, CHUNK), lambda b, h, d, p: (0, cidx(b, d, p)))],
        out_specs=pl.BlockSpec((1, CHUNK, ML_HEAD_DIM), lambda b, h, d, p: (d, cidx(b, d, p), h)),
        out_shape=jax.ShapeDtypeStruct((2, n, ML_INNER), F32),
        scratch_shapes=[pltpu.VMEM((ML_HEAD_DIM, ML_HEAD_DIM), F32),
                        pltpu.VMEM((1, ML_HEAD_DIM), F32),
                        pltpu.VMEM((1, 1), F32)],
        compiler_params=_cp(("parallel", "parallel", "parallel", "arbitrary")),
        name="ml_scan",
    )(q, k, v, g, gt)


def _ml_finish_kernel(h_ref, xc_ref, z_ref, gn_ref, sk_ref, o_ref):
    hh = h_ref[0] + h_ref[1]
    z = z_ref[...].astype(F32)
    gate = _silu(z)
    for hd in range(ML_HEADS):
        sl = slice(hd * ML_HEAD_DIM, (hd + 1) * ML_HEAD_DIM)
        seg = hh[:, sl]
        mu = jnp.mean(seg, axis=-1, keepdims=True)
        cen = seg - mu
        var = jnp.mean(cen * cen, axis=-1, keepdims=True)
        hn = cen * lax.rsqrt(var + EPS) * gn_ref[:, sl]
        a = (hn + sk_ref[:, sl] * xc_ref[:, sl].astype(F32)) * gate[:, sl]
        o_ref[:, sl] = a.astype(o_ref.dtype)


def ml_finish(hs, xc, up, gn_w, skip):
    n = xc.shape[0]
    tm = CHUNK
    vec = pl.BlockSpec((1, ML_INNER), lambda i: (0, 0))
    return pl.pallas_call(
        _ml_finish_kernel,
        grid=(n // tm,),
        in_specs=[pl.BlockSpec((2, tm, ML_INNER), lambda i: (0, i, 0)),
                  pl.BlockSpec((tm, ML_INNER), lambda i: (i, 0)),
                  pl.BlockSpec((tm, ML_INNER), lambda i: (i, 1)),
                  vec, vec],
        out_specs=pl.BlockSpec((tm, ML_INNER), lambda i: (i, 0)),
        out_shape=jax.ShapeDtypeStruct((n, ML_INNER), BF16),
        compiler_params=_cp(("parallel",)),
        name="ml_finish",
    )(hs, xc, up, gn_w.astype(F32).reshape(1, ML_INNER), skip.astype(F32).reshape(1, ML_INNER))


def _router_kernel(x_ref, g_ref, sh_ref, sc_ref, w_ref, b_ref, h_ref, o_ref):
    x = x_ref[...]
    y = x * lax.rsqrt(jnp.mean(x * x, axis=-1, keepdims=True) + EPS) * g_ref[...]
    hf = y * (1.0 + sc_ref[0, 0]) + sh_ref[0, 0]
    h_hi = hf.astype(BF16)
    h_ref[...] = hf
    h_lo = (hf - h_hi.astype(F32)).astype(BF16)
    logits = (jnp.dot(h_hi, w_ref[0], preferred_element_type=F32)
              + jnp.dot(h_hi, w_ref[1], preferred_element_type=F32)
              + jnp.dot(h_lo, w_ref[0], preferred_element_type=F32)) + b_ref[...]
    lane = lax.broadcasted_iota(jnp.int32, (1, LANES), 1).astype(F32)
    big = 1e9
    ninf = -jnp.inf
    is_g = jnp.logical_and(lane >= MOE_E, lane < MOE_E + MOE_GROUPS)
    gl = jnp.where(is_g, logits, ninf)
    gmax = jnp.max(gl, axis=1, keepdims=True)
    g_val = 1.0 / jnp.sum(jnp.exp(gl - gmax), axis=1, keepdims=True)
    g_idx = jnp.min(jnp.where(gl == gmax, lane, big), axis=1, keepdims=True) - MOE_E
    lo = g_idx * MOE_EPG
    sel = jnp.logical_and(lane >= lo, lane < lo + MOE_EPG)
    el = jnp.where(sel, logits, ninf)
    e1 = jnp.max(el, axis=1, keepdims=True)
    esum = jnp.sum(jnp.exp(el - e1), axis=1, keepdims=True)
    i1 = jnp.min(jnp.where(el == e1, lane, big), axis=1, keepdims=True)
    el2 = jnp.where(lane == i1, ninf, el)
    e2 = jnp.max(el2, axis=1, keepdims=True)
    i2 = jnp.min(jnp.where(el2 == e2, lane, big), axis=1, keepdims=True)
    p1 = 1.0 / esum
    p2 = jnp.exp(e2 - e1) / esum
    w1 = g_val * p1 / (p1 + p2)
    w2 = g_val * p2 / (p1 + p2)
    gates = jnp.where(lane == i1, w1, 0.0) + jnp.where(lane == i2, w2, 0.0)
    marks = jnp.where(jnp.logical_or(lane == i1 + SEL_LANE, lane == i2 + SEL_LANE), 1.0, 0.0)
    o_ref[...] = gates + marks


def moe_router(s, g, mod, k_shift, k_scale, wr, br, nb):
    n, d = s.shape
    tm = _tile(n, 512)
    return pl.pallas_call(
        _router_kernel,
        grid=(n // tm,),
        in_specs=[pl.BlockSpec((tm, d), lambda i: (i, 0)),
                  pl.BlockSpec((1, d), lambda i: (0, 0)),
                  pl.BlockSpec((1, 1, 1, d), lambda i: (_mod_row(i, tm, nb), k_shift, 0, 0)),
                  pl.BlockSpec((1, 1, 1, d), lambda i: (_mod_row(i, tm, nb), k_scale, 0, 0)),
                  pl.BlockSpec((2, d, LANES), lambda i: (0, 0, 0)),
                  pl.BlockSpec((1, LANES), lambda i: (0, 0))],
        out_specs=[pl.BlockSpec((tm, d), lambda i: (i, 0)),
                   pl.BlockSpec((tm, LANES), lambda i: (i, 0))],
        out_shape=[jax.ShapeDtypeStruct((n, d), F32),
                   jax.ShapeDtypeStruct((n, LANES), F32)],
        compiler_params=_cp(("parallel",)),
        name="moe_router",
    )(s, g.reshape(1, d), mod, mod, wr, br)


def _rank_kernel(route_ref, rank_ref, cnt_ref, carry_ref):
    i = pl.program_id(0)

    @pl.when(i == 0)
    def _():
        carry_ref[...] = jnp.zeros_like(carry_ref)

    tm = route_ref.shape[0]
    lane = lax.broadcasted_iota(jnp.int32, (1, LANES), 1)
    marks = jnp.where(lane >= SEL_LANE, route_ref[...], 0.0)
    rr = lax.broadcasted_iota(jnp.int32, (tm, tm), 0)
    cc = lax.broadcasted_iota(jnp.int32, (tm, tm), 1)
    below = jnp.where(cc < rr, 1.0, 0.0).astype(BF16)
    rank_ref[...] = jnp.dot(below, marks.astype(BF16), preferred_element_type=F32) + carry_ref[...]
    carry_ref[...] += jnp.sum(marks, axis=0, keepdims=True)
    cnt_ref[...] = carry_ref[...]


def moe_rank(route):
    n = route.shape[0]
    tm = _tile(n, 512)
    return pl.pallas_call(
        _rank_kernel,
        grid=(n // tm,),
        in_specs=[pl.BlockSpec((tm, LANES), lambda i: (i, 0))],
        out_specs=[pl.BlockSpec((tm, LANES), lambda i: (i, 0)),
                   pl.BlockSpec((1, LANES), lambda i: (0, 0))],
        out_shape=[jax.ShapeDtypeStruct((n, LANES), F32),
                   jax.ShapeDtypeStruct((1, LANES), F32)],
        scratch_shapes=[pltpu.VMEM((1, LANES), F32)],
        compiler_params=_cp(("arbitrary",)),
        name="moe_rank",
    )(route)


def _pos_kernel(route_ref, rank_ref, off_ref, o_ref):
    lane = lax.broadcasted_iota(jnp.int32, (1, LANES), 1).astype(F32)
    route = route_ref[...]
    marked = jnp.logical_and(lane >= SEL_LANE, route > 0.5)
    p = rank_ref[...] + off_ref[...]
    lane_a = jnp.min(jnp.where(marked, lane, 1e9), axis=1, keepdims=True)
    lane_b = jnp.max(jnp.where(marked, lane, -1.0), axis=1, keepdims=True)

    def pick(src, at):
        return jnp.sum(jnp.where(lane == at, src, 0.0), axis=1, keepdims=True)

    out = jnp.where(lane == 0.0, pick(p, lane_a), 0.0)
    out = out + jnp.where(lane == 1.0, pick(p, lane_b), 0.0)
    out = out + jnp.where(lane == 2.0, pick(route, lane_a - SEL_LANE), 0.0)
    out = out + jnp.where(lane == 3.0, pick(route, lane_b - SEL_LANE), 0.0)
    o_ref[...] = out


def moe_positions(route, rank, off):
    n = route.shape[0]
    tm = _tile(n, 512)
    blk = pl.BlockSpec((tm, LANES), lambda i: (i, 0))
    return pl.pallas_call(
        _pos_kernel,
        grid=(n // tm,),
        in_specs=[blk, blk, pl.BlockSpec((1, LANES), lambda i: (0, 0))],
        out_specs=blk,
        out_shape=jax.ShapeDtypeStruct((n, LANES), F32),
        compiler_params=_cp(("parallel",)),
        name="moe_positions",
    )(route, rank, off)


def _row_copy(src, src_row, dst, dst_row, sem):
    return pltpu.make_async_copy(src.at[pl.ds(src_row, 1)], dst.at[pl.ds(dst_row, 1)], sem)


def _dispatch_kernel(pos_ref, h_hbm, xs_in_hbm, xs_hbm, sem):
    del xs_in_hbm
    base = pl.program_id(0) * CHUNK

    def issue(r, carry):
        _row_copy(h_hbm, base + r, xs_hbm, pos_ref[0, 0, r], sem).start()
        _row_copy(h_hbm, base + r, xs_hbm, pos_ref[0, 0, CHUNK + r], sem).start()
        return carry

    lax.fori_loop(0, CHUNK, issue, 0)

    def drain(r, carry):
        _row_copy(h_hbm, 0, xs_hbm, 0, sem).wait()
        return carry

    lax.fori_loop(0, 2 * CHUNK, drain, 0)


def moe_dispatch(pos, h, n_sorted):
    n, d = h.shape
    xs0 = jnp.zeros((n_sorted, d), h.dtype)
    return pl.pallas_call(
        _dispatch_kernel,
        grid=(n // CHUNK,),
        in_specs=[pl.BlockSpec((1, 1, 2 * CHUNK), lambda i: (i, 0, 0), memory_space=pltpu.SMEM),
                  pl.BlockSpec(memory_space=pl.ANY),
                  pl.BlockSpec(memory_space=pl.ANY)],
        out_specs=pl.BlockSpec(memory_space=pl.ANY),
        out_shape=jax.ShapeDtypeStruct((n_sorted, d), h.dtype),
        scratch_shapes=[pltpu.SemaphoreType.DMA],
        input_output_aliases={2: 0},
        compiler_params=_cp(("arbitrary",)),
        name="moe_dispatch",
    )(pos, h, xs0)


def _experts_kernel(te_ref, nt_ref, x_ref, w1_ref, w3_ref, w2_ref, o_ref):
    del te_ref
    active = pl.program_id(0) < nt_ref[0]

    @pl.when(active)
    def _():
        x = x_ref[...].astype(BF16)
        a = jnp.dot(x, w1_ref[0], preferred_element_type=F32)
        b = jnp.dot(x, w3_ref[0], preferred_element_type=F32)
        he = (_silu(a) * b).astype(BF16)
        o_ref[...] = jnp.dot(he, w2_ref[0], preferred_element_type=F32)

    @pl.when(jnp.logical_not(active))
    def _():
        o_ref[...] = jnp.zeros_like(o_ref)


def moe_experts(tile_expert, n_tiles, xs, w1, w3, w2):
    n_sorted, d = xs.shape

    def row(i, te, nt):
        return (jnp.minimum(i, nt[0] - 1), 0)

    return pl.pallas_call(
        _experts_kernel,
        grid_spec=pltpu.PrefetchScalarGridSpec(
            num_scalar_prefetch=2,
            grid=(n_sorted // MOE_TM,),
            in_specs=[pl.BlockSpec((MOE_TM, d), row),
                      pl.BlockSpec((1, d, MOE_HIDDEN), lambda i, te, nt: (te[i], 0, 0)),
                      pl.BlockSpec((1, d, MOE_HIDDEN), lambda i, te, nt: (te[i], 0, 0)),
                      pl.BlockSpec((1, MOE_HIDDEN, d), lambda i, te, nt: (te[i], 0, 0))],
            out_specs=pl.BlockSpec((MOE_TM, d), lambda i, te, nt: (i, 0))),
        out_shape=jax.ShapeDtypeStruct((n_sorted, d), F32),
        compiler_params=_cp(("arbitrary",)),
        name="moe_experts",
    )(tile_expert, n_tiles, xs, w1, w3, w2)


def _combine_kernel(pos_ref, meta_ref, s_ref, g_ref, ys_hbm, o_ref, buf_ref, sem):
    def issue(r, carry):
        _row_copy(ys_hbm, pos_ref[0, 0, r], buf_ref.at[0], r, sem).start()
        _row_copy(ys_hbm, pos_ref[0, 0, CHUNK + r], buf_ref.at[1], r, sem).start()
        return carry

    lax.fori_loop(0, CHUNK, issue, 0)

    def drain(r, carry):
        _row_copy(ys_hbm, 0, buf_ref.at[0], 0, sem).wait()
        return carry

    lax.fori_loop(0, 2 * CHUNK, drain, 0)
    meta = meta_ref[...]
    y = meta[:, 2:3] * buf_ref[0] + meta[:, 3:4] * buf_ref[1]
    o_ref[...] = s_ref[...] + g_ref[0, 0] * y


def moe_combine(pos, meta, s, mod, k_gate, ys, nb):
    n, d = s.shape
    tm = CHUNK
    return pl.pallas_call(
        _combine_kernel,
        grid=(n // tm,),
        in_specs=[pl.BlockSpec((1, 1, 2 * CHUNK), lambda i: (i, 0, 0), memory_space=pltpu.SMEM),
                  pl.BlockSpec((tm, LANES), lambda i: (i, 0)),
                  pl.BlockSpec((tm, d), lambda i: (i, 0)),
                  pl.BlockSpec((1, 1, 1, d), lambda i: (_mod_row(i, tm, nb), k_gate, 0, 0)),
                  pl.BlockSpec(memory_space=pl.ANY)],
        out_specs=pl.BlockSpec((tm, d), lambda i: (i, 0)),
        out_shape=jax.ShapeDtypeStruct((n, d), F32),
        scratch_shapes=[pltpu.VMEM((2, tm, d), F32), pltpu.SemaphoreType.DMA],
        input_output_aliases={2: 0},
        compiler_params=_cp(("arbitrary",)),
        name="moe_combine",
    )(pos, meta, s, mod, ys)


def _na_layer(h, w_qkv, w_o, rpb, s, mod, nb):
    qkv = matmul(h, w_qkv.astype(BF16), BF16)
    o = na_attention(qkv, _na_bias_tables(rpb), nb)
    return matmul_residual(o, w_o.astype(BF16), s, mod, 2, nb)


def _da_layer(h, w_qkv, lam, subln_g, w_o, lambda_init, s, mod, nb):
    qkv = matmul(h, w_qkv.astype(BF16), F32)
    qkv = rope_cast(qkv, _rope_tables(nb), nb)
    o = da_attention(qkv, lam, subln_g, lambda_init, nb)
    return matmul_residual(o, w_o.astype(BF16), s, mod, 2, nb)


def _ml_layer(h, w_up, conv_w, conv_b, w_q, w_k, w_v, w_gate, b_gate, gn_w, skip, w_down, s, mod, nb):
    up = matmul(h, w_up.astype(BF16), BF16)
    xc, q, k, v = ml_conv_qkv(up, conv_w, conv_b, _block_diag(w_q), _block_diag(w_k), _block_diag(w_v), nb)
    ng = w_gate.shape[1]
    wg = jnp.pad(w_gate, ((0, 0), (0, LANES - ng))).reshape(3, ML_INNER, LANES).astype(BF16)
    bg = jnp.pad(b_gate.astype(F32), (0, LANES - ng)).reshape(1, LANES)
    g = ml_gates(q, k, v, wg, bg)
    gt = g[:, :ng].T
    hs = ml_scan(q, k, v, g, gt, nb)
    a = ml_finish(hs, xc, up, gn_w, skip)
    return matmul_residual(a, w_down.astype(BF16), s, mod, 2, nb)


def _moe_layer(norm_g, w_group, b_group, w_router, b_router, w1, w3, w2, s, mod, nb):
    d = s.shape[1]
    pad = LANES - MOE_E - MOE_GROUPS
    wr = jnp.concatenate([w_router, w_group, jnp.zeros((d, pad), w_router.dtype)], axis=1).astype(F32)
    wr_hi = wr.astype(BF16)
    wr_lo = (wr - wr_hi.astype(F32)).astype(BF16)
    br = jnp.concatenate([b_router, b_group, jnp.zeros((pad,), b_router.dtype)]).astype(F32).reshape(1, LANES)
    h, route = moe_router(s, norm_g.astype(F32), mod, 3, 4, jnp.stack([wr_hi, wr_lo]), br, nb)
    rank, cnt = moe_rank(route)
    n = s.shape[0]
    cnt_e = cnt[0, SEL_LANE:SEL_LANE + MOE_E].astype(jnp.int32)
    gsz = ((cnt_e + MOE_TM - 1) // MOE_TM) * MOE_TM
    ends = jnp.cumsum(gsz)
    off_row = jnp.zeros((1, LANES), F32).at[0, SEL_LANE:SEL_LANE + MOE_E].set((ends - gsz).astype(F32))
    meta = moe_positions(route, rank, off_row)
    n_sorted = 2 * n + MOE_E * MOE_TM
    n_tiles = (ends[-1:] // MOE_TM).astype(jnp.int32)
    tile_expert = jnp.searchsorted(ends // MOE_TM, jnp.arange(n_sorted // MOE_TM), side='right')
    tile_expert = jnp.minimum(tile_expert, MOE_E - 1).astype(jnp.int32)
    pos = meta[:, :2].astype(jnp.int32).reshape(n // CHUNK, CHUNK, 2)
    pos = pos.transpose(0, 2, 1).reshape(n // CHUNK, 1, 2 * CHUNK)
    xs = moe_dispatch(pos, h, n_sorted)
    ys = moe_experts(tile_expert, n_tiles, xs, w1.astype(BF16), w3.astype(BF16), w2.astype(BF16))
    return moe_combine(pos, meta, s, mod, 5, ys, nb)


def kernel(x, c, ctx, c_ctx, mod_w, mod_b, norm_g, final_g, na_w_qkv, na_w_o, na_rpb, ml_w_up, ml_conv_w, ml_conv_b, ml_w_q, ml_w_k, ml_w_v, ml_w_gate, ml_b_gate, ml_gn_w, ml_skip, ml_w_down, da_w_qkv, da_lambda, da_subln_g, da_w_o, moe_w_group, moe_b_group, moe_w_router, moe_b_router, moe_w1, moe_w3, moe_w2):
    nb, seq, d = x.shape
    assert (seq, d, ctx.shape[1]) == (SEQ, D_MODEL, CTX_LEN) and nb < 16
    depth = mod_w.shape[0]
    nx = nb * seq
    s = jnp.concatenate([x.reshape(nx, d), ctx.reshape(nb * CTX_LEN, d)], axis=0).astype(F32)
    cvec = jnp.concatenate([c, c_ctx[None, :], jnp.zeros((16 - nb - 1, d), c.dtype)], axis=0).astype(F32)
    mods = mod_vectors(cvec, mod_w, mod_b).reshape(depth, 16, 6, 1, d)

    for i in range(depth):
        kind, j = i % N_MIXERS, i // N_MIXERS
        mod = mods[i]
        h = norm_mod(s, norm_g[i, 0].astype(F32), mod, 0, 1, nb)
        if kind == 0:
            s = _na_layer(h, na_w_qkv[j], na_w_o[j], na_rpb[j], s, mod, nb)
        elif kind == 1:
            s = _ml_layer(h, ml_w_up[j], ml_conv_w[j], ml_conv_b[j], ml_w_q[j], ml_w_k[j], ml_w_v[j],
                          ml_w_gate[j], ml_b_gate[j], ml_gn_w[j], ml_skip[j], ml_w_down[j], s, mod, nb)
        else:
            lambda_init = 0.8 - 0.6 * math.exp(-0.3 * i)
            s = _da_layer(h, da_w_qkv[j], da_lambda[j], da_subln_g[j], da_w_o[j], lambda_init, s, mod, nb)
        s = _moe_layer(norm_g[i, 1], moe_w_group[i], moe_b_group[i], moe_w_router[i], moe_b_router[i],
                       moe_w1[i], moe_w3[i], moe_w2[i], s, mod, nb)
    return final_norm(s, final_g.astype(F32), nx).reshape(nb, seq, d)
```

```python
import functools
import math

import numpy as np
import jax
import jax.numpy as jnp
from jax import lax
from jax.experimental import pallas as pl
from jax.experimental.pallas import tpu as pltpu

F32 = jnp.float32
BF16 = jnp.bfloat16

D_MODEL = 1024
SEQ = 4096
CTX_LEN = 256
GRID_W = 64
N_MIXERS = 3
EPS = 1e-6

NA_HEADS = 16
NA_WIN_H = 8
NA_WIN_W = 16

ML_HEADS = 4
ML_INNER = 2 * D_MODEL
ML_HEAD_DIM = ML_INNER // ML_HEADS
ML_BLOCK = 4
ML_CONV_K = 5

DA_HEADS = 8
DA_HEAD_DIM = 64
ROPE_BASE = 10000.0

MOE_GROUPS = 4
MOE_EPG = 8
MOE_E = MOE_GROUPS * MOE_EPG
MOE_HIDDEN = 512

LANES = 128
CHUNK = 256
X_CHUNKS = SEQ // CHUNK
PROJ_TM = 512
NEG = -1e30
VMEM_LIMIT = 56 * 1024 * 1024


def _cp(sem, vmem=VMEM_LIMIT):
    return pltpu.CompilerParams(dimension_semantics=sem, vmem_limit_bytes=vmem)


def _tile(n, pref):
    tm = pref
    while n % tm:
        tm //= 2
    assert tm >= CHUNK
    return tm


def _mod_row(i, tm, nb):
    return jnp.where(i < nb * (SEQ // tm), i // (SEQ // tm), nb)


def _nt(a, b):
    return lax.dot_general(a, b, (((1,), (1,)), ((), ())), preferred_element_type=F32)


def _tn(a, b):
    return lax.dot_general(a, b, (((0,), (0,)), ((), ())), preferred_element_type=F32)


def _silu(x):
    return x * jax.nn.sigmoid(x)


def _head_rows(q, lane):
    zero = jnp.zeros_like(q)
    return jnp.concatenate([jnp.where(lane < 64, q, zero), jnp.where(lane >= 64, q, zero)], axis=0)


def _mod_kernel(c_ref, w_ref, b_ref, o_ref):
    a = _silu(c_ref[...]).astype(BF16)
    o_ref[0] = jnp.dot(a, w_ref[0].astype(BF16), preferred_element_type=F32) + b_ref[0]


def mod_vectors(cvec, mod_w, mod_b):
    depth, d, n6 = mod_w.shape
    tn = 1024
    return pl.pallas_call(
        _mod_kernel,
        grid=(depth, n6 // tn),
        in_specs=[pl.BlockSpec((16, d), lambda l, j: (0, 0)),
                  pl.BlockSpec((1, d, tn), lambda l, j: (l, 0, j)),
                  pl.BlockSpec((1, 1, tn), lambda l, j: (l, 0, j))],
        out_specs=pl.BlockSpec((1, 16, tn), lambda l, j: (l, 0, j)),
        out_shape=jax.ShapeDtypeStruct((depth, 16, n6), F32),
        compiler_params=_cp(("parallel", "parallel")),
        name="mod_vectors",
    )(cvec, mod_w, mod_b.reshape(depth, 1, n6))


def _norm_mod_kernel(x_ref, g_ref, sh_ref, sc_ref, o_ref):
    x = x_ref[...]
    y = x * lax.rsqrt(jnp.mean(x * x, axis=-1, keepdims=True) + EPS) * g_ref[...]
    o_ref[...] = (y * (1.0 + sc_ref[0, 0]) + sh_ref[0, 0]).astype(o_ref.dtype)


def norm_mod(s, g, mod, k_shift, k_scale, nb):
    n, d = s.shape
    tm = _tile(n, PROJ_TM)
    return pl.pallas_call(
        _norm_mod_kernel,
        grid=(n // tm,),
        in_specs=[pl.BlockSpec((tm, d), lambda i: (i, 0)),
                  pl.BlockSpec((1, d), lambda i: (0, 0)),
                  pl.BlockSpec((1, 1, 1, d), lambda i: (_mod_row(i, tm, nb), k_shift, 0, 0)),
                  pl.BlockSpec((1, 1, 1, d), lambda i: (_mod_row(i, tm, nb), k_scale, 0, 0))],
        out_specs=pl.BlockSpec((tm, d), lambda i: (i, 0)),
        out_shape=jax.ShapeDtypeStruct((n, d), BF16),
        compiler_params=_cp(("parallel",)),
        name="norm_mod",
    )(s, g.reshape(1, d), mod, mod)


def _final_norm_kernel(x_ref, g_ref, o_ref):
    x = x_ref[...]
    o_ref[...] = x * lax.rsqrt(jnp.mean(x * x, axis=-1, keepdims=True) + EPS) * g_ref[...]


def final_norm(s, g, n_rows):
    d = s.shape[1]
    tm = _tile(n_rows, PROJ_TM)
    return pl.pallas_call(
        _final_norm_kernel,
        grid=(n_rows // tm,),
        in_specs=[pl.BlockSpec((tm, d), lambda i: (i, 0)),
                  pl.BlockSpec((1, d), lambda i: (0, 0))],
        out_specs=pl.BlockSpec((tm, d), lambda i: (i, 0)),
        out_shape=jax.ShapeDtypeStruct((n_rows, d), F32),
        compiler_params=_cp(("parallel",)),
        name="final_norm",
    )(s, g.reshape(1, d))


def _mm_kernel(a_ref, w_ref, o_ref):
    o_ref[...] = jnp.dot(a_ref[...], w_ref[...], preferred_element_type=F32).astype(o_ref.dtype)


def matmul(a, w, out_dtype, tn=1024):
    n, k = a.shape
    tm = _tile(n, PROJ_TM)
    nout = w.shape[1]
    tn = min(tn, nout)
    return pl.pallas_call(
        _mm_kernel,
        grid=(nout // tn, n // tm),
        in_specs=[pl.BlockSpec((tm, k), lambda j, i: (i, 0)),
                  pl.BlockSpec((k, tn), lambda j, i: (0, j))],
        out_specs=pl.BlockSpec((tm, tn), lambda j, i: (i, j)),
        out_shape=jax.ShapeDtypeStruct((n, nout), out_dtype),
        compiler_params=_cp(("parallel", "parallel")),
        name="matmul",
    )(a, w)


def _mm_res_kernel(a_ref, w_ref, r_ref, g_ref, o_ref):
    acc = jnp.dot(a_ref[...], w_ref[...], preferred_element_type=F32)
    o_ref[...] = r_ref[...] + g_ref[0, 0] * acc


def matmul_residual(a, w, s, mod, k_gate, nb):
    n, k = a.shape
    d = w.shape[1]
    tm = _tile(n, PROJ_TM)
    return pl.pallas_call(
        _mm_res_kernel,
        grid=(n // tm,),
        in_specs=[pl.BlockSpec((tm, k), lambda i: (i, 0)),
                  pl.BlockSpec((k, d), lambda i: (0, 0)),
                  pl.BlockSpec((tm, d), lambda i: (i, 0)),
                  pl.BlockSpec((1, 1, 1, d), lambda i: (_mod_row(i, tm, nb), k_gate, 0, 0))],
        out_specs=pl.BlockSpec((tm, d), lambda i: (i, 0)),
        out_shape=jax.ShapeDtypeStruct((n, d), F32),
        input_output_aliases={2: 0},
        compiler_params=_cp(("parallel",)),
        name="matmul_residual",
    )(a, w, s, mod)


def _na_bias_tables(rpb):
    o = np.arange(NA_WIN_H)[:, None]
    j = np.arange(NA_WIN_H)[None, :]
    dy = j - o + NA_WIN_H - 1
    qc = np.arange(GRID_W)[:, None]
    kc = np.arange(GRID_W)[None, :]
    dx = np.clip(kc - qc, -(NA_WIN_W - 1), NA_WIN_W - 1) + NA_WIN_W - 1
    w_start = np.clip(qc - NA_WIN_W // 2, 0, GRID_W - NA_WIN_W)
    valid = (kc >= w_start) & (kc < w_start + NA_WIN_W)
    tbl = rpb.astype(F32)[:, dy][:, :, :, dx]
    tbl = jnp.where(jnp.asarray(valid)[None, None, None], tbl, NEG)
    tbl = tbl.reshape(NA_HEADS // 2, 2, NA_WIN_H, NA_WIN_H, GRID_W, GRID_W)
    tbl = tbl.transpose(2, 0, 3, 5, 1, 4)
    return tbl.reshape(NA_WIN_H, NA_HEADS // 2, NA_WIN_H * GRID_W, 2 * GRID_W)


def _na_kernel(q_ref, kx_ref, vx_ref, kc_ref, vc_ref, bias_ref, o_ref, s_sc, p_sc):
    t = pl.program_id(2)
    lane = lax.broadcasted_iota(jnp.int32, (1, LANES), 1)
    nwin = NA_WIN_H * GRID_W

    def softmax_cols(s):
        m = jnp.max(s, axis=0, keepdims=True)
        p = jnp.exp(s - m)
        return (p * (1.0 / jnp.sum(p, axis=0, keepdims=True))).astype(BF16)

    @pl.when(t < X_CHUNKS)
    def _():
        rows = CHUNK // GRID_W
        starts = []
        for i in range(rows):
            r = t * rows + i
            rs = jnp.clip(r - NA_WIN_H // 2, 0, SEQ // GRID_W - NA_WIN_H)
            start = pl.multiple_of(rs * GRID_W, GRID_W)
            starts.append(start)
            qq = _head_rows(q_ref[i * GRID_W:(i + 1) * GRID_W, :] * 0.125, lane)
            s_sc[i, 0:nwin, :] = _nt(kx_ref[pl.ds(start, nwin), :], qq) + bias_ref[r - rs, 0]
            s_sc[i, nwin:, :] = _nt(kc_ref[...], qq)
        for i in range(rows):
            p_sc[i] = softmax_cols(s_sc[i])
        for i in range(rows):
            o = (_tn(p_sc[i, 0:nwin, :], vx_ref[pl.ds(starts[i], nwin), :])
                 + _tn(p_sc[i, nwin:, :], vc_ref[...]))
            o = jnp.where(lane < 64, o[0:GRID_W], o[GRID_W:])
            o_ref[i * GRID_W:(i + 1) * GRID_W, :] = o.astype(o_ref.dtype)

    @pl.when(t == X_CHUNKS)
    def _():
        q = q_ref[...] * 0.125
        outs = []
        for hh in range(2):
            qm = jnp.where((lane // 64) == hh, q, jnp.zeros_like(q))
            p = softmax_cols(_nt(kc_ref[...], qm))
            outs.append(_tn(p, vc_ref[...]))
        o_ref[...] = jnp.where(lane < 64, outs[0], outs[1]).astype(o_ref.dtype)


def na_attention(qkv, bias, nb):
    n = qkv.shape[0]
    d = D_MODEL
    ncb = d // LANES
    xc = nb * X_CHUNKS
    nkeys = NA_WIN_H * GRID_W + CTX_LEN

    def qtile(b, t):
        return jnp.where(t < X_CHUNKS, b * X_CHUNKS + t, xc + b)

    return pl.pallas_call(
        _na_kernel,
        grid=(nb, ncb, X_CHUNKS + 1),
        in_specs=[pl.BlockSpec((CHUNK, LANES), lambda b, h, t: (qtile(b, t), h)),
                  pl.BlockSpec((SEQ, LANES), lambda b, h, t: (b, ncb + h)),
                  pl.BlockSpec((SEQ, LANES), lambda b, h, t: (b, 2 * ncb + h)),
                  pl.BlockSpec((CHUNK, LANES), lambda b, h, t: (xc + b, ncb + h)),
                  pl.BlockSpec((CHUNK, LANES), lambda b, h, t: (xc + b, 2 * ncb + h)),
                  pl.BlockSpec((NA_WIN_H, 1, NA_WIN_H * GRID_W, LANES), lambda b, h, t: (0, h, 0, 0))],
        out_specs=pl.BlockSpec((CHUNK, LANES), lambda b, h, t: (qtile(b, t), h)),
        out_shape=jax.ShapeDtypeStruct((n, d), BF16),
        scratch_shapes=[pltpu.VMEM((CHUNK // GRID_W, nkeys, LANES), F32),
                        pltpu.VMEM((CHUNK // GRID_W, nkeys, LANES), BF16)],
        compiler_params=_cp(("parallel", "parallel", "arbitrary")),
        name="na_attention",
    )(qkv, qkv, qkv, qkv, qkv, bias)


def _rope_tables(tm):
    half = DA_HEAD_DIM // 2
    freqs = ROPE_BASE ** (-np.arange(0, half, 2, dtype=np.float32) / half)
    t = np.arange(SEQ)
    row, col = t // GRID_W, t % GRID_W
    lane = np.arange(LANES)
    l64 = lane % DA_HEAD_DIM
    use_col = (l64 // half) == 1
    l32 = l64 % half
    fi = l32 % (half // 2)
    second = l32 >= half // 2
    pos = np.where(use_col[None, :], col[:, None], row[:, None]).astype(np.float32)
    ang = pos * freqs[fi][None, :]
    cos = np.cos(ang).astype(np.float32)
    sin = np.sin(ang).astype(np.float32)
    sa = np.where(second[None, :], 0.0, -sin).astype(np.float32)
    sb = np.where(second[None, :], sin, 0.0).astype(np.float32)
    ident = np.ones((tm, LANES), np.float32)
    zero = np.zeros((tm, LANES), np.float32)
    return (jnp.asarray(np.concatenate([cos, ident])),
            jnp.asarray(np.concatenate([sa, zero])),
            jnp.asarray(np.concatenate([sb, zero])))


def _mm_rope_kernel(a_ref, w_ref, c_ref, sa_ref, sb_ref, o_ref):
    j = pl.program_id(0)
    acc = jnp.dot(a_ref[...], w_ref[...], preferred_element_type=F32)

    @pl.when(j < 2)
    def _():
        scale = jnp.where(j == 0, DA_HEAD_DIM ** -0.5 * math.log2(math.e), 1.0)
        cos = c_ref[...] * scale
        sa = sa_ref[...] * scale
        sb = sb_ref[...] * scale
        for g in range(acc.shape[1] // LANES):
            sl = slice(g * LANES, (g + 1) * LANES)
            x = acc[:, sl]
            y = x * cos + pltpu.roll(x, LANES - 16, 1) * sa + pltpu.roll(x, 16, 1) * sb
            o_ref[:, sl] = y.astype(o_ref.dtype)

    @pl.when(j >= 2)
    def _():
        o_ref[...] = acc.astype(o_ref.dtype)


def matmul_rope(a, w, nb):
    n, k = a.shape
    d = D_MODEL
    tm = _tile(n, PROJ_TM)
    cos, sa, sb = _rope_tables(tm)
    xt = nb * (SEQ // tm)

    def tab(j, i):
        return (jnp.where(i < xt, i % (SEQ // tm), SEQ // tm), 0)

    tspec = pl.BlockSpec((tm, LANES), tab)
    return pl.pallas_call(
        _mm_rope_kernel,
        grid=(3, n // tm),
        in_specs=[pl.BlockSpec((tm, k), lambda j, i: (i, 0)),
                  pl.BlockSpec((k, d), lambda j, i: (0, j)),
                  tspec, tspec, tspec],
        out_specs=pl.BlockSpec((tm, d), lambda j, i: (i, j)),
        out_shape=jax.ShapeDtypeStruct((n, 3 * d), BF16),
        compiler_params=_cp(("parallel", "parallel")),
        name="matmul_rope",
    )(a, w, cos, sa, sb)


def _da_kernel(q_ref, kx_ref, kc_ref, vtx_ref, vtc_ref, lam_ref, g_ref, o_ref,
               s_sc, acc_sc, m_sc, l_sc, *, lambda_init):
    t = pl.program_id(2)
    lane = lax.broadcasted_iota(jnp.int32, (1, LANES), 1)
    qq = _head_rows(q_ref[...], lane)
    m_sc[...] = jnp.full_like(m_sc, NEG)
    l_sc[...] = jnp.zeros_like(l_sc)
    acc_sc[...] = jnp.zeros_like(acc_sc)

    def scores(slot, k):
        s_sc[slot] = _nt(k, qq)

    def kx(c):
        return kx_ref[pl.ds(pl.multiple_of(c * CHUNK, CHUNK), CHUNK), :]

    def fold(slot, vt):
        for g in range(2):
            sl = slice(g * CHUNK, (g + 1) * CHUNK)
            s = s_sc[slot, :, sl]
            m_old = m_sc[:, sl]
            m_new = jnp.maximum(m_old, jnp.max(s, axis=0, keepdims=True))
            alpha = jnp.exp2(m_old - m_new)
            p = jnp.exp2(s - m_new)
            l_sc[:, sl] = alpha * l_sc[:, sl] + jnp.sum(p, axis=0, keepdims=True)
            m_sc[:, sl] = m_new
            acc_sc[:, sl] = acc_sc[:, sl] * alpha + jnp.dot(vt, p.astype(BF16), preferred_element_type=F32)

    @pl.when(t < X_CHUNKS)
    def _():
        scores(0, kx(0))

        def body(i, carry):
            scores(1, kx(2 * i + 1))
            fold(0, vtx_ref[0, 2 * i])
            scores(0, kx(2 * i + 2))
            fold(1, vtx_ref[0, 2 * i + 1])
            return carry

        lax.fori_loop(0, X_CHUNKS // 2 - 1, body, 0)
        scores(1, kx(X_CHUNKS - 1))
        fold(0, vtx_ref[0, X_CHUNKS - 2])
        scores(0, kc_ref[...])
        fold(1, vtx_ref[0, X_CHUNKS - 1])
        fold(0, vtc_ref[0, 0])

    @pl.when(t == X_CHUNKS)
    def _():
        scores(0, kc_ref[...])
        fold(0, vtc_ref[0, 0])

    ot = acc_sc[...] / l_sc[...]
    lam = lam_ref[...]
    lam_full = (jnp.exp(jnp.sum(lam[0:1] * lam[1:2], axis=1, keepdims=True))
                - jnp.exp(jnp.sum(lam[2:3] * lam[3:4], axis=1, keepdims=True)) + lambda_init)
    od = ot[:, :CHUNK] - lam_full * ot[:, CHUNK:]
    y = od * lax.rsqrt(jnp.mean(od * od, axis=0, keepdims=True) + EPS) * g_ref[...]
    o_ref[...] = (y * (1.0 - lambda_init)).T.astype(o_ref.dtype)


def da_attention(qkv, lam, subln_g, lambda_init, nb):
    n = qkv.shape[0]
    d = D_MODEL
    ncb = d // LANES
    xc = nb * X_CHUNKS
    vt = qkv[:, 2 * d:].reshape(n // CHUNK, CHUNK, ncb, LANES).transpose(2, 0, 3, 1)

    def qtile(b, t):
        return jnp.where(t < X_CHUNKS, b * X_CHUNKS + t, xc + b)

    return pl.pallas_call(
        functools.partial(_da_kernel, lambda_init=lambda_init),
        grid=(nb, ncb, X_CHUNKS + 1),
        in_specs=[pl.BlockSpec((CHUNK, LANES), lambda b, h, t: (qtile(b, t), h)),
                  pl.BlockSpec((SEQ, LANES), lambda b, h, t: (b, ncb + h)),
                  pl.BlockSpec((CHUNK, LANES), lambda b, h, t: (xc + b, ncb + h)),
                  pl.BlockSpec((1, X_CHUNKS, LANES, CHUNK), lambda b, h, t: (h, b, 0, 0)),
                  pl.BlockSpec((1, 1, LANES, CHUNK), lambda b, h, t: (h, xc + b, 0, 0)),
                  pl.BlockSpec((4, DA_HEAD_DIM), lambda b, h, t: (0, 0)),
                  pl.BlockSpec((LANES, 1), lambda b, h, t: (0, 0))],
        out_specs=pl.BlockSpec((CHUNK, LANES), lambda b, h, t: (qtile(b, t), h)),
        out_shape=jax.ShapeDtypeStruct((n, d), BF16),
        scratch_shapes=[pltpu.VMEM((2, CHUNK, 2 * CHUNK), F32),
                        pltpu.VMEM((LANES, 2 * CHUNK), F32),
                        pltpu.VMEM((1, 2 * CHUNK), F32),
                        pltpu.VMEM((1, 2 * CHUNK), F32)],
        compiler_params=_cp(("parallel", "parallel", "arbitrary")),
        name="da_attention",
    )(qkv, qkv, qkv, vt, vt, lam.astype(F32), subln_g.astype(F32).reshape(LANES, 1))


def _block_diag(w):
    per = LANES // ML_BLOCK
    wr = w.reshape(ML_INNER // LANES, per, ML_BLOCK, ML_BLOCK)
    eye = jnp.eye(per, dtype=w.dtype)
    return jnp.einsum('cgio,gh->cgiho', wr, eye).reshape(ML_INNER // LANES, LANES, LANES).astype(BF16)


def _ml_conv_kernel(p_ref, c_ref, n_ref, cw_ref, cb_ref, wq_ref, wk_ref, wv_ref,
                    xc_ref, q_ref, k_ref, v_ref, *, n_xchunks):
    i = pl.program_id(0)
    is_x = i < n_xchunks
    j = i % X_CHUNKS
    halo = 16
    cur = c_ref[...].astype(F32)
    prev = p_ref[CHUNK - halo:CHUNK, :].astype(F32)
    nxt = n_ref[0:halo, :].astype(F32)
    prev = jnp.where(jnp.logical_and(is_x, j > 0), prev, jnp.zeros_like(prev))
    nxt = jnp.where(jnp.logical_and(is_x, j < X_CHUNKS - 1), nxt, jnp.zeros_like(nxt))
    xp = jnp.concatenate([prev, cur, nxt], axis=0)
    rows = CHUNK + 2 * halo
    y = cb_ref[...] + cw_ref[ML_CONV_K // 2:ML_CONV_K // 2 + 1, :] * cur
    for tap in range(ML_CONV_K):
        dlt = tap - ML_CONV_K // 2
        if dlt == 0:
            continue
        shifted = pltpu.roll(xp, (-dlt) % rows, 0)[halo:halo + CHUNK]
        y = y + cw_ref[tap:tap + 1, :] * shifted
    xcb = _silu(y).astype(BF16)
    xc_ref[...] = xcb
    xm = c_ref[...]
    for s in range(xcb.shape[1] // LANES):
        sl = slice(s * LANES, (s + 1) * LANES)
        q_ref[:, sl] = jnp.dot(xcb[:, sl], wq_ref[s], preferred_element_type=F32).astype(BF16)
        k_ref[:, sl] = jnp.dot(xcb[:, sl], wk_ref[s], preferred_element_type=F32).astype(BF16)
        v_ref[:, sl] = jnp.dot(xm[:, sl], wv_ref[s], preferred_element_type=F32).astype(BF16)


def ml_conv_qkv(up, conv_w, conv_b, wq, wk, wv, nb):
    n = up.shape[0]
    nchunks = n // CHUNK
    cw = 512
    ncb = ML_INNER // cw
    sub = cw // LANES
    blk = pl.BlockSpec((CHUNK, cw), lambda i, c: (i, c))
    wspec = pl.BlockSpec((sub, LANES, LANES), lambda i, c: (c, 0, 0))
    out = jax.ShapeDtypeStruct((n, ML_INNER), BF16)
    return pl.pallas_call(
        functools.partial(_ml_conv_kernel, n_xchunks=nb * X_CHUNKS),
        grid=(nchunks, ncb),
        in_specs=[pl.BlockSpec((CHUNK, cw), lambda i, c: (jnp.maximum(i - 1, 0), c)),
                  blk,
                  pl.BlockSpec((CHUNK, cw), lambda i, c: (jnp.minimum(i + 1, nchunks - 1), c)),
                  pl.BlockSpec((ML_CONV_K, cw), lambda i, c: (0, c)),
                  pl.BlockSpec((1, cw), lambda i, c: (0, c)),
                  wspec, wspec, wspec],
        out_specs=[blk, blk, blk, blk],
        out_shape=[out, out, out, out],
        compiler_params=_cp(("parallel", "parallel")),
        name="ml_conv_qkv",
    )(up, up, up, conv_w.astype(F32), conv_b.astype(F32).reshape(1, ML_INNER), wq, wk, wv)


def _ml_gate_kernel(q_ref, k_ref, v_ref, w_ref, b_ref, o_ref):
    acc = jnp.dot(q_ref[...], w_ref[0], preferred_element_type=F32)
    acc = acc + jnp.dot(k_ref[...], w_ref[1], preferred_element_type=F32)
    acc = acc + jnp.dot(v_ref[...], w_ref[2], preferred_element_type=F32)
    o_ref[...] = acc + b_ref[...]


def ml_gates(q, k, v, wg, bg):
    n = q.shape[0]
    tm = _tile(n, PROJ_TM)
    blk = pl.BlockSpec((tm, ML_INNER), lambda i: (i, 0))
    return pl.pallas_call(
        _ml_gate_kernel,
        grid=(n // tm,),
        in_specs=[blk, blk, blk,
                  pl.BlockSpec((3, ML_INNER, LANES), lambda i: (0, 0, 0)),
                  pl.BlockSpec((1, LANES), lambda i: (0, 0))],
        out_specs=pl.BlockSpec((tm, LANES), lambda i: (i, 0)),
        out_shape=jax.ShapeDtypeStruct((n, LANES), F32),
        compiler_params=_cp(("parallel",)),
        name="ml_gates",
    )(q, k, v, wg, bg)


def _log_sigmoid(x):
    return jnp.minimum(x, 0.0) - jnp.log(1.0 + jnp.exp(-jnp.abs(x)))


def _ml_scan_kernel(q_ref, k_ref, v_ref, g_ref, gt_ref, o_ref, c_sc, n_sc, m_sc):
    h = pl.program_id(1)
    d = pl.program_id(2)
    p = pl.program_id(3)
    ninf = -jnp.inf

    @pl.when(p == 0)
    def _():
        c_sc[...] = jnp.zeros_like(c_sc)
        n_sc[...] = jnp.zeros_like(n_sc)
        m_sc[...] = jnp.full_like(m_sc, ninf)

    L = CHUNK
    col_i = d * 2 * ML_HEADS + h
    col_f = col_i + ML_HEADS
    lane = lax.broadcasted_iota(jnp.int32, (1, LANES), 1)
    g = g_ref[...]
    i_col = jnp.sum(jnp.where(lane == col_i, g, 0.0), axis=1, keepdims=True)
    f_col = _log_sigmoid(jnp.sum(jnp.where(lane == col_f, g, 0.0), axis=1, keepdims=True))
    i_row = gt_ref[pl.ds(col_i, 1), :]
    f_row = _log_sigmoid(gt_ref[pl.ds(col_f, 1), :])

    jj = lax.broadcasted_iota(jnp.int32, (L, L), 0)
    ss = lax.broadcasted_iota(jnp.int32, (L, L), 1)
    fwd = d == 0
    bwd = d == 1
    valid = jnp.logical_or(jnp.logical_and(ss <= jj, fwd), jnp.logical_and(ss >= jj, bwd))
    valid_t = jnp.logical_or(jnp.logical_and(jj <= ss, fwd), jnp.logical_and(jj >= ss, bwd))
    bcum_col = jnp.sum(jnp.where(valid, f_row, 0.0), axis=1, keepdims=True)
    bcum_row = jnp.sum(jnp.where(valid_t, f_col, 0.0), axis=0, keepdims=True)
    b_last = jnp.sum(f_row, axis=1, keepdims=True)
    m_prev = m_sc[...]

    logd = jnp.where(valid, bcum_col - bcum_row + i_row, ninf)
    log_inter = bcum_col + m_prev
    m_j = jnp.maximum(log_inter, jnp.max(logd, axis=1, keepdims=True))
    dmat = jnp.exp(logd - m_j)
    inter = jnp.exp(log_inter - m_j)

    q = q_ref[...]
    k = k_ref[...]
    v = v_ref[...]
    scale = ML_HEAD_DIM ** -0.5
    sc = _nt(q, k) * scale * dmat
    c_prev = c_sc[...]
    n_prev = n_sc[...]
    qc = jnp.dot(q, c_prev.astype(BF16), preferred_element_type=F32) * scale
    num = inter * qc + jnp.dot(sc.astype(BF16), v, preferred_element_type=F32)
    qn = jnp.sum(q.astype(F32) * n_prev, axis=1, keepdims=True) * scale
    den = inter * qn + jnp.sum(sc, axis=1, keepdims=True)
    o_ref[0] = num / jnp.maximum(jnp.abs(den), jnp.exp(-m_j))

    ls = b_last - bcum_col + i_col
    m_new = jnp.maximum(b_last + m_prev, jnp.max(ls, axis=0, keepdims=True))
    w = jnp.exp(ls - m_new)
    decay = jnp.exp(b_last + m_prev - m_new)
    kw = k.astype(F32) * w
    c_sc[...] = decay * c_prev + _tn(kw.astype(BF16), v)
    n_sc[...] = decay * n_prev + jnp.sum(kw, axis=0, keepdims=True)
    m_sc[...] = m_new


def ml_scan(q, k, v, g, gt, nb):
    n = q.shape[0]
    xc = nb * X_CHUNKS

    def cidx(b, d, p):
        xi = jnp.where(d == 0, p - 1, X_CHUNKS - p)
        return jnp.where(p == 0, xc + b, b * X_CHUNKS + xi)

    blk = pl.BlockSpec((CHUNK, ML_HEAD_DIM), lambda b, h, d, p: (cidx(b, d, p), h))
    return pl.pallas_call(
        _ml_scan_kernel,
        grid=(nb, ML_HEADS, 2, X_CHUNKS + 1),
        in_specs=[blk, blk, blk,
                  pl.BlockSpec((CHUNK, LANES), lambda b, h, d, p: (cidx(b, d, p), 0)),
                  pl.BlockSpec((16, CHUNK), lambda b, h, d, p: (0, cidx(b, d, p)))],
        out_specs=pl.BlockSpec((1, CHUNK, ML_HEAD_DIM), lambda b, h, d, p: (d, cidx(b, d, p), h)),
        out_shape=jax.ShapeDtypeStruct((2, n, ML_INNER), F32),
        scratch_shapes=[pltpu.VMEM((ML_HEAD_DIM, ML_HEAD_DIM), F32),
                        pltpu.VMEM((1, ML_HEAD_DIM), F32),
                        pltpu.VMEM((1, 1), F32)],
        compiler_params=_cp(("parallel", "parallel", "parallel", "arbitrary")),
        name="ml_scan",
    )(q, k, v, g, gt)


def _ml_finish_kernel(h_ref, xc_ref, z_ref, gn_ref, sk_ref, o_ref):
    hh = h_ref[0] + h_ref[1]
    z = z_ref[...].astype(F32)
    gate = _silu(z)
    for hd in range(ML_HEADS):
        sl = slice(hd * ML_HEAD_DIM, (hd + 1) * ML_HEAD_DIM)
        seg = hh[:, sl]
        mu = jnp.mean(seg, axis=-1, keepdims=True)
        cen = seg - mu
        var = jnp.mean(cen * cen, axis=-1, keepdims=True)
        hn = cen * lax.rsqrt(var + EPS) * gn_ref[:, sl]
        a = (hn + sk_ref[:, sl] * xc_ref[:, sl].astype(F32)) * gate[:, sl]
        o_ref[:, sl] = a.astype(o_ref.dtype)


def ml_finish(hs, xc, up, gn_w, skip):
    n = xc.shape[0]
    tm = CHUNK
    vec = pl.BlockSpec((1, ML_INNER), lambda i: (0, 0))
    return pl.pallas_call(
        _ml_finish_kernel,
        grid=(n // tm,),
        in_specs=[pl.BlockSpec((2, tm, ML_INNER), lambda i: (0, i, 0)),
                  pl.BlockSpec((tm, ML_INNER), lambda i: (i, 0)),
                  pl.BlockSpec((tm, ML_INNER), lambda i: (i, 1)),
                  vec, vec],
        out_specs=pl.BlockSpec((tm, ML_INNER), lambda i: (i, 0)),
        out_shape=jax.ShapeDtypeStruct((n, ML_INNER), BF16),
        compiler_params=_cp(("parallel",)),
        name="ml_finish",
    )(hs, xc, up, gn_w.astype(F32).reshape(1, ML_INNER), skip.astype(F32).reshape(1, ML_INNER))


def _router_kernel(x_ref, g_ref, sh_ref, sc_ref, w_ref, b_ref, h_ref, o_ref):
    x = x_ref[...]
    y = x * lax.rsqrt(jnp.mean(x * x, axis=-1, keepdims=True) + EPS) * g_ref[...]
    hf = y * (1.0 + sc_ref[0, 0]) + sh_ref[0, 0]
    h_hi = hf.astype(BF16)
    h_ref[...] = h_hi
    h_lo = (hf - h_hi.astype(F32)).astype(BF16)
    logits = (jnp.dot(h_hi, w_ref[0], preferred_element_type=F32)
              + jnp.dot(h_hi, w_ref[1], preferred_element_type=F32)
              + jnp.dot(h_lo, w_ref[0], preferred_element_type=F32)) + b_ref[...]
    lane = lax.broadcasted_iota(jnp.int32, (1, LANES), 1).astype(F32)
    big = 1e9
    ninf = -jnp.inf
    is_g = jnp.logical_and(lane >= MOE_E, lane < MOE_E + MOE_GROUPS)
    gl = jnp.where(is_g, logits, ninf)
    gmax = jnp.max(gl, axis=1, keepdims=True)
    g_val = 1.0 / jnp.sum(jnp.exp(gl - gmax), axis=1, keepdims=True)
    g_idx = jnp.min(jnp.where(gl == gmax, lane, big), axis=1, keepdims=True) - MOE_E
    lo = g_idx * MOE_EPG
    sel = jnp.logical_and(lane >= lo, lane < lo + MOE_EPG)
    el = jnp.where(sel, logits, ninf)
    e1 = jnp.max(el, axis=1, keepdims=True)
    esum = jnp.sum(jnp.exp(el - e1), axis=1, keepdims=True)
    i1 = jnp.min(jnp.where(el == e1, lane, big), axis=1, keepdims=True)
    el2 = jnp.where(lane == i1, ninf, el)
    e2 = jnp.max(el2, axis=1, keepdims=True)
    i2 = jnp.min(jnp.where(el2 == e2, lane, big), axis=1, keepdims=True)
    p1 = 1.0 / esum
    p2 = jnp.exp(e2 - e1) / esum
    w1 = g_val * p1 / (p1 + p2)
    w2 = g_val * p2 / (p1 + p2)
    o_ref[...] = jnp.where(lane == i1, w1, 0.0) + jnp.where(lane == i2, w2, 0.0)


def moe_router(s, g, mod, k_shift, k_scale, wr, br, nb):
    n, d = s.shape
    tm = _tile(n, PROJ_TM)
    return pl.pallas_call(
        _router_kernel,
        grid=(n // tm,),
        in_specs=[pl.BlockSpec((tm, d), lambda i: (i, 0)),
                  pl.BlockSpec((1, d), lambda i: (0, 0)),
                  pl.BlockSpec((1, 1, 1, d), lambda i: (_mod_row(i, tm, nb), k_shift, 0, 0)),
                  pl.BlockSpec((1, 1, 1, d), lambda i: (_mod_row(i, tm, nb), k_scale, 0, 0)),
                  pl.BlockSpec((2, d, LANES), lambda i: (0, 0, 0)),
                  pl.BlockSpec((1, LANES), lambda i: (0, 0))],
        out_specs=[pl.BlockSpec((tm, d), lambda i: (i, 0)),
                   pl.BlockSpec((tm, LANES), lambda i: (i, 0))],
        out_shape=[jax.ShapeDtypeStruct((n, d), BF16),
                   jax.ShapeDtypeStruct((n, LANES), F32)],
        compiler_params=_cp(("parallel",)),
        name="moe_router",
    )(s, g.reshape(1, d), mod, mod, wr, br)


def _moe_kernel(h_ref, gates_ref, w1_ref, w3_ref, w2_ref, r_ref, g_ref, o_ref, acc_ref):
    e = pl.program_id(1)

    @pl.when(e == 0)
    def _():
        acc_ref[...] = jnp.zeros_like(acc_ref)

    x = h_ref[...]
    a = jnp.dot(x, w1_ref[0], preferred_element_type=F32)
    b = jnp.dot(x, w3_ref[0], preferred_element_type=F32)
    he = (_silu(a) * b).astype(BF16)
    y = jnp.dot(he, w2_ref[0], preferred_element_type=F32)
    lane = lax.broadcasted_iota(jnp.int32, (1, LANES), 1)
    ge = jnp.sum(jnp.where(lane == e, gates_ref[...], 0.0), axis=1, keepdims=True)
    acc_ref[...] += y * ge

    @pl.when(e == MOE_E - 1)
    def _():
        o_ref[...] = r_ref[...] + g_ref[0, 0] * acc_ref[...]


def moe_dense(h, gates, w1, w3, w2, s, mod, k_gate, nb):
    n, d = h.shape
    tm = _tile(n, 1024)
    return pl.pallas_call(
        _moe_kernel,
        grid=(n // tm, MOE_E),
        in_specs=[pl.BlockSpec((tm, d), lambda i, e: (i, 0)),
                  pl.BlockSpec((tm, LANES), lambda i, e: (i, 0)),
                  pl.BlockSpec((1, d, MOE_HIDDEN), lambda i, e: (e, 0, 0)),
                  pl.BlockSpec((1, d, MOE_HIDDEN), lambda i, e: (e, 0, 0)),
                  pl.BlockSpec((1, MOE_HIDDEN, d), lambda i, e: (e, 0, 0)),
                  pl.BlockSpec((tm, d), lambda i, e: (i, 0)),
                  pl.BlockSpec((1, 1, 1, d), lambda i, e: (_mod_row(i, tm, nb), k_gate, 0, 0))],
        out_specs=pl.BlockSpec((tm, d), lambda i, e: (i, 0)),
        out_shape=jax.ShapeDtypeStruct((n, d), F32),
        scratch_shapes=[pltpu.VMEM((tm, d), F32)],
        input_output_aliases={5: 0},
        compiler_params=_cp(("parallel", "arbitrary")),
        name="moe_dense",
    )(h, gates, w1, w3, w2, s, mod)


def _na_layer(h, w_qkv, w_o, rpb, s, mod, nb):
    qkv = matmul(h, w_qkv.astype(BF16), BF16)
    o = na_attention(qkv, _na_bias_tables(rpb), nb)
    return matmul_residual(o, w_o.astype(BF16), s, mod, 2, nb)


def _da_layer(h, w_qkv, lam, subln_g, w_o, lambda_init, s, mod, nb):
    qkv = matmul_rope(h, w_qkv.astype(BF16), nb)
    o = da_attention(qkv, lam, subln_g, lambda_init, nb)
    return matmul_residual(o, w_o.astype(BF16), s, mod, 2, nb)


def _ml_layer(h, w_up, conv_w, conv_b, w_q, w_k, w_v, w_gate, b_gate, gn_w, skip, w_down, s, mod, nb):
    up = matmul(h, w_up.astype(BF16), BF16)
    xc, q, k, v = ml_conv_qkv(up, conv_w, conv_b, _block_diag(w_q), _block_diag(w_k), _block_diag(w_v), nb)
    ng = w_gate.shape[1]
    wg = jnp.pad(w_gate, ((0, 0), (0, LANES - ng))).reshape(3, ML_INNER, LANES).astype(BF16)
    bg = jnp.pad(b_gate.astype(F32), (0, LANES - ng)).reshape(1, LANES)
    g = ml_gates(q, k, v, wg, bg)
    gt = g[:, :ng].T
    hs = ml_scan(q, k, v, g, gt, nb)
    a = ml_finish(hs, xc, up, gn_w, skip)
    return matmul_residual(a, w_down.astype(BF16), s, mod, 2, nb)


def _moe_layer(norm_g, w_group, b_group, w_router, b_router, w1, w3, w2, s, mod, nb):
    d = s.shape[1]
    pad = LANES - MOE_E - MOE_GROUPS
    wr = jnp.concatenate([w_router, w_group, jnp.zeros((d, pad), w_router.dtype)], axis=1).astype(F32)
    wr_hi = wr.astype(BF16)
    wr_lo = (wr - wr_hi.astype(F32)).astype(BF16)
    br = jnp.concatenate([b_router, b_group, jnp.zeros((pad,), b_router.dtype)]).astype(F32).reshape(1, LANES)
    h, gates = moe_router(s, norm_g.astype(F32), mod, 3, 4, jnp.stack([wr_hi, wr_lo]), br, nb)
    return moe_dense(h, gates, w1.astype(BF16), w3.astype(BF16), w2.astype(BF16), s, mod, 5, nb)


def kernel(x, c, ctx, c_ctx, mod_w, mod_b, norm_g, final_g, na_w_qkv, na_w_o, na_rpb, ml_w_up, ml_conv_w, ml_conv_b, ml_w_q, ml_w_k, ml_w_v, ml_w_gate, ml_b_gate, ml_gn_w, ml_skip, ml_w_down, da_w_qkv, da_lambda, da_subln_g, da_w_o, moe_w_group, moe_b_group, moe_w_router, moe_b_router, moe_w1, moe_w3, moe_w2):
    nb, seq, d = x.shape
    assert (seq, d, ctx.shape[1]) == (SEQ, D_MODEL, CTX_LEN) and nb < 16
    depth = mod_w.shape[0]
    nx = nb * seq
    s = jnp.concatenate([x.reshape(nx, d), ctx.reshape(nb * CTX_LEN, d)], axis=0).astype(F32)
    cvec = jnp.concatenate([c, c_ctx[None, :], jnp.zeros((16 - nb - 1, d), c.dtype)], axis=0).astype(F32)
    mods = mod_vectors(cvec, mod_w, mod_b).reshape(depth, 16, 6, 1, d)

    for i in range(depth):
        kind, j = i % N_MIXERS, i // N_MIXERS
        mod = mods[i]
        h = norm_mod(s, norm_g[i, 0].astype(F32), mod, 0, 1, nb)
        if kind == 0:
            s = _na_layer(h, na_w_qkv[j], na_w_o[j], na_rpb[j], s, mod, nb)
        elif kind == 1:
            s = _ml_layer(h, ml_w_up[j], ml_conv_w[j], ml_conv_b[j], ml_w_q[j], ml_w_k[j], ml_w_v[j],
                          ml_w_gate[j], ml_b_gate[j], ml_gn_w[j], ml_skip[j], ml_w_down[j], s, mod, nb)
        else:
            lambda_init = 0.8 - 0.6 * math.exp(-0.3 * i)
            s = _da_layer(h, da_w_qkv[j], da_lambda[j], da_subln_g[j], da_w_o[j], lambda_init, s, mod, nb)
        s = _moe_layer(norm_g[i, 1], moe_w_group[i], moe_b_group[i], moe_w_router[i], moe_b_router[i],
                       moe_w1[i], moe_w3[i], moe_w2[i], s, mod, nb)
    return final_norm(s, final_g.astype(F32), nx).reshape(nb, seq, d)
```

```python
import functools
import math

import numpy as np
import jax
import jax.numpy as jnp
from jax import lax
from jax.experimental import pallas as pl
from jax.experimental.pallas import tpu as pltpu

F32 = jnp.float32
BF16 = jnp.bfloat16

D_MODEL = 1024
SEQ = 4096
CTX_LEN = 256
GRID_W = 64
N_MIXERS = 3
EPS = 1e-6

NA_HEADS = 16
NA_WIN_H = 8
NA_WIN_W = 16

ML_HEADS = 4
ML_INNER = 2 * D_MODEL
ML_HEAD_DIM = ML_INNER // ML_HEADS
ML_BLOCK = 4
ML_CONV_K = 5

DA_HEADS = 8
DA_HEAD_DIM = 64
ROPE_BASE = 10000.0

MOE_GROUPS = 4
MOE_EPG = 8
MOE_E = MOE_GROUPS * MOE_EPG
MOE_HIDDEN = 512
MOE_TM = 256
SEL_LANE = 64

LANES = 128
CHUNK = 256
X_CHUNKS = SEQ // CHUNK
PROJ_TM = 512
NEG = -1e30
VMEM_LIMIT = 56 * 1024 * 1024


def _cp(sem, vmem=VMEM_LIMIT):
    return pltpu.CompilerParams(dimension_semantics=sem, vmem_limit_bytes=vmem)


def _tile(n, pref):
    tm = pref
    while n % tm:
        tm //= 2
    assert tm >= CHUNK
    return tm


def _mod_row(i, tm, nb):
    return jnp.where(i < nb * (SEQ // tm), i // (SEQ // tm), nb)


def _nt(a, b):
    return lax.dot_general(a, b, (((1,), (1,)), ((), ())), preferred_element_type=F32)


def _tn(a, b):
    return lax.dot_general(a, b, (((0,), (0,)), ((), ())), preferred_element_type=F32)


def _silu(x):
    return x * jax.nn.sigmoid(x)


def _head_rows(q, lane):
    zero = jnp.zeros_like(q)
    return jnp.concatenate([jnp.where(lane < 64, q, zero), jnp.where(lane >= 64, q, zero)], axis=0)


def _mod_kernel(c_ref, w_ref, b_ref, o_ref):
    a = _silu(c_ref[...]).astype(BF16)
    o_ref[0] = jnp.dot(a, w_ref[0].astype(BF16), preferred_element_type=F32) + b_ref[0]


def mod_vectors(cvec, mod_w, mod_b):
    depth, d, n6 = mod_w.shape
    tn = 1024
    return pl.pallas_call(
        _mod_kernel,
        grid=(depth, n6 // tn),
        in_specs=[pl.BlockSpec((16, d), lambda l, j: (0, 0)),
                  pl.BlockSpec((1, d, tn), lambda l, j: (l, 0, j)),
                  pl.BlockSpec((1, 1, tn), lambda l, j: (l, 0, j))],
        out_specs=pl.BlockSpec((1, 16, tn), lambda l, j: (l, 0, j)),
        out_shape=jax.ShapeDtypeStruct((depth, 16, n6), F32),
        compiler_params=_cp(("parallel", "parallel")),
        name="mod_vectors",
    )(cvec, mod_w, mod_b.reshape(depth, 1, n6))


def _norm_mod_kernel(x_ref, g_ref, sh_ref, sc_ref, o_ref):
    x = x_ref[...]
    y = x * lax.rsqrt(jnp.mean(x * x, axis=-1, keepdims=True) + EPS) * g_ref[...]
    o_ref[...] = (y * (1.0 + sc_ref[0, 0]) + sh_ref[0, 0]).astype(o_ref.dtype)


def norm_mod(s, g, mod, k_shift, k_scale, nb):
    n, d = s.shape
    tm = _tile(n, PROJ_TM)
    return pl.pallas_call(
        _norm_mod_kernel,
        grid=(n // tm,),
        in_specs=[pl.BlockSpec((tm, d), lambda i: (i, 0)),
                  pl.BlockSpec((1, d), lambda i: (0, 0)),
                  pl.BlockSpec((1, 1, 1, d), lambda i: (_mod_row(i, tm, nb), k_shift, 0, 0)),
                  pl.BlockSpec((1, 1, 1, d), lambda i: (_mod_row(i, tm, nb), k_scale, 0, 0))],
        out_specs=pl.BlockSpec((tm, d), lambda i: (i, 0)),
        out_shape=jax.ShapeDtypeStruct((n, d), BF16),
        compiler_params=_cp(("parallel",)),
        name="norm_mod",
    )(s, g.reshape(1, d), mod, mod)


def _final_norm_kernel(x_ref, g_ref, o_ref):
    x = x_ref[...]
    o_ref[...] = x * lax.rsqrt(jnp.mean(x * x, axis=-1, keepdims=True) + EPS) * g_ref[...]


def final_norm(s, g, n_rows):
    d = s.shape[1]
    tm = _tile(n_rows, PROJ_TM)
    return pl.pallas_call(
        _final_norm_kernel,
        grid=(n_rows // tm,),
        in_specs=[pl.BlockSpec((tm, d), lambda i: (i, 0)),
                  pl.BlockSpec((1, d), lambda i: (0, 0))],
        out_specs=pl.BlockSpec((tm, d), lambda i: (i, 0)),
        out_shape=jax.ShapeDtypeStruct((n_rows, d), F32),
        compiler_params=_cp(("parallel",)),
        name="final_norm",
    )(s, g.reshape(1, d))


def _mm_kernel(a_ref, w_ref, o_ref):
    o_ref[...] = jnp.dot(a_ref[...], w_ref[...], preferred_element_type=F32).astype(o_ref.dtype)


def matmul(a, w, out_dtype, tn=1024):
    n, k = a.shape
    tm = _tile(n, PROJ_TM)
    nout = w.shape[1]
    tn = min(tn, nout)
    return pl.pallas_call(
        _mm_kernel,
        grid=(nout // tn, n // tm),
        in_specs=[pl.BlockSpec((tm, k), lambda j, i: (i, 0)),
                  pl.BlockSpec((k, tn), lambda j, i: (0, j))],
        out_specs=pl.BlockSpec((tm, tn), lambda j, i: (i, j)),
        out_shape=jax.ShapeDtypeStruct((n, nout), out_dtype),
        compiler_params=_cp(("parallel", "parallel")),
        name="matmul",
    )(a, w)


def _mm_res_kernel(a_ref, w_ref, r_ref, g_ref, o_ref):
    acc = jnp.dot(a_ref[...], w_ref[...], preferred_element_type=F32)
    o_ref[...] = r_ref[...] + g_ref[0, 0] * acc


def matmul_residual(a, w, s, mod, k_gate, nb):
    n, k = a.shape
    d = w.shape[1]
    tm = _tile(n, PROJ_TM)
    return pl.pallas_call(
        _mm_res_kernel,
        grid=(n // tm,),
        in_specs=[pl.BlockSpec((tm, k), lambda i: (i, 0)),
                  pl.BlockSpec((k, d), lambda i: (0, 0)),
                  pl.BlockSpec((tm, d), lambda i: (i, 0)),
                  pl.BlockSpec((1, 1, 1, d), lambda i: (_mod_row(i, tm, nb), k_gate, 0, 0))],
        out_specs=pl.BlockSpec((tm, d), lambda i: (i, 0)),
        out_shape=jax.ShapeDtypeStruct((n, d), F32),
        input_output_aliases={2: 0},
        compiler_params=_cp(("parallel",)),
        name="matmul_residual",
    )(a, w, s, mod)


def _na_bias_tables(rpb):
    o = np.arange(NA_WIN_H)[:, None]
    j = np.arange(NA_WIN_H)[None, :]
    dy = j - o + NA_WIN_H - 1
    qc = np.arange(GRID_W)[:, None]
    kc = np.arange(GRID_W)[None, :]
    dx = np.clip(kc - qc, -(NA_WIN_W - 1), NA_WIN_W - 1) + NA_WIN_W - 1
    w_start = np.clip(qc - NA_WIN_W // 2, 0, GRID_W - NA_WIN_W)
    valid = (kc >= w_start) & (kc < w_start + NA_WIN_W)
    tbl = rpb.astype(F32)[:, dy][:, :, :, dx]
    tbl = jnp.where(jnp.asarray(valid)[None, None, None], tbl, NEG)
    tbl = tbl.reshape(NA_HEADS // 2, 2, NA_WIN_H, NA_WIN_H, GRID_W, GRID_W)
    tbl = tbl.transpose(2, 0, 3, 5, 1, 4)
    return tbl.reshape(NA_WIN_H, NA_HEADS // 2, NA_WIN_H * GRID_W, 2 * GRID_W)


def _na_kernel(q_ref, kx_ref, vx_ref, kc_ref, vc_ref, bias_ref, o_ref, s_sc, p_sc):
    t = pl.program_id(2)
    lane = lax.broadcasted_iota(jnp.int32, (1, LANES), 1)
    nwin = NA_WIN_H * GRID_W

    def softmax_cols(s):
        m = jnp.max(s, axis=0, keepdims=True)
        p = jnp.exp(s - m)
        return (p * (1.0 / jnp.sum(p, axis=0, keepdims=True))).astype(BF16)

    @pl.when(t < X_CHUNKS)
    def _():
        rows = CHUNK // GRID_W
        starts = []
        for i in range(rows):
            r = t * rows + i
            rs = jnp.clip(r - NA_WIN_H // 2, 0, SEQ // GRID_W - NA_WIN_H)
            start = pl.multiple_of(rs * GRID_W, GRID_W)
            starts.append(start)
            qq = _head_rows(q_ref[i * GRID_W:(i + 1) * GRID_W, :] * 0.125, lane)
            s_sc[i, 0:nwin, :] = _nt(kx_ref[pl.ds(start, nwin), :], qq) + bias_ref[r - rs, 0]
            s_sc[i, nwin:, :] = _nt(kc_ref[...], qq)
        for i in range(rows):
            p_sc[i] = softmax_cols(s_sc[i])
        for i in range(rows):
            o = (_tn(p_sc[i, 0:nwin, :], vx_ref[pl.ds(starts[i], nwin), :])
                 + _tn(p_sc[i, nwin:, :], vc_ref[...]))
            o = jnp.where(lane < 64, o[0:GRID_W], o[GRID_W:])
            o_ref[i * GRID_W:(i + 1) * GRID_W, :] = o.astype(o_ref.dtype)

    @pl.when(t == X_CHUNKS)
    def _():
        q = q_ref[...] * 0.125
        outs = []
        for hh in range(2):
            qm = jnp.where((lane // 64) == hh, q, jnp.zeros_like(q))
            p = softmax_cols(_nt(kc_ref[...], qm))
            outs.append(_tn(p, vc_ref[...]))
        o_ref[...] = jnp.where(lane < 64, outs[0], outs[1]).astype(o_ref.dtype)


def na_attention(qkv, bias, nb):
    n = qkv.shape[0]
    d = D_MODEL
    ncb = d // LANES
    xc = nb * X_CHUNKS
    nkeys = NA_WIN_H * GRID_W + CTX_LEN

    def qtile(b, t):
        return jnp.where(t < X_CHUNKS, b * X_CHUNKS + t, xc + b)

    return pl.pallas_call(
        _na_kernel,
        grid=(nb, ncb, X_CHUNKS + 1),
        in_specs=[pl.BlockSpec((CHUNK, LANES), lambda b, h, t: (qtile(b, t), h)),
                  pl.BlockSpec((SEQ, LANES), lambda b, h, t: (b, ncb + h)),
                  pl.BlockSpec((SEQ, LANES), lambda b, h, t: (b, 2 * ncb + h)),
                  pl.BlockSpec((CHUNK, LANES), lambda b, h, t: (xc + b, ncb + h)),
                  pl.BlockSpec((CHUNK, LANES), lambda b, h, t: (xc + b, 2 * ncb + h)),
                  pl.BlockSpec((NA_WIN_H, 1, NA_WIN_H * GRID_W, LANES), lambda b, h, t: (0, h, 0, 0))],
        out_specs=pl.BlockSpec((CHUNK, LANES), lambda b, h, t: (qtile(b, t), h)),
        out_shape=jax.ShapeDtypeStruct((n, d), BF16),
        scratch_shapes=[pltpu.VMEM((CHUNK // GRID_W, nkeys, LANES), F32),
                        pltpu.VMEM((CHUNK // GRID_W, nkeys, LANES), BF16)],
        compiler_params=_cp(("parallel", "parallel", "arbitrary")),
        name="na_attention",
    )(qkv, qkv, qkv, qkv, qkv, bias)


def _rope_tables(tm):
    half = DA_HEAD_DIM // 2
    freqs = ROPE_BASE ** (-np.arange(0, half, 2, dtype=np.float32) / half)
    t = np.arange(SEQ)
    row, col = t // GRID_W, t % GRID_W
    lane = np.arange(LANES)
    l64 = lane % DA_HEAD_DIM
    use_col = (l64 // half) == 1
    l32 = l64 % half
    fi = l32 % (half // 2)
    second = l32 >= half // 2
    pos = np.where(use_col[None, :], col[:, None], row[:, None]).astype(np.float32)
    ang = pos * freqs[fi][None, :]
    cos = np.cos(ang).astype(np.float32)
    sin = np.sin(ang).astype(np.float32)
    sa = np.where(second[None, :], 0.0, -sin).astype(np.float32)
    sb = np.where(second[None, :], sin, 0.0).astype(np.float32)
    ident = np.ones((tm, LANES), np.float32)
    zero = np.zeros((tm, LANES), np.float32)
    return (jnp.asarray(np.concatenate([cos, ident])),
            jnp.asarray(np.concatenate([sa, zero])),
            jnp.asarray(np.concatenate([sb, zero])))


def _mm_rope_kernel(a_ref, w_ref, c_ref, sa_ref, sb_ref, o_ref):
    j = pl.program_id(0)
    acc = jnp.dot(a_ref[...], w_ref[...], preferred_element_type=F32)

    @pl.when(j < 2)
    def _():
        scale = jnp.where(j == 0, DA_HEAD_DIM ** -0.5 * math.log2(math.e), 1.0)
        cos = c_ref[...] * scale
        sa = sa_ref[...] * scale
        sb = sb_ref[...] * scale
        for g in range(acc.shape[1] // LANES):
            sl = slice(g * LANES, (g + 1) * LANES)
            x = acc[:, sl]
            y = x * cos + pltpu.roll(x, LANES - 16, 1) * sa + pltpu.roll(x, 16, 1) * sb
            o_ref[:, sl] = y.astype(o_ref.dtype)

    @pl.when(j >= 2)
    def _():
        o_ref[...] = acc.astype(o_ref.dtype)


def matmul_rope(a, w, nb):
    n, k = a.shape
    d = D_MODEL
    tm = _tile(n, PROJ_TM)
    cos, sa, sb = _rope_tables(tm)
    xt = nb * (SEQ // tm)

    def tab(j, i):
        return (jnp.where(i < xt, i % (SEQ // tm), SEQ // tm), 0)

    tspec = pl.BlockSpec((tm, LANES), tab)
    return pl.pallas_call(
        _mm_rope_kernel,
        grid=(3, n // tm),
        in_specs=[pl.BlockSpec((tm, k), lambda j, i: (i, 0)),
                  pl.BlockSpec((k, d), lambda j, i: (0, j)),
                  tspec, tspec, tspec],
        out_specs=pl.BlockSpec((tm, d), lambda j, i: (i, j)),
        out_shape=jax.ShapeDtypeStruct((n, 3 * d), BF16),
        compiler_params=_cp(("parallel", "parallel")),
        name="matmul_rope",
    )(a, w, cos, sa, sb)


def _da_kernel(q_ref, kx_ref, kc_ref, vtx_ref, vtc_ref, lam_ref, g_ref, o_ref,
               s_sc, acc_sc, m_sc, l_sc, *, lambda_init):
    t = pl.program_id(2)
    lane = lax.broadcasted_iota(jnp.int32, (1, LANES), 1)
    qq = _head_rows(q_ref[...], lane)
    m_sc[...] = jnp.full_like(m_sc, NEG)
    l_sc[...] = jnp.zeros_like(l_sc)
    acc_sc[...] = jnp.zeros_like(acc_sc)

    def scores(slot, k):
        s_sc[slot] = _nt(k, qq)

    def kx(c):
        return kx_ref[pl.ds(pl.multiple_of(c * CHUNK, CHUNK), CHUNK), :]

    def fold(slot, vt):
        for g in range(2):
            sl = slice(g * CHUNK, (g + 1) * CHUNK)
            s = s_sc[slot, :, sl]
            m_old = m_sc[:, sl]
            m_new = jnp.maximum(m_old, jnp.max(s, axis=0, keepdims=True))
            alpha = jnp.exp2(m_old - m_new)
            p = jnp.exp2(s - m_new)
            l_sc[:, sl] = alpha * l_sc[:, sl] + jnp.sum(p, axis=0, keepdims=True)
            m_sc[:, sl] = m_new
            acc_sc[:, sl] = acc_sc[:, sl] * alpha + jnp.dot(vt, p.astype(BF16), preferred_element_type=F32)

    @pl.when(t < X_CHUNKS)
    def _():
        scores(0, kx(0))

        def body(i, carry):
            scores(1, kx(2 * i + 1))
            fold(0, vtx_ref[0, 2 * i])
            scores(0, kx(2 * i + 2))
            fold(1, vtx_ref[0, 2 * i + 1])
            return carry

        lax.fori_loop(0, X_CHUNKS // 2 - 1, body, 0)
        scores(1, kx(X_CHUNKS - 1))
        fold(0, vtx_ref[0, X_CHUNKS - 2])
        scores(0, kc_ref[...])
        fold(1, vtx_ref[0, X_CHUNKS - 1])
        fold(0, vtc_ref[0, 0])

    @pl.when(t == X_CHUNKS)
    def _():
        scores(0, kc_ref[...])
        fold(0, vtc_ref[0, 0])

    ot = acc_sc[...] / l_sc[...]
    lam = lam_ref[...]
    lam_full = (jnp.exp(jnp.sum(lam[0:1] * lam[1:2], axis=1, keepdims=True))
                - jnp.exp(jnp.sum(lam[2:3] * lam[3:4], axis=1, keepdims=True)) + lambda_init)
    od = ot[:, :CHUNK] - lam_full * ot[:, CHUNK:]
    y = od * lax.rsqrt(jnp.mean(od * od, axis=0, keepdims=True) + EPS) * g_ref[...]
    o_ref[...] = (y * (1.0 - lambda_init)).T.astype(o_ref.dtype)


def da_attention(qkv, lam, subln_g, lambda_init, nb):
    n = qkv.shape[0]
    d = D_MODEL
    ncb = d // LANES
    xc = nb * X_CHUNKS
    vt = qkv[:, 2 * d:].reshape(n // CHUNK, CHUNK, ncb, LANES).transpose(2, 0, 3, 1)

    def qtile(b, t):
        return jnp.where(t < X_CHUNKS, b * X_CHUNKS + t, xc + b)

    return pl.pallas_call(
        functools.partial(_da_kernel, lambda_init=lambda_init),
        grid=(nb, ncb, X_CHUNKS + 1),
        in_specs=[pl.BlockSpec((CHUNK, LANES), lambda b, h, t: (qtile(b, t), h)),
                  pl.BlockSpec((SEQ, LANES), lambda b, h, t: (b, ncb + h)),
                  pl.BlockSpec((CHUNK, LANES), lambda b, h, t: (xc + b, ncb + h)),
                  pl.BlockSpec((1, X_CHUNKS, LANES, CHUNK), lambda b, h, t: (h, b, 0, 0)),
                  pl.BlockSpec((1, 1, LANES, CHUNK), lambda b, h, t: (h, xc + b, 0, 0)),
                  pl.BlockSpec((4, DA_HEAD_DIM), lambda b, h, t: (0, 0)),
                  pl.BlockSpec((LANES, 1), lambda b, h, t: (0, 0))],
        out_specs=pl.BlockSpec((CHUNK, LANES), lambda b, h, t: (qtile(b, t), h)),
        out_shape=jax.ShapeDtypeStruct((n, d), BF16),
        scratch_shapes=[pltpu.VMEM((2, CHUNK, 2 * CHUNK), F32),
                        pltpu.VMEM((LANES, 2 * CHUNK), F32),
                        pltpu.VMEM((1, 2 * CHUNK), F32),
                        pltpu.VMEM((1, 2 * CHUNK), F32)],
        compiler_params=_cp(("parallel", "parallel", "arbitrary")),
        name="da_attention",
    )(qkv, qkv, qkv, vt, vt, lam.astype(F32), subln_g.astype(F32).reshape(LANES, 1))


def _block_diag(w):
    per = LANES // ML_BLOCK
    wr = w.reshape(ML_INNER // LANES, per, ML_BLOCK, ML_BLOCK)
    eye = jnp.eye(per, dtype=w.dtype)
    return jnp.einsum('cgio,gh->cgiho', wr, eye).reshape(ML_INNER // LANES, LANES, LANES).astype(BF16)


def _ml_conv_kernel(p_ref, c_ref, n_ref, cw_ref, cb_ref, wq_ref, wk_ref, wv_ref,
                    xc_ref, q_ref, k_ref, v_ref, *, n_xchunks):
    i = pl.program_id(0)
    is_x = i < n_xchunks
    j = i % X_CHUNKS
    halo = 16
    cur = c_ref[...].astype(F32)
    prev = p_ref[CHUNK - halo:CHUNK, :].astype(F32)
    nxt = n_ref[0:halo, :].astype(F32)
    prev = jnp.where(jnp.logical_and(is_x, j > 0), prev, jnp.zeros_like(prev))
    nxt = jnp.where(jnp.logical_and(is_x, j < X_CHUNKS - 1), nxt, jnp.zeros_like(nxt))
    xp = jnp.concatenate([prev, cur, nxt], axis=0)
    rows = CHUNK + 2 * halo
    y = cb_ref[...] + cw_ref[ML_CONV_K // 2:ML_CONV_K // 2 + 1, :] * cur
    for tap in range(ML_CONV_K):
        dlt = tap - ML_CONV_K // 2
        if dlt == 0:
            continue
        shifted = pltpu.roll(xp, (-dlt) % rows, 0)[halo:halo + CHUNK]
        y = y + cw_ref[tap:tap + 1, :] * shifted
    xcb = _silu(y).astype(BF16)
    xc_ref[...] = xcb
    xm = c_ref[...]
    for s in range(xcb.shape[1] // LANES):
        sl = slice(s * LANES, (s + 1) * LANES)
        q_ref[:, sl] = jnp.dot(xcb[:, sl], wq_ref[s], preferred_element_type=F32).astype(BF16)
        k_ref[:, sl] = jnp.dot(xcb[:, sl], wk_ref[s], preferred_element_type=F32).astype(BF16)
        v_ref[:, sl] = jnp.dot(xm[:, sl], wv_ref[s], preferred_element_type=F32).astype(BF16)


def ml_conv_qkv(up, conv_w, conv_b, wq, wk, wv, nb):
    n = up.shape[0]
    nchunks = n // CHUNK
    cw = 512
    ncb = ML_INNER // cw
    sub = cw // LANES
    blk = pl.BlockSpec((CHUNK, cw), lambda i, c: (i, c))
    wspec = pl.BlockSpec((sub, LANES, LANES), lambda i, c: (c, 0, 0))
    out = jax.ShapeDtypeStruct((n, ML_INNER), BF16)
    return pl.pallas_call(
        functools.partial(_ml_conv_kernel, n_xchunks=nb * X_CHUNKS),
        grid=(nchunks, ncb),
        in_specs=[pl.BlockSpec((CHUNK, cw), lambda i, c: (jnp.maximum(i - 1, 0), c)),
                  blk,
                  pl.BlockSpec((CHUNK, cw), lambda i, c: (jnp.minimum(i + 1, nchunks - 1), c)),
                  pl.BlockSpec((ML_CONV_K, cw), lambda i, c: (0, c)),
                  pl.BlockSpec((1, cw), lambda i, c: (0, c)),
                  wspec, wspec, wspec],
        out_specs=[blk, blk, blk, blk],
        out_shape=[out, out, out, out],
        compiler_params=_cp(("parallel", "parallel")),
        name="ml_conv_qkv",
    )(up, up, up, conv_w.astype(F32), conv_b.astype(F32).reshape(1, ML_INNER), wq, wk, wv)


def _ml_gate_kernel(q_ref, k_ref, v_ref, w_ref, b_ref, o_ref):
    acc = jnp.dot(q_ref[...], w_ref[0], preferred_element_type=F32)
    acc = acc + jnp.dot(k_ref[...], w_ref[1], preferred_element_type=F32)
    acc = acc + jnp.dot(v_ref[...], w_ref[2], preferred_element_type=F32)
    o_ref[...] = acc + b_ref[...]


def ml_gates(q, k, v, wg, bg):
    n = q.shape[0]
    tm = _tile(n, PROJ_TM)
    blk = pl.BlockSpec((tm, ML_INNER), lambda i: (i, 0))
    return pl.pallas_call(
        _ml_gate_kernel,
        grid=(n // tm,),
        in_specs=[blk, blk, blk,
                  pl.BlockSpec((3, ML_INNER, LANES), lambda i: (0, 0, 0)),
                  pl.BlockSpec((1, LANES), lambda i: (0, 0))],
        out_specs=pl.BlockSpec((tm, LANES), lambda i: (i, 0)),
        out_shape=jax.ShapeDtypeStruct((n, LANES), F32),
        compiler_params=_cp(("parallel",)),
        name="ml_gates",
    )(q, k, v, wg, bg)


def _log_sigmoid(x):
    return jnp.minimum(x, 0.0) - jnp.log(1.0 + jnp.exp(-jnp.abs(x)))


def _ml_scan_kernel(q_ref, k_ref, v_ref, g_ref, gt_ref, o_ref, c_sc, n_sc, m_sc):
    h = pl.program_id(1)
    d = pl.program_id(2)
    p = pl.program_id(3)
    ninf = -jnp.inf

    @pl.when(p == 0)
    def _():
        c_sc[...] = jnp.zeros_like(c_sc)
        n_sc[...] = jnp.zeros_like(n_sc)
        m_sc[...] = jnp.full_like(m_sc, ninf)

    L = CHUNK
    col_i = d * 2 * ML_HEADS + h
    col_f = col_i + ML_HEADS
    lane = lax.broadcasted_iota(jnp.int32, (1, LANES), 1)
    g = g_ref[...]
    i_col = jnp.sum(jnp.where(lane == col_i, g, 0.0), axis=1, keepdims=True)
    f_col = _log_sigmoid(jnp.sum(jnp.where(lane == col_f, g, 0.0), axis=1, keepdims=True))
    i_row = gt_ref[pl.ds(col_i, 1), :]
    f_row = _log_sigmoid(gt_ref[pl.ds(col_f, 1), :])

    jj = lax.broadcasted_iota(jnp.int32, (L, L), 0)
    ss = lax.broadcasted_iota(jnp.int32, (L, L), 1)
    fwd = d == 0
    bwd = d == 1
    valid = jnp.logical_or(jnp.logical_and(ss <= jj, fwd), jnp.logical_and(ss >= jj, bwd))
    valid_t = jnp.logical_or(jnp.logical_and(jj <= ss, fwd), jnp.logical_and(jj >= ss, bwd))
    bcum_col = jnp.sum(jnp.where(valid, f_row, 0.0), axis=1, keepdims=True)
    bcum_row = jnp.sum(jnp.where(valid_t, f_col, 0.0), axis=0, keepdims=True)
    b_last = jnp.sum(f_row, axis=1, keepdims=True)
    m_prev = m_sc[...]

    logd = jnp.where(valid, bcum_col - bcum_row + i_row, ninf)
    log_inter = bcum_col + m_prev
    m_j = jnp.maximum(log_inter, jnp.max(logd, axis=1, keepdims=True))
    dmat = jnp.exp(logd - m_j)
    inter = jnp.exp(log_inter - m_j)

    q = q_ref[...]
    k = k_ref[...]
    v = v_ref[...]
    scale = ML_HEAD_DIM ** -0.5
    sc = _nt(q, k) * scale * dmat
    c_prev = c_sc[...]
    n_prev = n_sc[...]
    qc = jnp.dot(q, c_prev.astype(BF16), preferred_element_type=F32) * scale
    num = inter * qc + jnp.dot(sc.astype(BF16), v, preferred_element_type=F32)
    qn = jnp.sum(q.astype(F32) * n_prev, axis=1, keepdims=True) * scale
    den = inter * qn + jnp.sum(sc, axis=1, keepdims=True)
    o_ref[0] = num / jnp.maximum(jnp.abs(den), jnp.exp(-m_j))

    ls = b_last - bcum_col + i_col
    m_new = jnp.maximum(b_last + m_prev, jnp.max(ls, axis=0, keepdims=True))
    w = jnp.exp(ls - m_new)
    decay = jnp.exp(b_last + m_prev - m_new)
    kw = k.astype(F32) * w
    c_sc[...] = decay * c_prev + _tn(kw.astype(BF16), v)
    n_sc[...] = decay * n_prev + jnp.sum(kw, axis=0, keepdims=True)
    m_sc[...] = m_new


def ml_scan(q, k, v, g, gt, nb):
    n = q.shape[0]
    xc = nb * X_CHUNKS

    def cidx(b, d, p):
        xi = jnp.where(d == 0, p - 1, X_CHUNKS - p)
        return jnp.where(p == 0, xc + b, b * X_CHUNKS + xi)

    blk = pl.BlockSpec((CHUNK, ML_HEAD_DIM), lambda b, h, d, p: (cidx(b, d, p), h))
    return pl.pallas_call(
        _ml_scan_kernel,
        grid=(nb, ML_HEADS, 2, X_CHUNKS + 1),
        in_specs=[blk, blk, blk,
                  pl.BlockSpec((CHUNK, LANES), lambda b, h, d, p: (cidx(b, d, p), 0)),
                  pl.BlockSpec((16, CHUNK), lambda b, h, d, p: (0, cidx(b, d, p)))],
        out_specs=pl.BlockSpec((1, CHUNK, ML_HEAD_DIM), lambda b, h, d, p: (d, cidx(b, d, p), h)),
        out_shape=jax.ShapeDtypeStruct((2, n, ML_INNER), F32),
        scratch_shapes=[pltpu.VMEM((ML_HEAD_DIM, ML_HEAD_DIM), F32),
                        pltpu.VMEM((1, ML_HEAD_DIM), F32),
                        pltpu.VMEM((1, 1), F32)],
        compiler_params=_cp(("parallel", "parallel", "parallel", "arbitrary")),
        name="ml_scan",
    )(q, k, v, g, gt)


def _ml_finish_kernel(h_ref, xc_ref, z_ref, gn_ref, sk_ref, o_ref):
    hh = h_ref[0] + h_ref[1]
    z = z_ref[...].astype(F32)
    gate = _silu(z)
    for hd in range(ML_HEADS):
        sl = slice(hd * ML_HEAD_DIM, (hd + 1) * ML_HEAD_DIM)
        seg = hh[:, sl]
        mu = jnp.mean(seg, axis=-1, keepdims=True)
        cen = seg - mu
        var = jnp.mean(cen * cen, axis=-1, keepdims=True)
        hn = cen * lax.rsqrt(var + EPS) * gn_ref[:, sl]
        a = (hn + sk_ref[:, sl] * xc_ref[:, sl].astype(F32)) * gate[:, sl]
        o_ref[:, sl] = a.astype(o_ref.dtype)


def ml_finish(hs, xc, up, gn_w, skip):
    n = xc.shape[0]
    tm = CHUNK
    vec = pl.BlockSpec((1, ML_INNER), lambda i: (0, 0))
    return pl.pallas_call(
        _ml_finish_kernel,
        grid=(n // tm,),
        in_specs=[pl.BlockSpec((2, tm, ML_INNER), lambda i: (0, i, 0)),
                  pl.BlockSpec((tm, ML_INNER), lambda i: (i, 0)),
                  pl.BlockSpec((tm, ML_INNER), lambda i: (i, 1)),
                  vec, vec],
        out_specs=pl.BlockSpec((tm, ML_INNER), lambda i: (i, 0)),
        out_shape=jax.ShapeDtypeStruct((n, ML_INNER), BF16),
        compiler_params=_cp(("parallel",)),
        name="ml_finish",
    )(hs, xc, up, gn_w.astype(F32).reshape(1, ML_INNER), skip.astype(F32).reshape(1, ML_INNER))


def _router_kernel(x_ref, g_ref, sh_ref, sc_ref, w_ref, b_ref, h_ref, o_ref):
    x = x_ref[...]
    y = x * lax.rsqrt(jnp.mean(x * x, axis=-1, keepdims=True) + EPS) * g_ref[...]
    hf = y * (1.0 + sc_ref[0, 0]) + sh_ref[0, 0]
    h_hi = hf.astype(BF16)
    h_ref[...] = hf
    h_lo = (hf - h_hi.astype(F32)).astype(BF16)
    logits = (jnp.dot(h_hi, w_ref[0], preferred_element_type=F32)
              + jnp.dot(h_hi, w_ref[1], preferred_element_type=F32)
              + jnp.dot(h_lo, w_ref[0], preferred_element_type=F32)) + b_ref[...]
    lane = lax.broadcasted_iota(jnp.int32, (1, LANES), 1).astype(F32)
    big = 1e9
    ninf = -jnp.inf
    is_g = jnp.logical_and(lane >= MOE_E, lane < MOE_E + MOE_GROUPS)
    gl = jnp.where(is_g, logits, ninf)
    gmax = jnp.max(gl, axis=1, keepdims=True)
    g_val = 1.0 / jnp.sum(jnp.exp(gl - gmax), axis=1, keepdims=True)
    g_idx = jnp.min(jnp.where(gl == gmax, lane, big), axis=1, keepdims=True) - MOE_E
    lo = g_idx * MOE_EPG
    sel = jnp.logical_and(lane >= lo, lane < lo + MOE_EPG)
    el = jnp.where(sel, logits, ninf)
    e1 = jnp.max(el, axis=1, keepdims=True)
    esum = jnp.sum(jnp.exp(el - e1), axis=1, keepdims=True)
    i1 = jnp.min(jnp.where(el == e1, lane, big), axis=1, keepdims=True)
    el2 = jnp.where(lane == i1, ninf, el)
    e2 = jnp.max(el2, axis=1, keepdims=True)
    i2 = jnp.min(jnp.where(el2 == e2, lane, big), axis=1, keepdims=True)
    p1 = 1.0 / esum
    p2 = jnp.exp(e2 - e1) / esum
    w1 = g_val * p1 / (p1 + p2)
    w2 = g_val * p2 / (p1 + p2)
    gates = jnp.where(lane == i1, w1, 0.0) + jnp.where(lane == i2, w2, 0.0)
    marks = jnp.where(jnp.logical_or(lane == i1 + SEL_LANE, lane == i2 + SEL_LANE), 1.0, 0.0)
    o_ref[...] = gates + marks


def moe_router(s, g, mod, k_shift, k_scale, wr, br, nb):
    n, d = s.shape
    tm = _tile(n, PROJ_TM)
    return pl.pallas_call(
        _router_kernel,
        grid=(n // tm,),
        in_specs=[pl.BlockSpec((tm, d), lambda i: (i, 0)),
                  pl.BlockSpec((1, d), lambda i: (0, 0)),
                  pl.BlockSpec((1, 1, 1, d), lambda i: (_mod_row(i, tm, nb), k_shift, 0, 0)),
                  pl.BlockSpec((1, 1, 1, d), lambda i: (_mod_row(i, tm, nb), k_scale, 0, 0)),
                  pl.BlockSpec((2, d, LANES), lambda i: (0, 0, 0)),
                  pl.BlockSpec((1, LANES), lambda i: (0, 0))],
        out_specs=[pl.BlockSpec((tm, d), lambda i: (i, 0)),
                   pl.BlockSpec((tm, LANES), lambda i: (i, 0))],
        out_shape=[jax.ShapeDtypeStruct((n, d), F32),
                   jax.ShapeDtypeStruct((n, LANES), F32)],
        compiler_params=_cp(("parallel",)),
        name="moe_router",
    )(s, g.reshape(1, d), mod, mod, wr, br)


def _rank_kernel(route_ref, rank_ref, cnt_ref, carry_ref):
    i = pl.program_id(0)

    @pl.when(i == 0)
    def _():
        carry_ref[...] = jnp.zeros_like(carry_ref)

    tm = route_ref.shape[0]
    lane = lax.broadcasted_iota(jnp.int32, (1, LANES), 1)
    marks = jnp.where(lane >= SEL_LANE, route_ref[...], 0.0)
    rr = lax.broadcasted_iota(jnp.int32, (tm, tm), 0)
    cc = lax.broadcasted_iota(jnp.int32, (tm, tm), 1)
    below = jnp.where(cc < rr, 1.0, 0.0).astype(BF16)
    rank_ref[...] = jnp.dot(below, marks.astype(BF16), preferred_element_type=F32) + carry_ref[...]
    carry_ref[...] += jnp.sum(marks, axis=0, keepdims=True)
    cnt_ref[...] = carry_ref[...]


def moe_rank(route):
    n = route.shape[0]
    tm = _tile(n, PROJ_TM)
    return pl.pallas_call(
        _rank_kernel,
        grid=(n // tm,),
        in_specs=[pl.BlockSpec((tm, LANES), lambda i: (i, 0))],
        out_specs=[pl.BlockSpec((tm, LANES), lambda i: (i, 0)),
                   pl.BlockSpec((1, LANES), lambda i: (0, 0))],
        out_shape=[jax.ShapeDtypeStruct((n, LANES), F32),
                   jax.ShapeDtypeStruct((1, LANES), F32)],
        scratch_shapes=[pltpu.VMEM((1, LANES), F32)],
        compiler_params=_cp(("arbitrary",)),
        name="moe_rank",
    )(route)


def _pos_kernel(route_ref, rank_ref, off_ref, o_ref):
    lane = lax.broadcasted_iota(jnp.int32, (1, LANES), 1).astype(F32)
    route = route_ref[...]
    marked = jnp.logical_and(lane >= SEL_LANE, route > 0.5)
    p = rank_ref[...] + off_ref[...]
    lane_a = jnp.min(jnp.where(marked, lane, 1e9), axis=1, keepdims=True)
    lane_b = jnp.max(jnp.where(marked, lane, -1.0), axis=1, keepdims=True)

    def pick(src, at):
        return jnp.sum(jnp.where(lane == at, src, 0.0), axis=1, keepdims=True)

    out = jnp.where(lane == 0.0, pick(p, lane_a), 0.0)
    out = out + jnp.where(lane == 1.0, pick(p, lane_b), 0.0)
    out = out + jnp.where(lane == 2.0, pick(route, lane_a - SEL_LANE), 0.0)
    out = out + jnp.where(lane == 3.0, pick(route, lane_b - SEL_LANE), 0.0)
    o_ref[...] = out


def moe_positions(route, rank, off):
    n = route.shape[0]
    tm = _tile(n, PROJ_TM)
    blk = pl.BlockSpec((tm, LANES), lambda i: (i, 0))
    return pl.pallas_call(
        _pos_kernel,
        grid=(n // tm,),
        in_specs=[blk, blk, pl.BlockSpec((1, LANES), lambda i: (0, 0))],
        out_specs=blk,
        out_shape=jax.ShapeDtypeStruct((n, LANES), F32),
        compiler_params=_cp(("parallel",)),
        name="moe_positions",
    )(route, rank, off)


def _row_copy(src, src_row, dst, dst_row, sem):
    return pltpu.make_async_copy(src.at[pl.ds(src_row, 1)], dst.at[pl.ds(dst_row, 1)], sem)


def _dispatch_kernel(pos_ref, h_ref, xs_in_hbm, xs_hbm, sem):
    del xs_in_hbm

    def issue(r, carry):
        _row_copy(h_ref, r, xs_hbm, pos_ref[0, 0, r], sem).start()
        _row_copy(h_ref, r, xs_hbm, pos_ref[0, 0, CHUNK + r], sem).start()
        return carry

    lax.fori_loop(0, CHUNK, issue, 0, unroll=8)
    for _ in range(2):
        pltpu.make_async_copy(h_ref, xs_hbm.at[pl.ds(0, CHUNK)], sem).wait()


def moe_dispatch(pos, h, n_sorted):
    n, d = h.shape
    xs0 = jnp.zeros((n_sorted, d), h.dtype)
    return pl.pallas_call(
        _dispatch_kernel,
        grid=(n // CHUNK,),
        in_specs=[pl.BlockSpec((1, 1, 2 * CHUNK), lambda i: (i, 0, 0), memory_space=pltpu.SMEM),
                  pl.BlockSpec((CHUNK, d), lambda i: (i, 0)),
                  pl.BlockSpec(memory_space=pl.ANY)],
        out_specs=pl.BlockSpec(memory_space=pl.ANY),
        out_shape=jax.ShapeDtypeStruct((n_sorted, d), h.dtype),
        scratch_shapes=[pltpu.SemaphoreType.DMA],
        input_output_aliases={2: 0},
        compiler_params=_cp(("arbitrary",)),
        name="moe_dispatch",
    )(pos, h, xs0)


def _experts_kernel(te_ref, nt_ref, x_ref, w1_ref, w3_ref, w2_ref, o_ref):
    del te_ref
    active = pl.program_id(0) < nt_ref[0]

    @pl.when(active)
    def _():
        x = x_ref[...].astype(BF16)
        a = jnp.dot(x, w1_ref[0], preferred_element_type=F32)
        b = jnp.dot(x, w3_ref[0], preferred_element_type=F32)
        he = (_silu(a) * b).astype(BF16)
        o_ref[...] = jnp.dot(he, w2_ref[0], preferred_element_type=F32)

    @pl.when(jnp.logical_not(active))
    def _():
        o_ref[...] = jnp.zeros_like(o_ref)


def moe_experts(tile_expert, n_tiles, xs, w1, w3, w2):
    n_sorted, d = xs.shape
    return pl.pallas_call(
        _experts_kernel,
        grid_spec=pltpu.PrefetchScalarGridSpec(
            num_scalar_prefetch=2,
            grid=(n_sorted // MOE_TM,),
            in_specs=[pl.BlockSpec((MOE_TM, d), lambda i, te, nt: (jnp.minimum(i, nt[0] - 1), 0)),
                      pl.BlockSpec((1, d, MOE_HIDDEN), lambda i, te, nt: (te[i], 0, 0)),
                      pl.BlockSpec((1, d, MOE_HIDDEN), lambda i, te, nt: (te[i], 0, 0)),
                      pl.BlockSpec((1, MOE_HIDDEN, d), lambda i, te, nt: (te[i], 0, 0))],
            out_specs=pl.BlockSpec((MOE_TM, d), lambda i, te, nt: (i, 0))),
        out_shape=jax.ShapeDtypeStruct((n_sorted, d), F32),
        compiler_params=_cp(("arbitrary",)),
        name="moe_experts",
    )(tile_expert, n_tiles, xs, w1, w3, w2)


def _combine_kernel(pos_ref, meta_ref, s_ref, g_ref, ys_hbm, o_ref, buf_ref, sem):
    def issue(r, carry):
        _row_copy(ys_hbm, pos_ref[0, 0, r], buf_ref.at[0], r, sem).start()
        _row_copy(ys_hbm, pos_ref[0, 0, CHUNK + r], buf_ref.at[1], r, sem).start()
        return carry

    lax.fori_loop(0, CHUNK, issue, 0, unroll=8)
    for slot in range(2):
        pltpu.make_async_copy(ys_hbm.at[pl.ds(0, CHUNK)], buf_ref.at[slot], sem).wait()
    meta = meta_ref[...]
    y = meta[:, 2:3] * buf_ref[0] + meta[:, 3:4] * buf_ref[1]
    o_ref[...] = s_ref[...] + g_ref[0, 0] * y


def moe_combine(pos, meta, s, mod, k_gate, ys, nb):
    n, d = s.shape
    tm = CHUNK
    return pl.pallas_call(
        _combine_kernel,
        grid=(n // tm,),
        in_specs=[pl.BlockSpec((1, 1, 2 * CHUNK), lambda i: (i, 0, 0), memory_space=pltpu.SMEM),
                  pl.BlockSpec((tm, LANES), lambda i: (i, 0)),
                  pl.BlockSpec((tm, d), lambda i: (i, 0)),
                  pl.BlockSpec((1, 1, 1, d), lambda i: (_mod_row(i, tm, nb), k_gate, 0, 0)),
                  pl.BlockSpec(memory_space=pl.ANY)],
        out_specs=pl.BlockSpec((tm, d), lambda i: (i, 0)),
        out_shape=jax.ShapeDtypeStruct((n, d), F32),
        scratch_shapes=[pltpu.VMEM((2, tm, d), F32), pltpu.SemaphoreType.DMA],
        input_output_aliases={2: 0},
        compiler_params=_cp(("arbitrary",)),
        name="moe_combine",
    )(pos, meta, s, mod, ys)


def _na_layer(h, w_qkv, w_o, rpb, s, mod, nb):
    qkv = matmul(h, w_qkv.astype(BF16), BF16)
    o = na_attention(qkv, _na_bias_tables(rpb), nb)
    return matmul_residual(o, w_o.astype(BF16), s, mod, 2, nb)


def _da_layer(h, w_qkv, lam, subln_g, w_o, lambda_init, s, mod, nb):
    qkv = matmul_rope(h, w_qkv.astype(BF16), nb)
    o = da_attention(qkv, lam, subln_g, lambda_init, nb)
    return matmul_residual(o, w_o.astype(BF16), s, mod, 2, nb)


def _ml_layer(h, w_up, conv_w, conv_b, w_q, w_k, w_v, w_gate, b_gate, gn_w, skip, w_down, s, mod, nb):
    up = matmul(h, w_up.astype(BF16), BF16)
    xc, q, k, v = ml_conv_qkv(up, conv_w, conv_b, _block_diag(w_q), _block_diag(w_k), _block_diag(w_v), nb)
    ng = w_gate.shape[1]
    wg = jnp.pad(w_gate, ((0, 0), (0, LANES - ng))).reshape(3, ML_INNER, LANES).astype(BF16)
    bg = jnp.pad(b_gate.astype(F32), (0, LANES - ng)).reshape(1, LANES)
    g = ml_gates(q, k, v, wg, bg)
    gt = g[:, :ng].T
    hs = ml_scan(q, k, v, g, gt, nb)
    a = ml_finish(hs, xc, up, gn_w, skip)
    return matmul_residual(a, w_down.astype(BF16), s, mod, 2, nb)


def _moe_layer(norm_g, w_group, b_group, w_router, b_router, w1, w3, w2, s, mod, nb):
    d = s.shape[1]
    pad = LANES - MOE_E - MOE_GROUPS
    wr = jnp.concatenate([w_router, w_group, jnp.zeros((d, pad), w_router.dtype)], axis=1).astype(F32)
    wr_hi = wr.astype(BF16)
    wr_lo = (wr - wr_hi.astype(F32)).astype(BF16)
    br = jnp.concatenate([b_router, b_group, jnp.zeros((pad,), b_router.dtype)]).astype(F32).reshape(1, LANES)
    h, route = moe_router(s, norm_g.astype(F32), mod, 3, 4, jnp.stack([wr_hi, wr_lo]), br, nb)
    rank, cnt = moe_rank(route)
    n = s.shape[0]
    cnt_e = cnt[0, SEL_LANE:SEL_LANE + MOE_E].astype(jnp.int32)
    gsz = ((cnt_e + MOE_TM - 1) // MOE_TM) * MOE_TM
    ends = jnp.cumsum(gsz)
    off_row = jnp.zeros((1, LANES), F32).at[0, SEL_LANE:SEL_LANE + MOE_E].set((ends - gsz).astype(F32))
    meta = moe_positions(route, rank, off_row)
    n_sorted = 2 * n + MOE_E * MOE_TM
    n_tiles = (ends[-1:] // MOE_TM).astype(jnp.int32)
    tile_ids = jnp.arange(n_sorted // MOE_TM, dtype=jnp.int32)
    tile_expert = jnp.sum((tile_ids[:, None] >= (ends // MOE_TM)[None, :]).astype(jnp.int32), axis=1)
    tile_expert = jnp.minimum(tile_expert, MOE_E - 1)
    pos = meta[:, :2].astype(jnp.int32).reshape(n // CHUNK, CHUNK, 2)
    pos = pos.transpose(0, 2, 1).reshape(n // CHUNK, 1, 2 * CHUNK)
    xs = moe_dispatch(pos, h, n_sorted)
    ys = moe_experts(tile_expert, n_tiles, xs, w1.astype(BF16), w3.astype(BF16), w2.astype(BF16))
    return moe_combine(pos, meta, s, mod, 5, ys, nb)


def kernel(x, c, ctx, c_ctx, mod_w, mod_b, norm_g, final_g, na_w_qkv, na_w_o, na_rpb, ml_w_up, ml_conv_w, ml_conv_b, ml_w_q, ml_w_k, ml_w_v, ml_w_gate, ml_b_gate, ml_gn_w, ml_skip, ml_w_down, da_w_qkv, da_lambda, da_subln_g, da_w_o, moe_w_group, moe_b_group, moe_w_router, moe_b_router, moe_w1, moe_w3, moe_w2):
    nb, seq, d = x.shape
    assert (seq, d, ctx.shape[1]) == (SEQ, D_MODEL, CTX_LEN) and nb < 16
    depth = mod_w.shape[0]
    nx = nb * seq
    s = jnp.concatenate([x.reshape(nx, d), ctx.reshape(nb * CTX_LEN, d)], axis=0).astype(F32)
    cvec = jnp.concatenate([c, c_ctx[None, :], jnp.zeros((16 - nb - 1, d), c.dtype)], axis=0).astype(F32)
    mods = mod_vectors(cvec, mod_w, mod_b).reshape(depth, 16, 6, 1, d)

    for i in range(depth):
        kind, j = i % N_MIXERS, i // N_MIXERS
        mod = mods[i]
        h = norm_mod(s, norm_g[i, 0].astype(F32), mod, 0, 1, nb)
        if kind == 0:
            s = _na_layer(h, na_w_qkv[j], na_w_o[j], na_rpb[j], s, mod, nb)
        elif kind == 1:
            s = _ml_layer(h, ml_w_up[j], ml_conv_w[j], ml_conv_b[j], ml_w_q[j], ml_w_k[j], ml_w_v[j],
                          ml_w_gate[j], ml_b_gate[j], ml_gn_w[j], ml_skip[j], ml_w_down[j], s, mod, nb)
        else:
            lambda_init = 0.8 - 0.6 * math.exp(-0.3 * i)
            s = _da_layer(h, da_w_qkv[j], da_lambda[j], da_subln_g[j], da_w_o[j], lambda_init, s, mod, nb)
        s = _moe_layer(norm_g[i, 1], moe_w_group[i], moe_b_group[i], moe_w_router[i], moe_b_router[i],
                       moe_w1[i], moe_w3[i], moe_w2[i], s, mod, nb)
    return final_norm(s, final_g.astype(F32), nx).reshape(nb, seq, d)
```

```python
import functools
import math

import numpy as np
import jax
import jax.numpy as jnp
from jax import lax
from jax.experimental import pallas as pl
from jax.experimental.pallas import tpu as pltpu

F32 = jnp.float32
BF16 = jnp.bfloat16

D_MODEL = 1024
SEQ = 4096
CTX_LEN = 256
GRID_W = 64
N_MIXERS = 3
EPS = 1e-6

NA_HEADS = 16
NA_WIN_H = 8
NA_WIN_W = 16

ML_HEADS = 4
ML_INNER = 2 * D_MODEL
ML_HEAD_DIM = ML_INNER // ML_HEADS
ML_BLOCK = 4
ML_CONV_K = 5

DA_HEADS = 8
DA_HEAD_DIM = 64
ROPE_BASE = 10000.0

MOE_GROUPS = 4
MOE_EPG = 8
MOE_E = MOE_GROUPS * MOE_EPG
MOE_HIDDEN = 512
MOE_TM = 256
SEL_LANE = 64

LANES = 128
CHUNK = 256
X_CHUNKS = SEQ // CHUNK
PROJ_TM = 512
DA_TQ = 512
NA_ROWS = 8
NEG = -1e30
VMEM_LIMIT = 56 * 1024 * 1024


def _cp(sem, vmem=VMEM_LIMIT):
    return pltpu.CompilerParams(dimension_semantics=sem, vmem_limit_bytes=vmem)


def _tile(n, pref):
    tm = pref
    while n % tm:
        tm //= 2
    assert tm >= CHUNK
    return tm


def _mod_row(i, tm, nb):
    return jnp.where(i < nb * (SEQ // tm), i // (SEQ // tm), nb)


def _nt(a, b):
    return lax.dot_general(a, b, (((1,), (1,)), ((), ())), preferred_element_type=F32)


def _tn(a, b):
    return lax.dot_general(a, b, (((0,), (0,)), ((), ())), preferred_element_type=F32)


def _silu(x):
    return x * jax.nn.sigmoid(x)


def _head_rows(q, lane):
    zero = jnp.zeros_like(q)
    return jnp.concatenate([jnp.where(lane < 64, q, zero), jnp.where(lane >= 64, q, zero)], axis=0)


def _mod_kernel(c_ref, w_ref, b_ref, o_ref):
    a = _silu(c_ref[...]).astype(BF16)
    o_ref[0] = jnp.dot(a, w_ref[0].astype(BF16), preferred_element_type=F32) + b_ref[0]


def mod_vectors(cvec, mod_w, mod_b):
    depth, d, n6 = mod_w.shape
    tn = 1024
    return pl.pallas_call(
        _mod_kernel,
        grid=(depth, n6 // tn),
        in_specs=[pl.BlockSpec((16, d), lambda l, j: (0, 0)),
                  pl.BlockSpec((1, d, tn), lambda l, j: (l, 0, j)),
                  pl.BlockSpec((1, 1, tn), lambda l, j: (l, 0, j))],
        out_specs=pl.BlockSpec((1, 16, tn), lambda l, j: (l, 0, j)),
        out_shape=jax.ShapeDtypeStruct((depth, 16, n6), F32),
        compiler_params=_cp(("parallel", "parallel")),
        name="mod_vectors",
    )(cvec, mod_w, mod_b.reshape(depth, 1, n6))


def _norm_mod_kernel(x_ref, g_ref, sh_ref, sc_ref, o_ref):
    x = x_ref[...]
    y = x * lax.rsqrt(jnp.mean(x * x, axis=-1, keepdims=True) + EPS) * g_ref[...]
    o_ref[...] = (y * (1.0 + sc_ref[0, 0]) + sh_ref[0, 0]).astype(o_ref.dtype)


def norm_mod(s, g, mod, k_shift, k_scale, nb):
    n, d = s.shape
    tm = _tile(n, PROJ_TM)
    return pl.pallas_call(
        _norm_mod_kernel,
        grid=(n // tm,),
        in_specs=[pl.BlockSpec((tm, d), lambda i: (i, 0)),
                  pl.BlockSpec((1, d), lambda i: (0, 0)),
                  pl.BlockSpec((1, 1, 1, d), lambda i: (_mod_row(i, tm, nb), k_shift, 0, 0)),
                  pl.BlockSpec((1, 1, 1, d), lambda i: (_mod_row(i, tm, nb), k_scale, 0, 0))],
        out_specs=pl.BlockSpec((tm, d), lambda i: (i, 0)),
        out_shape=jax.ShapeDtypeStruct((n, d), BF16),
        compiler_params=_cp(("parallel",)),
        name="norm_mod",
    )(s, g.reshape(1, d), mod, mod)


def _final_norm_kernel(x_ref, g_ref, o_ref):
    x = x_ref[...]
    o_ref[...] = x * lax.rsqrt(jnp.mean(x * x, axis=-1, keepdims=True) + EPS) * g_ref[...]


def final_norm(s, g, n_rows):
    d = s.shape[1]
    tm = _tile(n_rows, PROJ_TM)
    return pl.pallas_call(
        _final_norm_kernel,
        grid=(n_rows // tm,),
        in_specs=[pl.BlockSpec((tm, d), lambda i: (i, 0)),
                  pl.BlockSpec((1, d), lambda i: (0, 0))],
        out_specs=pl.BlockSpec((tm, d), lambda i: (i, 0)),
        out_shape=jax.ShapeDtypeStruct((n_rows, d), F32),
        compiler_params=_cp(("parallel",)),
        name="final_norm",
    )(s, g.reshape(1, d))


def _mm_kernel(a_ref, w_ref, o_ref):
    o_ref[...] = jnp.dot(a_ref[...], w_ref[...], preferred_element_type=F32).astype(o_ref.dtype)


def matmul(a, w, out_dtype, tn=1024):
    n, k = a.shape
    tm = _tile(n, PROJ_TM)
    nout = w.shape[1]
    tn = min(tn, nout)
    return pl.pallas_call(
        _mm_kernel,
        grid=(nout // tn, n // tm),
        in_specs=[pl.BlockSpec((tm, k), lambda j, i: (i, 0)),
                  pl.BlockSpec((k, tn), lambda j, i: (0, j))],
        out_specs=pl.BlockSpec((tm, tn), lambda j, i: (i, j)),
        out_shape=jax.ShapeDtypeStruct((n, nout), out_dtype),
        compiler_params=_cp(("parallel", "parallel")),
        name="matmul",
    )(a, w)


def _mm_res_kernel(a_ref, w_ref, r_ref, g_ref, o_ref):
    acc = jnp.dot(a_ref[...], w_ref[...], preferred_element_type=F32)
    o_ref[...] = r_ref[...] + g_ref[0, 0] * acc


def matmul_residual(a, w, s, mod, k_gate, nb):
    n, k = a.shape
    d = w.shape[1]
    tm = _tile(n, PROJ_TM)
    return pl.pallas_call(
        _mm_res_kernel,
        grid=(n // tm,),
        in_specs=[pl.BlockSpec((tm, k), lambda i: (i, 0)),
                  pl.BlockSpec((k, d), lambda i: (0, 0)),
                  pl.BlockSpec((tm, d), lambda i: (i, 0)),
                  pl.BlockSpec((1, 1, 1, d), lambda i: (_mod_row(i, tm, nb), k_gate, 0, 0))],
        out_specs=pl.BlockSpec((tm, d), lambda i: (i, 0)),
        out_shape=jax.ShapeDtypeStruct((n, d), F32),
        input_output_aliases={2: 0},
        compiler_params=_cp(("parallel",)),
        name="matmul_residual",
    )(a, w, s, mod)


def _na_bias_tables(rpb):
    o = np.arange(NA_WIN_H)[:, None]
    j = np.arange(NA_WIN_H)[None, :]
    dy = j - o + NA_WIN_H - 1
    qc = np.arange(GRID_W)[:, None]
    kc = np.arange(GRID_W)[None, :]
    dx = np.clip(kc - qc, -(NA_WIN_W - 1), NA_WIN_W - 1) + NA_WIN_W - 1
    w_start = np.clip(qc - NA_WIN_W // 2, 0, GRID_W - NA_WIN_W)
    valid = (kc >= w_start) & (kc < w_start + NA_WIN_W)
    tbl = rpb.astype(F32)[:, dy][:, :, :, dx]
    tbl = jnp.where(jnp.asarray(valid)[None, None, None], tbl, NEG)
    tbl = tbl.reshape(NA_HEADS // 2, 2, NA_WIN_H, NA_WIN_H, GRID_W, GRID_W)
    tbl = tbl.transpose(2, 0, 3, 5, 1, 4)
    return tbl.reshape(NA_WIN_H, NA_HEADS // 2, NA_WIN_H * GRID_W, 2 * GRID_W)


def _softmax_cols(s):
    m = jnp.max(s, axis=0, keepdims=True)
    p = jnp.exp(s - m)
    return (p * (1.0 / jnp.sum(p, axis=0, keepdims=True))).astype(BF16)


def _na_x_kernel(q_ref, kx_ref, vx_ref, kc_ref, vc_ref, bias_ref, o_ref, s_sc, p_sc):
    t = pl.program_id(2)
    lane = lax.broadcasted_iota(jnp.int32, (1, LANES), 1)
    nwin = NA_WIN_H * GRID_W
    starts = []
    for i in range(NA_ROWS):
        r = t * NA_ROWS + i
        rs = jnp.clip(r - NA_WIN_H // 2, 0, SEQ // GRID_W - NA_WIN_H)
        start = pl.multiple_of(rs * GRID_W, GRID_W)
        starts.append(start)
        qq = _head_rows(q_ref[i * GRID_W:(i + 1) * GRID_W, :] * 0.125, lane)
        s_sc[i, 0:nwin, :] = _nt(kx_ref[pl.ds(start, nwin), :], qq) + bias_ref[r - rs, 0]
        s_sc[i, nwin:, :] = _nt(kc_ref[...], qq)
    for i in range(NA_ROWS):
        p_sc[i] = _softmax_cols(s_sc[i])
    for i in range(NA_ROWS):
        o = (_tn(p_sc[i, 0:nwin, :], vx_ref[pl.ds(starts[i], nwin), :])
             + _tn(p_sc[i, nwin:, :], vc_ref[...]))
        o = jnp.where(lane < 64, o[0:GRID_W], o[GRID_W:])
        o_ref[i * GRID_W:(i + 1) * GRID_W, :] = o.astype(o_ref.dtype)


def _na_ctx_kernel(q_ref, kc_ref, vc_ref, o_in_hbm, o_ref):
    del o_in_hbm
    lane = lax.broadcasted_iota(jnp.int32, (1, LANES), 1)
    q = q_ref[...] * 0.125
    outs = []
    for hh in range(2):
        qm = jnp.where((lane // 64) == hh, q, jnp.zeros_like(q))
        p = _softmax_cols(_nt(kc_ref[...], qm))
        outs.append(_tn(p, vc_ref[...]))
    o_ref[...] = jnp.where(lane < 64, outs[0], outs[1]).astype(o_ref.dtype)


def na_attention(qkv, bias, nb):
    n = qkv.shape[0]
    d = D_MODEL
    ncb = d // LANES
    xc = nb * X_CHUNKS
    nkeys = NA_WIN_H * GRID_W + CTX_LEN
    tq = NA_ROWS * GRID_W
    steps = SEQ // tq
    o = pl.pallas_call(
        _na_x_kernel,
        grid=(nb, ncb, steps),
        in_specs=[pl.BlockSpec((tq, LANES), lambda b, h, t: (b * steps + t, h)),
                  pl.BlockSpec((SEQ, LANES), lambda b, h, t: (b, ncb + h)),
                  pl.BlockSpec((SEQ, LANES), lambda b, h, t: (b, 2 * ncb + h)),
                  pl.BlockSpec((CHUNK, LANES), lambda b, h, t: (xc + b, ncb + h)),
                  pl.BlockSpec((CHUNK, LANES), lambda b, h, t: (xc + b, 2 * ncb + h)),
                  pl.BlockSpec((NA_WIN_H, 1, NA_WIN_H * GRID_W, LANES), lambda b, h, t: (0, h, 0, 0))],
        out_specs=pl.BlockSpec((tq, LANES), lambda b, h, t: (b * steps + t, h)),
        out_shape=jax.ShapeDtypeStruct((n, d), BF16),
        scratch_shapes=[pltpu.VMEM((NA_ROWS, nkeys, LANES), F32),
                        pltpu.VMEM((NA_ROWS, nkeys, LANES), BF16)],
        compiler_params=_cp(("parallel", "parallel", "arbitrary")),
        name="na_attention_x",
    )(qkv, qkv, qkv, qkv, qkv, bias)
    return pl.pallas_call(
        _na_ctx_kernel,
        grid=(nb, ncb),
        in_specs=[pl.BlockSpec((CHUNK, LANES), lambda b, h: (xc + b, h)),
                  pl.BlockSpec((CHUNK, LANES), lambda b, h: (xc + b, ncb + h)),
                  pl.BlockSpec((CHUNK, LANES), lambda b, h: (xc + b, 2 * ncb + h)),
                  pl.BlockSpec(memory_space=pl.ANY)],
        out_specs=pl.BlockSpec((CHUNK, LANES), lambda b, h: (xc + b, h)),
        out_shape=jax.ShapeDtypeStruct((n, d), BF16),
        input_output_aliases={3: 0},
        compiler_params=_cp(("parallel", "parallel")),
        name="na_attention_ctx",
    )(qkv, qkv, qkv, o)


def _rope_tables(tm):
    half = DA_HEAD_DIM // 2
    freqs = ROPE_BASE ** (-np.arange(0, half, 2, dtype=np.float32) / half)
    t = np.arange(SEQ)
    row, col = t // GRID_W, t % GRID_W
    lane = np.arange(LANES)
    l64 = lane % DA_HEAD_DIM
    use_col = (l64 // half) == 1
    l32 = l64 % half
    fi = l32 % (half // 2)
    second = l32 >= half // 2
    pos = np.where(use_col[None, :], col[:, None], row[:, None]).astype(np.float32)
    ang = pos * freqs[fi][None, :]
    cos = np.cos(ang).astype(np.float32)
    sin = np.sin(ang).astype(np.float32)
    sa = np.where(second[None, :], 0.0, -sin).astype(np.float32)
    sb = np.where(second[None, :], sin, 0.0).astype(np.float32)
    ident = np.ones((tm, LANES), np.float32)
    zero = np.zeros((tm, LANES), np.float32)
    return (jnp.asarray(np.concatenate([cos, ident])),
            jnp.asarray(np.concatenate([sa, zero])),
            jnp.asarray(np.concatenate([sb, zero])))


def _mm_rope_kernel(a_ref, w_ref, c_ref, sa_ref, sb_ref, o_ref):
    j = pl.program_id(0)
    acc = jnp.dot(a_ref[...], w_ref[...], preferred_element_type=F32)

    @pl.when(j < 2)
    def _():
        scale = jnp.where(j == 0, DA_HEAD_DIM ** -0.5 * math.log2(math.e), 1.0)
        cos = c_ref[...] * scale
        sa = sa_ref[...] * scale
        sb = sb_ref[...] * scale
        for g in range(acc.shape[1] // LANES):
            sl = slice(g * LANES, (g + 1) * LANES)
            x = acc[:, sl]
            y = x * cos + pltpu.roll(x, LANES - 16, 1) * sa + pltpu.roll(x, 16, 1) * sb
            o_ref[:, sl] = y.astype(o_ref.dtype)

    @pl.when(j >= 2)
    def _():
        o_ref[...] = acc.astype(o_ref.dtype)


def matmul_rope(a, w, nb):
    n, k = a.shape
    d = D_MODEL
    tm = _tile(n, PROJ_TM)
    cos, sa, sb = _rope_tables(tm)
    xt = nb * (SEQ // tm)

    def tab(j, i):
        return (jnp.where(i < xt, i % (SEQ // tm), SEQ // tm), 0)

    tspec = pl.BlockSpec((tm, LANES), tab)
    return pl.pallas_call(
        _mm_rope_kernel,
        grid=(3, n // tm),
        in_specs=[pl.BlockSpec((tm, k), lambda j, i: (i, 0)),
                  pl.BlockSpec((k, d), lambda j, i: (0, j)),
                  tspec, tspec, tspec],
        out_specs=pl.BlockSpec((tm, d), lambda j, i: (i, j)),
        out_shape=jax.ShapeDtypeStruct((n, 3 * d), BF16),
        compiler_params=_cp(("parallel", "parallel")),
        name="matmul_rope",
    )(a, w, cos, sa, sb)


def _da_body(q_ref, lam_ref, g_ref, o_ref, s_sc, acc_sc, m_sc, l_sc, lambda_init, run_chunks):
    tq = q_ref.shape[0]
    lane = lax.broadcasted_iota(jnp.int32, (1, LANES), 1)
    qq = _head_rows(q_ref[...], lane)
    m_sc[...] = jnp.full_like(m_sc, NEG)
    l_sc[...] = jnp.zeros_like(l_sc)
    acc_sc[...] = jnp.zeros_like(acc_sc)

    def scores(slot, k):
        s_sc[slot] = _nt(k, qq)

    def fold(slot, vt):
        for g in range(2 * tq // CHUNK):
            sl = slice(g * CHUNK, (g + 1) * CHUNK)
            s = s_sc[slot, :, sl]
            m_old = m_sc[:, sl]
            m_new = jnp.maximum(m_old, jnp.max(s, axis=0, keepdims=True))
            alpha = jnp.exp2(m_old - m_new)
            p = jnp.exp2(s - m_new)
            l_sc[:, sl] = alpha * l_sc[:, sl] + jnp.sum(p, axis=0, keepdims=True)
            m_sc[:, sl] = m_new
            acc_sc[:, sl] = acc_sc[:, sl] * alpha + jnp.dot(vt, p.astype(BF16), preferred_element_type=F32)

    run_chunks(scores, fold)

    ot = acc_sc[...] / l_sc[...]
    lam = lam_ref[...]
    lam_full = (jnp.exp(jnp.sum(lam[0:1] * lam[1:2], axis=1, keepdims=True))
                - jnp.exp(jnp.sum(lam[2:3] * lam[3:4], axis=1, keepdims=True)) + lambda_init)
    od = ot[:, :tq] - lam_full * ot[:, tq:]
    y = od * lax.rsqrt(jnp.mean(od * od, axis=0, keepdims=True) + EPS) * g_ref[...]
    o_ref[...] = (y * (1.0 - lambda_init)).T.astype(o_ref.dtype)


def _da_x_kernel(q_ref, kx_ref, kc_ref, vtx_ref, vtc_ref, lam_ref, g_ref, o_ref,
                 s_sc, acc_sc, m_sc, l_sc, *, lambda_init):
    def kx(c):
        return kx_ref[pl.ds(pl.multiple_of(c * CHUNK, CHUNK), CHUNK), :]

    def run_chunks(scores, fold):
        scores(0, kx(0))

        def body(i, carry):
            scores(1, kx(2 * i + 1))
            fold(0, vtx_ref[0, 2 * i])
            scores(0, kx(2 * i + 2))
            fold(1, vtx_ref[0, 2 * i + 1])
            return carry

        lax.fori_loop(0, X_CHUNKS // 2 - 1, body, 0)
        scores(1, kx(X_CHUNKS - 1))
        fold(0, vtx_ref[0, X_CHUNKS - 2])
        scores(0, kc_ref[...])
        fold(1, vtx_ref[0, X_CHUNKS - 1])
        fold(0, vtc_ref[0, 0])

    _da_body(q_ref, lam_ref, g_ref, o_ref, s_sc, acc_sc, m_sc, l_sc, lambda_init, run_chunks)


def _da_ctx_kernel(q_ref, kc_ref, vtc_ref, lam_ref, g_ref, o_in_hbm, o_ref,
                   s_sc, acc_sc, m_sc, l_sc, *, lambda_init):
    del o_in_hbm

    def run_chunks(scores, fold):
        scores(0, kc_ref[...])
        fold(0, vtc_ref[0, 0])

    _da_body(q_ref, lam_ref, g_ref, o_ref, s_sc, acc_sc, m_sc, l_sc, lambda_init, run_chunks)


def _da_scratch(tq):
    return [pltpu.VMEM((2, CHUNK, 2 * tq), F32), pltpu.VMEM((LANES, 2 * tq), F32),
            pltpu.VMEM((1, 2 * tq), F32), pltpu.VMEM((1, 2 * tq), F32)]


def da_attention(qkv, lam, subln_g, lambda_init, nb):
    n = qkv.shape[0]
    d = D_MODEL
    ncb = d // LANES
    xc = nb * X_CHUNKS
    tq = DA_TQ
    vt = qkv[:, 2 * d:].reshape(n // CHUNK, CHUNK, ncb, LANES).transpose(2, 0, 3, 1)
    lam = lam.astype(F32)
    gcol = subln_g.astype(F32).reshape(LANES, 1)
    o = pl.pallas_call(
        functools.partial(_da_x_kernel, lambda_init=lambda_init),
        grid=(nb, ncb, SEQ // tq),
        in_specs=[pl.BlockSpec((tq, LANES), lambda b, h, t: (b * (SEQ // tq) + t, h)),
                  pl.BlockSpec((SEQ, LANES), lambda b, h, t: (b, ncb + h)),
                  pl.BlockSpec((CHUNK, LANES), lambda b, h, t: (xc + b, ncb + h)),
                  pl.BlockSpec((1, X_CHUNKS, LANES, CHUNK), lambda b, h, t: (h, b, 0, 0)),
                  pl.BlockSpec((1, 1, LANES, CHUNK), lambda b, h, t: (h, xc + b, 0, 0)),
                  pl.BlockSpec((4, DA_HEAD_DIM), lambda b, h, t: (0, 0)),
                  pl.BlockSpec((LANES, 1), lambda b, h, t: (0, 0))],
        out_specs=pl.BlockSpec((tq, LANES), lambda b, h, t: (b * (SEQ // tq) + t, h)),
        out_shape=jax.ShapeDtypeStruct((n, d), BF16),
        scratch_shapes=_da_scratch(tq),
        compiler_params=_cp(("parallel", "parallel", "arbitrary")),
        name="da_attention_x",
    )(qkv, qkv, qkv, vt, vt, lam, gcol)
    return pl.pallas_call(
        functools.partial(_da_ctx_kernel, lambda_init=lambda_init),
        grid=(nb, ncb),
        in_specs=[pl.BlockSpec((CHUNK, LANES), lambda b, h: (xc + b, h)),
                  pl.BlockSpec((CHUNK, LANES), lambda b, h: (xc + b, ncb + h)),
                  pl.BlockSpec((1, 1, LANES, CHUNK), lambda b, h: (h, xc + b, 0, 0)),
                  pl.BlockSpec((4, DA_HEAD_DIM), lambda b, h: (0, 0)),
                  pl.BlockSpec((LANES, 1), lambda b, h: (0, 0)),
                  pl.BlockSpec(memory_space=pl.ANY)],
        out_specs=pl.BlockSpec((CHUNK, LANES), lambda b, h: (xc + b, h)),
        out_shape=jax.ShapeDtypeStruct((n, d), BF16),
        scratch_shapes=_da_scratch(CHUNK),
        input_output_aliases={5: 0},
        compiler_params=_cp(("parallel", "parallel")),
        name="da_attention_ctx",
    )(qkv, qkv, vt, lam, gcol, o)


def _block_diag(w):
    per = LANES // ML_BLOCK
    wr = w.reshape(ML_INNER // LANES, per, ML_BLOCK, ML_BLOCK)
    eye = jnp.eye(per, dtype=w.dtype)
    return jnp.einsum('cgio,gh->cgiho', wr, eye).reshape(ML_INNER // LANES, LANES, LANES).astype(BF16)


def _ml_conv_kernel(p_ref, c_ref, n_ref, cw_ref, cb_ref, wq_ref, wk_ref, wv_ref,
                    xc_ref, q_ref, k_ref, v_ref, *, n_xchunks):
    i = pl.program_id(0)
    is_x = i < n_xchunks
    j = i % X_CHUNKS
    halo = 16
    cur = c_ref[...].astype(F32)
    prev = p_ref[CHUNK - halo:CHUNK, :].astype(F32)
    nxt = n_ref[0:halo, :].astype(F32)
    prev = jnp.where(jnp.logical_and(is_x, j > 0), prev, jnp.zeros_like(prev))
    nxt = jnp.where(jnp.logical_and(is_x, j < X_CHUNKS - 1), nxt, jnp.zeros_like(nxt))
    xp = jnp.concatenate([prev, cur, nxt], axis=0)
    rows = CHUNK + 2 * halo
    y = cb_ref[...] + cw_ref[ML_CONV_K // 2:ML_CONV_K // 2 + 1, :] * cur
    for tap in range(ML_CONV_K):
        dlt = tap - ML_CONV_K // 2
        if dlt == 0:
            continue
        shifted = pltpu.roll(xp, (-dlt) % rows, 0)[halo:halo + CHUNK]
        y = y + cw_ref[tap:tap + 1, :] * shifted
    xcb = _silu(y).astype(BF16)
    xc_ref[...] = xcb
    xm = c_ref[...]
    for s in range(xcb.shape[1] // LANES):
        sl = slice(s * LANES, (s + 1) * LANES)
        q_ref[:, sl] = jnp.dot(xcb[:, sl], wq_ref[s], preferred_element_type=F32).astype(BF16)
        k_ref[:, sl] = jnp.dot(xcb[:, sl], wk_ref[s], preferred_element_type=F32).astype(BF16)
        v_ref[:, sl] = jnp.dot(xm[:, sl], wv_ref[s], preferred_element_type=F32).astype(BF16)


def ml_conv_qkv(up, conv_w, conv_b, wq, wk, wv, nb):
    n = up.shape[0]
    nchunks = n // CHUNK
    cw = 512
    ncb = ML_INNER // cw
    sub = cw // LANES
    blk = pl.BlockSpec((CHUNK, cw), lambda i, c: (i, c))
    wspec = pl.BlockSpec((sub, LANES, LANES), lambda i, c: (c, 0, 0))
    out = jax.ShapeDtypeStruct((n, ML_INNER), BF16)
    return pl.pallas_call(
        functools.partial(_ml_conv_kernel, n_xchunks=nb * X_CHUNKS),
        grid=(nchunks, ncb),
        in_specs=[pl.BlockSpec((CHUNK, cw), lambda i, c: (jnp.maximum(i - 1, 0), c)),
                  blk,
                  pl.BlockSpec((CHUNK, cw), lambda i, c: (jnp.minimum(i + 1, nchunks - 1), c)),
                  pl.BlockSpec((ML_CONV_K, cw), lambda i, c: (0, c)),
                  pl.BlockSpec((1, cw), lambda i, c: (0, c)),
                  wspec, wspec, wspec],
        out_specs=[blk, blk, blk, blk],
        out_shape=[out, out, out, out],
        compiler_params=_cp(("parallel", "parallel")),
        name="ml_conv_qkv",
    )(up, up, up, conv_w.astype(F32), conv_b.astype(F32).reshape(1, ML_INNER), wq, wk, wv)


def _ml_gate_kernel(q_ref, k_ref, v_ref, w_ref, b_ref, o_ref):
    acc = jnp.dot(q_ref[...], w_ref[0], preferred_element_type=F32)
    acc = acc + jnp.dot(k_ref[...], w_ref[1], preferred_element_type=F32)
    acc = acc + jnp.dot(v_ref[...], w_ref[2], preferred_element_type=F32)
    o_ref[...] = acc + b_ref[...]


def ml_gates(q, k, v, wg, bg):
    n = q.shape[0]
    tm = _tile(n, PROJ_TM)
    blk = pl.BlockSpec((tm, ML_INNER), lambda i: (i, 0))
    return pl.pallas_call(
        _ml_gate_kernel,
        grid=(n // tm,),
        in_specs=[blk, blk, blk,
                  pl.BlockSpec((3, ML_INNER, LANES), lambda i: (0, 0, 0)),
                  pl.BlockSpec((1, LANES), lambda i: (0, 0))],
        out_specs=pl.BlockSpec((tm, LANES), lambda i: (i, 0)),
        out_shape=jax.ShapeDtypeStruct((n, LANES), F32),
        compiler_params=_cp(("parallel",)),
        name="ml_gates",
    )(q, k, v, wg, bg)


def _log_sigmoid(x):
    return jnp.minimum(x, 0.0) - jnp.log(1.0 + jnp.exp(-jnp.abs(x)))


def _ml_scan_kernel(q_ref, k_ref, v_ref, g_ref, gt_ref, o_ref, c_sc, n_sc, m_sc):
    h = pl.program_id(1)
    d = pl.program_id(2)
    p = pl.program_id(3)
    ninf = -jnp.inf

    @pl.when(p == 0)
    def _():
        c_sc[...] = jnp.zeros_like(c_sc)
        n_sc[...] = jnp.zeros_like(n_sc)
        m_sc[...] = jnp.full_like(m_sc, ninf)

    L = CHUNK
    col_i = d * 2 * ML_HEADS + h
    col_f = col_i + ML_HEADS
    lane = lax.broadcasted_iota(jnp.int32, (1, LANES), 1)
    g = g_ref[...]
    i_col = jnp.sum(jnp.where(lane == col_i, g, 0.0), axis=1, keepdims=True)
    f_col = _log_sigmoid(jnp.sum(jnp.where(lane == col_f, g, 0.0), axis=1, keepdims=True))
    i_row = gt_ref[pl.ds(col_i, 1), :]
    f_row = _log_sigmoid(gt_ref[pl.ds(col_f, 1), :])

    jj = lax.broadcasted_iota(jnp.int32, (L, L), 0)
    ss = lax.broadcasted_iota(jnp.int32, (L, L), 1)
    fwd = d == 0
    bwd = d == 1
    valid = jnp.logical_or(jnp.logical_and(ss <= jj, fwd), jnp.logical_and(ss >= jj, bwd))
    valid_t = jnp.logical_or(jnp.logical_and(jj <= ss, fwd), jnp.logical_and(jj >= ss, bwd))
    bcum_col = jnp.sum(jnp.where(valid, f_row, 0.0), axis=1, keepdims=True)
    bcum_row = jnp.sum(jnp.where(valid_t, f_col, 0.0), axis=0, keepdims=True)
    b_last = jnp.sum(f_row, axis=1, keepdims=True)
    m_prev = m_sc[...]

    logd = jnp.where(valid, bcum_col - bcum_row + i_row, ninf)
    log_inter = bcum_col + m_prev
    m_j = jnp.maximum(log_inter, jnp.max(logd, axis=1, keepdims=True))
    dmat = jnp.exp(logd - m_j)
    inter = jnp.exp(log_inter - m_j)

    q = q_ref[...]
    k = k_ref[...]
    v = v_ref[...]
    scale = ML_HEAD_DIM ** -0.5
    sc = _nt(q, k) * scale * dmat
    c_prev = c_sc[...]
    n_prev = n_sc[...]
    qc = jnp.dot(q, c_prev.astype(BF16), preferred_element_type=F32) * scale
    num = inter * qc + jnp.dot(sc.astype(BF16), v, preferred_element_type=F32)
    qn = jnp.sum(q.astype(F32) * n_prev, axis=1, keepdims=True) * scale
    den = inter * qn + jnp.sum(sc, axis=1, keepdims=True)
    o_ref[0] = num / jnp.maximum(jnp.abs(den), jnp.exp(-m_j))

    ls = b_last - bcum_col + i_col
    m_new = jnp.maximum(b_last + m_prev, jnp.max(ls, axis=0, keepdims=True))
    w = jnp.exp(ls - m_new)
    decay = jnp.exp(b_last + m_prev - m_new)
    kw = k.astype(F32) * w
    c_sc[...] = decay * c_prev + _tn(kw.astype(BF16), v)
    n_sc[...] = decay * n_prev + jnp.sum(kw, axis=0, keepdims=True)
    m_sc[...] = m_new


def ml_scan(q, k, v, g, gt, nb):
    n = q.shape[0]
    xc = nb * X_CHUNKS

    def cidx(b, d, p):
        xi = jnp.where(d == 0, p - 1, X_CHUNKS - p)
        return jnp.where(p == 0, xc + b, b * X_CHUNKS + xi)

    blk = pl.BlockSpec((CHUNK, ML_HEAD_DIM), lambda b, h, d, p: (cidx(b, d, p), h))
    return pl.pallas_call(
        _ml_scan_kernel,
        grid=(nb, ML_HEADS, 2, X_CHUNKS + 1),
        in_specs=[blk, blk, blk,
                  pl.BlockSpec((CHUNK, LANES), lambda b, h, d, p: (cidx(b, d, p), 0)),
                  pl.BlockSpec((16, CHUNK), lambda b, h, d, p: (0, cidx(b, d, p)))],
        out_specs=pl.BlockSpec((1, CHUNK, ML_HEAD_DIM), lambda b, h, d, p: (d, cidx(b, d, p), h)),
        out_shape=jax.ShapeDtypeStruct((2, n, ML_INNER), F32),
        scratch_shapes=[pltpu.VMEM((ML_HEAD_DIM, ML_HEAD_DIM), F32),
                        pltpu.VMEM((1, ML_HEAD_DIM), F32),
                        pltpu.VMEM((1, 1), F32)],
        compiler_params=_cp(("parallel", "parallel", "parallel", "arbitrary")),
        name="ml_scan",
    )(q, k, v, g, gt)


def _ml_finish_kernel(h_ref, xc_ref, z_ref, gn_ref, sk_ref, o_ref):
    hh = h_ref[0] + h_ref[1]
    z = z_ref[...].astype(F32)
    gate = _silu(z)
    for hd in range(ML_HEADS):
        sl = slice(hd * ML_HEAD_DIM, (hd + 1) * ML_HEAD_DIM)
        seg = hh[:, sl]
        mu = jnp.mean(seg, axis=-1, keepdims=True)
        cen = seg - mu
        var = jnp.mean(cen * cen, axis=-1, keepdims=True)
        hn = cen * lax.rsqrt(var + EPS) * gn_ref[:, sl]
        a = (hn + sk_ref[:, sl] * xc_ref[:, sl].astype(F32)) * gate[:, sl]
        o_ref[:, sl] = a.astype(o_ref.dtype)


def ml_finish(hs, xc, up, gn_w, skip):
    n = xc.shape[0]
    tm = CHUNK
    vec = pl.BlockSpec((1, ML_INNER), lambda i: (0, 0))
    return pl.pallas_call(
        _ml_finish_kernel,
        grid=(n // tm,),
        in_specs=[pl.BlockSpec((2, tm, ML_INNER), lambda i: (0, i, 0)),
                  pl.BlockSpec((tm, ML_INNER), lambda i: (i, 0)),
                  pl.BlockSpec((tm, ML_INNER), lambda i: (i, 1)),
                  vec, vec],
        out_specs=pl.BlockSpec((tm, ML_INNER), lambda i: (i, 0)),
        out_shape=jax.ShapeDtypeStruct((n, ML_INNER), BF16),
        compiler_params=_cp(("parallel",)),
        name="ml_finish",
    )(hs, xc, up, gn_w.astype(F32).reshape(1, ML_INNER), skip.astype(F32).reshape(1, ML_INNER))


def _router_kernel(x_ref, g_ref, sh_ref, sc_ref, w_ref, b_ref, h_ref, o_ref):
    x = x_ref[...]
    y = x * lax.rsqrt(jnp.mean(x * x, axis=-1, keepdims=True) + EPS) * g_ref[...]
    hf = y * (1.0 + sc_ref[0, 0]) + sh_ref[0, 0]
    h_hi = hf.astype(BF16)
    h_ref[...] = hf
    h_lo = (hf - h_hi.astype(F32)).astype(BF16)
    logits = (jnp.dot(h_hi, w_ref[0], preferred_element_type=F32)
              + jnp.dot(h_hi, w_ref[1], preferred_element_type=F32)
              + jnp.dot(h_lo, w_ref[0], preferred_element_type=F32)) + b_ref[...]
    lane = lax.broadcasted_iota(jnp.int32, (1, LANES), 1).astype(F32)
    big = 1e9
    ninf = -jnp.inf
    is_g = jnp.logical_and(lane >= MOE_E, lane < MOE_E + MOE_GROUPS)
    gl = jnp.where(is_g, logits, ninf)
    gmax = jnp.max(gl, axis=1, keepdims=True)
    g_val = 1.0 / jnp.sum(jnp.exp(gl - gmax), axis=1, keepdims=True)
    g_idx = jnp.min(jnp.where(gl == gmax, lane, big), axis=1, keepdims=True) - MOE_E
    lo = g_idx * MOE_EPG
    sel = jnp.logical_and(lane >= lo, lane < lo + MOE_EPG)
    el = jnp.where(sel, logits, ninf)
    e1 = jnp.max(el, axis=1, keepdims=True)
    esum = jnp.sum(jnp.exp(el - e1), axis=1, keepdims=True)
    i1 = jnp.min(jnp.where(el == e1, lane, big), axis=1, keepdims=True)
    el2 = jnp.where(lane == i1, ninf, el)
    e2 = jnp.max(el2, axis=1, keepdims=True)
    i2 = jnp.min(jnp.where(el2 == e2, lane, big), axis=1, keepdims=True)
    p1 = 1.0 / esum
    p2 = jnp.exp(e2 - e1) / esum
    w1 = g_val * p1 / (p1 + p2)
    w2 = g_val * p2 / (p1 + p2)
    gates = jnp.where(lane == i1, w1, 0.0) + jnp.where(lane == i2, w2, 0.0)
    marks = jnp.where(jnp.logical_or(lane == i1 + SEL_LANE, lane == i2 + SEL_LANE), 1.0, 0.0)
    o_ref[...] = gates + marks


def moe_router(s, g, mod, k_shift, k_scale, wr, br, nb):
    n, d = s.shape
    tm = _tile(n, PROJ_TM)
    return pl.pallas_call(
        _router_kernel,
        grid=(n // tm,),
        in_specs=[pl.BlockSpec((tm, d), lambda i: (i, 0)),
                  pl.BlockSpec((1, d), lambda i: (0, 0)),
                  pl.BlockSpec((1, 1, 1, d), lambda i: (_mod_row(i, tm, nb), k_shift, 0, 0)),
                  pl.BlockSpec((1, 1, 1, d), lambda i: (_mod_row(i, tm, nb), k_scale, 0, 0)),
                  pl.BlockSpec((2, d, LANES), lambda i: (0, 0, 0)),
                  pl.BlockSpec((1, LANES), lambda i: (0, 0))],
        out_specs=[pl.BlockSpec((tm, d), lambda i: (i, 0)),
                   pl.BlockSpec((tm, LANES), lambda i: (i, 0))],
        out_shape=[jax.ShapeDtypeStruct((n, d), F32),
                   jax.ShapeDtypeStruct((n, LANES), F32)],
        compiler_params=_cp(("parallel",)),
        name="moe_router",
    )(s, g.reshape(1, d), mod, mod, wr, br)


def _rank_kernel(route_ref, rank_ref, cnt_ref, carry_ref):
    i = pl.program_id(0)

    @pl.when(i == 0)
    def _():
        carry_ref[...] = jnp.zeros_like(carry_ref)

    tm = route_ref.shape[0]
    lane = lax.broadcasted_iota(jnp.int32, (1, LANES), 1)
    marks = jnp.where(lane >= SEL_LANE, route_ref[...], 0.0)
    rr = lax.broadcasted_iota(jnp.int32, (tm, tm), 0)
    cc = lax.broadcasted_iota(jnp.int32, (tm, tm), 1)
    below = jnp.where(cc < rr, 1.0, 0.0).astype(BF16)
    rank_ref[...] = jnp.dot(below, marks.astype(BF16), preferred_element_type=F32) + carry_ref[...]
    carry_ref[...] += jnp.sum(marks, axis=0, keepdims=True)
    cnt_ref[...] = carry_ref[...]


def moe_rank(route):
    n = route.shape[0]
    tm = _tile(n, PROJ_TM)
    return pl.pallas_call(
        _rank_kernel,
        grid=(n // tm,),
        in_specs=[pl.BlockSpec((tm, LANES), lambda i: (i, 0))],
        out_specs=[pl.BlockSpec((tm, LANES), lambda i: (i, 0)),
                   pl.BlockSpec((1, LANES), lambda i: (0, 0))],
        out_shape=[jax.ShapeDtypeStruct((n, LANES), F32),
                   jax.ShapeDtypeStruct((1, LANES), F32)],
        scratch_shapes=[pltpu.VMEM((1, LANES), F32)],
        compiler_params=_cp(("arbitrary",)),
        name="moe_rank",
    )(route)


def _pos_kernel(route_ref, rank_ref, off_ref, o_ref):
    lane = lax.broadcasted_iota(jnp.int32, (1, LANES), 1).astype(F32)
    route = route_ref[...]
    marked = jnp.logical_and(lane >= SEL_LANE, route > 0.5)
    p = rank_ref[...] + off_ref[...]
    lane_a = jnp.min(jnp.where(marked, lane, 1e9), axis=1, keepdims=True)
    lane_b = jnp.max(jnp.where(marked, lane, -1.0), axis=1, keepdims=True)

    def pick(src, at):
        return jnp.sum(jnp.where(lane == at, src, 0.0), axis=1, keepdims=True)

    out = jnp.where(lane == 0.0, pick(p, lane_a), 0.0)
    out = out + jnp.where(lane == 1.0, pick(p, lane_b), 0.0)
    out = out + jnp.where(lane == 2.0, pick(route, lane_a - SEL_LANE), 0.0)
    out = out + jnp.where(lane == 3.0, pick(route, lane_b - SEL_LANE), 0.0)
    o_ref[...] = out


def moe_positions(route, rank, off):
    n = route.shape[0]
    tm = _tile(n, PROJ_TM)
    blk = pl.BlockSpec((tm, LANES), lambda i: (i, 0))
    return pl.pallas_call(
        _pos_kernel,
        grid=(n // tm,),
        in_specs=[blk, blk, pl.BlockSpec((1, LANES), lambda i: (0, 0))],
        out_specs=blk,
        out_shape=jax.ShapeDtypeStruct((n, LANES), F32),
        compiler_params=_cp(("parallel",)),
        name="moe_positions",
    )(route, rank, off)


def _row_copy(src, src_row, dst, dst_row, sem):
    return pltpu.make_async_copy(src.at[pl.ds(src_row, 1)], dst.at[pl.ds(dst_row, 1)], sem)


def _dispatch_kernel(last_ref, pos_ref, h_ref, xs_hbm, zero_sc, sem):
    @pl.when(pl.program_id(0) == 0)
    def _():
        zero_sc[...] = jnp.zeros_like(zero_sc)

        def fill(e, carry):
            start = pl.multiple_of(last_ref[e], MOE_TM)
            pltpu.make_async_copy(zero_sc, xs_hbm.at[pl.ds(start, MOE_TM)], sem).start()
            return carry

        lax.fori_loop(0, MOE_E, fill, 0)

        def drain(e, carry):
            pltpu.make_async_copy(zero_sc, xs_hbm.at[pl.ds(0, MOE_TM)], sem).wait()
            return carry

        lax.fori_loop(0, MOE_E, drain, 0)

    def issue(r, carry):
        _row_copy(h_ref, r, xs_hbm, pos_ref[0, 0, r], sem).start()
        _row_copy(h_ref, r, xs_hbm, pos_ref[0, 0, CHUNK + r], sem).start()
        return carry

    lax.fori_loop(0, CHUNK, issue, 0, unroll=8)
    for _ in range(2):
        pltpu.make_async_copy(h_ref, xs_hbm.at[pl.ds(0, CHUNK)], sem).wait()


def moe_dispatch(last_tile_start, pos, h, n_sorted):
    n, d = h.shape
    return pl.pallas_call(
        _dispatch_kernel,
        grid_spec=pltpu.PrefetchScalarGridSpec(
            num_scalar_prefetch=1,
            grid=(n // CHUNK,),
            in_specs=[pl.BlockSpec((1, 1, 2 * CHUNK), lambda i, last: (i, 0, 0), memory_space=pltpu.SMEM),
                      pl.BlockSpec((CHUNK, d), lambda i, last: (i, 0))],
            out_specs=pl.BlockSpec(memory_space=pl.ANY),
            scratch_shapes=[pltpu.VMEM((MOE_TM, d), h.dtype), pltpu.SemaphoreType.DMA]),
        out_shape=jax.ShapeDtypeStruct((n_sorted, d), h.dtype),
        compiler_params=_cp(("arbitrary",)),
        name="moe_dispatch",
    )(last_tile_start, pos, h)


def _experts_kernel(te_ref, nt_ref, x_ref, w1_ref, w3_ref, w2_ref, o_ref, w1_sc, w3_sc, w2_sc):
    i = pl.program_id(0)
    active = i < nt_ref[0]

    @pl.when(jnp.logical_and(active, jnp.logical_or(i == 0, te_ref[i] != te_ref[jnp.maximum(i - 1, 0)])))
    def _():
        w1_sc[...] = w1_ref[0].astype(BF16)
        w3_sc[...] = w3_ref[0].astype(BF16)
        w2_sc[...] = w2_ref[0].astype(BF16)

    @pl.when(active)
    def _():
        x = x_ref[...].astype(BF16)
        a = jnp.dot(x, w1_sc[...], preferred_element_type=F32)
        b = jnp.dot(x, w3_sc[...], preferred_element_type=F32)
        he = (_silu(a) * b).astype(BF16)
        o_ref[...] = jnp.dot(he, w2_sc[...], preferred_element_type=F32)

    @pl.when(jnp.logical_not(active))
    def _():
        o_ref[...] = jnp.zeros_like(o_ref)


def moe_experts(tile_expert, n_tiles, xs, w1, w3, w2):
    n_sorted, d = xs.shape
    return pl.pallas_call(
        _experts_kernel,
        grid_spec=pltpu.PrefetchScalarGridSpec(
            num_scalar_prefetch=2,
            grid=(n_sorted // MOE_TM,),
            in_specs=[pl.BlockSpec((MOE_TM, d), lambda i, te, nt: (jnp.minimum(i, nt[0] - 1), 0)),
                      pl.BlockSpec((1, d, MOE_HIDDEN), lambda i, te, nt: (te[i], 0, 0)),
                      pl.BlockSpec((1, d, MOE_HIDDEN), lambda i, te, nt: (te[i], 0, 0)),
                      pl.BlockSpec((1, MOE_HIDDEN, d), lambda i, te, nt: (te[i], 0, 0))],
            out_specs=pl.BlockSpec((MOE_TM, d), lambda i, te, nt: (i, 0)),
            scratch_shapes=[pltpu.VMEM((d, MOE_HIDDEN), BF16), pltpu.VMEM((d, MOE_HIDDEN), BF16),
                            pltpu.VMEM((MOE_HIDDEN, d), BF16)]),
        out_shape=jax.ShapeDtypeStruct((n_sorted, d), F32),
        compiler_params=_cp(("arbitrary",)),
        name="moe_experts",
    )(tile_expert, n_tiles, xs, w1, w3, w2)


def _combine_kernel(pos_ref, meta_ref, s_ref, g_ref, ys_hbm, o_ref, buf_ref, sem):
    def issue(r, carry):
        _row_copy(ys_hbm, pos_ref[0, 0, r], buf_ref.at[0], r, sem).start()
        _row_copy(ys_hbm, pos_ref[0, 0, CHUNK + r], buf_ref.at[1], r, sem).start()
        return carry

    lax.fori_loop(0, CHUNK, issue, 0, unroll=8)
    for slot in range(2):
        pltpu.make_async_copy(ys_hbm.at[pl.ds(0, CHUNK)], buf_ref.at[slot], sem).wait()
    meta = meta_ref[...]
    y = meta[:, 2:3] * buf_ref[0] + meta[:, 3:4] * buf_ref[1]
    o_ref[...] = s_ref[...] + g_ref[0, 0] * y


def moe_combine(pos, meta, s, mod, k_gate, ys, nb):
    n, d = s.shape
    tm = CHUNK
    return pl.pallas_call(
        _combine_kernel,
        grid=(n // tm,),
        in_specs=[pl.BlockSpec((1, 1, 2 * CHUNK), lambda i: (i, 0, 0), memory_space=pltpu.SMEM),
                  pl.BlockSpec((tm, LANES), lambda i: (i, 0)),
                  pl.BlockSpec((tm, d), lambda i: (i, 0)),
                  pl.BlockSpec((1, 1, 1, d), lambda i: (_mod_row(i, tm, nb), k_gate, 0, 0)),
                  pl.BlockSpec(memory_space=pl.ANY)],
        out_specs=pl.BlockSpec((tm, d), lambda i: (i, 0)),
        out_shape=jax.ShapeDtypeStruct((n, d), F32),
        scratch_shapes=[pltpu.VMEM((2, tm, d), F32), pltpu.SemaphoreType.DMA],
        input_output_aliases={2: 0},
        compiler_params=_cp(("arbitrary",)),
        name="moe_combine",
    )(pos, meta, s, mod, ys)


def _na_layer(h, w_qkv, w_o, rpb, s, mod, nb):
    qkv = matmul(h, w_qkv.astype(BF16), BF16)
    o = na_attention(qkv, _na_bias_tables(rpb), nb)
    return matmul_residual(o, w_o.astype(BF16), s, mod, 2, nb)


def _da_layer(h, w_qkv, lam, subln_g, w_o, lambda_init, s, mod, nb):
    qkv = matmul_rope(h, w_qkv.astype(BF16), nb)
    o = da_attention(qkv, lam, subln_g, lambda_init, nb)
    return matmul_residual(o, w_o.astype(BF16), s, mod, 2, nb)


def _ml_layer(h, w_up, conv_w, conv_b, w_q, w_k, w_v, w_gate, b_gate, gn_w, skip, w_down, s, mod, nb):
    up = matmul(h, w_up.astype(BF16), BF16)
    xc, q, k, v = ml_conv_qkv(up, conv_w, conv_b, _block_diag(w_q), _block_diag(w_k), _block_diag(w_v), nb)
    ng = w_gate.shape[1]
    wg = jnp.pad(w_gate, ((0, 0), (0, LANES - ng))).reshape(3, ML_INNER, LANES).astype(BF16)
    bg = jnp.pad(b_gate.astype(F32), (0, LANES - ng)).reshape(1, LANES)
    g = ml_gates(q, k, v, wg, bg)
    gt = g[:, :ng].T
    hs = ml_scan(q, k, v, g, gt, nb)
    a = ml_finish(hs, xc, up, gn_w, skip)
    return matmul_residual(a, w_down.astype(BF16), s, mod, 2, nb)


def _moe_layer(norm_g, w_group, b_group, w_router, b_router, w1, w3, w2, s, mod, nb):
    d = s.shape[1]
    pad = LANES - MOE_E - MOE_GROUPS
    wr = jnp.concatenate([w_router, w_group, jnp.zeros((d, pad), w_router.dtype)], axis=1).astype(F32)
    wr_hi = wr.astype(BF16)
    wr_lo = (wr - wr_hi.astype(F32)).astype(BF16)
    br = jnp.concatenate([b_router, b_group, jnp.zeros((pad,), b_router.dtype)]).astype(F32).reshape(1, LANES)
    h, route = moe_router(s, norm_g.astype(F32), mod, 3, 4, jnp.stack([wr_hi, wr_lo]), br, nb)
    rank, cnt = moe_rank(route)
    n = s.shape[0]
    cnt_e = cnt[0, SEL_LANE:SEL_LANE + MOE_E].astype(jnp.int32)
    gsz = ((cnt_e + MOE_TM - 1) // MOE_TM) * MOE_TM
    ends = jnp.cumsum(gsz)
    off_row = jnp.zeros((1, LANES), F32).at[0, SEL_LANE:SEL_LANE + MOE_E].set((ends - gsz).astype(F32))
    meta = moe_positions(route, rank, off_row)
    n_sorted = 2 * n + MOE_E * MOE_TM
    n_tiles = (ends[-1:] // MOE_TM).astype(jnp.int32)
    tile_ids = jnp.arange(n_sorted // MOE_TM, dtype=jnp.int32)
    tile_expert = jnp.sum((tile_ids[:, None] >= (ends // MOE_TM)[None, :]).astype(jnp.int32), axis=1)
    tile_expert = jnp.minimum(tile_expert, MOE_E - 1)
    pos = meta[:, :2].astype(jnp.int32).reshape(n // CHUNK, CHUNK, 2)
    pos = pos.transpose(0, 2, 1).reshape(n // CHUNK, 1, 2 * CHUNK)
    last_tile_start = jnp.maximum(ends - MOE_TM, 0).astype(jnp.int32)
    xs = moe_dispatch(last_tile_start, pos, h, n_sorted)
    ys = moe_experts(tile_expert, n_tiles, xs, w1.astype(F32), w3.astype(F32), w2.astype(F32))
    return moe_combine(pos, meta, s, mod, 5, ys, nb)


def kernel(x, c, ctx, c_ctx, mod_w, mod_b, norm_g, final_g, na_w_qkv, na_w_o, na_rpb, ml_w_up, ml_conv_w, ml_conv_b, ml_w_q, ml_w_k, ml_w_v, ml_w_gate, ml_b_gate, ml_gn_w, ml_skip, ml_w_down, da_w_qkv, da_lambda, da_subln_g, da_w_o, moe_w_group, moe_b_group, moe_w_router, moe_b_router, moe_w1, moe_w3, moe_w2):
    nb, seq, d = x.shape
    assert (seq, d, ctx.shape[1]) == (SEQ, D_MODEL, CTX_LEN) and nb < 16
    depth = mod_w.shape[0]
    nx = nb * seq
    s = jnp.concatenate([x.reshape(nx, d), ctx.reshape(nb * CTX_LEN, d)], axis=0).astype(F32)
    cvec = jnp.concatenate([c, c_ctx[None, :], jnp.zeros((16 - nb - 1, d), c.dtype)], axis=0).astype(F32)
    mods = mod_vectors(cvec, mod_w, mod_b).reshape(depth, 16, 6, 1, d)

    for i in range(depth):
        kind, j = i % N_MIXERS, i // N_MIXERS
        mod = mods[i]
        h = norm_mod(s, norm_g[i, 0].astype(F32), mod, 0, 1, nb)
        if kind == 0:
            s = _na_layer(h, na_w_qkv[j], na_w_o[j], na_rpb[j], s, mod, nb)
        elif kind == 1:
            s = _ml_layer(h, ml_w_up[j], ml_conv_w[j], ml_conv_b[j], ml_w_q[j], ml_w_k[j], ml_w_v[j],
                          ml_w_gate[j], ml_b_gate[j], ml_gn_w[j], ml_skip[j], ml_w_down[j], s, mod, nb)
        else:
            lambda_init = 0.8 - 0.6 * math.exp(-0.3 * i)
            s = _da_layer(h, da_w_qkv[j], da_lambda[j], da_subln_g[j], da_w_o[j], lambda_init, s, mod, nb)
        s = _moe_layer(norm_g[i, 1], moe_w_group[i], moe_b_group[i], moe_w_router[i], moe_b_router[i],
                       moe_w1[i], moe_w3[i], moe_w2[i], s, mod, nb)
    return final_norm(s, final_g.astype(F32), nx).reshape(nb, seq, d)
```

```python
import functools
import math

import numpy as np
import jax
import jax.numpy as jnp
from jax import lax
from jax.experimental import pallas as pl
from jax.experimental.pallas import tpu as pltpu

F32 = jnp.float32
BF16 = jnp.bfloat16

D_MODEL = 1024
SEQ = 4096
CTX_LEN = 256
GRID_W = 64
N_MIXERS = 3
EPS = 1e-6

NA_HEADS = 16
NA_WIN_H = 8
NA_WIN_W = 16

ML_HEADS = 4
ML_INNER = 2 * D_MODEL
ML_HEAD_DIM = ML_INNER // ML_HEADS
ML_BLOCK = 4
ML_CONV_K = 5

DA_HEADS = 8
DA_HEAD_DIM = 64
ROPE_BASE = 10000.0

MOE_GROUPS = 4
MOE_EPG = 8
MOE_E = MOE_GROUPS * MOE_EPG
MOE_HIDDEN = 512
MOE_TM = 256
SEL_LANE = 64

LANES = 128
CHUNK = 256
X_CHUNKS = SEQ // CHUNK
PROJ_TM = 512
DA_TQ = 512
DA_VT_ROWS = 128 + 16
NA_ROWS = 8
NEG = -1e30
VMEM_LIMIT = 56 * 1024 * 1024


def _cp(sem, vmem=VMEM_LIMIT):
    return pltpu.CompilerParams(dimension_semantics=sem, vmem_limit_bytes=vmem)


def _tile(n, pref):
    tm = pref
    while n % tm:
        tm //= 2
    assert tm >= CHUNK
    return tm


def _mod_row(i, tm, nb):
    return jnp.where(i < nb * (SEQ // tm), i // (SEQ // tm), nb)


def _nt(a, b):
    return lax.dot_general(a, b, (((1,), (1,)), ((), ())), preferred_element_type=F32)


def _tn(a, b):
    return lax.dot_general(a, b, (((0,), (0,)), ((), ())), preferred_element_type=F32)


def _silu(x):
    return x * jax.nn.sigmoid(x)


def _head_rows(q, lane):
    zero = jnp.zeros_like(q)
    return jnp.concatenate([jnp.where(lane < 64, q, zero), jnp.where(lane >= 64, q, zero)], axis=0)


def _mod_kernel(c_ref, w_ref, b_ref, o_ref):
    a = _silu(c_ref[...]).astype(BF16)
    o_ref[0] = jnp.dot(a, w_ref[0].astype(BF16), preferred_element_type=F32) + b_ref[0]


def mod_vectors(cvec, mod_w, mod_b):
    depth, d, n6 = mod_w.shape
    tn = 1024
    return pl.pallas_call(
        _mod_kernel,
        grid=(depth, n6 // tn),
        in_specs=[pl.BlockSpec((16, d), lambda l, j: (0, 0)),
                  pl.BlockSpec((1, d, tn), lambda l, j: (l, 0, j)),
                  pl.BlockSpec((1, 1, tn), lambda l, j: (l, 0, j))],
        out_specs=pl.BlockSpec((1, 16, tn), lambda l, j: (l, 0, j)),
        out_shape=jax.ShapeDtypeStruct((depth, 16, n6), F32),
        compiler_params=_cp(("parallel", "parallel")),
        name="mod_vectors",
    )(cvec, mod_w, mod_b.reshape(depth, 1, n6))


def _norm_mod_kernel(x_ref, g_ref, sh_ref, sc_ref, o_ref):
    x = x_ref[...]
    y = x * lax.rsqrt(jnp.mean(x * x, axis=-1, keepdims=True) + EPS) * g_ref[...]
    o_ref[...] = (y * (1.0 + sc_ref[0, 0]) + sh_ref[0, 0]).astype(o_ref.dtype)


def norm_mod(s, g, mod, k_shift, k_scale, nb):
    n, d = s.shape
    tm = _tile(n, PROJ_TM)
    return pl.pallas_call(
        _norm_mod_kernel,
        grid=(n // tm,),
        in_specs=[pl.BlockSpec((tm, d), lambda i: (i, 0)),
                  pl.BlockSpec((1, d), lambda i: (0, 0)),
                  pl.BlockSpec((1, 1, 1, d), lambda i: (_mod_row(i, tm, nb), k_shift, 0, 0)),
                  pl.BlockSpec((1, 1, 1, d), lambda i: (_mod_row(i, tm, nb), k_scale, 0, 0))],
        out_specs=pl.BlockSpec((tm, d), lambda i: (i, 0)),
        out_shape=jax.ShapeDtypeStruct((n, d), BF16),
        compiler_params=_cp(("parallel",)),
        name="norm_mod",
    )(s, g.reshape(1, d), mod, mod)


def _final_norm_kernel(x_ref, g_ref, o_ref):
    x = x_ref[...]
    o_ref[...] = x * lax.rsqrt(jnp.mean(x * x, axis=-1, keepdims=True) + EPS) * g_ref[...]


def final_norm(s, g, n_rows):
    d = s.shape[1]
    tm = _tile(n_rows, PROJ_TM)
    return pl.pallas_call(
        _final_norm_kernel,
        grid=(n_rows // tm,),
        in_specs=[pl.BlockSpec((tm, d), lambda i: (i, 0)),
                  pl.BlockSpec((1, d), lambda i: (0, 0))],
        out_specs=pl.BlockSpec((tm, d), lambda i: (i, 0)),
        out_shape=jax.ShapeDtypeStruct((n_rows, d), F32),
        compiler_params=_cp(("parallel",)),
        name="final_norm",
    )(s, g.reshape(1, d))


def _mm_kernel(a_ref, w_ref, o_ref):
    o_ref[...] = jnp.dot(a_ref[...], w_ref[...], preferred_element_type=F32).astype(o_ref.dtype)


def matmul(a, w, out_dtype, tn=1024):
    n, k = a.shape
    tm = _tile(n, PROJ_TM)
    nout = w.shape[1]
    tn = min(tn, nout)
    return pl.pallas_call(
        _mm_kernel,
        grid=(nout // tn, n // tm),
        in_specs=[pl.BlockSpec((tm, k), lambda j, i: (i, 0)),
                  pl.BlockSpec((k, tn), lambda j, i: (0, j))],
        out_specs=pl.BlockSpec((tm, tn), lambda j, i: (i, j)),
        out_shape=jax.ShapeDtypeStruct((n, nout), out_dtype),
        compiler_params=_cp(("parallel", "parallel")),
        name="matmul",
    )(a, w)


def _mm_res_kernel(a_ref, w_ref, r_ref, g_ref, o_ref):
    acc = jnp.dot(a_ref[...], w_ref[...], preferred_element_type=F32)
    o_ref[...] = r_ref[...] + g_ref[0, 0] * acc


def matmul_residual(a, w, s, mod, k_gate, nb):
    n, k = a.shape
    d = w.shape[1]
    tm = _tile(n, PROJ_TM)
    return pl.pallas_call(
        _mm_res_kernel,
        grid=(n // tm,),
        in_specs=[pl.BlockSpec((tm, k), lambda i: (i, 0)),
                  pl.BlockSpec((k, d), lambda i: (0, 0)),
                  pl.BlockSpec((tm, d), lambda i: (i, 0)),
                  pl.BlockSpec((1, 1, 1, d), lambda i: (_mod_row(i, tm, nb), k_gate, 0, 0))],
        out_specs=pl.BlockSpec((tm, d), lambda i: (i, 0)),
        out_shape=jax.ShapeDtypeStruct((n, d), F32),
        input_output_aliases={2: 0},
        compiler_params=_cp(("parallel",)),
        name="matmul_residual",
    )(a, w, s, mod)


def _na_bias_tables(rpb):
    o = np.arange(NA_WIN_H)[:, None]
    j = np.arange(NA_WIN_H)[None, :]
    dy = j - o + NA_WIN_H - 1
    qc = np.arange(GRID_W)[:, None]
    kc = np.arange(GRID_W)[None, :]
    dx = np.clip(kc - qc, -(NA_WIN_W - 1), NA_WIN_W - 1) + NA_WIN_W - 1
    w_start = np.clip(qc - NA_WIN_W // 2, 0, GRID_W - NA_WIN_W)
    valid = (kc >= w_start) & (kc < w_start + NA_WIN_W)
    tbl = rpb.astype(F32)[:, dy][:, :, :, dx]
    tbl = jnp.where(jnp.asarray(valid)[None, None, None], tbl, NEG)
    tbl = tbl.reshape(NA_HEADS // 2, 2, NA_WIN_H, NA_WIN_H, GRID_W, GRID_W)
    tbl = tbl.transpose(2, 0, 3, 5, 1, 4)
    return tbl.reshape(NA_WIN_H, NA_HEADS // 2, NA_WIN_H * GRID_W, 2 * GRID_W)


def _softmax_cols(s):
    m = jnp.max(s, axis=0, keepdims=True)
    p = jnp.exp(s - m)
    return (p * (1.0 / jnp.sum(p, axis=0, keepdims=True))).astype(BF16)


def _na_x_kernel(q_ref, kx_ref, vx_ref, kc_ref, vc_ref, bias_ref, o_ref, s_sc, p_sc):
    t = pl.program_id(2)
    lane = lax.broadcasted_iota(jnp.int32, (1, LANES), 1)
    nwin = NA_WIN_H * GRID_W
    starts = []
    for i in range(NA_ROWS):
        r = t * NA_ROWS + i
        rs = jnp.clip(r - NA_WIN_H // 2, 0, SEQ // GRID_W - NA_WIN_H)
        start = pl.multiple_of(rs * GRID_W, GRID_W)
        starts.append(start)
        qq = _head_rows(q_ref[i * GRID_W:(i + 1) * GRID_W, :] * 0.125, lane)
        s_sc[i, 0:nwin, :] = _nt(kx_ref[pl.ds(start, nwin), :], qq) + bias_ref[r - rs, 0]
        s_sc[i, nwin:, :] = _nt(kc_ref[...], qq)
    for i in range(NA_ROWS):
        p_sc[i] = _softmax_cols(s_sc[i])
    for i in range(NA_ROWS):
        o = (_tn(p_sc[i, 0:nwin, :], vx_ref[pl.ds(starts[i], nwin), :])
             + _tn(p_sc[i, nwin:, :], vc_ref[...]))
        o = jnp.where(lane < 64, o[0:GRID_W], o[GRID_W:])
        o_ref[i * GRID_W:(i + 1) * GRID_W, :] = o.astype(o_ref.dtype)


def _na_ctx_kernel(q_ref, kc_ref, vc_ref, o_in_hbm, o_ref):
    del o_in_hbm
    lane = lax.broadcasted_iota(jnp.int32, (1, LANES), 1)
    q = q_ref[...] * 0.125
    outs = []
    for hh in range(2):
        qm = jnp.where((lane // 64) == hh, q, jnp.zeros_like(q))
        p = _softmax_cols(_nt(kc_ref[...], qm))
        outs.append(_tn(p, vc_ref[...]))
    o_ref[...] = jnp.where(lane < 64, outs[0], outs[1]).astype(o_ref.dtype)


def na_attention(qkv, bias, nb):
    n = qkv.shape[0]
    d = D_MODEL
    ncb = d // LANES
    xc = nb * X_CHUNKS
    nkeys = NA_WIN_H * GRID_W + CTX_LEN
    tq = NA_ROWS * GRID_W
    steps = SEQ // tq
    o = pl.pallas_call(
        _na_x_kernel,
        grid=(nb, ncb, steps),
        in_specs=[pl.BlockSpec((tq, LANES), lambda b, h, t: (b * steps + t, h)),
                  pl.BlockSpec((SEQ, LANES), lambda b, h, t: (b, ncb + h)),
                  pl.BlockSpec((SEQ, LANES), lambda b, h, t: (b, 2 * ncb + h)),
                  pl.BlockSpec((CHUNK, LANES), lambda b, h, t: (xc + b, ncb + h)),
                  pl.BlockSpec((CHUNK, LANES), lambda b, h, t: (xc + b, 2 * ncb + h)),
                  pl.BlockSpec((NA_WIN_H, 1, NA_WIN_H * GRID_W, LANES), lambda b, h, t: (0, h, 0, 0))],
        out_specs=pl.BlockSpec((tq, LANES), lambda b, h, t: (b * steps + t, h)),
        out_shape=jax.ShapeDtypeStruct((n, d), BF16),
        scratch_shapes=[pltpu.VMEM((NA_ROWS, nkeys, LANES), F32),
                        pltpu.VMEM((NA_ROWS, nkeys, LANES), BF16)],
        compiler_params=_cp(("parallel", "parallel", "arbitrary")),
        name="na_attention_x",
    )(qkv, qkv, qkv, qkv, qkv, bias)
    return pl.pallas_call(
        _na_ctx_kernel,
        grid=(nb, ncb),
        in_specs=[pl.BlockSpec((CHUNK, LANES), lambda b, h: (xc + b, h)),
                  pl.BlockSpec((CHUNK, LANES), lambda b, h: (xc + b, ncb + h)),
                  pl.BlockSpec((CHUNK, LANES), lambda b, h: (xc + b, 2 * ncb + h)),
                  pl.BlockSpec(memory_space=pl.ANY)],
        out_specs=pl.BlockSpec((CHUNK, LANES), lambda b, h: (xc + b, h)),
        out_shape=jax.ShapeDtypeStruct((n, d), BF16),
        input_output_aliases={3: 0},
        compiler_params=_cp(("parallel", "parallel")),
        name="na_attention_ctx",
    )(qkv, qkv, qkv, o)


def _rope_tables(tm):
    half = DA_HEAD_DIM // 2
    freqs = ROPE_BASE ** (-np.arange(0, half, 2, dtype=np.float32) / half)
    t = np.arange(SEQ)
    row, col = t // GRID_W, t % GRID_W
    lane = np.arange(LANES)
    l64 = lane % DA_HEAD_DIM
    use_col = (l64 // half) == 1
    l32 = l64 % half
    fi = l32 % (half // 2)
    second = l32 >= half // 2
    pos = np.where(use_col[None, :], col[:, None], row[:, None]).astype(np.float32)
    ang = pos * freqs[fi][None, :]
    cos = np.cos(ang).astype(np.float32)
    sin = np.sin(ang).astype(np.float32)
    sa = np.where(second[None, :], 0.0, -sin).astype(np.float32)
    sb = np.where(second[None, :], sin, 0.0).astype(np.float32)
    ident = np.ones((tm, LANES), np.float32)
    zero = np.zeros((tm, LANES), np.float32)
    return (jnp.asarray(np.concatenate([cos, ident])),
            jnp.asarray(np.concatenate([sa, zero])),
            jnp.asarray(np.concatenate([sb, zero])))


def _mm_rope_kernel(a_ref, w_ref, c_ref, sa_ref, sb_ref, o_ref):
    j = pl.program_id(0)
    acc = jnp.dot(a_ref[...], w_ref[...], preferred_element_type=F32)

    @pl.when(j < 2)
    def _():
        scale = jnp.where(j == 0, DA_HEAD_DIM ** -0.5 * math.log2(math.e), 1.0)
        cos = c_ref[...] * scale
        sa = sa_ref[...] * scale
        sb = sb_ref[...] * scale
        for g in range(acc.shape[1] // LANES):
            sl = slice(g * LANES, (g + 1) * LANES)
            x = acc[:, sl]
            y = x * cos + pltpu.roll(x, LANES - 16, 1) * sa + pltpu.roll(x, 16, 1) * sb
            o_ref[:, sl] = y.astype(o_ref.dtype)

    @pl.when(j >= 2)
    def _():
        o_ref[...] = acc.astype(o_ref.dtype)


def matmul_rope(a, w, nb):
    n, k = a.shape
    d = D_MODEL
    tm = _tile(n, PROJ_TM)
    cos, sa, sb = _rope_tables(tm)
    xt = nb * (SEQ // tm)

    def tab(j, i):
        return (jnp.where(i < xt, i % (SEQ // tm), SEQ // tm), 0)

    tspec = pl.BlockSpec((tm, LANES), tab)
    return pl.pallas_call(
        _mm_rope_kernel,
        grid=(3, n // tm),
        in_specs=[pl.BlockSpec((tm, k), lambda j, i: (i, 0)),
                  pl.BlockSpec((k, d), lambda j, i: (0, j)),
                  tspec, tspec, tspec],
        out_specs=pl.BlockSpec((tm, d), lambda j, i: (i, j)),
        out_shape=jax.ShapeDtypeStruct((n, 3 * d), BF16),
        compiler_params=_cp(("parallel", "parallel")),
        name="matmul_rope",
    )(a, w, cos, sa, sb)


def _da_body(q_ref, lam_ref, g_ref, o_ref, s_sc, acc_sc, m_sc, lambda_init, run_chunks):
    tq = q_ref.shape[0]
    lane = lax.broadcasted_iota(jnp.int32, (1, LANES), 1)
    qq = _head_rows(q_ref[...], lane)
    m_sc[...] = jnp.full_like(m_sc, NEG)
    acc_sc[...] = jnp.zeros_like(acc_sc)

    def scores(slot, k):
        s_sc[slot] = _nt(k, qq)

    def fold(slot, vt):
        for g in range(2 * tq // CHUNK):
            sl = slice(g * CHUNK, (g + 1) * CHUNK)
            s = s_sc[slot, :, sl]
            m_old = m_sc[:, sl]
            m_new = jnp.maximum(m_old, jnp.max(s, axis=0, keepdims=True))
            alpha = jnp.exp2(m_old - m_new)
            p = jnp.exp2(s - m_new)
            m_sc[:, sl] = m_new
            acc_sc[:, sl] = acc_sc[:, sl] * alpha + jnp.dot(vt, p.astype(BF16), preferred_element_type=F32)

    run_chunks(scores, fold)

    ot = acc_sc[0:LANES, :] / acc_sc[LANES:LANES + 1, :]
    lam = lam_ref[...]
    lam_full = (jnp.exp(jnp.sum(lam[0:1] * lam[1:2], axis=1, keepdims=True))
                - jnp.exp(jnp.sum(lam[2:3] * lam[3:4], axis=1, keepdims=True)) + lambda_init)
    od = ot[:, :tq] - lam_full * ot[:, tq:]
    y = od * lax.rsqrt(jnp.mean(od * od, axis=0, keepdims=True) + EPS) * g_ref[...]
    o_ref[...] = (y * (1.0 - lambda_init)).T.astype(o_ref.dtype)


def _da_x_kernel(q_ref, kx_ref, kc_ref, vtx_ref, vtc_ref, lam_ref, g_ref, o_ref,
                 s_sc, acc_sc, m_sc, *, lambda_init):
    def kx(c):
        return kx_ref[pl.ds(pl.multiple_of(c * CHUNK, CHUNK), CHUNK), :]

    def run_chunks(scores, fold):
        scores(0, kx(0))

        def body(i, carry):
            scores(1, kx(2 * i + 1))
            fold(0, vtx_ref[0, 2 * i])
            scores(0, kx(2 * i + 2))
            fold(1, vtx_ref[0, 2 * i + 1])
            return carry

        lax.fori_loop(0, X_CHUNKS // 2 - 1, body, 0)
        scores(1, kx(X_CHUNKS - 1))
        fold(0, vtx_ref[0, X_CHUNKS - 2])
        scores(0, kc_ref[...])
        fold(1, vtx_ref[0, X_CHUNKS - 1])
        fold(0, vtc_ref[0, 0])

    _da_body(q_ref, lam_ref, g_ref, o_ref, s_sc, acc_sc, m_sc, lambda_init, run_chunks)


def _da_ctx_kernel(q_ref, kc_ref, vtc_ref, lam_ref, g_ref, o_in_hbm, o_ref,
                   s_sc, acc_sc, m_sc, *, lambda_init):
    del o_in_hbm

    def run_chunks(scores, fold):
        scores(0, kc_ref[...])
        fold(0, vtc_ref[0, 0])

    _da_body(q_ref, lam_ref, g_ref, o_ref, s_sc, acc_sc, m_sc, lambda_init, run_chunks)


def _da_scratch(tq):
    return [pltpu.VMEM((2, CHUNK, 2 * tq), F32), pltpu.VMEM((DA_VT_ROWS, 2 * tq), F32),
            pltpu.VMEM((1, 2 * tq), F32)]


def da_attention(qkv, lam, subln_g, lambda_init, nb):
    n = qkv.shape[0]
    d = D_MODEL
    ncb = d // LANES
    xc = nb * X_CHUNKS
    tq = DA_TQ
    vt = qkv[:, 2 * d:].reshape(n // CHUNK, CHUNK, ncb, LANES).transpose(2, 0, 3, 1)
    extra = jnp.zeros((ncb, n // CHUNK, DA_VT_ROWS - LANES, CHUNK), BF16).at[:, :, 0, :].set(1.0)
    vt = jnp.concatenate([vt, extra], axis=2)
    lam = lam.astype(F32)
    gcol = subln_g.astype(F32).reshape(LANES, 1)
    o = pl.pallas_call(
        functools.partial(_da_x_kernel, lambda_init=lambda_init),
        grid=(nb, ncb, SEQ // tq),
        in_specs=[pl.BlockSpec((tq, LANES), lambda b, h, t: (b * (SEQ // tq) + t, h)),
                  pl.BlockSpec((SEQ, LANES), lambda b, h, t: (b, ncb + h)),
                  pl.BlockSpec((CHUNK, LANES), lambda b, h, t: (xc + b, ncb + h)),
                  pl.BlockSpec((1, X_CHUNKS, DA_VT_ROWS, CHUNK), lambda b, h, t: (h, b, 0, 0)),
                  pl.BlockSpec((1, 1, DA_VT_ROWS, CHUNK), lambda b, h, t: (h, xc + b, 0, 0)),
                  pl.BlockSpec((4, DA_HEAD_DIM), lambda b, h, t: (0, 0)),
                  pl.BlockSpec((LANES, 1), lambda b, h, t: (0, 0))],
        out_specs=pl.BlockSpec((tq, LANES), lambda b, h, t: (b * (SEQ // tq) + t, h)),
        out_shape=jax.ShapeDtypeStruct((n, d), BF16),
        scratch_shapes=_da_scratch(tq),
        compiler_params=_cp(("parallel", "parallel", "arbitrary")),
        name="da_attention_x",
    )(qkv, qkv, qkv, vt, vt, lam, gcol)
    return pl.pallas_call(
        functools.partial(_da_ctx_kernel, lambda_init=lambda_init),
        grid=(nb, ncb),
        in_specs=[pl.BlockSpec((CHUNK, LANES), lambda b, h: (xc + b, h)),
                  pl.BlockSpec((CHUNK, LANES), lambda b, h: (xc + b, ncb + h)),
                  pl.BlockSpec((1, 1, DA_VT_ROWS, CHUNK), lambda b, h: (h, xc + b, 0, 0)),
                  pl.BlockSpec((4, DA_HEAD_DIM), lambda b, h: (0, 0)),
                  pl.BlockSpec((LANES, 1), lambda b, h: (0, 0)),
                  pl.BlockSpec(memory_space=pl.ANY)],
        out_specs=pl.BlockSpec((CHUNK, LANES), lambda b, h: (xc + b, h)),
        out_shape=jax.ShapeDtypeStruct((n, d), BF16),
        scratch_shapes=_da_scratch(CHUNK),
        input_output_aliases={5: 0},
        compiler_params=_cp(("parallel", "parallel")),
        name="da_attention_ctx",
    )(qkv, qkv, vt, lam, gcol, o)


def _block_diag(w):
    per = LANES // ML_BLOCK
    wr = w.reshape(ML_INNER // LANES, per, ML_BLOCK, ML_BLOCK)
    eye = jnp.eye(per, dtype=w.dtype)
    return jnp.einsum('cgio,gh->cgiho', wr, eye).reshape(ML_INNER // LANES, LANES, LANES).astype(BF16)


def _ml_conv_kernel(p_ref, c_ref, n_ref, cw_ref, cb_ref, wq_ref, wk_ref, wv_ref,
                    xc_ref, q_ref, k_ref, v_ref, *, n_xchunks):
    i = pl.program_id(0)
    is_x = i < n_xchunks
    j = i % X_CHUNKS
    halo = 16
    cur = c_ref[...].astype(F32)
    prev = p_ref[CHUNK - halo:CHUNK, :].astype(F32)
    nxt = n_ref[0:halo, :].astype(F32)
    prev = jnp.where(jnp.logical_and(is_x, j > 0), prev, jnp.zeros_like(prev))
    nxt = jnp.where(jnp.logical_and(is_x, j < X_CHUNKS - 1), nxt, jnp.zeros_like(nxt))
    xp = jnp.concatenate([prev, cur, nxt], axis=0)
    rows = CHUNK + 2 * halo
    y = cb_ref[...] + cw_ref[ML_CONV_K // 2:ML_CONV_K // 2 + 1, :] * cur
    for tap in range(ML_CONV_K):
        dlt = tap - ML_CONV_K // 2
        if dlt == 0:
            continue
        shifted = pltpu.roll(xp, (-dlt) % rows, 0)[halo:halo + CHUNK]
        y = y + cw_ref[tap:tap + 1, :] * shifted
    xcb = _silu(y).astype(BF16)
    xc_ref[...] = xcb
    xm = c_ref[...]
    for s in range(xcb.shape[1] // LANES):
        sl = slice(s * LANES, (s + 1) * LANES)
        q_ref[:, sl] = jnp.dot(xcb[:, sl], wq_ref[s], preferred_element_type=F32).astype(BF16)
        k_ref[:, sl] = jnp.dot(xcb[:, sl], wk_ref[s], preferred_element_type=F32).astype(BF16)
        v_ref[:, sl] = jnp.dot(xm[:, sl], wv_ref[s], preferred_element_type=F32).astype(BF16)


def ml_conv_qkv(up, conv_w, conv_b, wq, wk, wv, nb):
    n = up.shape[0]
    nchunks = n // CHUNK
    cw = 512
    ncb = ML_INNER // cw
    sub = cw // LANES
    blk = pl.BlockSpec((CHUNK, cw), lambda i, c: (i, c))
    wspec = pl.BlockSpec((sub, LANES, LANES), lambda i, c: (c, 0, 0))
    out = jax.ShapeDtypeStruct((n, ML_INNER), BF16)
    return pl.pallas_call(
        functools.partial(_ml_conv_kernel, n_xchunks=nb * X_CHUNKS),
        grid=(nchunks, ncb),
        in_specs=[pl.BlockSpec((CHUNK, cw), lambda i, c: (jnp.maximum(i - 1, 0), c)),
                  blk,
                  pl.BlockSpec((CHUNK, cw), lambda i, c: (jnp.minimum(i + 1, nchunks - 1), c)),
                  pl.BlockSpec((ML_CONV_K, cw), lambda i, c: (0, c)),
                  pl.BlockSpec((1, cw), lambda i, c: (0, c)),
                  wspec, wspec, wspec],
        out_specs=[blk, blk, blk, blk],
        out_shape=[out, out, out, out],
        compiler_params=_cp(("parallel", "parallel")),
        name="ml_conv_qkv",
    )(up, up, up, conv_w.astype(F32), conv_b.astype(F32).reshape(1, ML_INNER), wq, wk, wv)


def _ml_gate_kernel(q_ref, k_ref, v_ref, w_ref, b_ref, o_ref):
    acc = jnp.dot(q_ref[...], w_ref[0], preferred_element_type=F32)
    acc = acc + jnp.dot(k_ref[...], w_ref[1], preferred_element_type=F32)
    acc = acc + jnp.dot(v_ref[...], w_ref[2], preferred_element_type=F32)
    o_ref[...] = acc + b_ref[...]


def ml_gates(q, k, v, wg, bg):
    n = q.shape[0]
    tm = _tile(n, PROJ_TM)
    blk = pl.BlockSpec((tm, ML_INNER), lambda i: (i, 0))
    return pl.pallas_call(
        _ml_gate_kernel,
        grid=(n // tm,),
        in_specs=[blk, blk, blk,
                  pl.BlockSpec((3, ML_INNER, LANES), lambda i: (0, 0, 0)),
                  pl.BlockSpec((1, LANES), lambda i: (0, 0))],
        out_specs=pl.BlockSpec((tm, LANES), lambda i: (i, 0)),
        out_shape=jax.ShapeDtypeStruct((n, LANES), F32),
        compiler_params=_cp(("parallel",)),
        name="ml_gates",
    )(q, k, v, wg, bg)


def _log_sigmoid(x):
    return jnp.minimum(x, 0.0) - jnp.log(1.0 + jnp.exp(-jnp.abs(x)))


def _ml_scan_kernel(q_ref, k_ref, v_ref, g_ref, gt_ref, o_ref, c_sc, n_sc, m_sc):
    h = pl.program_id(1)
    d = pl.program_id(2)
    p = pl.program_id(3)
    ninf = -jnp.inf

    @pl.when(p == 0)
    def _():
        c_sc[...] = jnp.zeros_like(c_sc)
        n_sc[...] = jnp.zeros_like(n_sc)
        m_sc[...] = jnp.full_like(m_sc, ninf)

    L = CHUNK
    col_i = d * 2 * ML_HEADS + h
    col_f = col_i + ML_HEADS
    lane = lax.broadcasted_iota(jnp.int32, (1, LANES), 1)
    g = g_ref[...]
    i_col = jnp.sum(jnp.where(lane == col_i, g, 0.0), axis=1, keepdims=True)
    f_col = _log_sigmoid(jnp.sum(jnp.where(lane == col_f, g, 0.0), axis=1, keepdims=True))
    i_row = gt_ref[pl.ds(col_i, 1), :]
    f_row = _log_sigmoid(gt_ref[pl.ds(col_f, 1), :])

    jj = lax.broadcasted_iota(jnp.int32, (L, L), 0)
    ss = lax.broadcasted_iota(jnp.int32, (L, L), 1)
    fwd = d == 0
    bwd = d == 1
    valid = jnp.logical_or(jnp.logical_and(ss <= jj, fwd), jnp.logical_and(ss >= jj, bwd))
    valid_t = jnp.logical_or(jnp.logical_and(jj <= ss, fwd), jnp.logical_and(jj >= ss, bwd))
    bcum_col = jnp.sum(jnp.where(valid, f_row, 0.0), axis=1, keepdims=True)
    bcum_row = jnp.sum(jnp.where(valid_t, f_col, 0.0), axis=0, keepdims=True)
    b_last = jnp.sum(f_row, axis=1, keepdims=True)
    m_prev = m_sc[...]

    logd = jnp.where(valid, bcum_col - bcum_row + i_row, ninf)
    log_inter = bcum_col + m_prev
    m_j = jnp.maximum(log_inter, jnp.max(logd, axis=1, keepdims=True))
    dmat = jnp.exp(logd - m_j)
    inter = jnp.exp(log_inter - m_j)

    q = q_ref[...]
    k = k_ref[...]
    v = v_ref[...]
    scale = ML_HEAD_DIM ** -0.5
    sc = _nt(q, k) * scale * dmat
    c_prev = c_sc[...]
    n_prev = n_sc[...]
    qc = jnp.dot(q, c_prev.astype(BF16), preferred_element_type=F32) * scale
    num = inter * qc + jnp.dot(sc.astype(BF16), v, preferred_element_type=F32)
    qn = jnp.sum(q.astype(F32) * n_prev, axis=1, keepdims=True) * scale
    den = inter * qn + jnp.sum(sc, axis=1, keepdims=True)
    o_ref[0] = num / jnp.maximum(jnp.abs(den), jnp.exp(-m_j))

    ls = b_last - bcum_col + i_col
    m_new = jnp.maximum(b_last + m_prev, jnp.max(ls, axis=0, keepdims=True))
    w = jnp.exp(ls - m_new)
    decay = jnp.exp(b_last + m_prev - m_new)
    kw = k.astype(F32) * w
    c_sc[...] = decay * c_prev + _tn(kw.astype(BF16), v)
    n_sc[...] = decay * n_prev + jnp.sum(kw, axis=0, keepdims=True)
    m_sc[...] = m_new


def ml_scan(q, k, v, g, gt, nb):
    n = q.shape[0]
    xc = nb * X_CHUNKS

    def cidx(b, d, p):
        xi = jnp.where(d == 0, p - 1, X_CHUNKS - p)
        return jnp.where(p == 0, xc + b, b * X_CHUNKS + xi)

    blk = pl.BlockSpec((CHUNK, ML_HEAD_DIM), lambda b, h, d, p: (cidx(b, d, p), h))
    return pl.pallas_call(
        _ml_scan_kernel,
        grid=(nb, ML_HEADS, 2, X_CHUNKS + 1),
        in_specs=[blk, blk, blk,
                  pl.BlockSpec((CHUNK, LANES), lambda b, h, d, p: (cidx(b, d, p), 0)),
                  pl.BlockSpec((16, CHUNK), lambda b, h, d, p: (0, cidx(b, d, p)))],
        out_specs=pl.BlockSpec((1, CHUNK, ML_HEAD_DIM), lambda b, h, d, p: (d, cidx(b, d, p), h)),
        out_shape=jax.ShapeDtypeStruct((2, n, ML_INNER), F32),
        scratch_shapes=[pltpu.VMEM((ML_HEAD_DIM, ML_HEAD_DIM), F32),
                        pltpu.VMEM((1, ML_HEAD_DIM), F32),
                        pltpu.VMEM((1, 1), F32)],
        compiler_params=_cp(("parallel", "parallel", "parallel", "arbitrary")),
        name="ml_scan",
    )(q, k, v, g, gt)


def _ml_finish_kernel(h_ref, xc_ref, z_ref, gn_ref, sk_ref, o_ref):
    hh = h_ref[0] + h_ref[1]
    z = z_ref[...].astype(F32)
    gate = _silu(z)
    for hd in range(ML_HEADS):
        sl = slice(hd * ML_HEAD_DIM, (hd + 1) * ML_HEAD_DIM)
        seg = hh[:, sl]
        mu = jnp.mean(seg, axis=-1, keepdims=True)
        cen = seg - mu
        var = jnp.mean(cen * cen, axis=-1, keepdims=True)
        hn = cen * lax.rsqrt(var + EPS) * gn_ref[:, sl]
        a = (hn + sk_ref[:, sl] * xc_ref[:, sl].astype(F32)) * gate[:, sl]
        o_ref[:, sl] = a.astype(o_ref.dtype)


def ml_finish(hs, xc, up, gn_w, skip):
    n = xc.shape[0]
    tm = CHUNK
    vec = pl.BlockSpec((1, ML_INNER), lambda i: (0, 0))
    return pl.pallas_call(
        _ml_finish_kernel,
        grid=(n // tm,),
        in_specs=[pl.BlockSpec((2, tm, ML_INNER), lambda i: (0, i, 0)),
                  pl.BlockSpec((tm, ML_INNER), lambda i: (i, 0)),
                  pl.BlockSpec((tm, ML_INNER), lambda i: (i, 1)),
                  vec, vec],
        out_specs=pl.BlockSpec((tm, ML_INNER), lambda i: (i, 0)),
        out_shape=jax.ShapeDtypeStruct((n, ML_INNER), BF16),
        compiler_params=_cp(("parallel",)),
        name="ml_finish",
    )(hs, xc, up, gn_w.astype(F32).reshape(1, ML_INNER), skip.astype(F32).reshape(1, ML_INNER))


def _router_kernel(x_ref, g_ref, sh_ref, sc_ref, w_ref, b_ref, h_ref, o_ref):
    x = x_ref[...]
    y = x * lax.rsqrt(jnp.mean(x * x, axis=-1, keepdims=True) + EPS) * g_ref[...]
    hf = y * (1.0 + sc_ref[0, 0]) + sh_ref[0, 0]
    h_hi = hf.astype(BF16)
    h_ref[...] = hf
    h_lo = (hf - h_hi.astype(F32)).astype(BF16)
    logits = (jnp.dot(h_hi, w_ref[0], preferred_element_type=F32)
              + jnp.dot(h_hi, w_ref[1], preferred_element_type=F32)
              + jnp.dot(h_lo, w_ref[0], preferred_element_type=F32)) + b_ref[...]
    lane = lax.broadcasted_iota(jnp.int32, (1, LANES), 1).astype(F32)
    big = 1e9
    ninf = -jnp.inf
    is_g = jnp.logical_and(lane >= MOE_E, lane < MOE_E + MOE_GROUPS)
    gl = jnp.where(is_g, logits, ninf)
    gmax = jnp.max(gl, axis=1, keepdims=True)
    g_val = 1.0 / jnp.sum(jnp.exp(gl - gmax), axis=1, keepdims=True)
    g_idx = jnp.min(jnp.where(gl == gmax, lane, big), axis=1, keepdims=True) - MOE_E
    lo = g_idx * MOE_EPG
    sel = jnp.logical_and(lane >= lo, lane < lo + MOE_EPG)
    el = jnp.where(sel, logits, ninf)
    e1 = jnp.max(el, axis=1, keepdims=True)
    esum = jnp.sum(jnp.exp(el - e1), axis=1, keepdims=True)
    i1 = jnp.min(jnp.where(el == e1, lane, big), axis=1, keepdims=True)
    el2 = jnp.where(lane == i1, ninf, el)
    e2 = jnp.max(el2, axis=1, keepdims=True)
    i2 = jnp.min(jnp.where(el2 == e2, lane, big), axis=1, keepdims=True)
    p1 = 1.0 / esum
    p2 = jnp.exp(e2 - e1) / esum
    w1 = g_val * p1 / (p1 + p2)
    w2 = g_val * p2 / (p1 + p2)
    gates = jnp.where(lane == i1, w1, 0.0) + jnp.where(lane == i2, w2, 0.0)
    marks = jnp.where(jnp.logical_or(lane == i1 + SEL_LANE, lane == i2 + SEL_LANE), 1.0, 0.0)
    o_ref[...] = gates + marks


def moe_router(s, g, mod, k_shift, k_scale, wr, br, nb):
    n, d = s.shape
    tm = _tile(n, PROJ_TM)
    return pl.pallas_call(
        _router_kernel,
        grid=(n // tm,),
        in_specs=[pl.BlockSpec((tm, d), lambda i: (i, 0)),
                  pl.BlockSpec((1, d), lambda i: (0, 0)),
                  pl.BlockSpec((1, 1, 1, d), lambda i: (_mod_row(i, tm, nb), k_shift, 0, 0)),
                  pl.BlockSpec((1, 1, 1, d), lambda i: (_mod_row(i, tm, nb), k_scale, 0, 0)),
                  pl.BlockSpec((2, d, LANES), lambda i: (0, 0, 0)),
                  pl.BlockSpec((1, LANES), lambda i: (0, 0))],
        out_specs=[pl.BlockSpec((tm, d), lambda i: (i, 0)),
                   pl.BlockSpec((tm, LANES), lambda i: (i, 0))],
        out_shape=[jax.ShapeDtypeStruct((n, d), F32),
                   jax.ShapeDtypeStruct((n, LANES), F32)],
        compiler_params=_cp(("parallel",)),
        name="moe_router",
    )(s, g.reshape(1, d), mod, mod, wr, br)


def _rank_kernel(route_ref, rank_ref, cnt_ref, carry_ref):
    i = pl.program_id(0)

    @pl.when(i == 0)
    def _():
        carry_ref[...] = jnp.zeros_like(carry_ref)

    tm = route_ref.shape[0]
    lane = lax.broadcasted_iota(jnp.int32, (1, LANES), 1)
    marks = jnp.where(lane >= SEL_LANE, route_ref[...], 0.0)
    rr = lax.broadcasted_iota(jnp.int32, (tm, tm), 0)
    cc = lax.broadcasted_iota(jnp.int32, (tm, tm), 1)
    below = jnp.where(cc < rr, 1.0, 0.0).astype(BF16)
    rank_ref[...] = jnp.dot(below, marks.astype(BF16), preferred_element_type=F32) + carry_ref[...]
    carry_ref[...] += jnp.sum(marks, axis=0, keepdims=True)
    cnt_ref[...] = carry_ref[...]


def moe_rank(route):
    n = route.shape[0]
    tm = _tile(n, PROJ_TM)
    return pl.pallas_call(
        _rank_kernel,
        grid=(n // tm,),
        in_specs=[pl.BlockSpec((tm, LANES), lambda i: (i, 0))],
        out_specs=[pl.BlockSpec((tm, LANES), lambda i: (i, 0)),
                   pl.BlockSpec((1, LANES), lambda i: (0, 0))],
        out_shape=[jax.ShapeDtypeStruct((n, LANES), F32),
                   jax.ShapeDtypeStruct((1, LANES), F32)],
        scratch_shapes=[pltpu.VMEM((1, LANES), F32)],
        compiler_params=_cp(("arbitrary",)),
        name="moe_rank",
    )(route)


def _pos_kernel(route_ref, rank_ref, off_ref, o_ref):
    lane = lax.broadcasted_iota(jnp.int32, (1, LANES), 1).astype(F32)
    route = route_ref[...]
    marked = jnp.logical_and(lane >= SEL_LANE, route > 0.5)
    p = rank_ref[...] + off_ref[...]
    lane_a = jnp.min(jnp.where(marked, lane, 1e9), axis=1, keepdims=True)
    lane_b = jnp.max(jnp.where(marked, lane, -1.0), axis=1, keepdims=True)

    def pick(src, at):
        return jnp.sum(jnp.where(lane == at, src, 0.0), axis=1, keepdims=True)

    out = jnp.where(lane == 0.0, pick(p, lane_a), 0.0)
    out = out + jnp.where(lane == 1.0, pick(p, lane_b), 0.0)
    out = out + jnp.where(lane == 2.0, pick(route, lane_a - SEL_LANE), 0.0)
    out = out + jnp.where(lane == 3.0, pick(route, lane_b - SEL_LANE), 0.0)
    o_ref[...] = out


def moe_positions(route, rank, off):
    n = route.shape[0]
    tm = _tile(n, PROJ_TM)
    blk = pl.BlockSpec((tm, LANES), lambda i: (i, 0))
    return pl.pallas_call(
        _pos_kernel,
        grid=(n // tm,),
        in_specs=[blk, blk, pl.BlockSpec((1, LANES), lambda i: (0, 0))],
        out_specs=blk,
        out_shape=jax.ShapeDtypeStruct((n, LANES), F32),
        compiler_params=_cp(("parallel",)),
        name="moe_positions",
    )(route, rank, off)


def _row_copy(src, src_row, dst, dst_row, sem):
    return pltpu.make_async_copy(src.at[pl.ds(src_row, 1)], dst.at[pl.ds(dst_row, 1)], sem)


def _dispatch_kernel(last_ref, pos_ref, h_ref, xs_hbm, zero_sc, sem):
    @pl.when(pl.program_id(0) == 0)
    def _():
        zero_sc[...] = jnp.zeros_like(zero_sc)

        def fill(e, carry):
            start = pl.multiple_of(last_ref[e], MOE_TM)
            pltpu.make_async_copy(zero_sc, xs_hbm.at[pl.ds(start, MOE_TM)], sem).start()
            return carry

        lax.fori_loop(0, MOE_E, fill, 0)

        def drain(e, carry):
            pltpu.make_async_copy(zero_sc, xs_hbm.at[pl.ds(0, MOE_TM)], sem).wait()
            return carry

        lax.fori_loop(0, MOE_E, drain, 0)

    def issue(r, carry):
        _row_copy(h_ref, r, xs_hbm, pos_ref[0, 0, r], sem).start()
        _row_copy(h_ref, r, xs_hbm, pos_ref[0, 0, CHUNK + r], sem).start()
        return carry

    lax.fori_loop(0, CHUNK, issue, 0, unroll=8)
    for _ in range(2):
        pltpu.make_async_copy(h_ref, xs_hbm.at[pl.ds(0, CHUNK)], sem).wait()


def moe_dispatch(last_tile_start, pos, h, n_sorted):
    n, d = h.shape
    return pl.pallas_call(
        _dispatch_kernel,
        grid_spec=pltpu.PrefetchScalarGridSpec(
            num_scalar_prefetch=1,
            grid=(n // CHUNK,),
            in_specs=[pl.BlockSpec((1, 1, 2 * CHUNK), lambda i, last: (i, 0, 0), memory_space=pltpu.SMEM),
                      pl.BlockSpec((CHUNK, d), lambda i, last: (i, 0))],
            out_specs=pl.BlockSpec(memory_space=pl.ANY),
            scratch_shapes=[pltpu.VMEM((MOE_TM, d), h.dtype), pltpu.SemaphoreType.DMA]),
        out_shape=jax.ShapeDtypeStruct((n_sorted, d), h.dtype),
        compiler_params=_cp(("arbitrary",)),
        name="moe_dispatch",
    )(last_tile_start, pos, h)


def _experts_kernel(te_ref, nt_ref, x_ref, w1_ref, w3_ref, w2_ref, o_ref, w1_sc, w3_sc, w2_sc):
    i = pl.program_id(0)
    active = i < nt_ref[0]

    @pl.when(jnp.logical_and(active, jnp.logical_or(i == 0, te_ref[i] != te_ref[jnp.maximum(i - 1, 0)])))
    def _():
        w1_sc[...] = w1_ref[0, 0].astype(BF16)
        w3_sc[...] = w3_ref[0, 0].astype(BF16)
        w2_sc[...] = w2_ref[0, 0].astype(BF16)

    @pl.when(active)
    def _():
        x = x_ref[...].astype(BF16)
        a = jnp.dot(x, w1_sc[...], preferred_element_type=F32)
        b = jnp.dot(x, w3_sc[...], preferred_element_type=F32)
        he = (_silu(a) * b).astype(BF16)
        o_ref[...] = jnp.dot(he, w2_sc[...], preferred_element_type=F32)

    @pl.when(jnp.logical_not(active))
    def _():
        o_ref[...] = jnp.zeros_like(o_ref)


def moe_experts(tile_expert, n_tiles, xs, w1, w3, w2, layer):
    n_sorted, d = xs.shape
    return pl.pallas_call(
        _experts_kernel,
        grid_spec=pltpu.PrefetchScalarGridSpec(
            num_scalar_prefetch=2,
            grid=(n_sorted // MOE_TM,),
            in_specs=[pl.BlockSpec((MOE_TM, d), lambda i, te, nt: (jnp.minimum(i, nt[0] - 1), 0)),
                      pl.BlockSpec((1, 1, d, MOE_HIDDEN), lambda i, te, nt: (layer, te[i], 0, 0)),
                      pl.BlockSpec((1, 1, d, MOE_HIDDEN), lambda i, te, nt: (layer, te[i], 0, 0)),
                      pl.BlockSpec((1, 1, MOE_HIDDEN, d), lambda i, te, nt: (layer, te[i], 0, 0))],
            out_specs=pl.BlockSpec((MOE_TM, d), lambda i, te, nt: (i, 0)),
            scratch_shapes=[pltpu.VMEM((d, MOE_HIDDEN), BF16), pltpu.VMEM((d, MOE_HIDDEN), BF16),
                            pltpu.VMEM((MOE_HIDDEN, d), BF16)]),
        out_shape=jax.ShapeDtypeStruct((n_sorted, d), F32),
        compiler_params=_cp(("arbitrary",)),
        name="moe_experts",
    )(tile_expert, n_tiles, xs, w1, w3, w2)


def _combine_kernel(pos_ref, meta_ref, s_ref, g_ref, ys_hbm, o_ref, buf_ref, sem):
    def issue(r, carry):
        _row_copy(ys_hbm, pos_ref[0, 0, r], buf_ref.at[0], r, sem).start()
        _row_copy(ys_hbm, pos_ref[0, 0, CHUNK + r], buf_ref.at[1], r, sem).start()
        return carry

    lax.fori_loop(0, CHUNK, issue, 0, unroll=8)
    for slot in range(2):
        pltpu.make_async_copy(ys_hbm.at[pl.ds(0, CHUNK)], buf_ref.at[slot], sem).wait()
    meta = meta_ref[...]
    y = meta[:, 2:3] * buf_ref[0] + meta[:, 3:4] * buf_ref[1]
    o_ref[...] = s_ref[...] + g_ref[0, 0] * y


def moe_combine(pos, meta, s, mod, k_gate, ys, nb):
    n, d = s.shape
    tm = CHUNK
    return pl.pallas_call(
        _combine_kernel,
        grid=(n // tm,),
        in_specs=[pl.BlockSpec((1, 1, 2 * CHUNK), lambda i: (i, 0, 0), memory_space=pltpu.SMEM),
                  pl.BlockSpec((tm, LANES), lambda i: (i, 0)),
                  pl.BlockSpec((tm, d), lambda i: (i, 0)),
                  pl.BlockSpec((1, 1, 1, d), lambda i: (_mod_row(i, tm, nb), k_gate, 0, 0)),
                  pl.BlockSpec(memory_space=pl.ANY)],
        out_specs=pl.BlockSpec((tm, d), lambda i: (i, 0)),
        out_shape=jax.ShapeDtypeStruct((n, d), F32),
        scratch_shapes=[pltpu.VMEM((2, tm, d), F32), pltpu.SemaphoreType.DMA],
        input_output_aliases={2: 0},
        compiler_params=_cp(("arbitrary",)),
        name="moe_combine",
    )(pos, meta, s, mod, ys)


def _na_layer(h, w_qkv, w_o, rpb, s, mod, nb):
    qkv = matmul(h, w_qkv.astype(BF16), BF16)
    o = na_attention(qkv, _na_bias_tables(rpb), nb)
    return matmul_residual(o, w_o.astype(BF16), s, mod, 2, nb)


def _da_layer(h, w_qkv, lam, subln_g, w_o, lambda_init, s, mod, nb):
    qkv = matmul_rope(h, w_qkv.astype(BF16), nb)
    o = da_attention(qkv, lam, subln_g, lambda_init, nb)
    return matmul_residual(o, w_o.astype(BF16), s, mod, 2, nb)


def _ml_layer(h, w_up, conv_w, conv_b, w_q, w_k, w_v, w_gate, b_gate, gn_w, skip, w_down, s, mod, nb):
    up = matmul(h, w_up.astype(BF16), BF16)
    xc, q, k, v = ml_conv_qkv(up, conv_w, conv_b, _block_diag(w_q), _block_diag(w_k), _block_diag(w_v), nb)
    ng = w_gate.shape[1]
    wg = jnp.pad(w_gate, ((0, 0), (0, LANES - ng))).reshape(3, ML_INNER, LANES).astype(BF16)
    bg = jnp.pad(b_gate.astype(F32), (0, LANES - ng)).reshape(1, LANES)
    g = ml_gates(q, k, v, wg, bg)
    gt = g[:, :ng].T
    hs = ml_scan(q, k, v, g, gt, nb)
    a = ml_finish(hs, xc, up, gn_w, skip)
    return matmul_residual(a, w_down.astype(BF16), s, mod, 2, nb)


def _moe_layer(norm_g, w_group, b_group, w_router, b_router, w1, w3, w2, layer, s, mod, nb):
    d = s.shape[1]
    pad = LANES - MOE_E - MOE_GROUPS
    wr = jnp.concatenate([w_router, w_group, jnp.zeros((d, pad), w_router.dtype)], axis=1).astype(F32)
    wr_hi = wr.astype(BF16)
    wr_lo = (wr - wr_hi.astype(F32)).astype(BF16)
    br = jnp.concatenate([b_router, b_group, jnp.zeros((pad,), b_router.dtype)]).astype(F32).reshape(1, LANES)
    h, route = moe_router(s, norm_g.astype(F32), mod, 3, 4, jnp.stack([wr_hi, wr_lo]), br, nb)
    rank, cnt = moe_rank(route)
    n = s.shape[0]
    cnt_e = cnt[0, SEL_LANE:SEL_LANE + MOE_E].astype(jnp.int32)
    gsz = ((cnt_e + MOE_TM - 1) // MOE_TM) * MOE_TM
    ends = jnp.cumsum(gsz)
    off_row = jnp.zeros((1, LANES), F32).at[0, SEL_LANE:SEL_LANE + MOE_E].set((ends - gsz).astype(F32))
    meta = moe_positions(route, rank, off_row)
    n_sorted = 2 * n + MOE_E * MOE_TM
    n_tiles = (ends[-1:] // MOE_TM).astype(jnp.int32)
    tile_ids = jnp.arange(n_sorted // MOE_TM, dtype=jnp.int32)
    tile_expert = jnp.sum((tile_ids[:, None] >= (ends // MOE_TM)[None, :]).astype(jnp.int32), axis=1)
    tile_expert = jnp.minimum(tile_expert, MOE_E - 1)
    pos = meta[:, :2].astype(jnp.int32).reshape(n // CHUNK, CHUNK, 2)
    pos = pos.transpose(0, 2, 1).reshape(n // CHUNK, 1, 2 * CHUNK)
    last_tile_start = jnp.maximum(ends - MOE_TM, 0).astype(jnp.int32)
    xs = moe_dispatch(last_tile_start, pos, h, n_sorted)
    ys = moe_experts(tile_expert, n_tiles, xs, w1.astype(F32), w3.astype(F32), w2.astype(F32), layer)
    return moe_combine(pos, meta, s, mod, 5, ys, nb)


def kernel(x, c, ctx, c_ctx, mod_w, mod_b, norm_g, final_g, na_w_qkv, na_w_o, na_rpb, ml_w_up, ml_conv_w, ml_conv_b, ml_w_q, ml_w_k, ml_w_v, ml_w_gate, ml_b_gate, ml_gn_w, ml_skip, ml_w_down, da_w_qkv, da_lambda, da_subln_g, da_w_o, moe_w_group, moe_b_group, moe_w_router, moe_b_router, moe_w1, moe_w3, moe_w2):
    nb, seq, d = x.shape
    assert (seq, d, ctx.shape[1]) == (SEQ, D_MODEL, CTX_LEN) and nb < 16
    depth = mod_w.shape[0]
    nx = nb * seq
    s = jnp.concatenate([x.reshape(nx, d), ctx.reshape(nb * CTX_LEN, d)], axis=0).astype(F32)
    cvec = jnp.concatenate([c, c_ctx[None, :], jnp.zeros((16 - nb - 1, d), c.dtype)], axis=0).astype(F32)
    mods = mod_vectors(cvec, mod_w, mod_b).reshape(depth, 16, 6, 1, d)

    for i in range(depth):
        kind, j = i % N_MIXERS, i // N_MIXERS
        mod = mods[i]
        h = norm_mod(s, norm_g[i, 0].astype(F32), mod, 0, 1, nb)
        if kind == 0:
            s = _na_layer(h, na_w_qkv[j], na_w_o[j], na_rpb[j], s, mod, nb)
        elif kind == 1:
            s = _ml_layer(h, ml_w_up[j], ml_conv_w[j], ml_conv_b[j], ml_w_q[j], ml_w_k[j], ml_w_v[j],
                          ml_w_gate[j], ml_b_gate[j], ml_gn_w[j], ml_skip[j], ml_w_down[j], s, mod, nb)
        else:
            lambda_init = 0.8 - 0.6 * math.exp(-0.3 * i)
            s = _da_layer(h, da_w_qkv[j], da_lambda[j], da_subln_g[j], da_w_o[j], lambda_init, s, mod, nb)
        s = _moe_layer(norm_g[i, 1], moe_w_group[i], moe_b_group[i], moe_w_router[i], moe_b_router[i],
                       moe_w1, moe_w3, moe_w2, i, s, mod, nb)
    return final_norm(s, final_g.astype(F32), nx).reshape(nb, seq, d)
```

```python
import functools
import math

import numpy as np
import jax
import jax.numpy as jnp
from jax import lax
from jax.experimental import pallas as pl
from jax.experimental.pallas import tpu as pltpu

F32 = jnp.float32
BF16 = jnp.bfloat16

D_MODEL = 1024
SEQ = 4096
CTX_LEN = 256
GRID_W = 64
N_MIXERS = 3
EPS = 1e-6

NA_HEADS = 16
NA_WIN_H = 8
NA_WIN_W = 16

ML_HEADS = 4
ML_INNER = 2 * D_MODEL
ML_HEAD_DIM = ML_INNER // ML_HEADS
ML_BLOCK = 4
ML_CONV_K = 5

DA_HEADS = 8
DA_HEAD_DIM = 64
ROPE_BASE = 10000.0

MOE_GROUPS = 4
MOE_EPG = 8
MOE_E = MOE_GROUPS * MOE_EPG
MOE_HIDDEN = 512
MOE_TM = 512
SEL_LANE = 64

LANES = 128
CHUNK = 256
X_CHUNKS = SEQ // CHUNK
PROJ_TM = 512
DA_TQ = 512
DA_VT_ROWS = 128 + 16
NA_ROWS = 8
NEG = -1e30
VMEM_LIMIT = 56 * 1024 * 1024


def _cp(sem, vmem=VMEM_LIMIT):
    return pltpu.CompilerParams(dimension_semantics=sem, vmem_limit_bytes=vmem)


def _tile(n, pref):
    tm = pref
    while n % tm:
        tm //= 2
    assert tm >= CHUNK
    return tm


def _mod_row(i, tm, nb):
    return jnp.where(i < nb * (SEQ // tm), i // (SEQ // tm), nb)


def _nt(a, b):
    return lax.dot_general(a, b, (((1,), (1,)), ((), ())), preferred_element_type=F32)


def _tn(a, b):
    return lax.dot_general(a, b, (((0,), (0,)), ((), ())), preferred_element_type=F32)


def _silu(x):
    return x * jax.nn.sigmoid(x)


def _head_rows(q, lane):
    zero = jnp.zeros_like(q)
    return jnp.concatenate([jnp.where(lane < 64, q, zero), jnp.where(lane >= 64, q, zero)], axis=0)


def _mod_kernel(c_ref, w_ref, b_ref, o_ref):
    a = _silu(c_ref[...]).astype(BF16)
    o_ref[0] = jnp.dot(a, w_ref[0].astype(BF16), preferred_element_type=F32) + b_ref[0]


def mod_vectors(cvec, mod_w, mod_b):
    depth, d, n6 = mod_w.shape
    tn = 1024
    return pl.pallas_call(
        _mod_kernel,
        grid=(depth, n6 // tn),
        in_specs=[pl.BlockSpec((16, d), lambda l, j: (0, 0)),
                  pl.BlockSpec((1, d, tn), lambda l, j: (l, 0, j)),
                  pl.BlockSpec((1, 1, tn), lambda l, j: (l, 0, j))],
        out_specs=pl.BlockSpec((1, 16, tn), lambda l, j: (l, 0, j)),
        out_shape=jax.ShapeDtypeStruct((depth, 16, n6), F32),
        compiler_params=_cp(("parallel", "parallel")),
        name="mod_vectors",
    )(cvec, mod_w, mod_b.reshape(depth, 1, n6))


def _norm_mod_kernel(x_ref, g_ref, sh_ref, sc_ref, o_ref):
    x = x_ref[...]
    y = x * lax.rsqrt(jnp.mean(x * x, axis=-1, keepdims=True) + EPS) * g_ref[...]
    o_ref[...] = (y * (1.0 + sc_ref[0, 0]) + sh_ref[0, 0]).astype(o_ref.dtype)


def norm_mod(s, g, mod, k_shift, k_scale, nb):
    n, d = s.shape
    tm = _tile(n, PROJ_TM)
    return pl.pallas_call(
        _norm_mod_kernel,
        grid=(n // tm,),
        in_specs=[pl.BlockSpec((tm, d), lambda i: (i, 0)),
                  pl.BlockSpec((1, d), lambda i: (0, 0)),
                  pl.BlockSpec((1, 1, 1, d), lambda i: (_mod_row(i, tm, nb), k_shift, 0, 0)),
                  pl.BlockSpec((1, 1, 1, d), lambda i: (_mod_row(i, tm, nb), k_scale, 0, 0))],
        out_specs=pl.BlockSpec((tm, d), lambda i: (i, 0)),
        out_shape=jax.ShapeDtypeStruct((n, d), BF16),
        compiler_params=_cp(("parallel",)),
        name="norm_mod",
    )(s, g.reshape(1, d), mod, mod)


def _final_norm_kernel(x_ref, g_ref, o_ref):
    x = x_ref[...]
    o_ref[...] = x * lax.rsqrt(jnp.mean(x * x, axis=-1, keepdims=True) + EPS) * g_ref[...]


def final_norm(s, g, n_rows):
    d = s.shape[1]
    tm = _tile(n_rows, PROJ_TM)
    return pl.pallas_call(
        _final_norm_kernel,
        grid=(n_rows // tm,),
        in_specs=[pl.BlockSpec((tm, d), lambda i: (i, 0)),
                  pl.BlockSpec((1, d), lambda i: (0, 0))],
        out_specs=pl.BlockSpec((tm, d), lambda i: (i, 0)),
        out_shape=jax.ShapeDtypeStruct((n_rows, d), F32),
        compiler_params=_cp(("parallel",)),
        name="final_norm",
    )(s, g.reshape(1, d))


def _mm_kernel(a_ref, w_ref, o_ref):
    o_ref[...] = jnp.dot(a_ref[...], w_ref[...], preferred_element_type=F32).astype(o_ref.dtype)


def matmul(a, w, out_dtype, tn=1024):
    n, k = a.shape
    tm = _tile(n, PROJ_TM)
    nout = w.shape[1]
    tn = min(tn, nout)
    return pl.pallas_call(
        _mm_kernel,
        grid=(nout // tn, n // tm),
        in_specs=[pl.BlockSpec((tm, k), lambda j, i: (i, 0)),
                  pl.BlockSpec((k, tn), lambda j, i: (0, j))],
        out_specs=pl.BlockSpec((tm, tn), lambda j, i: (i, j)),
        out_shape=jax.ShapeDtypeStruct((n, nout), out_dtype),
        compiler_params=_cp(("parallel", "parallel")),
        name="matmul",
    )(a, w)


def _mm_res_kernel(a_ref, w_ref, r_ref, g_ref, o_ref):
    acc = jnp.dot(a_ref[...], w_ref[...], preferred_element_type=F32)
    o_ref[...] = r_ref[...] + g_ref[0, 0] * acc


def matmul_residual(a, w, s, mod, k_gate, nb):
    n, k = a.shape
    d = w.shape[1]
    tm = _tile(n, PROJ_TM)
    return pl.pallas_call(
        _mm_res_kernel,
        grid=(n // tm,),
        in_specs=[pl.BlockSpec((tm, k), lambda i: (i, 0)),
                  pl.BlockSpec((k, d), lambda i: (0, 0)),
                  pl.BlockSpec((tm, d), lambda i: (i, 0)),
                  pl.BlockSpec((1, 1, 1, d), lambda i: (_mod_row(i, tm, nb), k_gate, 0, 0))],
        out_specs=pl.BlockSpec((tm, d), lambda i: (i, 0)),
        out_shape=jax.ShapeDtypeStruct((n, d), F32),
        input_output_aliases={2: 0},
        compiler_params=_cp(("parallel",)),
        name="matmul_residual",
    )(a, w, s, mod)


def _na_bias_tables(rpb):
    o = np.arange(NA_WIN_H)[:, None]
    j = np.arange(NA_WIN_H)[None, :]
    dy = j - o + NA_WIN_H - 1
    qc = np.arange(GRID_W)[:, None]
    kc = np.arange(GRID_W)[None, :]
    dx = np.clip(kc - qc, -(NA_WIN_W - 1), NA_WIN_W - 1) + NA_WIN_W - 1
    w_start = np.clip(qc - NA_WIN_W // 2, 0, GRID_W - NA_WIN_W)
    valid = (kc >= w_start) & (kc < w_start + NA_WIN_W)
    tbl = rpb.astype(F32)[:, dy][:, :, :, dx]
    tbl = jnp.where(jnp.asarray(valid)[None, None, None], tbl, NEG)
    tbl = tbl.reshape(NA_HEADS // 2, 2, NA_WIN_H, NA_WIN_H, GRID_W, GRID_W)
    tbl = tbl.transpose(2, 0, 3, 5, 1, 4)
    return tbl.reshape(NA_WIN_H, NA_HEADS // 2, NA_WIN_H * GRID_W, 2 * GRID_W)


def _softmax_cols(s):
    m = jnp.max(s, axis=0, keepdims=True)
    p = jnp.exp(s - m)
    return (p * (1.0 / jnp.sum(p, axis=0, keepdims=True))).astype(BF16)


def _na_x_kernel(q_ref, kx_ref, vx_ref, kc_ref, vc_ref, bias_ref, o_ref, s_sc, p_sc):
    t = pl.program_id(2)
    lane = lax.broadcasted_iota(jnp.int32, (1, LANES), 1)
    nwin = NA_WIN_H * GRID_W
    starts = []
    for i in range(NA_ROWS):
        r = t * NA_ROWS + i
        rs = jnp.clip(r - NA_WIN_H // 2, 0, SEQ // GRID_W - NA_WIN_H)
        start = pl.multiple_of(rs * GRID_W, GRID_W)
        starts.append(start)
        qq = _head_rows(q_ref[i * GRID_W:(i + 1) * GRID_W, :] * 0.125, lane)
        s_sc[i, 0:nwin, :] = _nt(kx_ref[pl.ds(start, nwin), :], qq) + bias_ref[r - rs, 0]
        s_sc[i, nwin:, :] = _nt(kc_ref[...], qq)
    for i in range(NA_ROWS):
        p_sc[i] = _softmax_cols(s_sc[i])
    for i in range(NA_ROWS):
        o = (_tn(p_sc[i, 0:nwin, :], vx_ref[pl.ds(starts[i], nwin), :])
             + _tn(p_sc[i, nwin:, :], vc_ref[...]))
        o = jnp.where(lane < 64, o[0:GRID_W], o[GRID_W:])
        o_ref[i * GRID_W:(i + 1) * GRID_W, :] = o.astype(o_ref.dtype)


def _na_ctx_kernel(q_ref, kc_ref, vc_ref, o_in_hbm, o_ref):
    del o_in_hbm
    lane = lax.broadcasted_iota(jnp.int32, (1, LANES), 1)
    q = q_ref[...] * 0.125
    outs = []
    for hh in range(2):
        qm = jnp.where((lane // 64) == hh, q, jnp.zeros_like(q))
        p = _softmax_cols(_nt(kc_ref[...], qm))
        outs.append(_tn(p, vc_ref[...]))
    o_ref[...] = jnp.where(lane < 64, outs[0], outs[1]).astype(o_ref.dtype)


def na_attention(qkv, bias, nb):
    n = qkv.shape[0]
    d = D_MODEL
    ncb = d // LANES
    xc = nb * X_CHUNKS
    nkeys = NA_WIN_H * GRID_W + CTX_LEN
    tq = NA_ROWS * GRID_W
    steps = SEQ // tq
    o = pl.pallas_call(
        _na_x_kernel,
        grid=(nb, ncb, steps),
        in_specs=[pl.BlockSpec((tq, LANES), lambda b, h, t: (b * steps + t, h)),
                  pl.BlockSpec((SEQ, LANES), lambda b, h, t: (b, ncb + h)),
                  pl.BlockSpec((SEQ, LANES), lambda b, h, t: (b, 2 * ncb + h)),
                  pl.BlockSpec((CHUNK, LANES), lambda b, h, t: (xc + b, ncb + h)),
                  pl.BlockSpec((CHUNK, LANES), lambda b, h, t: (xc + b, 2 * ncb + h)),
                  pl.BlockSpec((NA_WIN_H, 1, NA_WIN_H * GRID_W, LANES), lambda b, h, t: (0, h, 0, 0))],
        out_specs=pl.BlockSpec((tq, LANES), lambda b, h, t: (b * steps + t, h)),
        out_shape=jax.ShapeDtypeStruct((n, d), BF16),
        scratch_shapes=[pltpu.VMEM((NA_ROWS, nkeys, LANES), F32),
                        pltpu.VMEM((NA_ROWS, nkeys, LANES), BF16)],
        compiler_params=_cp(("parallel", "parallel", "arbitrary")),
        name="na_attention_x",
    )(qkv, qkv, qkv, qkv, qkv, bias)
    return pl.pallas_call(
        _na_ctx_kernel,
        grid=(nb, ncb),
        in_specs=[pl.BlockSpec((CHUNK, LANES), lambda b, h: (xc + b, h)),
                  pl.BlockSpec((CHUNK, LANES), lambda b, h: (xc + b, ncb + h)),
                  pl.BlockSpec((CHUNK, LANES), lambda b, h: (xc + b, 2 * ncb + h)),
                  pl.BlockSpec(memory_space=pl.ANY)],
        out_specs=pl.BlockSpec((CHUNK, LANES), lambda b, h: (xc + b, h)),
        out_shape=jax.ShapeDtypeStruct((n, d), BF16),
        input_output_aliases={3: 0},
        compiler_params=_cp(("parallel", "parallel")),
        name="na_attention_ctx",
    )(qkv, qkv, qkv, o)


def _rope_tables(tm):
    half = DA_HEAD_DIM // 2
    freqs = ROPE_BASE ** (-np.arange(0, half, 2, dtype=np.float32) / half)
    t = np.arange(SEQ)
    row, col = t // GRID_W, t % GRID_W
    lane = np.arange(LANES)
    l64 = lane % DA_HEAD_DIM
    use_col = (l64 // half) == 1
    l32 = l64 % half
    fi = l32 % (half // 2)
    second = l32 >= half // 2
    pos = np.where(use_col[None, :], col[:, None], row[:, None]).astype(np.float32)
    ang = pos * freqs[fi][None, :]
    cos = np.cos(ang).astype(np.float32)
    sin = np.sin(ang).astype(np.float32)
    sa = np.where(second[None, :], 0.0, -sin).astype(np.float32)
    sb = np.where(second[None, :], sin, 0.0).astype(np.float32)
    ident = np.ones((tm, LANES), np.float32)
    zero = np.zeros((tm, LANES), np.float32)
    return (jnp.asarray(np.concatenate([cos, ident])),
            jnp.asarray(np.concatenate([sa, zero])),
            jnp.asarray(np.concatenate([sb, zero])))


def _mm_rope_kernel(a_ref, w_ref, c_ref, sa_ref, sb_ref, o_ref):
    j = pl.program_id(0)
    acc = jnp.dot(a_ref[...], w_ref[...], preferred_element_type=F32)

    @pl.when(j < 2)
    def _():
        scale = jnp.where(j == 0, DA_HEAD_DIM ** -0.5 * math.log2(math.e), 1.0)
        cos = c_ref[...] * scale
        sa = sa_ref[...] * scale
        sb = sb_ref[...] * scale
        for g in range(acc.shape[1] // LANES):
            sl = slice(g * LANES, (g + 1) * LANES)
            x = acc[:, sl]
            y = x * cos + pltpu.roll(x, LANES - 16, 1) * sa + pltpu.roll(x, 16, 1) * sb
            o_ref[:, sl] = y.astype(o_ref.dtype)

    @pl.when(j >= 2)
    def _():
        o_ref[...] = acc.astype(o_ref.dtype)


def matmul_rope(a, w, nb):
    n, k = a.shape
    d = D_MODEL
    tm = _tile(n, PROJ_TM)
    cos, sa, sb = _rope_tables(tm)
    xt = nb * (SEQ // tm)

    def tab(j, i):
        return (jnp.where(i < xt, i % (SEQ // tm), SEQ // tm), 0)

    tspec = pl.BlockSpec((tm, LANES), tab)
    return pl.pallas_call(
        _mm_rope_kernel,
        grid=(3, n // tm),
        in_specs=[pl.BlockSpec((tm, k), lambda j, i: (i, 0)),
                  pl.BlockSpec((k, d), lambda j, i: (0, j)),
                  tspec, tspec, tspec],
        out_specs=pl.BlockSpec((tm, d), lambda j, i: (i, j)),
        out_shape=jax.ShapeDtypeStruct((n, 3 * d), BF16),
        compiler_params=_cp(("parallel", "parallel")),
        name="matmul_rope",
    )(a, w, cos, sa, sb)


def _da_body(q_ref, lam_ref, g_ref, o_ref, s_sc, acc_sc, m_sc, lambda_init, run_chunks):
    tq = q_ref.shape[0]
    lane = lax.broadcasted_iota(jnp.int32, (1, LANES), 1)
    qq = _head_rows(q_ref[...], lane)
    m_sc[...] = jnp.full_like(m_sc, NEG)
    acc_sc[...] = jnp.zeros_like(acc_sc)

    def scores(slot, k):
        s_sc[slot] = _nt(k, qq)

    def fold(slot, vt):
        for g in range(2 * tq // CHUNK):
            sl = slice(g * CHUNK, (g + 1) * CHUNK)
            s = s_sc[slot, :, sl]
            m_old = m_sc[:, sl]
            m_new = jnp.maximum(m_old, jnp.max(s, axis=0, keepdims=True))
            alpha = jnp.exp2(m_old - m_new)
            p = jnp.exp2(s - m_new)
            m_sc[:, sl] = m_new
            acc_sc[:, sl] = acc_sc[:, sl] * alpha + jnp.dot(vt, p.astype(BF16), preferred_element_type=F32)

    run_chunks(scores, fold)

    ot = acc_sc[0:LANES, :] / acc_sc[LANES:LANES + 1, :]
    lam = lam_ref[...]
    lam_full = (jnp.exp(jnp.sum(lam[0:1] * lam[1:2], axis=1, keepdims=True))
                - jnp.exp(jnp.sum(lam[2:3] * lam[3:4], axis=1, keepdims=True)) + lambda_init)
    od = ot[:, :tq] - lam_full * ot[:, tq:]
    y = od * lax.rsqrt(jnp.mean(od * od, axis=0, keepdims=True) + EPS) * g_ref[...]
    o_ref[...] = (y * (1.0 - lambda_init)).T.astype(o_ref.dtype)


def _da_x_kernel(q_ref, kx_ref, kc_ref, vtx_ref, vtc_ref, lam_ref, g_ref, o_ref,
                 s_sc, acc_sc, m_sc, *, lambda_init):
    def kx(c):
        return kx_ref[pl.ds(pl.multiple_of(c * CHUNK, CHUNK), CHUNK), :]

    def run_chunks(scores, fold):
        scores(0, kx(0))

        def body(i, carry):
            scores(1, kx(2 * i + 1))
            fold(0, vtx_ref[0, 2 * i])
            scores(0, kx(2 * i + 2))
            fold(1, vtx_ref[0, 2 * i + 1])
            return carry

        lax.fori_loop(0, X_CHUNKS // 2 - 1, body, 0)
        scores(1, kx(X_CHUNKS - 1))
        fold(0, vtx_ref[0, X_CHUNKS - 2])
        scores(0, kc_ref[...])
        fold(1, vtx_ref[0, X_CHUNKS - 1])
        fold(0, vtc_ref[0, 0])

    _da_body(q_ref, lam_ref, g_ref, o_ref, s_sc, acc_sc, m_sc, lambda_init, run_chunks)


def _da_ctx_kernel(q_ref, kc_ref, vtc_ref, lam_ref, g_ref, o_in_hbm, o_ref,
                   s_sc, acc_sc, m_sc, *, lambda_init):
    del o_in_hbm

    def run_chunks(scores, fold):
        scores(0, kc_ref[...])
        fold(0, vtc_ref[0, 0])

    _da_body(q_ref, lam_ref, g_ref, o_ref, s_sc, acc_sc, m_sc, lambda_init, run_chunks)


def _da_scratch(tq):
    return [pltpu.VMEM((2, CHUNK, 2 * tq), F32), pltpu.VMEM((DA_VT_ROWS, 2 * tq), F32),
            pltpu.VMEM((1, 2 * tq), F32)]


def da_attention(qkv, lam, subln_g, lambda_init, nb):
    n = qkv.shape[0]
    d = D_MODEL
    ncb = d // LANES
    xc = nb * X_CHUNKS
    tq = DA_TQ
    vt = qkv[:, 2 * d:].reshape(n // CHUNK, CHUNK, ncb, LANES).transpose(2, 0, 3, 1)
    extra = jnp.zeros((ncb, n // CHUNK, DA_VT_ROWS - LANES, CHUNK), BF16).at[:, :, 0, :].set(1.0)
    vt = jnp.concatenate([vt, extra], axis=2)
    lam = lam.astype(F32)
    gcol = subln_g.astype(F32).reshape(LANES, 1)
    o = pl.pallas_call(
        functools.partial(_da_x_kernel, lambda_init=lambda_init),
        grid=(nb, ncb, SEQ // tq),
        in_specs=[pl.BlockSpec((tq, LANES), lambda b, h, t: (b * (SEQ // tq) + t, h)),
                  pl.BlockSpec((SEQ, LANES), lambda b, h, t: (b, ncb + h)),
                  pl.BlockSpec((CHUNK, LANES), lambda b, h, t: (xc + b, ncb + h)),
                  pl.BlockSpec((1, X_CHUNKS, DA_VT_ROWS, CHUNK), lambda b, h, t: (h, b, 0, 0)),
                  pl.BlockSpec((1, 1, DA_VT_ROWS, CHUNK), lambda b, h, t: (h, xc + b, 0, 0)),
                  pl.BlockSpec((4, DA_HEAD_DIM), lambda b, h, t: (0, 0)),
                  pl.BlockSpec((LANES, 1), lambda b, h, t: (0, 0))],
        out_specs=pl.BlockSpec((tq, LANES), lambda b, h, t: (b * (SEQ // tq) + t, h)),
        out_shape=jax.ShapeDtypeStruct((n, d), BF16),
        scratch_shapes=_da_scratch(tq),
        compiler_params=_cp(("parallel", "parallel", "arbitrary")),
        name="da_attention_x",
    )(qkv, qkv, qkv, vt, vt, lam, gcol)
    return pl.pallas_call(
        functools.partial(_da_ctx_kernel, lambda_init=lambda_init),
        grid=(nb, ncb),
        in_specs=[pl.BlockSpec((CHUNK, LANES), lambda b, h: (xc + b, h)),
                  pl.BlockSpec((CHUNK, LANES), lambda b, h: (xc + b, ncb + h)),
                  pl.BlockSpec((1, 1, DA_VT_ROWS, CHUNK), lambda b, h: (h, xc + b, 0, 0)),
                  pl.BlockSpec((4, DA_HEAD_DIM), lambda b, h: (0, 0)),
                  pl.BlockSpec((LANES, 1), lambda b, h: (0, 0)),
                  pl.BlockSpec(memory_space=pl.ANY)],
        out_specs=pl.BlockSpec((CHUNK, LANES), lambda b, h: (xc + b, h)),
        out_shape=jax.ShapeDtypeStruct((n, d), BF16),
        scratch_shapes=_da_scratch(CHUNK),
        input_output_aliases={5: 0},
        compiler_params=_cp(("parallel", "parallel")),
        name="da_attention_ctx",
    )(qkv, qkv, vt, lam, gcol, o)


def _block_diag(w):
    per = LANES // ML_BLOCK
    wr = w.reshape(ML_INNER // LANES, per, ML_BLOCK, ML_BLOCK)
    eye = jnp.eye(per, dtype=w.dtype)
    return jnp.einsum('cgio,gh->cgiho', wr, eye).reshape(ML_INNER // LANES, LANES, LANES).astype(BF16)


def _ml_conv_kernel(p_ref, c_ref, n_ref, cw_ref, cb_ref, wq_ref, wk_ref, wv_ref,
                    xc_ref, q_ref, k_ref, v_ref, *, n_xchunks):
    i = pl.program_id(0)
    is_x = i < n_xchunks
    j = i % X_CHUNKS
    halo = 16
    cur = c_ref[...].astype(F32)
    prev = p_ref[CHUNK - halo:CHUNK, :].astype(F32)
    nxt = n_ref[0:halo, :].astype(F32)
    prev = jnp.where(jnp.logical_and(is_x, j > 0), prev, jnp.zeros_like(prev))
    nxt = jnp.where(jnp.logical_and(is_x, j < X_CHUNKS - 1), nxt, jnp.zeros_like(nxt))
    xp = jnp.concatenate([prev, cur, nxt], axis=0)
    rows = CHUNK + 2 * halo
    y = cb_ref[...] + cw_ref[ML_CONV_K // 2:ML_CONV_K // 2 + 1, :] * cur
    for tap in range(ML_CONV_K):
        dlt = tap - ML_CONV_K // 2
        if dlt == 0:
            continue
        shifted = pltpu.roll(xp, (-dlt) % rows, 0)[halo:halo + CHUNK]
        y = y + cw_ref[tap:tap + 1, :] * shifted
    xcb = _silu(y).astype(BF16)
    xc_ref[...] = xcb
    xm = c_ref[...]
    for s in range(xcb.shape[1] // LANES):
        sl = slice(s * LANES, (s + 1) * LANES)
        q_ref[:, sl] = jnp.dot(xcb[:, sl], wq_ref[s], preferred_element_type=F32).astype(BF16)
        k_ref[:, sl] = jnp.dot(xcb[:, sl], wk_ref[s], preferred_element_type=F32).astype(BF16)
        v_ref[:, sl] = jnp.dot(xm[:, sl], wv_ref[s], preferred_element_type=F32).astype(BF16)


def ml_conv_qkv(up, conv_w, conv_b, wq, wk, wv, nb):
    n = up.shape[0]
    nchunks = n // CHUNK
    cw = 512
    ncb = ML_INNER // cw
    sub = cw // LANES
    blk = pl.BlockSpec((CHUNK, cw), lambda i, c: (i, c))
    wspec = pl.BlockSpec((sub, LANES, LANES), lambda i, c: (c, 0, 0))
    out = jax.ShapeDtypeStruct((n, ML_INNER), BF16)
    return pl.pallas_call(
        functools.partial(_ml_conv_kernel, n_xchunks=nb * X_CHUNKS),
        grid=(nchunks, ncb),
        in_specs=[pl.BlockSpec((CHUNK, cw), lambda i, c: (jnp.maximum(i - 1, 0), c)),
                  blk,
                  pl.BlockSpec((CHUNK, cw), lambda i, c: (jnp.minimum(i + 1, nchunks - 1), c)),
                  pl.BlockSpec((ML_CONV_K, cw), lambda i, c: (0, c)),
                  pl.BlockSpec((1, cw), lambda i, c: (0, c)),
                  wspec, wspec, wspec],
        out_specs=[blk, blk, blk, blk],
        out_shape=[out, out, out, out],
        compiler_params=_cp(("parallel", "parallel")),
        name="ml_conv_qkv",
    )(up, up, up, conv_w.astype(F32), conv_b.astype(F32).reshape(1, ML_INNER), wq, wk, wv)


def _ml_gate_kernel(q_ref, k_ref, v_ref, w_ref, b_ref, o_ref):
    acc = jnp.dot(q_ref[...], w_ref[0], preferred_element_type=F32)
    acc = acc + jnp.dot(k_ref[...], w_ref[1], preferred_element_type=F32)
    acc = acc + jnp.dot(v_ref[...], w_ref[2], preferred_element_type=F32)
    o_ref[...] = acc + b_ref[...]


def ml_gates(q, k, v, wg, bg):
    n = q.shape[0]
    tm = _tile(n, PROJ_TM)
    blk = pl.BlockSpec((tm, ML_INNER), lambda i: (i, 0))
    return pl.pallas_call(
        _ml_gate_kernel,
        grid=(n // tm,),
        in_specs=[blk, blk, blk,
                  pl.BlockSpec((3, ML_INNER, LANES), lambda i: (0, 0, 0)),
                  pl.BlockSpec((1, LANES), lambda i: (0, 0))],
        out_specs=pl.BlockSpec((tm, LANES), lambda i: (i, 0)),
        out_shape=jax.ShapeDtypeStruct((n, LANES), F32),
        compiler_params=_cp(("parallel",)),
        name="ml_gates",
    )(q, k, v, wg, bg)


def _log_sigmoid(x):
    return jnp.minimum(x, 0.0) - jnp.log(1.0 + jnp.exp(-jnp.abs(x)))


def _ml_scan_kernel(q_ref, k_ref, v_ref, g_ref, gt_ref, o_ref, c_sc, n_sc, m_sc):
    h = pl.program_id(1)
    d = pl.program_id(2)
    p = pl.program_id(3)
    ninf = -jnp.inf

    @pl.when(p == 0)
    def _():
        c_sc[...] = jnp.zeros_like(c_sc)
        n_sc[...] = jnp.zeros_like(n_sc)
        m_sc[...] = jnp.full_like(m_sc, ninf)

    L = CHUNK
    col_i = d * 2 * ML_HEADS + h
    col_f = col_i + ML_HEADS
    lane = lax.broadcasted_iota(jnp.int32, (1, LANES), 1)
    g = g_ref[...]
    i_col = jnp.sum(jnp.where(lane == col_i, g, 0.0), axis=1, keepdims=True)
    f_col = _log_sigmoid(jnp.sum(jnp.where(lane == col_f, g, 0.0), axis=1, keepdims=True))
    i_row = gt_ref[pl.ds(col_i, 1), :]
    f_row = _log_sigmoid(gt_ref[pl.ds(col_f, 1), :])

    jj = lax.broadcasted_iota(jnp.int32, (L, L), 0)
    ss = lax.broadcasted_iota(jnp.int32, (L, L), 1)
    fwd = d == 0
    bwd = d == 1
    valid = jnp.logical_or(jnp.logical_and(ss <= jj, fwd), jnp.logical_and(ss >= jj, bwd))
    valid_t = jnp.logical_or(jnp.logical_and(jj <= ss, fwd), jnp.logical_and(jj >= ss, bwd))
    bcum_col = jnp.sum(jnp.where(valid, f_row, 0.0), axis=1, keepdims=True)
    bcum_row = jnp.sum(jnp.where(valid_t, f_col, 0.0), axis=0, keepdims=True)
    b_last = jnp.sum(f_row, axis=1, keepdims=True)
    m_prev = m_sc[...]

    logd = jnp.where(valid, bcum_col - bcum_row + i_row, ninf)
    log_inter = bcum_col + m_prev
    m_j = jnp.maximum(log_inter, jnp.max(logd, axis=1, keepdims=True))
    dmat = jnp.exp(logd - m_j)
    inter = jnp.exp(log_inter - m_j)

    q = q_ref[...]
    k = k_ref[...]
    v = v_ref[...]
    scale = ML_HEAD_DIM ** -0.5
    sc = _nt(q, k) * scale * dmat
    c_prev = c_sc[...]
    n_prev = n_sc[...]
    qc = jnp.dot(q, c_prev.astype(BF16), preferred_element_type=F32) * scale
    num = inter * qc + jnp.dot(sc.astype(BF16), v, preferred_element_type=F32)
    qn = jnp.sum(q.astype(F32) * n_prev, axis=1, keepdims=True) * scale
    den = inter * qn + jnp.sum(sc, axis=1, keepdims=True)
    o_ref[0] = num / jnp.maximum(jnp.abs(den), jnp.exp(-m_j))

    ls = b_last - bcum_col + i_col
    m_new = jnp.maximum(b_last + m_prev, jnp.max(ls, axis=0, keepdims=True))
    w = jnp.exp(ls - m_new)
    decay = jnp.exp(b_last + m_prev - m_new)
    kw = k.astype(F32) * w
    c_sc[...] = decay * c_prev + _tn(kw.astype(BF16), v)
    n_sc[...] = decay * n_prev + jnp.sum(kw, axis=0, keepdims=True)
    m_sc[...] = m_new


def ml_scan(q, k, v, g, gt, nb):
    n = q.shape[0]
    xc = nb * X_CHUNKS

    def cidx(b, d, p):
        xi = jnp.where(d == 0, p - 1, X_CHUNKS - p)
        return jnp.where(p == 0, xc + b, b * X_CHUNKS + xi)

    blk = pl.BlockSpec((CHUNK, ML_HEAD_DIM), lambda b, h, d, p: (cidx(b, d, p), h))
    return pl.pallas_call(
        _ml_scan_kernel,
        grid=(nb, ML_HEADS, 2, X_CHUNKS + 1),
        in_specs=[blk, blk, blk,
                  pl.BlockSpec((CHUNK, LANES), lambda b, h, d, p: (cidx(b, d, p), 0)),
                  pl.BlockSpec((16, CHUNK), lambda b, h, d, p: (0, cidx(b, d, p)))],
        out_specs=pl.BlockSpec((1, CHUNK, ML_HEAD_DIM), lambda b, h, d, p: (d, cidx(b, d, p), h)),
        out_shape=jax.ShapeDtypeStruct((2, n, ML_INNER), F32),
        scratch_shapes=[pltpu.VMEM((ML_HEAD_DIM, ML_HEAD_DIM), F32),
                        pltpu.VMEM((1, ML_HEAD_DIM), F32),
                        pltpu.VMEM((1, 1), F32)],
        compiler_params=_cp(("parallel", "parallel", "parallel", "arbitrary")),
        name="ml_scan",
    )(q, k, v, g, gt)


def _ml_finish_kernel(h_ref, xc_ref, z_ref, gn_ref, sk_ref, o_ref):
    hh = h_ref[0] + h_ref[1]
    z = z_ref[...].astype(F32)
    gate = _silu(z)
    for hd in range(ML_HEADS):
        sl = slice(hd * ML_HEAD_DIM, (hd + 1) * ML_HEAD_DIM)
        seg = hh[:, sl]
        mu = jnp.mean(seg, axis=-1, keepdims=True)
        cen = seg - mu
        var = jnp.mean(cen * cen, axis=-1, keepdims=True)
        hn = cen * lax.rsqrt(var + EPS) * gn_ref[:, sl]
        a = (hn + sk_ref[:, sl] * xc_ref[:, sl].astype(F32)) * gate[:, sl]
        o_ref[:, sl] = a.astype(o_ref.dtype)


def ml_finish(hs, xc, up, gn_w, skip):
    n = xc.shape[0]
    tm = CHUNK
    vec = pl.BlockSpec((1, ML_INNER), lambda i: (0, 0))
    return pl.pallas_call(
        _ml_finish_kernel,
        grid=(n // tm,),
        in_specs=[pl.BlockSpec((2, tm, ML_INNER), lambda i: (0, i, 0)),
                  pl.BlockSpec((tm, ML_INNER), lambda i: (i, 0)),
                  pl.BlockSpec((tm, ML_INNER), lambda i: (i, 1)),
                  vec, vec],
        out_specs=pl.BlockSpec((tm, ML_INNER), lambda i: (i, 0)),
        out_shape=jax.ShapeDtypeStruct((n, ML_INNER), BF16),
        compiler_params=_cp(("parallel",)),
        name="ml_finish",
    )(hs, xc, up, gn_w.astype(F32).reshape(1, ML_INNER), skip.astype(F32).reshape(1, ML_INNER))


def _router_kernel(x_ref, g_ref, sh_ref, sc_ref, w_ref, b_ref, h_ref, o_ref):
    x = x_ref[...]
    y = x * lax.rsqrt(jnp.mean(x * x, axis=-1, keepdims=True) + EPS) * g_ref[...]
    hf = y * (1.0 + sc_ref[0, 0]) + sh_ref[0, 0]
    h_hi = hf.astype(BF16)
    h_ref[...] = hf
    h_lo = (hf - h_hi.astype(F32)).astype(BF16)
    logits = (jnp.dot(h_hi, w_ref[0], preferred_element_type=F32)
              + jnp.dot(h_hi, w_ref[1], preferred_element_type=F32)
              + jnp.dot(h_lo, w_ref[0], preferred_element_type=F32)) + b_ref[...]
    lane = lax.broadcasted_iota(jnp.int32, (1, LANES), 1).astype(F32)
    big = 1e9
    ninf = -jnp.inf
    is_g = jnp.logical_and(lane >= MOE_E, lane < MOE_E + MOE_GROUPS)
    gl = jnp.where(is_g, logits, ninf)
    gmax = jnp.max(gl, axis=1, keepdims=True)
    g_val = 1.0 / jnp.sum(jnp.exp(gl - gmax), axis=1, keepdims=True)
    g_idx = jnp.min(jnp.where(gl == gmax, lane, big), axis=1, keepdims=True) - MOE_E
    lo = g_idx * MOE_EPG
    sel = jnp.logical_and(lane >= lo, lane < lo + MOE_EPG)
    el = jnp.where(sel, logits, ninf)
    e1 = jnp.max(el, axis=1, keepdims=True)
    esum = jnp.sum(jnp.exp(el - e1), axis=1, keepdims=True)
    i1 = jnp.min(jnp.where(el == e1, lane, big), axis=1, keepdims=True)
    el2 = jnp.where(lane == i1, ninf, el)
    e2 = jnp.max(el2, axis=1, keepdims=True)
    i2 = jnp.min(jnp.where(el2 == e2, lane, big), axis=1, keepdims=True)
    p1 = 1.0 / esum
    p2 = jnp.exp(e2 - e1) / esum
    w1 = g_val * p1 / (p1 + p2)
    w2 = g_val * p2 / (p1 + p2)
    gates = jnp.where(lane == i1, w1, 0.0) + jnp.where(lane == i2, w2, 0.0)
    marks = jnp.where(jnp.logical_or(lane == i1 + SEL_LANE, lane == i2 + SEL_LANE), 1.0, 0.0)
    o_ref[...] = gates + marks


def moe_router(s, g, mod, k_shift, k_scale, wr, br, nb):
    n, d = s.shape
    tm = _tile(n, PROJ_TM)
    return pl.pallas_call(
        _router_kernel,
        grid=(n // tm,),
        in_specs=[pl.BlockSpec((tm, d), lambda i: (i, 0)),
                  pl.BlockSpec((1, d), lambda i: (0, 0)),
                  pl.BlockSpec((1, 1, 1, d), lambda i: (_mod_row(i, tm, nb), k_shift, 0, 0)),
                  pl.BlockSpec((1, 1, 1, d), lambda i: (_mod_row(i, tm, nb), k_scale, 0, 0)),
                  pl.BlockSpec((2, d, LANES), lambda i: (0, 0, 0)),
                  pl.BlockSpec((1, LANES), lambda i: (0, 0))],
        out_specs=[pl.BlockSpec((tm, d), lambda i: (i, 0)),
                   pl.BlockSpec((tm, LANES), lambda i: (i, 0))],
        out_shape=[jax.ShapeDtypeStruct((n, d), F32),
                   jax.ShapeDtypeStruct((n, LANES), F32)],
        compiler_params=_cp(("parallel",)),
        name="moe_router",
    )(s, g.reshape(1, d), mod, mod, wr, br)


def _rank_kernel(route_ref, rank_ref, cnt_ref, carry_ref):
    i = pl.program_id(0)

    @pl.when(i == 0)
    def _():
        carry_ref[...] = jnp.zeros_like(carry_ref)

    tm = route_ref.shape[0]
    lane = lax.broadcasted_iota(jnp.int32, (1, LANES), 1)
    marks = jnp.where(lane >= SEL_LANE, route_ref[...], 0.0)
    rr = lax.broadcasted_iota(jnp.int32, (tm, tm), 0)
    cc = lax.broadcasted_iota(jnp.int32, (tm, tm), 1)
    below = jnp.where(cc < rr, 1.0, 0.0).astype(BF16)
    rank_ref[...] = jnp.dot(below, marks.astype(BF16), preferred_element_type=F32) + carry_ref[...]
    carry_ref[...] += jnp.sum(marks, axis=0, keepdims=True)
    cnt_ref[...] = carry_ref[...]


def moe_rank(route):
    n = route.shape[0]
    tm = _tile(n, PROJ_TM)
    return pl.pallas_call(
        _rank_kernel,
        grid=(n // tm,),
        in_specs=[pl.BlockSpec((tm, LANES), lambda i: (i, 0))],
        out_specs=[pl.BlockSpec((tm, LANES), lambda i: (i, 0)),
                   pl.BlockSpec((1, LANES), lambda i: (0, 0))],
        out_shape=[jax.ShapeDtypeStruct((n, LANES), F32),
                   jax.ShapeDtypeStruct((1, LANES), F32)],
        scratch_shapes=[pltpu.VMEM((1, LANES), F32)],
        compiler_params=_cp(("arbitrary",)),
        name="moe_rank",
    )(route)


def _pos_kernel(route_ref, rank_ref, off_ref, o_ref):
    lane = lax.broadcasted_iota(jnp.int32, (1, LANES), 1).astype(F32)
    route = route_ref[...]
    marked = jnp.logical_and(lane >= SEL_LANE, route > 0.5)
    p = rank_ref[...] + off_ref[...]
    lane_a = jnp.min(jnp.where(marked, lane, 1e9), axis=1, keepdims=True)
    lane_b = jnp.max(jnp.where(marked, lane, -1.0), axis=1, keepdims=True)

    def pick(src, at):
        return jnp.sum(jnp.where(lane == at, src, 0.0), axis=1, keepdims=True)

    out = jnp.where(lane == 0.0, pick(p, lane_a), 0.0)
    out = out + jnp.where(lane == 1.0, pick(p, lane_b), 0.0)
    out = out + jnp.where(lane == 2.0, pick(route, lane_a - SEL_LANE), 0.0)
    out = out + jnp.where(lane == 3.0, pick(route, lane_b - SEL_LANE), 0.0)
    o_ref[...] = out


def moe_positions(route, rank, off):
    n = route.shape[0]
    tm = _tile(n, PROJ_TM)
    blk = pl.BlockSpec((tm, LANES), lambda i: (i, 0))
    return pl.pallas_call(
        _pos_kernel,
        grid=(n // tm,),
        in_specs=[blk, blk, pl.BlockSpec((1, LANES), lambda i: (0, 0))],
        out_specs=blk,
        out_shape=jax.ShapeDtypeStruct((n, LANES), F32),
        compiler_params=_cp(("parallel",)),
        name="moe_positions",
    )(route, rank, off)


def _row_copy(src, src_row, dst, dst_row, sem):
    return pltpu.make_async_copy(src.at[pl.ds(src_row, 1)], dst.at[pl.ds(dst_row, 1)], sem)


def _dispatch_kernel(last_ref, pos_ref, h_ref, xs_hbm, zero_sc, sem):
    @pl.when(pl.program_id(0) == 0)
    def _():
        zero_sc[...] = jnp.zeros_like(zero_sc)

        def fill(e, carry):
            start = pl.multiple_of(last_ref[e], MOE_TM)
            pltpu.make_async_copy(zero_sc, xs_hbm.at[pl.ds(start, MOE_TM)], sem).start()
            return carry

        lax.fori_loop(0, MOE_E, fill, 0)

        def drain(e, carry):
            pltpu.make_async_copy(zero_sc, xs_hbm.at[pl.ds(0, MOE_TM)], sem).wait()
            return carry

        lax.fori_loop(0, MOE_E, drain, 0)

    def issue(r, carry):
        _row_copy(h_ref, r, xs_hbm, pos_ref[0, 0, r], sem).start()
        _row_copy(h_ref, r, xs_hbm, pos_ref[0, 0, CHUNK + r], sem).start()
        return carry

    lax.fori_loop(0, CHUNK, issue, 0, unroll=8)
    for _ in range(2):
        pltpu.make_async_copy(h_ref, xs_hbm.at[pl.ds(0, CHUNK)], sem).wait()


def moe_dispatch(last_tile_start, pos, h, n_sorted):
    n, d = h.shape
    return pl.pallas_call(
        _dispatch_kernel,
        grid_spec=pltpu.PrefetchScalarGridSpec(
            num_scalar_prefetch=1,
            grid=(n // CHUNK,),
            in_specs=[pl.BlockSpec((1, 1, 2 * CHUNK), lambda i, last: (i, 0, 0), memory_space=pltpu.SMEM),
                      pl.BlockSpec((CHUNK, d), lambda i, last: (i, 0))],
            out_specs=pl.BlockSpec(memory_space=pl.ANY),
            scratch_shapes=[pltpu.VMEM((MOE_TM, d), h.dtype), pltpu.SemaphoreType.DMA]),
        out_shape=jax.ShapeDtypeStruct((n_sorted, d), h.dtype),
        compiler_params=_cp(("arbitrary",)),
        name="moe_dispatch",
    )(last_tile_start, pos, h)


def _experts_kernel(te_ref, nt_ref, x_ref, w1_ref, w3_ref, w2_ref, o_ref, w1_sc, w3_sc, w2_sc):
    i = pl.program_id(0)
    active = i < nt_ref[0]

    @pl.when(jnp.logical_and(active, jnp.logical_or(i == 0, te_ref[i] != te_ref[jnp.maximum(i - 1, 0)])))
    def _():
        w1_sc[...] = w1_ref[0, 0].astype(BF16)
        w3_sc[...] = w3_ref[0, 0].astype(BF16)
        w2_sc[...] = w2_ref[0, 0].astype(BF16)

    @pl.when(active)
    def _():
        x = x_ref[...].astype(BF16)
        a = jnp.dot(x, w1_sc[...], preferred_element_type=F32)
        b = jnp.dot(x, w3_sc[...], preferred_element_type=F32)
        he = (_silu(a) * b).astype(BF16)
        o_ref[...] = jnp.dot(he, w2_sc[...], preferred_element_type=F32)

    @pl.when(jnp.logical_not(active))
    def _():
        o_ref[...] = jnp.zeros_like(o_ref)


def moe_experts(tile_expert, n_tiles, xs, w1, w3, w2, layer):
    n_sorted, d = xs.shape
    return pl.pallas_call(
        _experts_kernel,
        grid_spec=pltpu.PrefetchScalarGridSpec(
            num_scalar_prefetch=2,
            grid=(n_sorted // MOE_TM,),
            in_specs=[pl.BlockSpec((MOE_TM, d), lambda i, te, nt: (jnp.minimum(i, nt[0] - 1), 0)),
                      pl.BlockSpec((1, 1, d, MOE_HIDDEN), lambda i, te, nt: (layer, te[i], 0, 0)),
                      pl.BlockSpec((1, 1, d, MOE_HIDDEN), lambda i, te, nt: (layer, te[i], 0, 0)),
                      pl.BlockSpec((1, 1, MOE_HIDDEN, d), lambda i, te, nt: (layer, te[i], 0, 0))],
            out_specs=pl.BlockSpec((MOE_TM, d), lambda i, te, nt: (i, 0)),
            scratch_shapes=[pltpu.VMEM((d, MOE_HIDDEN), BF16), pltpu.VMEM((d, MOE_HIDDEN), BF16),
                            pltpu.VMEM((MOE_HIDDEN, d), BF16)]),
        out_shape=jax.ShapeDtypeStruct((n_sorted, d), F32),
        compiler_params=_cp(("arbitrary",)),
        name="moe_experts",
    )(tile_expert, n_tiles, xs, w1, w3, w2)


def _combine_kernel(pos_ref, nxt_ref, meta_ref, s_ref, g_ref, ys_hbm, o_ref, buf_ref, sem):
    i = pl.program_id(0)
    slot = i % 2

    def gather(p_ref, dst_slot):
        def issue(r, carry):
            _row_copy(ys_hbm, p_ref[0, 0, r], buf_ref.at[dst_slot, 0], r, sem.at[dst_slot]).start()
            _row_copy(ys_hbm, p_ref[0, 0, CHUNK + r], buf_ref.at[dst_slot, 1], r, sem.at[dst_slot]).start()
            return carry

        lax.fori_loop(0, CHUNK, issue, 0, unroll=8)

    @pl.when(i == 0)
    def _():
        gather(pos_ref, 0)

    @pl.when(i + 1 < pl.num_programs(0))
    def _():
        gather(nxt_ref, 1 - slot)

    for k in range(2):
        pltpu.make_async_copy(ys_hbm.at[pl.ds(0, CHUNK)], buf_ref.at[slot, k], sem.at[slot]).wait()
    meta = meta_ref[...]
    y = meta[:, 2:3] * buf_ref[slot, 0] + meta[:, 3:4] * buf_ref[slot, 1]
    o_ref[...] = s_ref[...] + g_ref[0, 0] * y


def moe_combine(pos, meta, s, mod, k_gate, ys, nb):
    n, d = s.shape
    tm = CHUNK
    last = n // tm - 1
    return pl.pallas_call(
        _combine_kernel,
        grid=(n // tm,),
        in_specs=[pl.BlockSpec((1, 1, 2 * CHUNK), lambda i: (i, 0, 0), memory_space=pltpu.SMEM),
                  pl.BlockSpec((1, 1, 2 * CHUNK), lambda i: (jnp.minimum(i + 1, last), 0, 0),
                               memory_space=pltpu.SMEM),
                  pl.BlockSpec((tm, LANES), lambda i: (i, 0)),
                  pl.BlockSpec((tm, d), lambda i: (i, 0)),
                  pl.BlockSpec((1, 1, 1, d), lambda i: (_mod_row(i, tm, nb), k_gate, 0, 0)),
                  pl.BlockSpec(memory_space=pl.ANY)],
        out_specs=pl.BlockSpec((tm, d), lambda i: (i, 0)),
        out_shape=jax.ShapeDtypeStruct((n, d), F32),
        scratch_shapes=[pltpu.VMEM((2, 2, tm, d), F32), pltpu.SemaphoreType.DMA((2,))],
        input_output_aliases={3: 0},
        compiler_params=_cp(("arbitrary",)),
        name="moe_combine",
    )(pos, pos, meta, s, mod, ys)


def _na_layer(h, w_qkv, w_o, rpb, s, mod, nb):
    qkv = matmul(h, w_qkv.astype(BF16), BF16)
    o = na_attention(qkv, _na_bias_tables(rpb), nb)
    return matmul_residual(o, w_o.astype(BF16), s, mod, 2, nb)


def _da_layer(h, w_qkv, lam, subln_g, w_o, lambda_init, s, mod, nb):
    qkv = matmul_rope(h, w_qkv.astype(BF16), nb)
    o = da_attention(qkv, lam, subln_g, lambda_init, nb)
    return matmul_residual(o, w_o.astype(BF16), s, mod, 2, nb)


def _ml_layer(h, w_up, conv_w, conv_b, w_q, w_k, w_v, w_gate, b_gate, gn_w, skip, w_down, s, mod, nb):
    up = matmul(h, w_up.astype(BF16), BF16)
    xc, q, k, v = ml_conv_qkv(up, conv_w, conv_b, _block_diag(w_q), _block_diag(w_k), _block_diag(w_v), nb)
    ng = w_gate.shape[1]
    wg = jnp.pad(w_gate, ((0, 0), (0, LANES - ng))).reshape(3, ML_INNER, LANES).astype(BF16)
    bg = jnp.pad(b_gate.astype(F32), (0, LANES - ng)).reshape(1, LANES)
    g = ml_gates(q, k, v, wg, bg)
    gt = g[:, :ng].T
    hs = ml_scan(q, k, v, g, gt, nb)
    a = ml_finish(hs, xc, up, gn_w, skip)
    return matmul_residual(a, w_down.astype(BF16), s, mod, 2, nb)


def _moe_layer(norm_g, w_group, b_group, w_router, b_router, w1, w3, w2, layer, s, mod, nb):
    d = s.shape[1]
    pad = LANES - MOE_E - MOE_GROUPS
    wr = jnp.concatenate([w_router, w_group, jnp.zeros((d, pad), w_router.dtype)], axis=1).astype(F32)
    wr_hi = wr.astype(BF16)
    wr_lo = (wr - wr_hi.astype(F32)).astype(BF16)
    br = jnp.concatenate([b_router, b_group, jnp.zeros((pad,), b_router.dtype)]).astype(F32).reshape(1, LANES)
    h, route = moe_router(s, norm_g.astype(F32), mod, 3, 4, jnp.stack([wr_hi, wr_lo]), br, nb)
    rank, cnt = moe_rank(route)
    n = s.shape[0]
    cnt_e = cnt[0, SEL_LANE:SEL_LANE + MOE_E].astype(jnp.int32)
    gsz = ((cnt_e + MOE_TM - 1) // MOE_TM) * MOE_TM
    ends = jnp.cumsum(gsz)
    off_row = jnp.zeros((1, LANES), F32).at[0, SEL_LANE:SEL_LANE + MOE_E].set((ends - gsz).astype(F32))
    meta = moe_positions(route, rank, off_row)
    n_sorted = 2 * n + MOE_E * MOE_TM
    n_tiles = (ends[-1:] // MOE_TM).astype(jnp.int32)
    tile_ids = jnp.arange(n_sorted // MOE_TM, dtype=jnp.int32)
    tile_expert = jnp.sum((tile_ids[:, None] >= (ends // MOE_TM)[None, :]).astype(jnp.int32), axis=1)
    tile_expert = jnp.minimum(tile_expert, MOE_E - 1)
    pos = meta[:, :2].astype(jnp.int32).reshape(n // CHUNK, CHUNK, 2)
    pos = pos.transpose(0, 2, 1).reshape(n // CHUNK, 1, 2 * CHUNK)
    last_tile_start = jnp.maximum(ends - MOE_TM, 0).astype(jnp.int32)
    xs = moe_dispatch(last_tile_start, pos, h, n_sorted)
    ys = moe_experts(tile_expert, n_tiles, xs, w1.astype(F32), w3.astype(F32), w2.astype(F32), layer)
    return moe_combine(pos, meta, s, mod, 5, ys, nb)


def kernel(x, c, ctx, c_ctx, mod_w, mod_b, norm_g, final_g, na_w_qkv, na_w_o, na_rpb, ml_w_up, ml_conv_w, ml_conv_b, ml_w_q, ml_w_k, ml_w_v, ml_w_gate, ml_b_gate, ml_gn_w, ml_skip, ml_w_down, da_w_qkv, da_lambda, da_subln_g, da_w_o, moe_w_group, moe_b_group, moe_w_router, moe_b_router, moe_w1, moe_w3, moe_w2):
    nb, seq, d = x.shape
    assert (seq, d, ctx.shape[1]) == (SEQ, D_MODEL, CTX_LEN) and nb < 16
    depth = mod_w.shape[0]
    nx = nb * seq
    s = jnp.concatenate([x.reshape(nx, d), ctx.reshape(nb * CTX_LEN, d)], axis=0).astype(F32)
    cvec = jnp.concatenate([c, c_ctx[None, :], jnp.zeros((16 - nb - 1, d), c.dtype)], axis=0).astype(F32)
    mods = mod_vectors(cvec, mod_w, mod_b).reshape(depth, 16, 6, 1, d)

    for i in range(depth):
        kind, j = i % N_MIXERS, i // N_MIXERS
        mod = mods[i]
        h = norm_mod(s, norm_g[i, 0].astype(F32), mod, 0, 1, nb)
        if kind == 0:
            s = _na_layer(h, na_w_qkv[j], na_w_o[j], na_rpb[j], s, mod, nb)
        elif kind == 1:
            s = _ml_layer(h, ml_w_up[j], ml_conv_w[j], ml_conv_b[j], ml_w_q[j], ml_w_k[j], ml_w_v[j],
                          ml_w_gate[j], ml_b_gate[j], ml_gn_w[j], ml_skip[j], ml_w_down[j], s, mod, nb)
        else:
            lambda_init = 0.8 - 0.6 * math.exp(-0.3 * i)
            s = _da_layer(h, da_w_qkv[j], da_lambda[j], da_subln_g[j], da_w_o[j], lambda_init, s, mod, nb)
        s = _moe_layer(norm_g[i, 1], moe_w_group[i], moe_b_group[i], moe_w_router[i], moe_b_router[i],
                       moe_w1, moe_w3, moe_w2, i, s, mod, nb)
    return final_norm(s, final_g.astype(F32), nx).reshape(nb, seq, d)
```

```python
import functools
import math

import numpy as np
import jax
import jax.numpy as jnp
from jax import lax
from jax.experimental import pallas as pl
from jax.experimental.pallas import tpu as pltpu

F32 = jnp.float32
BF16 = jnp.bfloat16

D_MODEL = 1024
SEQ = 4096
CTX_LEN = 256
GRID_W = 64
N_MIXERS = 3
EPS = 1e-6

NA_HEADS = 16
NA_WIN_H = 8
NA_WIN_W = 16

ML_HEADS = 4
ML_INNER = 2 * D_MODEL
ML_HEAD_DIM = ML_INNER // ML_HEADS
ML_BLOCK = 4
ML_CONV_K = 5

DA_HEADS = 8
DA_HEAD_DIM = 64
ROPE_BASE = 10000.0

MOE_GROUPS = 4
MOE_EPG = 8
MOE_E = MOE_GROUPS * MOE_EPG
MOE_HIDDEN = 512
MOE_TM = 512
MOE_TOK = 512
SEL_LANE = 64

LANES = 128
CHUNK = 256
X_CHUNKS = SEQ // CHUNK
PROJ_TM = 1024
DA_TQ = 512
DA_VT_ROWS = 128 + 16
NA_ROWS = 8
NEG = -1e30
VMEM_LIMIT = 56 * 1024 * 1024


def _cp(sem, vmem=VMEM_LIMIT):
    return pltpu.CompilerParams(dimension_semantics=sem, vmem_limit_bytes=vmem)


def _tile(n, pref):
    tm = pref
    while n % tm:
        tm //= 2
    assert tm >= CHUNK
    return tm


def _mod_row(i, tm, nb):
    return jnp.where(i < nb * (SEQ // tm), i // (SEQ // tm), nb)


def _nt(a, b):
    return lax.dot_general(a, b, (((1,), (1,)), ((), ())), preferred_element_type=F32)


def _tn(a, b):
    return lax.dot_general(a, b, (((0,), (0,)), ((), ())), preferred_element_type=F32)


def _silu(x):
    return x * jax.nn.sigmoid(x)


def _head_rows(q, lane):
    zero = jnp.zeros_like(q)
    return jnp.concatenate([jnp.where(lane < 64, q, zero), jnp.where(lane >= 64, q, zero)], axis=0)


def _mod_kernel(c_ref, w_ref, b_ref, o_ref):
    a = _silu(c_ref[...]).astype(BF16)
    o_ref[0] = jnp.dot(a, w_ref[0].astype(BF16), preferred_element_type=F32) + b_ref[0]


def mod_vectors(cvec, mod_w, mod_b):
    depth, d, n6 = mod_w.shape
    tn = 1024
    return pl.pallas_call(
        _mod_kernel,
        grid=(depth, n6 // tn),
        in_specs=[pl.BlockSpec((16, d), lambda l, j: (0, 0)),
                  pl.BlockSpec((1, d, tn), lambda l, j: (l, 0, j)),
                  pl.BlockSpec((1, 1, tn), lambda l, j: (l, 0, j))],
        out_specs=pl.BlockSpec((1, 16, tn), lambda l, j: (l, 0, j)),
        out_shape=jax.ShapeDtypeStruct((depth, 16, n6), F32),
        compiler_params=_cp(("parallel", "parallel")),
        name="mod_vectors",
    )(cvec, mod_w, mod_b.reshape(depth, 1, n6))


def _norm_mod_kernel(x_ref, g_ref, sh_ref, sc_ref, o_ref):
    x = x_ref[...]
    y = x * lax.rsqrt(jnp.mean(x * x, axis=-1, keepdims=True) + EPS) * g_ref[...]
    o_ref[...] = (y * (1.0 + sc_ref[0, 0]) + sh_ref[0, 0]).astype(o_ref.dtype)


def norm_mod(s, g, mod, k_shift, k_scale, nb):
    n, d = s.shape
    tm = _tile(n, PROJ_TM)
    return pl.pallas_call(
        _norm_mod_kernel,
        grid=(n // tm,),
        in_specs=[pl.BlockSpec((tm, d), lambda i: (i, 0)),
                  pl.BlockSpec((1, d), lambda i: (0, 0)),
                  pl.BlockSpec((1, 1, 1, d), lambda i: (_mod_row(i, tm, nb), k_shift, 0, 0)),
                  pl.BlockSpec((1, 1, 1, d), lambda i: (_mod_row(i, tm, nb), k_scale, 0, 0))],
        out_specs=pl.BlockSpec((tm, d), lambda i: (i, 0)),
        out_shape=jax.ShapeDtypeStruct((n, d), BF16),
        compiler_params=_cp(("parallel",)),
        name="norm_mod",
    )(s, g.reshape(1, d), mod, mod)


def _final_norm_kernel(x_ref, g_ref, o_ref):
    x = x_ref[...]
    o_ref[...] = x * lax.rsqrt(jnp.mean(x * x, axis=-1, keepdims=True) + EPS) * g_ref[...]


def final_norm(s, g, n_rows):
    d = s.shape[1]
    tm = _tile(n_rows, PROJ_TM)
    return pl.pallas_call(
        _final_norm_kernel,
        grid=(n_rows // tm,),
        in_specs=[pl.BlockSpec((tm, d), lambda i: (i, 0)),
                  pl.BlockSpec((1, d), lambda i: (0, 0))],
        out_specs=pl.BlockSpec((tm, d), lambda i: (i, 0)),
        out_shape=jax.ShapeDtypeStruct((n_rows, d), F32),
        compiler_params=_cp(("parallel",)),
        name="final_norm",
    )(s, g.reshape(1, d))


def _mm_kernel(a_ref, w_ref, o_ref):
    o_ref[...] = jnp.dot(a_ref[...], w_ref[...], preferred_element_type=F32).astype(o_ref.dtype)


def matmul(a, w, out_dtype, tn=1024):
    n, k = a.shape
    tm = _tile(n, PROJ_TM)
    nout = w.shape[1]
    tn = min(tn, nout)
    return pl.pallas_call(
        _mm_kernel,
        grid=(nout // tn, n // tm),
        in_specs=[pl.BlockSpec((tm, k), lambda j, i: (i, 0)),
                  pl.BlockSpec((k, tn), lambda j, i: (0, j))],
        out_specs=pl.BlockSpec((tm, tn), lambda j, i: (i, j)),
        out_shape=jax.ShapeDtypeStruct((n, nout), out_dtype),
        compiler_params=_cp(("parallel", "parallel")),
        name="matmul",
    )(a, w)


def _mm_res_kernel(a_ref, w_ref, r_ref, g_ref, o_ref):
    acc = jnp.dot(a_ref[...], w_ref[...], preferred_element_type=F32)
    o_ref[...] = r_ref[...] + g_ref[0, 0] * acc


def matmul_residual(a, w, s, mod, k_gate, nb):
    n, k = a.shape
    d = w.shape[1]
    tm = _tile(n, PROJ_TM)
    return pl.pallas_call(
        _mm_res_kernel,
        grid=(n // tm,),
        in_specs=[pl.BlockSpec((tm, k), lambda i: (i, 0)),
                  pl.BlockSpec((k, d), lambda i: (0, 0)),
                  pl.BlockSpec((tm, d), lambda i: (i, 0)),
                  pl.BlockSpec((1, 1, 1, d), lambda i: (_mod_row(i, tm, nb), k_gate, 0, 0))],
        out_specs=pl.BlockSpec((tm, d), lambda i: (i, 0)),
        out_shape=jax.ShapeDtypeStruct((n, d), F32),
        input_output_aliases={2: 0},
        compiler_params=_cp(("parallel",)),
        name="matmul_residual",
    )(a, w, s, mod)


def _na_bias_tables(rpb):
    o = np.arange(NA_WIN_H)[:, None]
    j = np.arange(NA_WIN_H)[None, :]
    dy = j - o + NA_WIN_H - 1
    qc = np.arange(GRID_W)[:, None]
    kc = np.arange(GRID_W)[None, :]
    dx = np.clip(kc - qc, -(NA_WIN_W - 1), NA_WIN_W - 1) + NA_WIN_W - 1
    w_start = np.clip(qc - NA_WIN_W // 2, 0, GRID_W - NA_WIN_W)
    valid = (kc >= w_start) & (kc < w_start + NA_WIN_W)
    tbl = rpb.astype(F32)[:, dy][:, :, :, dx]
    tbl = jnp.where(jnp.asarray(valid)[None, None, None], tbl, NEG)
    tbl = tbl.reshape(NA_HEADS // 2, 2, NA_WIN_H, NA_WIN_H, GRID_W, GRID_W)
    tbl = tbl.transpose(2, 0, 3, 5, 1, 4)
    return tbl.reshape(NA_WIN_H, NA_HEADS // 2, NA_WIN_H * GRID_W, 2 * GRID_W)


def _softmax_cols(s):
    m = jnp.max(s, axis=0, keepdims=True)
    p = jnp.exp(s - m)
    return (p * (1.0 / jnp.sum(p, axis=0, keepdims=True))).astype(BF16)


def _na_x_kernel(q_ref, kx_ref, vx_ref, kc_ref, vc_ref, bias_ref, o_ref, s_sc, p_sc):
    t = pl.program_id(2)
    lane = lax.broadcasted_iota(jnp.int32, (1, LANES), 1)
    nwin = NA_WIN_H * GRID_W
    starts = []
    for i in range(NA_ROWS):
        r = t * NA_ROWS + i
        rs = jnp.clip(r - NA_WIN_H // 2, 0, SEQ // GRID_W - NA_WIN_H)
        start = pl.multiple_of(rs * GRID_W, GRID_W)
        starts.append(start)
        qq = _head_rows(q_ref[i * GRID_W:(i + 1) * GRID_W, :] * 0.125, lane)
        s_sc[i, 0:nwin, :] = _nt(kx_ref[pl.ds(start, nwin), :], qq) + bias_ref[r - rs, 0]
        s_sc[i, nwin:, :] = _nt(kc_ref[...], qq)
    for i in range(NA_ROWS):
        p_sc[i] = _softmax_cols(s_sc[i])
    for i in range(NA_ROWS):
        o = (_tn(p_sc[i, 0:nwin, :], vx_ref[pl.ds(starts[i], nwin), :])
             + _tn(p_sc[i, nwin:, :], vc_ref[...]))
        o = jnp.where(lane < 64, o[0:GRID_W], o[GRID_W:])
        o_ref[i * GRID_W:(i + 1) * GRID_W, :] = o.astype(o_ref.dtype)


def _na_ctx_kernel(q_ref, kc_ref, vc_ref, o_in_hbm, o_ref):
    del o_in_hbm
    lane = lax.broadcasted_iota(jnp.int32, (1, LANES), 1)
    q = q_ref[...] * 0.125
    outs = []
    for hh in range(2):
        qm = jnp.where((lane // 64) == hh, q, jnp.zeros_like(q))
        p = _softmax_cols(_nt(kc_ref[...], qm))
        outs.append(_tn(p, vc_ref[...]))
    o_ref[...] = jnp.where(lane < 64, outs[0], outs[1]).astype(o_ref.dtype)


def na_attention(qkv, bias, nb):
    n = qkv.shape[0]
    d = D_MODEL
    ncb = d // LANES
    xc = nb * X_CHUNKS
    nkeys = NA_WIN_H * GRID_W + CTX_LEN
    tq = NA_ROWS * GRID_W
    steps = SEQ // tq
    o = pl.pallas_call(
        _na_x_kernel,
        grid=(nb, ncb, steps),
        in_specs=[pl.BlockSpec((tq, LANES), lambda b, h, t: (b * steps + t, h)),
                  pl.BlockSpec((SEQ, LANES), lambda b, h, t: (b, ncb + h)),
                  pl.BlockSpec((SEQ, LANES), lambda b, h, t: (b, 2 * ncb + h)),
                  pl.BlockSpec((CHUNK, LANES), lambda b, h, t: (xc + b, ncb + h)),
                  pl.BlockSpec((CHUNK, LANES), lambda b, h, t: (xc + b, 2 * ncb + h)),
                  pl.BlockSpec((NA_WIN_H, 1, NA_WIN_H * GRID_W, LANES), lambda b, h, t: (0, h, 0, 0))],
        out_specs=pl.BlockSpec((tq, LANES), lambda b, h, t: (b * steps + t, h)),
        out_shape=jax.ShapeDtypeStruct((n, d), BF16),
        scratch_shapes=[pltpu.VMEM((NA_ROWS, nkeys, LANES), F32),
                        pltpu.VMEM((NA_ROWS, nkeys, LANES), BF16)],
        compiler_params=_cp(("parallel", "parallel", "arbitrary")),
        name="na_attention_x",
    )(qkv, qkv, qkv, qkv, qkv, bias)
    return pl.pallas_call(
        _na_ctx_kernel,
        grid=(nb, ncb),
        in_specs=[pl.BlockSpec((CHUNK, LANES), lambda b, h: (xc + b, h)),
                  pl.BlockSpec((CHUNK, LANES), lambda b, h: (xc + b, ncb + h)),
                  pl.BlockSpec((CHUNK, LANES), lambda b, h: (xc + b, 2 * ncb + h)),
                  pl.BlockSpec(memory_space=pl.ANY)],
        out_specs=pl.BlockSpec((CHUNK, LANES), lambda b, h: (xc + b, h)),
        out_shape=jax.ShapeDtypeStruct((n, d), BF16),
        input_output_aliases={3: 0},
        compiler_params=_cp(("parallel", "parallel")),
        name="na_attention_ctx",
    )(qkv, qkv, qkv, o)


def _rope_tables(tm):
    half = DA_HEAD_DIM // 2
    freqs = ROPE_BASE ** (-np.arange(0, half, 2, dtype=np.float32) / half)
    t = np.arange(SEQ)
    row, col = t // GRID_W, t % GRID_W
    lane = np.arange(LANES)
    l64 = lane % DA_HEAD_DIM
    use_col = (l64 // half) == 1
    l32 = l64 % half
    fi = l32 % (half // 2)
    second = l32 >= half // 2
    pos = np.where(use_col[None, :], col[:, None], row[:, None]).astype(np.float32)
    ang = pos * freqs[fi][None, :]
    cos = np.cos(ang).astype(np.float32)
    sin = np.sin(ang).astype(np.float32)
    sa = np.where(second[None, :], 0.0, -sin).astype(np.float32)
    sb = np.where(second[None, :], sin, 0.0).astype(np.float32)
    ident = np.ones((tm, LANES), np.float32)
    zero = np.zeros((tm, LANES), np.float32)
    return (jnp.asarray(np.concatenate([cos, ident])),
            jnp.asarray(np.concatenate([sa, zero])),
            jnp.asarray(np.concatenate([sb, zero])))


def _mm_rope_kernel(a_ref, w_ref, c_ref, sa_ref, sb_ref, o_ref):
    j = pl.program_id(0)
    acc = jnp.dot(a_ref[...], w_ref[...], preferred_element_type=F32)

    @pl.when(j < 2)
    def _():
        scale = jnp.where(j == 0, DA_HEAD_DIM ** -0.5 * math.log2(math.e), 1.0)
        cos = c_ref[...] * scale
        sa = sa_ref[...] * scale
        sb = sb_ref[...] * scale
        for g in range(acc.shape[1] // LANES):
            sl = slice(g * LANES, (g + 1) * LANES)
            x = acc[:, sl]
            y = x * cos + pltpu.roll(x, LANES - 16, 1) * sa + pltpu.roll(x, 16, 1) * sb
            o_ref[:, sl] = y.astype(o_ref.dtype)

    @pl.when(j >= 2)
    def _():
        o_ref[...] = acc.astype(o_ref.dtype)


def matmul_rope(a, w, nb):
    n, k = a.shape
    d = D_MODEL
    tm = _tile(n, PROJ_TM)
    cos, sa, sb = _rope_tables(tm)
    xt = nb * (SEQ // tm)

    def tab(j, i):
        return (jnp.where(i < xt, i % (SEQ // tm), SEQ // tm), 0)

    tspec = pl.BlockSpec((tm, LANES), tab)
    return pl.pallas_call(
        _mm_rope_kernel,
        grid=(3, n // tm),
        in_specs=[pl.BlockSpec((tm, k), lambda j, i: (i, 0)),
                  pl.BlockSpec((k, d), lambda j, i: (0, j)),
                  tspec, tspec, tspec],
        out_specs=pl.BlockSpec((tm, d), lambda j, i: (i, j)),
        out_shape=jax.ShapeDtypeStruct((n, 3 * d), BF16),
        compiler_params=_cp(("parallel", "parallel")),
        name="matmul_rope",
    )(a, w, cos, sa, sb)


def _da_body(q_ref, lam_ref, g_ref, o_ref, s_sc, acc_sc, m_sc, lambda_init, run_chunks):
    tq = q_ref.shape[0]
    lane = lax.broadcasted_iota(jnp.int32, (1, LANES), 1)
    qq = _head_rows(q_ref[...], lane)
    m_sc[...] = jnp.full_like(m_sc, NEG)
    acc_sc[...] = jnp.zeros_like(acc_sc)

    def scores(slot, k):
        s_sc[slot] = _nt(k, qq)

    def fold(slot, vt):
        for g in range(2 * tq // CHUNK):
            sl = slice(g * CHUNK, (g + 1) * CHUNK)
            s = s_sc[slot, :, sl]
            m_old = m_sc[:, sl]
            m_new = jnp.maximum(m_old, jnp.max(s, axis=0, keepdims=True))
            alpha = jnp.exp2(m_old - m_new)
            p = jnp.exp2(s - m_new)
            m_sc[:, sl] = m_new
            acc_sc[:, sl] = acc_sc[:, sl] * alpha + jnp.dot(vt, p.astype(BF16), preferred_element_type=F32)

    run_chunks(scores, fold)

    ot = acc_sc[0:LANES, :] / acc_sc[LANES:LANES + 1, :]
    lam = lam_ref[...]
    lam_full = (jnp.exp(jnp.sum(lam[0:1] * lam[1:2], axis=1, keepdims=True))
                - jnp.exp(jnp.sum(lam[2:3] * lam[3:4], axis=1, keepdims=True)) + lambda_init)
    od = ot[:, :tq] - lam_full * ot[:, tq:]
    y = od * lax.rsqrt(jnp.mean(od * od, axis=0, keepdims=True) + EPS) * g_ref[...]
    o_ref[...] = (y * (1.0 - lambda_init)).T.astype(o_ref.dtype)


def _da_x_kernel(q_ref, kx_ref, kc_ref, vtx_ref, vtc_ref, lam_ref, g_ref, o_ref,
                 s_sc, acc_sc, m_sc, *, lambda_init):
    def kx(c):
        return kx_ref[pl.ds(pl.multiple_of(c * CHUNK, CHUNK), CHUNK), :]

    def run_chunks(scores, fold):
        scores(0, kx(0))

        def body(i, carry):
            scores(1, kx(2 * i + 1))
            fold(0, vtx_ref[0, 2 * i])
            scores(0, kx(2 * i + 2))
            fold(1, vtx_ref[0, 2 * i + 1])
            return carry

        lax.fori_loop(0, X_CHUNKS // 2 - 1, body, 0)
        scores(1, kx(X_CHUNKS - 1))
        fold(0, vtx_ref[0, X_CHUNKS - 2])
        scores(0, kc_ref[...])
        fold(1, vtx_ref[0, X_CHUNKS - 1])
        fold(0, vtc_ref[0, 0])

    _da_body(q_ref, lam_ref, g_ref, o_ref, s_sc, acc_sc, m_sc, lambda_init, run_chunks)


def _da_ctx_kernel(q_ref, kc_ref, vtc_ref, lam_ref, g_ref, o_in_hbm, o_ref,
                   s_sc, acc_sc, m_sc, *, lambda_init):
    del o_in_hbm

    def run_chunks(scores, fold):
        scores(0, kc_ref[...])
        fold(0, vtc_ref[0, 0])

    _da_body(q_ref, lam_ref, g_ref, o_ref, s_sc, acc_sc, m_sc, lambda_init, run_chunks)


def _da_scratch(tq):
    return [pltpu.VMEM((2, CHUNK, 2 * tq), F32), pltpu.VMEM((DA_VT_ROWS, 2 * tq), F32),
            pltpu.VMEM((1, 2 * tq), F32)]


def da_attention(qkv, lam, subln_g, lambda_init, nb):
    n = qkv.shape[0]
    d = D_MODEL
    ncb = d // LANES
    xc = nb * X_CHUNKS
    tq = DA_TQ
    vt = qkv[:, 2 * d:].reshape(n // CHUNK, CHUNK, ncb, LANES).transpose(2, 0, 3, 1)
    extra = jnp.zeros((ncb, n // CHUNK, DA_VT_ROWS - LANES, CHUNK), BF16).at[:, :, 0, :].set(1.0)
    vt = jnp.concatenate([vt, extra], axis=2)
    lam = lam.astype(F32)
    gcol = subln_g.astype(F32).reshape(LANES, 1)
    o = pl.pallas_call(
        functools.partial(_da_x_kernel, lambda_init=lambda_init),
        grid=(nb, ncb, SEQ // tq),
        in_specs=[pl.BlockSpec((tq, LANES), lambda b, h, t: (b * (SEQ // tq) + t, h)),
                  pl.BlockSpec((SEQ, LANES), lambda b, h, t: (b, ncb + h)),
                  pl.BlockSpec((CHUNK, LANES), lambda b, h, t: (xc + b, ncb + h)),
                  pl.BlockSpec((1, X_CHUNKS, DA_VT_ROWS, CHUNK), lambda b, h, t: (h, b, 0, 0)),
                  pl.BlockSpec((1, 1, DA_VT_ROWS, CHUNK), lambda b, h, t: (h, xc + b, 0, 0)),
                  pl.BlockSpec((4, DA_HEAD_DIM), lambda b, h, t: (0, 0)),
                  pl.BlockSpec((LANES, 1), lambda b, h, t: (0, 0))],
        out_specs=pl.BlockSpec((tq, LANES), lambda b, h, t: (b * (SEQ // tq) + t, h)),
        out_shape=jax.ShapeDtypeStruct((n, d), BF16),
        scratch_shapes=_da_scratch(tq),
        compiler_params=_cp(("parallel", "parallel", "arbitrary")),
        name="da_attention_x",
    )(qkv, qkv, qkv, vt, vt, lam, gcol)
    return pl.pallas_call(
        functools.partial(_da_ctx_kernel, lambda_init=lambda_init),
        grid=(nb, ncb),
        in_specs=[pl.BlockSpec((CHUNK, LANES), lambda b, h: (xc + b, h)),
                  pl.BlockSpec((CHUNK, LANES), lambda b, h: (xc + b, ncb + h)),
                  pl.BlockSpec((1, 1, DA_VT_ROWS, CHUNK), lambda b, h: (h, xc + b, 0, 0)),
                  pl.BlockSpec((4, DA_HEAD_DIM), lambda b, h: (0, 0)),
                  pl.BlockSpec((LANES, 1), lambda b, h: (0, 0)),
                  pl.BlockSpec(memory_space=pl.ANY)],
        out_specs=pl.BlockSpec((CHUNK, LANES), lambda b, h: (xc + b, h)),
        out_shape=jax.ShapeDtypeStruct((n, d), BF16),
        scratch_shapes=_da_scratch(CHUNK),
        input_output_aliases={5: 0},
        compiler_params=_cp(("parallel", "parallel")),
        name="da_attention_ctx",
    )(qkv, qkv, vt, lam, gcol, o)


def _block_diag(w):
    per = LANES // ML_BLOCK
    wr = w.reshape(ML_INNER // LANES, per, ML_BLOCK, ML_BLOCK)
    eye = jnp.eye(per, dtype=w.dtype)
    return jnp.einsum('cgio,gh->cgiho', wr, eye).reshape(ML_INNER // LANES, LANES, LANES).astype(BF16)


def _ml_conv_kernel(p_ref, c_ref, n_ref, cw_ref, cb_ref, wq_ref, wk_ref, wv_ref,
                    xc_ref, q_ref, k_ref, v_ref, *, n_xchunks):
    i = pl.program_id(0)
    is_x = i < n_xchunks
    j = i % X_CHUNKS
    halo = 16
    cur = c_ref[...].astype(F32)
    prev = p_ref[CHUNK - halo:CHUNK, :].astype(F32)
    nxt = n_ref[0:halo, :].astype(F32)
    prev = jnp.where(jnp.logical_and(is_x, j > 0), prev, jnp.zeros_like(prev))
    nxt = jnp.where(jnp.logical_and(is_x, j < X_CHUNKS - 1), nxt, jnp.zeros_like(nxt))
    xp = jnp.concatenate([prev, cur, nxt], axis=0)
    rows = CHUNK + 2 * halo
    y = cb_ref[...] + cw_ref[ML_CONV_K // 2:ML_CONV_K // 2 + 1, :] * cur
    for tap in range(ML_CONV_K):
        dlt = tap - ML_CONV_K // 2
        if dlt == 0:
            continue
        shifted = pltpu.roll(xp, (-dlt) % rows, 0)[halo:halo + CHUNK]
        y = y + cw_ref[tap:tap + 1, :] * shifted
    xcb = _silu(y).astype(BF16)
    xc_ref[...] = xcb
    xm = c_ref[...]
    for s in range(xcb.shape[1] // LANES):
        sl = slice(s * LANES, (s + 1) * LANES)
        q_ref[:, sl] = jnp.dot(xcb[:, sl], wq_ref[s], preferred_element_type=F32).astype(BF16)
        k_ref[:, sl] = jnp.dot(xcb[:, sl], wk_ref[s], preferred_element_type=F32).astype(BF16)
        v_ref[:, sl] = jnp.dot(xm[:, sl], wv_ref[s], preferred_element_type=F32).astype(BF16)


def ml_conv_qkv(up, conv_w, conv_b, wq, wk, wv, nb):
    n = up.shape[0]
    nchunks = n // CHUNK
    cw = 512
    ncb = ML_INNER // cw
    sub = cw // LANES
    blk = pl.BlockSpec((CHUNK, cw), lambda i, c: (i, c))
    wspec = pl.BlockSpec((sub, LANES, LANES), lambda i, c: (c, 0, 0))
    out = jax.ShapeDtypeStruct((n, ML_INNER), BF16)
    return pl.pallas_call(
        functools.partial(_ml_conv_kernel, n_xchunks=nb * X_CHUNKS),
        grid=(nchunks, ncb),
        in_specs=[pl.BlockSpec((CHUNK, cw), lambda i, c: (jnp.maximum(i - 1, 0), c)),
                  blk,
                  pl.BlockSpec((CHUNK, cw), lambda i, c: (jnp.minimum(i + 1, nchunks - 1), c)),
                  pl.BlockSpec((ML_CONV_K, cw), lambda i, c: (0, c)),
                  pl.BlockSpec((1, cw), lambda i, c: (0, c)),
                  wspec, wspec, wspec],
        out_specs=[blk, blk, blk, blk],
        out_shape=[out, out, out, out],
        compiler_params=_cp(("parallel", "parallel")),
        name="ml_conv_qkv",
    )(up, up, up, conv_w.astype(F32), conv_b.astype(F32).reshape(1, ML_INNER), wq, wk, wv)


def _ml_gate_kernel(q_ref, k_ref, v_ref, w_ref, b_ref, o_ref):
    acc = jnp.dot(q_ref[...], w_ref[0], preferred_element_type=F32)
    acc = acc + jnp.dot(k_ref[...], w_ref[1], preferred_element_type=F32)
    acc = acc + jnp.dot(v_ref[...], w_ref[2], preferred_element_type=F32)
    o_ref[...] = acc + b_ref[...]


def ml_gates(q, k, v, wg, bg):
    n = q.shape[0]
    tm = _tile(n, PROJ_TM)
    blk = pl.BlockSpec((tm, ML_INNER), lambda i: (i, 0))
    return pl.pallas_call(
        _ml_gate_kernel,
        grid=(n // tm,),
        in_specs=[blk, blk, blk,
                  pl.BlockSpec((3, ML_INNER, LANES), lambda i: (0, 0, 0)),
                  pl.BlockSpec((1, LANES), lambda i: (0, 0))],
        out_specs=pl.BlockSpec((tm, LANES), lambda i: (i, 0)),
        out_shape=jax.ShapeDtypeStruct((n, LANES), F32),
        compiler_params=_cp(("parallel",)),
        name="ml_gates",
    )(q, k, v, wg, bg)


def _log_sigmoid(x):
    return jnp.minimum(x, 0.0) - jnp.log(1.0 + jnp.exp(-jnp.abs(x)))


def _ml_scan_kernel(q_ref, k_ref, v_ref, g_ref, gt_ref, o_ref, c_sc, n_sc, m_sc):
    h = pl.program_id(1)
    d = pl.program_id(2)
    p = pl.program_id(3)
    ninf = -jnp.inf

    @pl.when(p == 0)
    def _():
        c_sc[...] = jnp.zeros_like(c_sc)
        n_sc[...] = jnp.zeros_like(n_sc)
        m_sc[...] = jnp.full_like(m_sc, ninf)

    L = CHUNK
    col_i = d * 2 * ML_HEADS + h
    col_f = col_i + ML_HEADS
    lane = lax.broadcasted_iota(jnp.int32, (1, LANES), 1)
    g = g_ref[...]
    i_col = jnp.sum(jnp.where(lane == col_i, g, 0.0), axis=1, keepdims=True)
    f_col = _log_sigmoid(jnp.sum(jnp.where(lane == col_f, g, 0.0), axis=1, keepdims=True))
    i_row = gt_ref[pl.ds(col_i, 1), :]
    f_row = _log_sigmoid(gt_ref[pl.ds(col_f, 1), :])

    jj = lax.broadcasted_iota(jnp.int32, (L, L), 0)
    ss = lax.broadcasted_iota(jnp.int32, (L, L), 1)
    fwd = d == 0
    bwd = d == 1
    valid = jnp.logical_or(jnp.logical_and(ss <= jj, fwd), jnp.logical_and(ss >= jj, bwd))
    valid_t = jnp.logical_or(jnp.logical_and(jj <= ss, fwd), jnp.logical_and(jj >= ss, bwd))
    bcum_col = jnp.sum(jnp.where(valid, f_row, 0.0), axis=1, keepdims=True)
    bcum_row = jnp.sum(jnp.where(valid_t, f_col, 0.0), axis=0, keepdims=True)
    b_last = jnp.sum(f_row, axis=1, keepdims=True)
    m_prev = m_sc[...]

    logd = jnp.where(valid, bcum_col - bcum_row + i_row, ninf)
    log_inter = bcum_col + m_prev
    m_j = jnp.maximum(log_inter, jnp.max(logd, axis=1, keepdims=True))
    dmat = jnp.exp(logd - m_j)
    inter = jnp.exp(log_inter - m_j)

    q = q_ref[...]
    k = k_ref[...]
    v = v_ref[...]
    scale = ML_HEAD_DIM ** -0.5
    sc = _nt(q, k) * scale * dmat
    c_prev = c_sc[...]
    n_prev = n_sc[...]
    qc = jnp.dot(q, c_prev.astype(BF16), preferred_element_type=F32) * scale
    num = inter * qc + jnp.dot(sc.astype(BF16), v, preferred_element_type=F32)
    qn = jnp.sum(q.astype(F32) * n_prev, axis=1, keepdims=True) * scale
    den = inter * qn + jnp.sum(sc, axis=1, keepdims=True)
    o_ref[0] = num / jnp.maximum(jnp.abs(den), jnp.exp(-m_j))

    ls = b_last - bcum_col + i_col
    m_new = jnp.maximum(b_last + m_prev, jnp.max(ls, axis=0, keepdims=True))
    w = jnp.exp(ls - m_new)
    decay = jnp.exp(b_last + m_prev - m_new)
    kw = k.astype(F32) * w
    c_sc[...] = decay * c_prev + _tn(kw.astype(BF16), v)
    n_sc[...] = decay * n_prev + jnp.sum(kw, axis=0, keepdims=True)
    m_sc[...] = m_new


def ml_scan(q, k, v, g, gt, nb):
    n = q.shape[0]
    xc = nb * X_CHUNKS

    def cidx(b, d, p):
        xi = jnp.where(d == 0, p - 1, X_CHUNKS - p)
        return jnp.where(p == 0, xc + b, b * X_CHUNKS + xi)

    blk = pl.BlockSpec((CHUNK, ML_HEAD_DIM), lambda b, h, d, p: (cidx(b, d, p), h))
    return pl.pallas_call(
        _ml_scan_kernel,
        grid=(nb, ML_HEADS, 2, X_CHUNKS + 1),
        in_specs=[blk, blk, blk,
                  pl.BlockSpec((CHUNK, LANES), lambda b, h, d, p: (cidx(b, d, p), 0)),
                  pl.BlockSpec((16, CHUNK), lambda b, h, d, p: (0, cidx(b, d, p)))],
        out_specs=pl.BlockSpec((1, CHUNK, ML_HEAD_DIM), lambda b, h, d, p: (d, cidx(b, d, p), h)),
        out_shape=jax.ShapeDtypeStruct((2, n, ML_INNER), F32),
        scratch_shapes=[pltpu.VMEM((ML_HEAD_DIM, ML_HEAD_DIM), F32),
                        pltpu.VMEM((1, ML_HEAD_DIM), F32),
                        pltpu.VMEM((1, 1), F32)],
        compiler_params=_cp(("parallel", "parallel", "parallel", "arbitrary")),
        name="ml_scan",
    )(q, k, v, g, gt)


def _ml_finish_kernel(h_ref, xc_ref, z_ref, gn_ref, sk_ref, o_ref):
    hh = h_ref[0] + h_ref[1]
    z = z_ref[...].astype(F32)
    gate = _silu(z)
    for hd in range(ML_HEADS):
        sl = slice(hd * ML_HEAD_DIM, (hd + 1) * ML_HEAD_DIM)
        seg = hh[:, sl]
        mu = jnp.mean(seg, axis=-1, keepdims=True)
        cen = seg - mu
        var = jnp.mean(cen * cen, axis=-1, keepdims=True)
        hn = cen * lax.rsqrt(var + EPS) * gn_ref[:, sl]
        a = (hn + sk_ref[:, sl] * xc_ref[:, sl].astype(F32)) * gate[:, sl]
        o_ref[:, sl] = a.astype(o_ref.dtype)


def ml_finish(hs, xc, up, gn_w, skip):
    n = xc.shape[0]
    tm = CHUNK
    vec = pl.BlockSpec((1, ML_INNER), lambda i: (0, 0))
    return pl.pallas_call(
        _ml_finish_kernel,
        grid=(n // tm,),
        in_specs=[pl.BlockSpec((2, tm, ML_INNER), lambda i: (0, i, 0)),
                  pl.BlockSpec((tm, ML_INNER), lambda i: (i, 0)),
                  pl.BlockSpec((tm, ML_INNER), lambda i: (i, 1)),
                  vec, vec],
        out_specs=pl.BlockSpec((tm, ML_INNER), lambda i: (i, 0)),
        out_shape=jax.ShapeDtypeStruct((n, ML_INNER), BF16),
        compiler_params=_cp(("parallel",)),
        name="ml_finish",
    )(hs, xc, up, gn_w.astype(F32).reshape(1, ML_INNER), skip.astype(F32).reshape(1, ML_INNER))


def _router_kernel(x_ref, g_ref, sh_ref, sc_ref, w_ref, b_ref, h_ref, o_ref):
    x = x_ref[...]
    y = x * lax.rsqrt(jnp.mean(x * x, axis=-1, keepdims=True) + EPS) * g_ref[...]
    hf = y * (1.0 + sc_ref[0, 0]) + sh_ref[0, 0]
    h_hi = hf.astype(BF16)
    h_ref[...] = hf
    h_lo = (hf - h_hi.astype(F32)).astype(BF16)
    logits = (jnp.dot(h_hi, w_ref[0], preferred_element_type=F32)
              + jnp.dot(h_hi, w_ref[1], preferred_element_type=F32)
              + jnp.dot(h_lo, w_ref[0], preferred_element_type=F32)) + b_ref[...]
    lane = lax.broadcasted_iota(jnp.int32, (1, LANES), 1).astype(F32)
    big = 1e9
    ninf = -jnp.inf
    is_g = jnp.logical_and(lane >= MOE_E, lane < MOE_E + MOE_GROUPS)
    gl = jnp.where(is_g, logits, ninf)
    gmax = jnp.max(gl, axis=1, keepdims=True)
    g_val = 1.0 / jnp.sum(jnp.exp(gl - gmax), axis=1, keepdims=True)
    g_idx = jnp.min(jnp.where(gl == gmax, lane, big), axis=1, keepdims=True) - MOE_E
    lo = g_idx * MOE_EPG
    sel = jnp.logical_and(lane >= lo, lane < lo + MOE_EPG)
    el = jnp.where(sel, logits, ninf)
    e1 = jnp.max(el, axis=1, keepdims=True)
    esum = jnp.sum(jnp.exp(el - e1), axis=1, keepdims=True)
    i1 = jnp.min(jnp.where(el == e1, lane, big), axis=1, keepdims=True)
    el2 = jnp.where(lane == i1, ninf, el)
    e2 = jnp.max(el2, axis=1, keepdims=True)
    i2 = jnp.min(jnp.where(el2 == e2, lane, big), axis=1, keepdims=True)
    p1 = 1.0 / esum
    p2 = jnp.exp(e2 - e1) / esum
    w1 = g_val * p1 / (p1 + p2)
    w2 = g_val * p2 / (p1 + p2)
    gates = jnp.where(lane == i1, w1, 0.0) + jnp.where(lane == i2, w2, 0.0)
    marks = jnp.where(jnp.logical_or(lane == i1 + SEL_LANE, lane == i2 + SEL_LANE), 1.0, 0.0)
    o_ref[...] = gates + marks


def moe_router(s, g, mod, k_shift, k_scale, wr, br, nb):
    n, d = s.shape
    tm = _tile(n, PROJ_TM)
    return pl.pallas_call(
        _router_kernel,
        grid=(n // tm,),
        in_specs=[pl.BlockSpec((tm, d), lambda i: (i, 0)),
                  pl.BlockSpec((1, d), lambda i: (0, 0)),
                  pl.BlockSpec((1, 1, 1, d), lambda i: (_mod_row(i, tm, nb), k_shift, 0, 0)),
                  pl.BlockSpec((1, 1, 1, d), lambda i: (_mod_row(i, tm, nb), k_scale, 0, 0)),
                  pl.BlockSpec((2, d, LANES), lambda i: (0, 0, 0)),
                  pl.BlockSpec((1, LANES), lambda i: (0, 0))],
        out_specs=[pl.BlockSpec((tm, d), lambda i: (i, 0)),
                   pl.BlockSpec((tm, LANES), lambda i: (i, 0))],
        out_shape=[jax.ShapeDtypeStruct((n, d), F32),
                   jax.ShapeDtypeStruct((n, LANES), F32)],
        compiler_params=_cp(("parallel",)),
        name="moe_router",
    )(s, g.reshape(1, d), mod, mod, wr, br)


def _rank_kernel(route_ref, rank_ref, cnt_ref, carry_ref):
    i = pl.program_id(0)

    @pl.when(i == 0)
    def _():
        carry_ref[...] = jnp.zeros_like(carry_ref)

    tm = route_ref.shape[0]
    lane = lax.broadcasted_iota(jnp.int32, (1, LANES), 1)
    marks = jnp.where(lane >= SEL_LANE, route_ref[...], 0.0)
    rr = lax.broadcasted_iota(jnp.int32, (tm, tm), 0)
    cc = lax.broadcasted_iota(jnp.int32, (tm, tm), 1)
    below = jnp.where(cc < rr, 1.0, 0.0).astype(BF16)
    rank_ref[...] = jnp.dot(below, marks.astype(BF16), preferred_element_type=F32) + carry_ref[...]
    carry_ref[...] += jnp.sum(marks, axis=0, keepdims=True)
    cnt_ref[...] = carry_ref[...]


def moe_rank(route):
    n = route.shape[0]
    tm = _tile(n, PROJ_TM)
    return pl.pallas_call(
        _rank_kernel,
        grid=(n // tm,),
        in_specs=[pl.BlockSpec((tm, LANES), lambda i: (i, 0))],
        out_specs=[pl.BlockSpec((tm, LANES), lambda i: (i, 0)),
                   pl.BlockSpec((1, LANES), lambda i: (0, 0))],
        out_shape=[jax.ShapeDtypeStruct((n, LANES), F32),
                   jax.ShapeDtypeStruct((1, LANES), F32)],
        scratch_shapes=[pltpu.VMEM((1, LANES), F32)],
        compiler_params=_cp(("arbitrary",)),
        name="moe_rank",
    )(route)


def _pos_kernel(route_ref, rank_ref, off_ref, o_ref):
    lane = lax.broadcasted_iota(jnp.int32, (1, LANES), 1).astype(F32)
    route = route_ref[...]
    marked = jnp.logical_and(lane >= SEL_LANE, route > 0.5)
    p = rank_ref[...] + off_ref[...]
    lane_a = jnp.min(jnp.where(marked, lane, 1e9), axis=1, keepdims=True)
    lane_b = jnp.max(jnp.where(marked, lane, -1.0), axis=1, keepdims=True)

    def pick(src, at):
        return jnp.sum(jnp.where(lane == at, src, 0.0), axis=1, keepdims=True)

    out = jnp.where(lane == 0.0, pick(p, lane_a), 0.0)
    out = out + jnp.where(lane == 1.0, pick(p, lane_b), 0.0)
    out = out + jnp.where(lane == 2.0, pick(route, lane_a - SEL_LANE), 0.0)
    out = out + jnp.where(lane == 3.0, pick(route, lane_b - SEL_LANE), 0.0)
    o_ref[...] = out


def moe_positions(route, rank, off):
    n = route.shape[0]
    tm = _tile(n, PROJ_TM)
    blk = pl.BlockSpec((tm, LANES), lambda i: (i, 0))
    return pl.pallas_call(
        _pos_kernel,
        grid=(n // tm,),
        in_specs=[blk, blk, pl.BlockSpec((1, LANES), lambda i: (0, 0))],
        out_specs=blk,
        out_shape=jax.ShapeDtypeStruct((n, LANES), F32),
        compiler_params=_cp(("parallel",)),
        name="moe_positions",
    )(route, rank, off)


def _row_copy(src, src_row, dst, dst_row, sem):
    return pltpu.make_async_copy(src.at[pl.ds(src_row, 1)], dst.at[pl.ds(dst_row, 1)], sem)


def _dispatch_kernel(last_ref, pos_ref, h_ref, xs_hbm, zero_sc, sem):
    @pl.when(pl.program_id(0) == 0)
    def _():
        zero_sc[...] = jnp.zeros_like(zero_sc)

        def fill(e, carry):
            start = pl.multiple_of(last_ref[e], MOE_TM)
            pltpu.make_async_copy(zero_sc, xs_hbm.at[pl.ds(start, MOE_TM)], sem).start()
            return carry

        lax.fori_loop(0, MOE_E, fill, 0)

        def drain(e, carry):
            pltpu.make_async_copy(zero_sc, xs_hbm.at[pl.ds(0, MOE_TM)], sem).wait()
            return carry

        lax.fori_loop(0, MOE_E, drain, 0)

    tok = h_ref.shape[0]

    def issue(r, carry):
        _row_copy(h_ref, r, xs_hbm, pos_ref[0, 0, r], sem).start()
        _row_copy(h_ref, r, xs_hbm, pos_ref[0, 0, tok + r], sem).start()
        return carry

    lax.fori_loop(0, tok, issue, 0, unroll=8)
    for _ in range(2):
        pltpu.make_async_copy(h_ref, xs_hbm.at[pl.ds(0, tok)], sem).wait()


def moe_dispatch(last_tile_start, pos, h, n_sorted):
    n, d = h.shape
    tok = pos.shape[2] // 2
    return pl.pallas_call(
        _dispatch_kernel,
        grid_spec=pltpu.PrefetchScalarGridSpec(
            num_scalar_prefetch=1,
            grid=(n // tok,),
            in_specs=[pl.BlockSpec((1, 1, 2 * tok), lambda i, last: (i, 0, 0), memory_space=pltpu.SMEM),
                      pl.BlockSpec((tok, d), lambda i, last: (i, 0))],
            out_specs=pl.BlockSpec(memory_space=pl.ANY),
            scratch_shapes=[pltpu.VMEM((MOE_TM, d), h.dtype), pltpu.SemaphoreType.DMA]),
        out_shape=jax.ShapeDtypeStruct((n_sorted, d), h.dtype),
        compiler_params=_cp(("arbitrary",)),
        name="moe_dispatch",
    )(last_tile_start, pos, h)


def _experts_kernel(te_ref, nt_ref, x_ref, w1_ref, w3_ref, w2_ref, o_ref, w1_sc, w3_sc, w2_sc):
    i = pl.program_id(0)
    active = i < nt_ref[0]

    @pl.when(jnp.logical_and(active, jnp.logical_or(i == 0, te_ref[i] != te_ref[jnp.maximum(i - 1, 0)])))
    def _():
        w1_sc[...] = w1_ref[0, 0].astype(BF16)
        w3_sc[...] = w3_ref[0, 0].astype(BF16)
        w2_sc[...] = w2_ref[0, 0].astype(BF16)

    @pl.when(active)
    def _():
        x = x_ref[...].astype(BF16)
        a = jnp.dot(x, w1_sc[...], preferred_element_type=F32)
        b = jnp.dot(x, w3_sc[...], preferred_element_type=F32)
        he = (_silu(a) * b).astype(BF16)
        o_ref[...] = jnp.dot(he, w2_sc[...], preferred_element_type=F32)

    @pl.when(jnp.logical_not(active))
    def _():
        o_ref[...] = jnp.zeros_like(o_ref)


def moe_experts(tile_expert, n_tiles, xs, w1, w3, w2, layer):
    n_sorted, d = xs.shape
    return pl.pallas_call(
        _experts_kernel,
        grid_spec=pltpu.PrefetchScalarGridSpec(
            num_scalar_prefetch=2,
            grid=(n_sorted // MOE_TM,),
            in_specs=[pl.BlockSpec((MOE_TM, d), lambda i, te, nt: (jnp.minimum(i, nt[0] - 1), 0)),
                      pl.BlockSpec((1, 1, d, MOE_HIDDEN), lambda i, te, nt: (layer, te[i], 0, 0)),
                      pl.BlockSpec((1, 1, d, MOE_HIDDEN), lambda i, te, nt: (layer, te[i], 0, 0)),
                      pl.BlockSpec((1, 1, MOE_HIDDEN, d), lambda i, te, nt: (layer, te[i], 0, 0))],
            out_specs=pl.BlockSpec((MOE_TM, d), lambda i, te, nt: (i, 0)),
            scratch_shapes=[pltpu.VMEM((d, MOE_HIDDEN), BF16), pltpu.VMEM((d, MOE_HIDDEN), BF16),
                            pltpu.VMEM((MOE_HIDDEN, d), BF16)]),
        out_shape=jax.ShapeDtypeStruct((n_sorted, d), F32),
        compiler_params=_cp(("arbitrary",)),
        name="moe_experts",
    )(tile_expert, n_tiles, xs, w1, w3, w2)


def _combine_kernel(pos_ref, nxt_ref, meta_ref, s_ref, g_ref, ys_hbm, o_ref, buf_ref, sem):
    i = pl.program_id(0)
    slot = i % 2
    tok = s_ref.shape[0]

    def gather(p_ref, dst_slot):
        def issue(r, carry):
            _row_copy(ys_hbm, p_ref[0, 0, r], buf_ref.at[dst_slot, 0], r, sem.at[dst_slot]).start()
            _row_copy(ys_hbm, p_ref[0, 0, tok + r], buf_ref.at[dst_slot, 1], r, sem.at[dst_slot]).start()
            return carry

        lax.fori_loop(0, tok, issue, 0, unroll=8)

    @pl.when(i == 0)
    def _():
        gather(pos_ref, 0)

    @pl.when(i + 1 < pl.num_programs(0))
    def _():
        gather(nxt_ref, 1 - slot)

    for k in range(2):
        pltpu.make_async_copy(ys_hbm.at[pl.ds(0, tok)], buf_ref.at[slot, k], sem.at[slot]).wait()
    meta = meta_ref[...]
    y = meta[:, 2:3] * buf_ref[slot, 0] + meta[:, 3:4] * buf_ref[slot, 1]
    o_ref[...] = s_ref[...] + g_ref[0, 0] * y


def moe_combine(pos, meta, s, mod, k_gate, ys, nb):
    n, d = s.shape
    tm = pos.shape[2] // 2
    last = n // tm - 1
    return pl.pallas_call(
        _combine_kernel,
        grid=(n // tm,),
        in_specs=[pl.BlockSpec((1, 1, 2 * tm), lambda i: (i, 0, 0), memory_space=pltpu.SMEM),
                  pl.BlockSpec((1, 1, 2 * tm), lambda i: (jnp.minimum(i + 1, last), 0, 0),
                               memory_space=pltpu.SMEM),
                  pl.BlockSpec((tm, LANES), lambda i: (i, 0)),
                  pl.BlockSpec((tm, d), lambda i: (i, 0)),
                  pl.BlockSpec((1, 1, 1, d), lambda i: (_mod_row(i, tm, nb), k_gate, 0, 0)),
                  pl.BlockSpec(memory_space=pl.ANY)],
        out_specs=pl.BlockSpec((tm, d), lambda i: (i, 0)),
        out_shape=jax.ShapeDtypeStruct((n, d), F32),
        scratch_shapes=[pltpu.VMEM((2, 2, tm, d), F32), pltpu.SemaphoreType.DMA((2,))],
        input_output_aliases={3: 0},
        compiler_params=_cp(("arbitrary",)),
        name="moe_combine",
    )(pos, pos, meta, s, mod, ys)


def _na_layer(h, w_qkv, w_o, rpb, s, mod, nb):
    qkv = matmul(h, w_qkv.astype(BF16), BF16)
    o = na_attention(qkv, _na_bias_tables(rpb), nb)
    return matmul_residual(o, w_o.astype(BF16), s, mod, 2, nb)


def _da_layer(h, w_qkv, lam, subln_g, w_o, lambda_init, s, mod, nb):
    qkv = matmul_rope(h, w_qkv.astype(BF16), nb)
    o = da_attention(qkv, lam, subln_g, lambda_init, nb)
    return matmul_residual(o, w_o.astype(BF16), s, mod, 2, nb)


def _ml_layer(h, w_up, conv_w, conv_b, w_q, w_k, w_v, w_gate, b_gate, gn_w, skip, w_down, s, mod, nb):
    up = matmul(h, w_up.astype(BF16), BF16)
    xc, q, k, v = ml_conv_qkv(up, conv_w, conv_b, _block_diag(w_q), _block_diag(w_k), _block_diag(w_v), nb)
    ng = w_gate.shape[1]
    wg = jnp.pad(w_gate, ((0, 0), (0, LANES - ng))).reshape(3, ML_INNER, LANES).astype(BF16)
    bg = jnp.pad(b_gate.astype(F32), (0, LANES - ng)).reshape(1, LANES)
    g = ml_gates(q, k, v, wg, bg)
    gt = g[:, :ng].T
    hs = ml_scan(q, k, v, g, gt, nb)
    a = ml_finish(hs, xc, up, gn_w, skip)
    return matmul_residual(a, w_down.astype(BF16), s, mod, 2, nb)


def _moe_layer(norm_g, w_group, b_group, w_router, b_router, w1, w3, w2, layer, s, mod, nb):
    d = s.shape[1]
    pad = LANES - MOE_E - MOE_GROUPS
    wr = jnp.concatenate([w_router, w_group, jnp.zeros((d, pad), w_router.dtype)], axis=1).astype(F32)
    wr_hi = wr.astype(BF16)
    wr_lo = (wr - wr_hi.astype(F32)).astype(BF16)
    br = jnp.concatenate([b_router, b_group, jnp.zeros((pad,), b_router.dtype)]).astype(F32).reshape(1, LANES)
    h, route = moe_router(s, norm_g.astype(F32), mod, 3, 4, jnp.stack([wr_hi, wr_lo]), br, nb)
    rank, cnt = moe_rank(route)
    n = s.shape[0]
    cnt_e = cnt[0, SEL_LANE:SEL_LANE + MOE_E].astype(jnp.int32)
    gsz = ((cnt_e + MOE_TM - 1) // MOE_TM) * MOE_TM
    ends = jnp.cumsum(gsz)
    off_row = jnp.zeros((1, LANES), F32).at[0, SEL_LANE:SEL_LANE + MOE_E].set((ends - gsz).astype(F32))
    meta = moe_positions(route, rank, off_row)
    n_sorted = 2 * n + MOE_E * MOE_TM
    n_tiles = (ends[-1:] // MOE_TM).astype(jnp.int32)
    tile_ids = jnp.arange(n_sorted // MOE_TM, dtype=jnp.int32)
    tile_expert = jnp.sum((tile_ids[:, None] >= (ends // MOE_TM)[None, :]).astype(jnp.int32), axis=1)
    tile_expert = jnp.minimum(tile_expert, MOE_E - 1)
    tok = _tile(n, MOE_TOK)
    pos = meta[:, :2].astype(jnp.int32).reshape(n // tok, tok, 2)
    pos = pos.transpose(0, 2, 1).reshape(n // tok, 1, 2 * tok)
    last_tile_start = jnp.maximum(ends - MOE_TM, 0).astype(jnp.int32)
    xs = moe_dispatch(last_tile_start, pos, h, n_sorted)
    ys = moe_experts(tile_expert, n_tiles, xs, w1.astype(F32), w3.astype(F32), w2.astype(F32), layer)
    return moe_combine(pos, meta, s, mod, 5, ys, nb)


def kernel(x, c, ctx, c_ctx, mod_w, mod_b, norm_g, final_g, na_w_qkv, na_w_o, na_rpb, ml_w_up, ml_conv_w, ml_conv_b, ml_w_q, ml_w_k, ml_w_v, ml_w_gate, ml_b_gate, ml_gn_w, ml_skip, ml_w_down, da_w_qkv, da_lambda, da_subln_g, da_w_o, moe_w_group, moe_b_group, moe_w_router, moe_b_router, moe_w1, moe_w3, moe_w2):
    nb, seq, d = x.shape
    assert (seq, d, ctx.shape[1]) == (SEQ, D_MODEL, CTX_LEN) and nb < 16
    depth = mod_w.shape[0]
    nx = nb * seq
    s = jnp.concatenate([x.reshape(nx, d), ctx.reshape(nb * CTX_LEN, d)], axis=0).astype(F32)
    cvec = jnp.concatenate([c, c_ctx[None, :], jnp.zeros((16 - nb - 1, d), c.dtype)], axis=0).astype(F32)
    mods = mod_vectors(cvec, mod_w, mod_b).reshape(depth, 16, 6, 1, d)

    for i in range(depth):
        kind, j = i % N_MIXERS, i // N_MIXERS
        mod = mods[i]
        h = norm_mod(s, norm_g[i, 0].astype(F32), mod, 0, 1, nb)
        if kind == 0:
            s = _na_layer(h, na_w_qkv[j], na_w_o[j], na_rpb[j], s, mod, nb)
        elif kind == 1:
            s = _ml_layer(h, ml_w_up[j], ml_conv_w[j], ml_conv_b[j], ml_w_q[j], ml_w_k[j], ml_w_v[j],
                          ml_w_gate[j], ml_b_gate[j], ml_gn_w[j], ml_skip[j], ml_w_down[j], s, mod, nb)
        else:
            lambda_init = 0.8 - 0.6 * math.exp(-0.3 * i)
            s = _da_layer(h, da_w_qkv[j], da_lambda[j], da_subln_g[j], da_w_o[j], lambda_init, s, mod, nb)
        s = _moe_layer(norm_g[i, 1], moe_w_group[i], moe_b_group[i], moe_w_router[i], moe_b_router[i],
                       moe_w1, moe_w3, moe_w2, i, s, mod, nb)
    return final_norm(s, final_g.astype(F32), nx).reshape(nb, seq, d)
```

```python
import functools
import math

import numpy as np
import jax
import jax.numpy as jnp
from jax import lax
from jax.experimental import pallas as pl
from jax.experimental.pallas import tpu as pltpu

F32 = jnp.float32
BF16 = jnp.bfloat16

D_MODEL = 1024
SEQ = 4096
CTX_LEN = 256
GRID_W = 64
N_MIXERS = 3
EPS = 1e-6

NA_HEADS = 16
NA_WIN_H = 8
NA_WIN_W = 16

ML_HEADS = 4
ML_INNER = 2 * D_MODEL
ML_HEAD_DIM = ML_INNER // ML_HEADS
ML_BLOCK = 4
ML_CONV_K = 5

DA_HEADS = 8
DA_HEAD_DIM = 64
ROPE_BASE = 10000.0

MOE_GROUPS = 4
MOE_EPG = 8
MOE_E = MOE_GROUPS * MOE_EPG
MOE_HIDDEN = 512
MOE_TM = 512
MOE_TOK = 512
SEL_LANE = 64

LANES = 128
CHUNK = 256
X_CHUNKS = SEQ // CHUNK
PROJ_TM = 1024
DA_TQ = 512
DA_VT_ROWS = 128 + 16
NA_ROWS = 8
NEG = -1e30
VMEM_LIMIT = 56 * 1024 * 1024


def _cp(sem, vmem=VMEM_LIMIT):
    return pltpu.CompilerParams(dimension_semantics=sem, vmem_limit_bytes=vmem)


def _tile(n, pref):
    tm = pref
    while n % tm:
        tm //= 2
    assert tm >= CHUNK
    return tm


def _mod_row(i, tm, nb):
    return jnp.where(i < nb * (SEQ // tm), i // (SEQ // tm), nb)


def _nt(a, b):
    return lax.dot_general(a, b, (((1,), (1,)), ((), ())), preferred_element_type=F32)


def _tn(a, b):
    return lax.dot_general(a, b, (((0,), (0,)), ((), ())), preferred_element_type=F32)


def _silu(x):
    return x * jax.nn.sigmoid(x)


def _head_rows(q, lane):
    zero = jnp.zeros_like(q)
    return jnp.concatenate([jnp.where(lane < 64, q, zero), jnp.where(lane >= 64, q, zero)], axis=0)


def _mod_kernel(c_ref, w_ref, b_ref, o_ref):
    a = _silu(c_ref[...]).astype(BF16)
    o_ref[0] = jnp.dot(a, w_ref[0].astype(BF16), preferred_element_type=F32) + b_ref[0]


def mod_vectors(cvec, mod_w, mod_b):
    depth, d, n6 = mod_w.shape
    tn = 1024
    return pl.pallas_call(
        _mod_kernel,
        grid=(depth, n6 // tn),
        in_specs=[pl.BlockSpec((16, d), lambda l, j: (0, 0)),
                  pl.BlockSpec((1, d, tn), lambda l, j: (l, 0, j)),
                  pl.BlockSpec((1, 1, tn), lambda l, j: (l, 0, j))],
        out_specs=pl.BlockSpec((1, 16, tn), lambda l, j: (l, 0, j)),
        out_shape=jax.ShapeDtypeStruct((depth, 16, n6), F32),
        compiler_params=_cp(("parallel", "parallel")),
        name="mod_vectors",
    )(cvec, mod_w, mod_b.reshape(depth, 1, n6))


def _norm_mod_kernel(x_ref, g_ref, sh_ref, sc_ref, o_ref):
    x = x_ref[...]
    y = x * lax.rsqrt(jnp.mean(x * x, axis=-1, keepdims=True) + EPS) * g_ref[...]
    o_ref[...] = (y * (1.0 + sc_ref[0, 0]) + sh_ref[0, 0]).astype(o_ref.dtype)


def norm_mod(s, g, mod, k_shift, k_scale, nb):
    n, d = s.shape
    tm = _tile(n, PROJ_TM)
    return pl.pallas_call(
        _norm_mod_kernel,
        grid=(n // tm,),
        in_specs=[pl.BlockSpec((tm, d), lambda i: (i, 0)),
                  pl.BlockSpec((1, d), lambda i: (0, 0)),
                  pl.BlockSpec((1, 1, 1, d), lambda i: (_mod_row(i, tm, nb), k_shift, 0, 0)),
                  pl.BlockSpec((1, 1, 1, d), lambda i: (_mod_row(i, tm, nb), k_scale, 0, 0))],
        out_specs=pl.BlockSpec((tm, d), lambda i: (i, 0)),
        out_shape=jax.ShapeDtypeStruct((n, d), BF16),
        compiler_params=_cp(("parallel",)),
        name="norm_mod",
    )(s, g.reshape(1, d), mod, mod)


def _final_norm_kernel(x_ref, g_ref, o_ref):
    x = x_ref[...]
    o_ref[0] = x * lax.rsqrt(jnp.mean(x * x, axis=-1, keepdims=True) + EPS) * g_ref[...]


def final_norm(s, g, nb):
    d = s.shape[1]
    tm = PROJ_TM
    per = SEQ // tm
    return pl.pallas_call(
        _final_norm_kernel,
        grid=(nb * per,),
        in_specs=[pl.BlockSpec((tm, d), lambda i: (i, 0)),
                  pl.BlockSpec((1, d), lambda i: (0, 0))],
        out_specs=pl.BlockSpec((1, tm, d), lambda i: (i // per, i % per, 0)),
        out_shape=jax.ShapeDtypeStruct((nb, SEQ, d), F32),
        compiler_params=_cp(("parallel",)),
        name="final_norm",
    )(s, g.reshape(1, d))


def _mm_kernel(a_ref, w_ref, o_ref):
    o_ref[...] = jnp.dot(a_ref[...], w_ref[...], preferred_element_type=F32).astype(o_ref.dtype)


def matmul(a, w, out_dtype, tn=1024):
    n, k = a.shape
    tm = _tile(n, PROJ_TM)
    nout = w.shape[1]
    tn = min(tn, nout)
    return pl.pallas_call(
        _mm_kernel,
        grid=(nout // tn, n // tm),
        in_specs=[pl.BlockSpec((tm, k), lambda j, i: (i, 0)),
                  pl.BlockSpec((k, tn), lambda j, i: (0, j))],
        out_specs=pl.BlockSpec((tm, tn), lambda j, i: (i, j)),
        out_shape=jax.ShapeDtypeStruct((n, nout), out_dtype),
        compiler_params=_cp(("parallel", "parallel")),
        name="matmul",
    )(a, w)


def _mm_res_kernel(a_ref, w_ref, r_ref, g_ref, o_ref):
    acc = jnp.dot(a_ref[...], w_ref[...], preferred_element_type=F32)
    o_ref[...] = r_ref[...] + g_ref[0, 0] * acc


def matmul_residual(a, w, s, mod, k_gate, nb):
    n, k = a.shape
    d = w.shape[1]
    tm = _tile(n, PROJ_TM)
    return pl.pallas_call(
        _mm_res_kernel,
        grid=(n // tm,),
        in_specs=[pl.BlockSpec((tm, k), lambda i: (i, 0)),
                  pl.BlockSpec((k, d), lambda i: (0, 0)),
                  pl.BlockSpec((tm, d), lambda i: (i, 0)),
                  pl.BlockSpec((1, 1, 1, d), lambda i: (_mod_row(i, tm, nb), k_gate, 0, 0))],
        out_specs=pl.BlockSpec((tm, d), lambda i: (i, 0)),
        out_shape=jax.ShapeDtypeStruct((n, d), F32),
        input_output_aliases={2: 0},
        compiler_params=_cp(("parallel",)),
        name="matmul_residual",
    )(a, w, s, mod)


def _na_bias_tables(rpb):
    o = np.arange(NA_WIN_H)[:, None]
    j = np.arange(NA_WIN_H)[None, :]
    dy = j - o + NA_WIN_H - 1
    qc = np.arange(GRID_W)[:, None]
    kc = np.arange(GRID_W)[None, :]
    dx = np.clip(kc - qc, -(NA_WIN_W - 1), NA_WIN_W - 1) + NA_WIN_W - 1
    w_start = np.clip(qc - NA_WIN_W // 2, 0, GRID_W - NA_WIN_W)
    valid = (kc >= w_start) & (kc < w_start + NA_WIN_W)
    tbl = rpb.astype(F32)[:, dy][:, :, :, dx]
    tbl = jnp.where(jnp.asarray(valid)[None, None, None], tbl, NEG)
    tbl = tbl.reshape(NA_HEADS // 2, 2, NA_WIN_H, NA_WIN_H, GRID_W, GRID_W)
    tbl = tbl.transpose(2, 0, 3, 5, 1, 4)
    return tbl.reshape(NA_WIN_H, NA_HEADS // 2, NA_WIN_H * GRID_W, 2 * GRID_W)


def _softmax_cols(s):
    m = jnp.max(s, axis=0, keepdims=True)
    p = jnp.exp(s - m)
    return (p * (1.0 / jnp.sum(p, axis=0, keepdims=True))).astype(BF16)


def _na_x_kernel(q_ref, kx_ref, vx_ref, kc_ref, vc_ref, bias_ref, o_ref, s_sc, p_sc):
    t = pl.program_id(2)
    lane = lax.broadcasted_iota(jnp.int32, (1, LANES), 1)
    nwin = NA_WIN_H * GRID_W
    starts = []
    for i in range(NA_ROWS):
        r = t * NA_ROWS + i
        rs = jnp.clip(r - NA_WIN_H // 2, 0, SEQ // GRID_W - NA_WIN_H)
        start = pl.multiple_of(rs * GRID_W, GRID_W)
        starts.append(start)
        qq = _head_rows(q_ref[i * GRID_W:(i + 1) * GRID_W, :] * 0.125, lane)
        s_sc[i, 0:nwin, :] = _nt(kx_ref[pl.ds(start, nwin), :], qq) + bias_ref[r - rs, 0]
        s_sc[i, nwin:, :] = _nt(kc_ref[...], qq)
    for i in range(NA_ROWS):
        p_sc[i] = _softmax_cols(s_sc[i])
    for i in range(NA_ROWS):
        o = (_tn(p_sc[i, 0:nwin, :], vx_ref[pl.ds(starts[i], nwin), :])
             + _tn(p_sc[i, nwin:, :], vc_ref[...]))
        o = jnp.where(lane < 64, o[0:GRID_W], o[GRID_W:])
        o_ref[i * GRID_W:(i + 1) * GRID_W, :] = o.astype(o_ref.dtype)


def _na_ctx_kernel(q_ref, kc_ref, vc_ref, o_in_hbm, o_ref):
    del o_in_hbm
    lane = lax.broadcasted_iota(jnp.int32, (1, LANES), 1)
    q = q_ref[...] * 0.125
    outs = []
    for hh in range(2):
        qm = jnp.where((lane // 64) == hh, q, jnp.zeros_like(q))
        p = _softmax_cols(_nt(kc_ref[...], qm))
        outs.append(_tn(p, vc_ref[...]))
    o_ref[...] = jnp.where(lane < 64, outs[0], outs[1]).astype(o_ref.dtype)


def na_attention(qkv, bias, nb):
    n = qkv.shape[0]
    d = D_MODEL
    ncb = d // LANES
    xc = nb * X_CHUNKS
    nkeys = NA_WIN_H * GRID_W + CTX_LEN
    tq = NA_ROWS * GRID_W
    steps = SEQ // tq
    o = pl.pallas_call(
        _na_x_kernel,
        grid=(nb, ncb, steps),
        in_specs=[pl.BlockSpec((tq, LANES), lambda b, h, t: (b * steps + t, h)),
                  pl.BlockSpec((SEQ, LANES), lambda b, h, t: (b, ncb + h)),
                  pl.BlockSpec((SEQ, LANES), lambda b, h, t: (b, 2 * ncb + h)),
                  pl.BlockSpec((CHUNK, LANES), lambda b, h, t: (xc + b, ncb + h)),
                  pl.BlockSpec((CHUNK, LANES), lambda b, h, t: (xc + b, 2 * ncb + h)),
                  pl.BlockSpec((NA_WIN_H, 1, NA_WIN_H * GRID_W, LANES), lambda b, h, t: (0, h, 0, 0))],
        out_specs=pl.BlockSpec((tq, LANES), lambda b, h, t: (b * steps + t, h)),
        out_shape=jax.ShapeDtypeStruct((n, d), BF16),
        scratch_shapes=[pltpu.VMEM((NA_ROWS, nkeys, LANES), F32),
                        pltpu.VMEM((NA_ROWS, nkeys, LANES), BF16)],
        compiler_params=_cp(("parallel", "parallel", "arbitrary")),
        name="na_attention_x",
    )(qkv, qkv, qkv, qkv, qkv, bias)
    return pl.pallas_call(
        _na_ctx_kernel,
        grid=(nb, ncb),
        in_specs=[pl.BlockSpec((CHUNK, LANES), lambda b, h: (xc + b, h)),
                  pl.BlockSpec((CHUNK, LANES), lambda b, h: (xc + b, ncb + h)),
                  pl.BlockSpec((CHUNK, LANES), lambda b, h: (xc + b, 2 * ncb + h)),
                  pl.BlockSpec(memory_space=pl.ANY)],
        out_specs=pl.BlockSpec((CHUNK, LANES), lambda b, h: (xc + b, h)),
        out_shape=jax.ShapeDtypeStruct((n, d), BF16),
        input_output_aliases={3: 0},
        compiler_params=_cp(("parallel", "parallel")),
        name="na_attention_ctx",
    )(qkv, qkv, qkv, o)


def _rope_tables(tm):
    half = DA_HEAD_DIM // 2
    freqs = ROPE_BASE ** (-np.arange(0, half, 2, dtype=np.float32) / half)
    t = np.arange(SEQ)
    row, col = t // GRID_W, t % GRID_W
    lane = np.arange(LANES)
    l64 = lane % DA_HEAD_DIM
    use_col = (l64 // half) == 1
    l32 = l64 % half
    fi = l32 % (half // 2)
    second = l32 >= half // 2
    pos = np.where(use_col[None, :], col[:, None], row[:, None]).astype(np.float32)
    ang = pos * freqs[fi][None, :]
    cos = np.cos(ang).astype(np.float32)
    sin = np.sin(ang).astype(np.float32)
    sa = np.where(second[None, :], 0.0, -sin).astype(np.float32)
    sb = np.where(second[None, :], sin, 0.0).astype(np.float32)
    ident = np.ones((tm, LANES), np.float32)
    zero = np.zeros((tm, LANES), np.float32)
    return (jnp.asarray(np.concatenate([cos, ident])),
            jnp.asarray(np.concatenate([sa, zero])),
            jnp.asarray(np.concatenate([sb, zero])))


def _mm_rope_kernel(a_ref, w_ref, c_ref, sa_ref, sb_ref, o_ref):
    j = pl.program_id(0)
    acc = jnp.dot(a_ref[...], w_ref[...], preferred_element_type=F32)

    @pl.when(j < 2)
    def _():
        scale = jnp.where(j == 0, DA_HEAD_DIM ** -0.5 * math.log2(math.e), 1.0)
        cos = c_ref[...] * scale
        sa = sa_ref[...] * scale
        sb = sb_ref[...] * scale
        for g in range(acc.shape[1] // LANES):
            sl = slice(g * LANES, (g + 1) * LANES)
            x = acc[:, sl]
            y = x * cos + pltpu.roll(x, LANES - 16, 1) * sa + pltpu.roll(x, 16, 1) * sb
            o_ref[:, sl] = y.astype(o_ref.dtype)

    @pl.when(j >= 2)
    def _():
        o_ref[...] = acc.astype(o_ref.dtype)


def matmul_rope(a, w, nb):
    n, k = a.shape
    d = D_MODEL
    tm = _tile(n, PROJ_TM)
    cos, sa, sb = _rope_tables(tm)
    xt = nb * (SEQ // tm)

    def tab(j, i):
        return (jnp.where(i < xt, i % (SEQ // tm), SEQ // tm), 0)

    tspec = pl.BlockSpec((tm, LANES), tab)
    return pl.pallas_call(
        _mm_rope_kernel,
        grid=(3, n // tm),
        in_specs=[pl.BlockSpec((tm, k), lambda j, i: (i, 0)),
                  pl.BlockSpec((k, d), lambda j, i: (0, j)),
                  tspec, tspec, tspec],
        out_specs=pl.BlockSpec((tm, d), lambda j, i: (i, j)),
        out_shape=jax.ShapeDtypeStruct((n, 3 * d), BF16),
        compiler_params=_cp(("parallel", "parallel")),
        name="matmul_rope",
    )(a, w, cos, sa, sb)


def _da_body(q_ref, lam_ref, g_ref, o_ref, s_sc, acc_sc, m_sc, lambda_init, run_chunks):
    tq = q_ref.shape[0]
    lane = lax.broadcasted_iota(jnp.int32, (1, LANES), 1)
    qq = _head_rows(q_ref[...], lane)
    m_sc[...] = jnp.full_like(m_sc, NEG)
    acc_sc[...] = jnp.zeros_like(acc_sc)

    def scores(slot, k):
        s_sc[slot] = _nt(k, qq)

    def fold(slot, vt):
        for g in range(2 * tq // CHUNK):
            sl = slice(g * CHUNK, (g + 1) * CHUNK)
            s = s_sc[slot, :, sl]
            m_old = m_sc[:, sl]
            m_new = jnp.maximum(m_old, jnp.max(s, axis=0, keepdims=True))
            alpha = jnp.exp2(m_old - m_new)
            p = jnp.exp2(s - m_new)
            m_sc[:, sl] = m_new
            acc_sc[:, sl] = acc_sc[:, sl] * alpha + jnp.dot(vt, p.astype(BF16), preferred_element_type=F32)

    run_chunks(scores, fold)

    ot = acc_sc[0:LANES, :] / acc_sc[LANES:LANES + 1, :]
    lam = lam_ref[...]
    lam_full = (jnp.exp(jnp.sum(lam[0:1] * lam[1:2], axis=1, keepdims=True))
                - jnp.exp(jnp.sum(lam[2:3] * lam[3:4], axis=1, keepdims=True)) + lambda_init)
    od = ot[:, :tq] - lam_full * ot[:, tq:]
    y = od * lax.rsqrt(jnp.mean(od * od, axis=0, keepdims=True) + EPS) * g_ref[...]
    o_ref[...] = (y * (1.0 - lambda_init)).T.astype(o_ref.dtype)


def _da_x_kernel(q_ref, kx_ref, kc_ref, vtx_ref, vtc_ref, lam_ref, g_ref, o_ref,
                 s_sc, acc_sc, m_sc, *, lambda_init):
    def kx(c):
        return kx_ref[pl.ds(pl.multiple_of(c * CHUNK, CHUNK), CHUNK), :]

    def run_chunks(scores, fold):
        scores(0, kx(0))

        def body(i, carry):
            scores(1, kx(2 * i + 1))
            fold(0, vtx_ref[0, 2 * i])
            scores(0, kx(2 * i + 2))
            fold(1, vtx_ref[0, 2 * i + 1])
            return carry

        lax.fori_loop(0, X_CHUNKS // 2 - 1, body, 0)
        scores(1, kx(X_CHUNKS - 1))
        fold(0, vtx_ref[0, X_CHUNKS - 2])
        scores(0, kc_ref[...])
        fold(1, vtx_ref[0, X_CHUNKS - 1])
        fold(0, vtc_ref[0, 0])

    _da_body(q_ref, lam_ref, g_ref, o_ref, s_sc, acc_sc, m_sc, lambda_init, run_chunks)


def _da_ctx_kernel(q_ref, kc_ref, vtc_ref, lam_ref, g_ref, o_in_hbm, o_ref,
                   s_sc, acc_sc, m_sc, *, lambda_init):
    del o_in_hbm

    def run_chunks(scores, fold):
        scores(0, kc_ref[...])
        fold(0, vtc_ref[0, 0])

    _da_body(q_ref, lam_ref, g_ref, o_ref, s_sc, acc_sc, m_sc, lambda_init, run_chunks)


def _da_scratch(tq):
    return [pltpu.VMEM((2, CHUNK, 2 * tq), F32), pltpu.VMEM((DA_VT_ROWS, 2 * tq), F32),
            pltpu.VMEM((1, 2 * tq), F32)]


def da_attention(qkv, lam, subln_g, lambda_init, nb):
    n = qkv.shape[0]
    d = D_MODEL
    ncb = d // LANES
    xc = nb * X_CHUNKS
    tq = DA_TQ
    vt = qkv[:, 2 * d:].reshape(n // CHUNK, CHUNK, ncb, LANES).transpose(2, 0, 3, 1)
    extra = jnp.zeros((ncb, n // CHUNK, DA_VT_ROWS - LANES, CHUNK), BF16).at[:, :, 0, :].set(1.0)
    vt = jnp.concatenate([vt, extra], axis=2)
    lam = lam.astype(F32)
    gcol = subln_g.astype(F32).reshape(LANES, 1)
    o = pl.pallas_call(
        functools.partial(_da_x_kernel, lambda_init=lambda_init),
        grid=(nb, ncb, SEQ // tq),
        in_specs=[pl.BlockSpec((tq, LANES), lambda b, h, t: (b * (SEQ // tq) + t, h)),
                  pl.BlockSpec((SEQ, LANES), lambda b, h, t: (b, ncb + h)),
                  pl.BlockSpec((CHUNK, LANES), lambda b, h, t: (xc + b, ncb + h)),
                  pl.BlockSpec((1, X_CHUNKS, DA_VT_ROWS, CHUNK), lambda b, h, t: (h, b, 0, 0)),
                  pl.BlockSpec((1, 1, DA_VT_ROWS, CHUNK), lambda b, h, t: (h, xc + b, 0, 0)),
                  pl.BlockSpec((4, DA_HEAD_DIM), lambda b, h, t: (0, 0)),
                  pl.BlockSpec((LANES, 1), lambda b, h, t: (0, 0))],
        out_specs=pl.BlockSpec((tq, LANES), lambda b, h, t: (b * (SEQ // tq) + t, h)),
        out_shape=jax.ShapeDtypeStruct((n, d), BF16),
        scratch_shapes=_da_scratch(tq),
        compiler_params=_cp(("parallel", "parallel", "arbitrary")),
        name="da_attention_x",
    )(qkv, qkv, qkv, vt, vt, lam, gcol)
    return pl.pallas_call(
        functools.partial(_da_ctx_kernel, lambda_init=lambda_init),
        grid=(nb, ncb),
        in_specs=[pl.BlockSpec((CHUNK, LANES), lambda b, h: (xc + b, h)),
                  pl.BlockSpec((CHUNK, LANES), lambda b, h: (xc + b, ncb + h)),
                  pl.BlockSpec((1, 1, DA_VT_ROWS, CHUNK), lambda b, h: (h, xc + b, 0, 0)),
                  pl.BlockSpec((4, DA_HEAD_DIM), lambda b, h: (0, 0)),
                  pl.BlockSpec((LANES, 1), lambda b, h: (0, 0)),
                  pl.BlockSpec(memory_space=pl.ANY)],
        out_specs=pl.BlockSpec((CHUNK, LANES), lambda b, h: (xc + b, h)),
        out_shape=jax.ShapeDtypeStruct((n, d), BF16),
        scratch_shapes=_da_scratch(CHUNK),
        input_output_aliases={5: 0},
        compiler_params=_cp(("parallel", "parallel")),
        name="da_attention_ctx",
    )(qkv, qkv, vt, lam, gcol, o)


def _block_diag(w):
    per = LANES // ML_BLOCK
    wr = w.reshape(ML_INNER // LANES, per, ML_BLOCK, ML_BLOCK)
    eye = jnp.eye(per, dtype=w.dtype)
    return jnp.einsum('cgio,gh->cgiho', wr, eye).reshape(ML_INNER // LANES, LANES, LANES).astype(BF16)


def _ml_conv_kernel(p_ref, c_ref, n_ref, cw_ref, cb_ref, wq_ref, wk_ref, wv_ref,
                    xc_ref, q_ref, k_ref, v_ref, *, n_xchunks):
    i = pl.program_id(0)
    is_x = i < n_xchunks
    j = i % X_CHUNKS
    halo = 16
    cur = c_ref[...].astype(F32)
    prev = p_ref[CHUNK - halo:CHUNK, :].astype(F32)
    nxt = n_ref[0:halo, :].astype(F32)
    prev = jnp.where(jnp.logical_and(is_x, j > 0), prev, jnp.zeros_like(prev))
    nxt = jnp.where(jnp.logical_and(is_x, j < X_CHUNKS - 1), nxt, jnp.zeros_like(nxt))
    xp = jnp.concatenate([prev, cur, nxt], axis=0)
    rows = CHUNK + 2 * halo
    y = cb_ref[...] + cw_ref[ML_CONV_K // 2:ML_CONV_K // 2 + 1, :] * cur
    for tap in range(ML_CONV_K):
        dlt = tap - ML_CONV_K // 2
        if dlt == 0:
            continue
        shifted = pltpu.roll(xp, (-dlt) % rows, 0)[halo:halo + CHUNK]
        y = y + cw_ref[tap:tap + 1, :] * shifted
    xcb = _silu(y).astype(BF16)
    xc_ref[...] = xcb
    xm = c_ref[...]
    for s in range(xcb.shape[1] // LANES):
        sl = slice(s * LANES, (s + 1) * LANES)
        q_ref[:, sl] = jnp.dot(xcb[:, sl], wq_ref[s], preferred_element_type=F32).astype(BF16)
        k_ref[:, sl] = jnp.dot(xcb[:, sl], wk_ref[s], preferred_element_type=F32).astype(BF16)
        v_ref[:, sl] = jnp.dot(xm[:, sl], wv_ref[s], preferred_element_type=F32).astype(BF16)


def ml_conv_qkv(up, conv_w, conv_b, wq, wk, wv, nb):
    n = up.shape[0]
    nchunks = n // CHUNK
    cw = 512
    ncb = ML_INNER // cw
    sub = cw // LANES
    blk = pl.BlockSpec((CHUNK, cw), lambda i, c: (i, c))
    wspec = pl.BlockSpec((sub, LANES, LANES), lambda i, c: (c, 0, 0))
    out = jax.ShapeDtypeStruct((n, ML_INNER), BF16)
    return pl.pallas_call(
        functools.partial(_ml_conv_kernel, n_xchunks=nb * X_CHUNKS),
        grid=(nchunks, ncb),
        in_specs=[pl.BlockSpec((CHUNK, cw), lambda i, c: (jnp.maximum(i - 1, 0), c)),
                  blk,
                  pl.BlockSpec((CHUNK, cw), lambda i, c: (jnp.minimum(i + 1, nchunks - 1), c)),
                  pl.BlockSpec((ML_CONV_K, cw), lambda i, c: (0, c)),
                  pl.BlockSpec((1, cw), lambda i, c: (0, c)),
                  wspec, wspec, wspec],
        out_specs=[blk, blk, blk, blk],
        out_shape=[out, out, out, out],
        compiler_params=_cp(("parallel", "parallel")),
        name="ml_conv_qkv",
    )(up, up, up, conv_w.astype(F32), conv_b.astype(F32).reshape(1, ML_INNER), wq, wk, wv)


def _ml_gate_kernel(q_ref, k_ref, v_ref, w_ref, b_ref, o_ref):
    acc = jnp.dot(q_ref[...], w_ref[0], preferred_element_type=F32)
    acc = acc + jnp.dot(k_ref[...], w_ref[1], preferred_element_type=F32)
    acc = acc + jnp.dot(v_ref[...], w_ref[2], preferred_element_type=F32)
    o_ref[...] = acc + b_ref[...]


def ml_gates(q, k, v, wg, bg):
    n = q.shape[0]
    tm = _tile(n, PROJ_TM)
    blk = pl.BlockSpec((tm, ML_INNER), lambda i: (i, 0))
    return pl.pallas_call(
        _ml_gate_kernel,
        grid=(n // tm,),
        in_specs=[blk, blk, blk,
                  pl.BlockSpec((3, ML_INNER, LANES), lambda i: (0, 0, 0)),
                  pl.BlockSpec((1, LANES), lambda i: (0, 0))],
        out_specs=pl.BlockSpec((tm, LANES), lambda i: (i, 0)),
        out_shape=jax.ShapeDtypeStruct((n, LANES), F32),
        compiler_params=_cp(("parallel",)),
        name="ml_gates",
    )(q, k, v, wg, bg)


def _log_sigmoid(x):
    return jnp.minimum(x, 0.0) - jnp.log(1.0 + jnp.exp(-jnp.abs(x)))


def _ml_scan_dir(d, h, p, q_ref, k_ref, v_ref, g_ref, gt_ref, o_ref, c_sc, n_sc, m_sc):
    ninf = -jnp.inf

    @pl.when(p == 0)
    def _():
        c_sc[...] = jnp.zeros_like(c_sc)
        n_sc[...] = jnp.zeros_like(n_sc)
        m_sc[...] = jnp.full_like(m_sc, ninf)

    L = CHUNK
    col_i = d * 2 * ML_HEADS + h
    col_f = col_i + ML_HEADS
    lane = lax.broadcasted_iota(jnp.int32, (1, LANES), 1)
    g = g_ref[...]
    i_col = jnp.sum(jnp.where(lane == col_i, g, 0.0), axis=1, keepdims=True)
    f_col = _log_sigmoid(jnp.sum(jnp.where(lane == col_f, g, 0.0), axis=1, keepdims=True))
    i_row = gt_ref[pl.ds(col_i, 1), :]
    f_row = _log_sigmoid(gt_ref[pl.ds(col_f, 1), :])

    jj = lax.broadcasted_iota(jnp.int32, (L, L), 0)
    ss = lax.broadcasted_iota(jnp.int32, (L, L), 1)
    valid = (ss <= jj) if d == 0 else (ss >= jj)
    valid_t = (jj <= ss) if d == 0 else (jj >= ss)
    bcum_col = jnp.sum(jnp.where(valid, f_row, 0.0), axis=1, keepdims=True)
    bcum_row = jnp.sum(jnp.where(valid_t, f_col, 0.0), axis=0, keepdims=True)
    b_last = jnp.sum(f_row, axis=1, keepdims=True)
    m_prev = m_sc[...]

    logd = jnp.where(valid, bcum_col - bcum_row + i_row, ninf)
    log_inter = bcum_col + m_prev
    m_j = jnp.maximum(log_inter, jnp.max(logd, axis=1, keepdims=True))
    dmat = jnp.exp(logd - m_j)
    inter = jnp.exp(log_inter - m_j)

    q = q_ref[...]
    k = k_ref[...]
    v = v_ref[...]
    scale = ML_HEAD_DIM ** -0.5
    sc = _nt(q, k) * scale * dmat
    c_prev = c_sc[...]
    n_prev = n_sc[...]
    qc = jnp.dot(q, c_prev.astype(BF16), preferred_element_type=F32) * scale
    num = inter * qc + jnp.dot(sc.astype(BF16), v, preferred_element_type=F32)
    qn = jnp.sum(q.astype(F32) * n_prev, axis=1, keepdims=True) * scale
    den = inter * qn + jnp.sum(sc, axis=1, keepdims=True)
    o_ref[...] = num / jnp.maximum(jnp.abs(den), jnp.exp(-m_j))

    ls = b_last - bcum_col + i_col
    m_new = jnp.maximum(b_last + m_prev, jnp.max(ls, axis=0, keepdims=True))
    w = jnp.exp(ls - m_new)
    decay = jnp.exp(b_last + m_prev - m_new)
    kw = k.astype(F32) * w
    c_sc[...] = decay * c_prev + _tn(kw.astype(BF16), v)
    n_sc[...] = decay * n_prev + jnp.sum(kw, axis=0, keepdims=True)
    m_sc[...] = m_new


def _ml_scan_kernel(qf_ref, kf_ref, vf_ref, gf_ref, gtf_ref, qb_ref, kb_ref, vb_ref, gb_ref, gtb_ref,
                    of_ref, ob_ref, c_sc, n_sc, m_sc):
    h = pl.program_id(1)
    p = pl.program_id(2)
    _ml_scan_dir(0, h, p, qf_ref, kf_ref, vf_ref, gf_ref, gtf_ref, of_ref, c_sc.at[0], n_sc.at[0], m_sc.at[0])
    _ml_scan_dir(1, h, p, qb_ref, kb_ref, vb_ref, gb_ref, gtb_ref, ob_ref, c_sc.at[1], n_sc.at[1], m_sc.at[1])


def ml_scan(q, k, v, g, gt, nb):
    n = q.shape[0]
    xc = nb * X_CHUNKS

    def cidx(b, d, p):
        xi = p - 1 if d == 0 else X_CHUNKS - p
        return jnp.where(p == 0, xc + b, b * X_CHUNKS + xi)

    def specs(d):
        blk = pl.BlockSpec((CHUNK, ML_HEAD_DIM), lambda b, h, p: (cidx(b, d, p), h))
        return [blk, blk, blk,
                pl.BlockSpec((CHUNK, LANES), lambda b, h, p: (cidx(b, d, p), 0)),
                pl.BlockSpec((16, CHUNK), lambda b, h, p: (0, cidx(b, d, p)))]

    out = jax.ShapeDtypeStruct((n, ML_INNER), F32)
    return pl.pallas_call(
        _ml_scan_kernel,
        grid=(nb, ML_HEADS, X_CHUNKS + 1),
        in_specs=specs(0) + specs(1),
        out_specs=[pl.BlockSpec((CHUNK, ML_HEAD_DIM), lambda b, h, p: (cidx(b, 0, p), h)),
                   pl.BlockSpec((CHUNK, ML_HEAD_DIM), lambda b, h, p: (cidx(b, 1, p), h))],
        out_shape=[out, out],
        scratch_shapes=[pltpu.VMEM((2, ML_HEAD_DIM, ML_HEAD_DIM), F32),
                        pltpu.VMEM((2, 1, ML_HEAD_DIM), F32),
                        pltpu.VMEM((2, 1, 1), F32)],
        compiler_params=_cp(("parallel", "parallel", "arbitrary")),
        name="ml_scan",
    )(q, k, v, g, gt, q, k, v, g, gt)


def _ml_finish_kernel(hf_ref, hb_ref, xc_ref, z_ref, gn_ref, sk_ref, o_ref):
    hh = hf_ref[...] + hb_ref[...]
    z = z_ref[...].astype(F32)
    gate = _silu(z)
    for hd in range(ML_HEADS):
        sl = slice(hd * ML_HEAD_DIM, (hd + 1) * ML_HEAD_DIM)
        seg = hh[:, sl]
        mu = jnp.mean(seg, axis=-1, keepdims=True)
        cen = seg - mu
        var = jnp.mean(cen * cen, axis=-1, keepdims=True)
        hn = cen * lax.rsqrt(var + EPS) * gn_ref[:, sl]
        a = (hn + sk_ref[:, sl] * xc_ref[:, sl].astype(F32)) * gate[:, sl]
        o_ref[:, sl] = a.astype(o_ref.dtype)


def ml_finish(hf, hb, xc, up, gn_w, skip):
    n = xc.shape[0]
    tm = CHUNK
    vec = pl.BlockSpec((1, ML_INNER), lambda i: (0, 0))
    return pl.pallas_call(
        _ml_finish_kernel,
        grid=(n // tm,),
        in_specs=[pl.BlockSpec((tm, ML_INNER), lambda i: (i, 0)),
                  pl.BlockSpec((tm, ML_INNER), lambda i: (i, 0)),
                  pl.BlockSpec((tm, ML_INNER), lambda i: (i, 0)),
                  pl.BlockSpec((tm, ML_INNER), lambda i: (i, 1)),
                  vec, vec],
        out_specs=pl.BlockSpec((tm, ML_INNER), lambda i: (i, 0)),
        out_shape=jax.ShapeDtypeStruct((n, ML_INNER), BF16),
        compiler_params=_cp(("parallel",)),
        name="ml_finish",
    )(hf, hb, xc, up, gn_w.astype(F32).reshape(1, ML_INNER), skip.astype(F32).reshape(1, ML_INNER))


def _router_kernel(x_ref, g_ref, sh_ref, sc_ref, w_ref, b_ref, h_ref, o_ref):
    x = x_ref[...]
    y = x * lax.rsqrt(jnp.mean(x * x, axis=-1, keepdims=True) + EPS) * g_ref[...]
    hf = y * (1.0 + sc_ref[0, 0]) + sh_ref[0, 0]
    h_hi = hf.astype(BF16)
    h_ref[...] = hf
    h_lo = (hf - h_hi.astype(F32)).astype(BF16)
    logits = (jnp.dot(h_hi, w_ref[0], preferred_element_type=F32)
              + jnp.dot(h_hi, w_ref[1], preferred_element_type=F32)
              + jnp.dot(h_lo, w_ref[0], preferred_element_type=F32)) + b_ref[...]
    lane = lax.broadcasted_iota(jnp.int32, (1, LANES), 1).astype(F32)
    big = 1e9
    ninf = -jnp.inf
    is_g = jnp.logical_and(lane >= MOE_E, lane < MOE_E + MOE_GROUPS)
    gl = jnp.where(is_g, logits, ninf)
    gmax = jnp.max(gl, axis=1, keepdims=True)
    g_val = 1.0 / jnp.sum(jnp.exp(gl - gmax), axis=1, keepdims=True)
    g_idx = jnp.min(jnp.where(gl == gmax, lane, big), axis=1, keepdims=True) - MOE_E
    lo = g_idx * MOE_EPG
    sel = jnp.logical_and(lane >= lo, lane < lo + MOE_EPG)
    el = jnp.where(sel, logits, ninf)
    e1 = jnp.max(el, axis=1, keepdims=True)
    esum = jnp.sum(jnp.exp(el - e1), axis=1, keepdims=True)
    i1 = jnp.min(jnp.where(el == e1, lane, big), axis=1, keepdims=True)
    el2 = jnp.where(lane == i1, ninf, el)
    e2 = jnp.max(el2, axis=1, keepdims=True)
    i2 = jnp.min(jnp.where(el2 == e2, lane, big), axis=1, keepdims=True)
    p1 = 1.0 / esum
    p2 = jnp.exp(e2 - e1) / esum
    w1 = g_val * p1 / (p1 + p2)
    w2 = g_val * p2 / (p1 + p2)
    gates = jnp.where(lane == i1, w1, 0.0) + jnp.where(lane == i2, w2, 0.0)
    marks = jnp.where(jnp.logical_or(lane == i1 + SEL_LANE, lane == i2 + SEL_LANE), 1.0, 0.0)
    o_ref[...] = gates + marks


def moe_router(s, g, mod, k_shift, k_scale, wr, br, nb):
    n, d = s.shape
    tm = _tile(n, PROJ_TM)
    return pl.pallas_call(
        _router_kernel,
        grid=(n // tm,),
        in_specs=[pl.BlockSpec((tm, d), lambda i: (i, 0)),
                  pl.BlockSpec((1, d), lambda i: (0, 0)),
                  pl.BlockSpec((1, 1, 1, d), lambda i: (_mod_row(i, tm, nb), k_shift, 0, 0)),
                  pl.BlockSpec((1, 1, 1, d), lambda i: (_mod_row(i, tm, nb), k_scale, 0, 0)),
                  pl.BlockSpec((2, d, LANES), lambda i: (0, 0, 0)),
                  pl.BlockSpec((1, LANES), lambda i: (0, 0))],
        out_specs=[pl.BlockSpec((tm, d), lambda i: (i, 0)),
                   pl.BlockSpec((tm, LANES), lambda i: (i, 0))],
        out_shape=[jax.ShapeDtypeStruct((n, d), F32),
                   jax.ShapeDtypeStruct((n, LANES), F32)],
        compiler_params=_cp(("parallel",)),
        name="moe_router",
    )(s, g.reshape(1, d), mod, mod, wr, br)


def _rank_kernel(route_ref, rank_ref, cnt_ref, carry_ref):
    i = pl.program_id(0)

    @pl.when(i == 0)
    def _():
        carry_ref[...] = jnp.zeros_like(carry_ref)

    tm = route_ref.shape[0]
    lane = lax.broadcasted_iota(jnp.int32, (1, LANES), 1)
    marks = jnp.where(lane >= SEL_LANE, route_ref[...], 0.0)
    rr = lax.broadcasted_iota(jnp.int32, (tm, tm), 0)
    cc = lax.broadcasted_iota(jnp.int32, (tm, tm), 1)
    below = jnp.where(cc < rr, 1.0, 0.0).astype(BF16)
    rank_ref[...] = jnp.dot(below, marks.astype(BF16), preferred_element_type=F32) + carry_ref[...]
    carry_ref[...] += jnp.sum(marks, axis=0, keepdims=True)
    cnt_ref[...] = carry_ref[...]


def moe_rank(route):
    n = route.shape[0]
    tm = _tile(n, PROJ_TM)
    return pl.pallas_call(
        _rank_kernel,
        grid=(n // tm,),
        in_specs=[pl.BlockSpec((tm, LANES), lambda i: (i, 0))],
        out_specs=[pl.BlockSpec((tm, LANES), lambda i: (i, 0)),
                   pl.BlockSpec((1, LANES), lambda i: (0, 0))],
        out_shape=[jax.ShapeDtypeStruct((n, LANES), F32),
                   jax.ShapeDtypeStruct((1, LANES), F32)],
        scratch_shapes=[pltpu.VMEM((1, LANES), F32)],
        compiler_params=_cp(("arbitrary",)),
        name="moe_rank",
    )(route)


def _pos_kernel(route_ref, rank_ref, off_ref, o_ref):
    lane = lax.broadcasted_iota(jnp.int32, (1, LANES), 1).astype(F32)
    route = route_ref[...]
    marked = jnp.logical_and(lane >= SEL_LANE, route > 0.5)
    p = rank_ref[...] + off_ref[...]
    lane_a = jnp.min(jnp.where(marked, lane, 1e9), axis=1, keepdims=True)
    lane_b = jnp.max(jnp.where(marked, lane, -1.0), axis=1, keepdims=True)

    def pick(src, at):
        return jnp.sum(jnp.where(lane == at, src, 0.0), axis=1, keepdims=True)

    out = jnp.where(lane == 0.0, pick(p, lane_a), 0.0)
    out = out + jnp.where(lane == 1.0, pick(p, lane_b), 0.0)
    out = out + jnp.where(lane == 2.0, pick(route, lane_a - SEL_LANE), 0.0)
    out = out + jnp.where(lane == 3.0, pick(route, lane_b - SEL_LANE), 0.0)
    o_ref[...] = out


def moe_positions(route, rank, off):
    n = route.shape[0]
    tm = _tile(n, PROJ_TM)
    blk = pl.BlockSpec((tm, LANES), lambda i: (i, 0))
    return pl.pallas_call(
        _pos_kernel,
        grid=(n // tm,),
        in_specs=[blk, blk, pl.BlockSpec((1, LANES), lambda i: (0, 0))],
        out_specs=blk,
        out_shape=jax.ShapeDtypeStruct((n, LANES), F32),
        compiler_params=_cp(("parallel",)),
        name="moe_positions",
    )(route, rank, off)


def _row_copy(src, src_row, dst, dst_row, sem):
    return pltpu.make_async_copy(src.at[pl.ds(src_row, 1)], dst.at[pl.ds(dst_row, 1)], sem)


def _dispatch_kernel(last_ref, pos_ref, h_ref, xs_hbm, zero_sc, sem):
    @pl.when(pl.program_id(0) == 0)
    def _():
        zero_sc[...] = jnp.zeros_like(zero_sc)

        def fill(e, carry):
            start = pl.multiple_of(last_ref[e], MOE_TM)
            pltpu.make_async_copy(zero_sc, xs_hbm.at[pl.ds(start, MOE_TM)], sem).start()
            return carry

        lax.fori_loop(0, MOE_E, fill, 0)

        def drain(e, carry):
            pltpu.make_async_copy(zero_sc, xs_hbm.at[pl.ds(0, MOE_TM)], sem).wait()
            return carry

        lax.fori_loop(0, MOE_E, drain, 0)

    tok = h_ref.shape[0]

    def issue(r, carry):
        _row_copy(h_ref, r, xs_hbm, pos_ref[0, 0, r], sem).start()
        _row_copy(h_ref, r, xs_hbm, pos_ref[0, 0, tok + r], sem).start()
        return carry

    lax.fori_loop(0, tok, issue, 0, unroll=8)
    for _ in range(2):
        pltpu.make_async_copy(h_ref, xs_hbm.at[pl.ds(0, tok)], sem).wait()


def moe_dispatch(last_tile_start, pos, h, n_sorted):
    n, d = h.shape
    tok = pos.shape[2] // 2
    return pl.pallas_call(
        _dispatch_kernel,
        grid_spec=pltpu.PrefetchScalarGridSpec(
            num_scalar_prefetch=1,
            grid=(n // tok,),
            in_specs=[pl.BlockSpec((1, 1, 2 * tok), lambda i, last: (i, 0, 0), memory_space=pltpu.SMEM),
                      pl.BlockSpec((tok, d), lambda i, last: (i, 0))],
            out_specs=pl.BlockSpec(memory_space=pl.ANY),
            scratch_shapes=[pltpu.VMEM((MOE_TM, d), h.dtype), pltpu.SemaphoreType.DMA]),
        out_shape=jax.ShapeDtypeStruct((n_sorted, d), h.dtype),
        compiler_params=_cp(("arbitrary",)),
        name="moe_dispatch",
    )(last_tile_start, pos, h)


def _experts_kernel(te_ref, nt_ref, x_ref, w1_ref, w3_ref, w2_ref, o_ref, w1_sc, w3_sc, w2_sc):
    i = pl.program_id(0)
    active = i < nt_ref[0]

    @pl.when(jnp.logical_and(active, jnp.logical_or(i == 0, te_ref[i] != te_ref[jnp.maximum(i - 1, 0)])))
    def _():
        w1_sc[...] = w1_ref[0, 0].astype(BF16)
        w3_sc[...] = w3_ref[0, 0].astype(BF16)
        w2_sc[...] = w2_ref[0, 0].astype(BF16)

    @pl.when(active)
    def _():
        x = x_ref[...].astype(BF16)
        a = jnp.dot(x, w1_sc[...], preferred_element_type=F32)
        b = jnp.dot(x, w3_sc[...], preferred_element_type=F32)
        he = (_silu(a) * b).astype(BF16)
        o_ref[...] = jnp.dot(he, w2_sc[...], preferred_element_type=F32)

    @pl.when(jnp.logical_not(active))
    def _():
        o_ref[...] = jnp.zeros_like(o_ref)


def moe_experts(tile_expert, n_tiles, xs, w1, w3, w2, layer):
    n_sorted, d = xs.shape
    return pl.pallas_call(
        _experts_kernel,
        grid_spec=pltpu.PrefetchScalarGridSpec(
            num_scalar_prefetch=2,
            grid=(n_sorted // MOE_TM,),
            in_specs=[pl.BlockSpec((MOE_TM, d), lambda i, te, nt: (jnp.minimum(i, nt[0] - 1), 0)),
                      pl.BlockSpec((1, 1, d, MOE_HIDDEN), lambda i, te, nt: (layer, te[i], 0, 0)),
                      pl.BlockSpec((1, 1, d, MOE_HIDDEN), lambda i, te, nt: (layer, te[i], 0, 0)),
                      pl.BlockSpec((1, 1, MOE_HIDDEN, d), lambda i, te, nt: (layer, te[i], 0, 0))],
            out_specs=pl.BlockSpec((MOE_TM, d), lambda i, te, nt: (i, 0)),
            scratch_shapes=[pltpu.VMEM((d, MOE_HIDDEN), BF16), pltpu.VMEM((d, MOE_HIDDEN), BF16),
                            pltpu.VMEM((MOE_HIDDEN, d), BF16)]),
        out_shape=jax.ShapeDtypeStruct((n_sorted, d), F32),
        compiler_params=_cp(("arbitrary",)),
        name="moe_experts",
    )(tile_expert, n_tiles, xs, w1, w3, w2)


def _combine_kernel(pos_ref, nxt_ref, meta_ref, s_ref, g_ref, ys_hbm, o_ref, buf_ref, sem):
    i = pl.program_id(0)
    slot = i % 2
    tok = s_ref.shape[0]

    def gather(p_ref, dst_slot):
        def issue(r, carry):
            _row_copy(ys_hbm, p_ref[0, 0, r], buf_ref.at[dst_slot, 0], r, sem.at[dst_slot]).start()
            _row_copy(ys_hbm, p_ref[0, 0, tok + r], buf_ref.at[dst_slot, 1], r, sem.at[dst_slot]).start()
            return carry

        lax.fori_loop(0, tok, issue, 0, unroll=8)

    @pl.when(i == 0)
    def _():
        gather(pos_ref, 0)

    @pl.when(i + 1 < pl.num_programs(0))
    def _():
        gather(nxt_ref, 1 - slot)

    for k in range(2):
        pltpu.make_async_copy(ys_hbm.at[pl.ds(0, tok)], buf_ref.at[slot, k], sem.at[slot]).wait()
    meta = meta_ref[...]
    y = meta[:, 2:3] * buf_ref[slot, 0] + meta[:, 3:4] * buf_ref[slot, 1]
    o_ref[...] = s_ref[...] + g_ref[0, 0] * y


def moe_combine(pos, meta, s, mod, k_gate, ys, nb):
    n, d = s.shape
    tm = pos.shape[2] // 2
    last = n // tm - 1
    return pl.pallas_call(
        _combine_kernel,
        grid=(n // tm,),
        in_specs=[pl.BlockSpec((1, 1, 2 * tm), lambda i: (i, 0, 0), memory_space=pltpu.SMEM),
                  pl.BlockSpec((1, 1, 2 * tm), lambda i: (jnp.minimum(i + 1, last), 0, 0),
                               memory_space=pltpu.SMEM),
                  pl.BlockSpec((tm, LANES), lambda i: (i, 0)),
                  pl.BlockSpec((tm, d), lambda i: (i, 0)),
                  pl.BlockSpec((1, 1, 1, d), lambda i: (_mod_row(i, tm, nb), k_gate, 0, 0)),
                  pl.BlockSpec(memory_space=pl.ANY)],
        out_specs=pl.BlockSpec((tm, d), lambda i: (i, 0)),
        out_shape=jax.ShapeDtypeStruct((n, d), F32),
        scratch_shapes=[pltpu.VMEM((2, 2, tm, d), F32), pltpu.SemaphoreType.DMA((2,))],
        input_output_aliases={3: 0},
        compiler_params=_cp(("arbitrary",)),
        name="moe_combine",
    )(pos, pos, meta, s, mod, ys)


def _na_layer(h, w_qkv, w_o, rpb, s, mod, nb):
    qkv = matmul(h, w_qkv.astype(BF16), BF16)
    o = na_attention(qkv, _na_bias_tables(rpb), nb)
    return matmul_residual(o, w_o.astype(BF16), s, mod, 2, nb)


def _da_layer(h, w_qkv, lam, subln_g, w_o, lambda_init, s, mod, nb):
    qkv = matmul_rope(h, w_qkv.astype(BF16), nb)
    o = da_attention(qkv, lam, subln_g, lambda_init, nb)
    return matmul_residual(o, w_o.astype(BF16), s, mod, 2, nb)


def _ml_layer(h, w_up, conv_w, conv_b, w_q, w_k, w_v, w_gate, b_gate, gn_w, skip, w_down, s, mod, nb):
    up = matmul(h, w_up.astype(BF16), BF16)
    xc, q, k, v = ml_conv_qkv(up, conv_w, conv_b, _block_diag(w_q), _block_diag(w_k), _block_diag(w_v), nb)
    ng = w_gate.shape[1]
    wg = jnp.pad(w_gate, ((0, 0), (0, LANES - ng))).reshape(3, ML_INNER, LANES).astype(BF16)
    bg = jnp.pad(b_gate.astype(F32), (0, LANES - ng)).reshape(1, LANES)
    g = ml_gates(q, k, v, wg, bg)
    gt = g[:, :ng].T
    hf, hb = ml_scan(q, k, v, g, gt, nb)
    a = ml_finish(hf, hb, xc, up, gn_w, skip)
    return matmul_residual(a, w_down.astype(BF16), s, mod, 2, nb)


def _moe_layer(norm_g, w_group, b_group, w_router, b_router, w1, w3, w2, layer, s, mod, nb):
    d = s.shape[1]
    pad = LANES - MOE_E - MOE_GROUPS
    wr = jnp.concatenate([w_router, w_group, jnp.zeros((d, pad), w_router.dtype)], axis=1).astype(F32)
    wr_hi = wr.astype(BF16)
    wr_lo = (wr - wr_hi.astype(F32)).astype(BF16)
    br = jnp.concatenate([b_router, b_group, jnp.zeros((pad,), b_router.dtype)]).astype(F32).reshape(1, LANES)
    h, route = moe_router(s, norm_g.astype(F32), mod, 3, 4, jnp.stack([wr_hi, wr_lo]), br, nb)
    rank, cnt = moe_rank(route)
    n = s.shape[0]
    cnt_e = cnt[0, SEL_LANE:SEL_LANE + MOE_E].astype(jnp.int32)
    gsz = ((cnt_e + MOE_TM - 1) // MOE_TM) * MOE_TM
    ends = jnp.cumsum(gsz)
    off_row = jnp.zeros((1, LANES), F32).at[0, SEL_LANE:SEL_LANE + MOE_E].set((ends - gsz).astype(F32))
    meta = moe_positions(route, rank, off_row)
    n_sorted = 2 * n + MOE_E * MOE_TM
    n_tiles = (ends[-1:] // MOE_TM).astype(jnp.int32)
    tile_ids = jnp.arange(n_sorted // MOE_TM, dtype=jnp.int32)
    tile_expert = jnp.sum((tile_ids[:, None] >= (ends // MOE_TM)[None, :]).astype(jnp.int32), axis=1)
    tile_expert = jnp.minimum(tile_expert, MOE_E - 1)
    tok = _tile(n, MOE_TOK)
    pos = meta[:, :2].astype(jnp.int32).reshape(n // tok, tok, 2)
    pos = pos.transpose(0, 2, 1).reshape(n // tok, 1, 2 * tok)
    last_tile_start = jnp.maximum(ends - MOE_TM, 0).astype(jnp.int32)
    xs = moe_dispatch(last_tile_start, pos, h, n_sorted)
    ys = moe_experts(tile_expert, n_tiles, xs, w1.astype(F32), w3.astype(F32), w2.astype(F32), layer)
    return moe_combine(pos, meta, s, mod, 5, ys, nb)


def kernel(x, c, ctx, c_ctx, mod_w, mod_b, norm_g, final_g, na_w_qkv, na_w_o, na_rpb, ml_w_up, ml_conv_w, ml_conv_b, ml_w_q, ml_w_k, ml_w_v, ml_w_gate, ml_b_gate, ml_gn_w, ml_skip, ml_w_down, da_w_qkv, da_lambda, da_subln_g, da_w_o, moe_w_group, moe_b_group, moe_w_router, moe_b_router, moe_w1, moe_w3, moe_w2):
    nb, seq, d = x.shape
    assert (seq, d, ctx.shape[1]) == (SEQ, D_MODEL, CTX_LEN) and nb < 16
    depth = mod_w.shape[0]
    nx = nb * seq
    s = jnp.concatenate([x.reshape(nx, d), ctx.reshape(nb * CTX_LEN, d)], axis=0).astype(F32)
    cvec = jnp.concatenate([c, c_ctx[None, :], jnp.zeros((16 - nb - 1, d), c.dtype)], axis=0).astype(F32)
    mods = mod_vectors(cvec, mod_w, mod_b).reshape(depth, 16, 6, 1, d)

    for i in range(depth):
        kind, j = i % N_MIXERS, i // N_MIXERS
        mod = mods[i]
        h = norm_mod(s, norm_g[i, 0].astype(F32), mod, 0, 1, nb)
        if kind == 0:
            s = _na_layer(h, na_w_qkv[j], na_w_o[j], na_rpb[j], s, mod, nb)
        elif kind == 1:
            s = _ml_layer(h, ml_w_up[j], ml_conv_w[j], ml_conv_b[j], ml_w_q[j], ml_w_k[j], ml_w_v[j],
                          ml_w_gate[j], ml_b_gate[j], ml_gn_w[j], ml_skip[j], ml_w_down[j], s, mod, nb)
        else:
            lambda_init = 0.8 - 0.6 * math.exp(-0.3 * i)
            s = _da_layer(h, da_w_qkv[j], da_lambda[j], da_subln_g[j], da_w_o[j], lambda_init, s, mod, nb)
        s = _moe_layer(norm_g[i, 1], moe_w_group[i], moe_b_group[i], moe_w_router[i], moe_b_router[i],
                       moe_w1, moe_w3, moe_w2, i, s, mod, nb)
    return final_norm(s, final_g.astype(F32), nb)
```

```python
import functools
import math

import numpy as np
import jax
import jax.numpy as jnp
from jax import lax
from jax.experimental import pallas as pl
from jax.experimental.pallas import tpu as pltpu

F32 = jnp.float32
BF16 = jnp.bfloat16

D_MODEL = 1024
SEQ = 4096
CTX_LEN = 256
GRID_W = 64
N_MIXERS = 3
EPS = 1e-6

NA_HEADS = 16
NA_WIN_H = 8
NA_WIN_W = 16

ML_HEADS = 4
ML_INNER = 2 * D_MODEL
ML_HEAD_DIM = ML_INNER // ML_HEADS
ML_BLOCK = 4
ML_CONV_K = 5

DA_HEADS = 8
DA_HEAD_DIM = 64
ROPE_BASE = 10000.0

MOE_GROUPS = 4
MOE_EPG = 8
MOE_E = MOE_GROUPS * MOE_EPG
MOE_HIDDEN = 512
MOE_TM = 512
MOE_TOK = 512
SEL_LANE = 64

LANES = 128
CHUNK = 256
X_CHUNKS = SEQ // CHUNK
PROJ_TM = 1024
DA_TQ = 512
DA_VT_ROWS = 128 + 16
NA_ROWS = 8
NEG = -1e30
VMEM_LIMIT = 56 * 1024 * 1024


def _cp(sem, vmem=VMEM_LIMIT):
    return pltpu.CompilerParams(dimension_semantics=sem, vmem_limit_bytes=vmem)


def _tile(n, pref):
    tm = pref
    while n % tm:
        tm //= 2
    assert tm >= CHUNK
    return tm


def _mod_row(i, tm, nb):
    return jnp.where(i < nb * (SEQ // tm), i // (SEQ // tm), nb)


def _nt(a, b):
    return lax.dot_general(a, b, (((1,), (1,)), ((), ())), preferred_element_type=F32)


def _tn(a, b):
    return lax.dot_general(a, b, (((0,), (0,)), ((), ())), preferred_element_type=F32)


def _silu(x):
    return x * jax.nn.sigmoid(x)


def _head_rows(q, lane):
    zero = jnp.zeros_like(q)
    return jnp.concatenate([jnp.where(lane < 64, q, zero), jnp.where(lane >= 64, q, zero)], axis=0)


def _mod_kernel(c_ref, w_ref, b_ref, o_ref):
    a = _silu(c_ref[...]).astype(BF16)
    o_ref[0] = jnp.dot(a, w_ref[0].astype(BF16), preferred_element_type=F32) + b_ref[0]


def mod_vectors(cvec, mod_w, mod_b):
    depth, d, n6 = mod_w.shape
    tn = 1024
    return pl.pallas_call(
        _mod_kernel,
        grid=(depth, n6 // tn),
        in_specs=[pl.BlockSpec((16, d), lambda l, j: (0, 0)),
                  pl.BlockSpec((1, d, tn), lambda l, j: (l, 0, j)),
                  pl.BlockSpec((1, 1, tn), lambda l, j: (l, 0, j))],
        out_specs=pl.BlockSpec((1, 16, tn), lambda l, j: (l, 0, j)),
        out_shape=jax.ShapeDtypeStruct((depth, 16, n6), F32),
        compiler_params=_cp(("parallel", "parallel")),
        name="mod_vectors",
    )(cvec, mod_w, mod_b.reshape(depth, 1, n6))


def _norm_mod_kernel(x_ref, g_ref, sh_ref, sc_ref, o_ref):
    x = x_ref[...]
    y = x * lax.rsqrt(jnp.mean(x * x, axis=-1, keepdims=True) + EPS) * g_ref[...]
    o_ref[...] = (y * (1.0 + sc_ref[0, 0]) + sh_ref[0, 0]).astype(o_ref.dtype)


def norm_mod(s, g, mod, k_shift, k_scale, nb):
    n, d = s.shape
    tm = _tile(n, PROJ_TM)
    return pl.pallas_call(
        _norm_mod_kernel,
        grid=(n // tm,),
        in_specs=[pl.BlockSpec((tm, d), lambda i: (i, 0)),
                  pl.BlockSpec((1, d), lambda i: (0, 0)),
                  pl.BlockSpec((1, 1, 1, d), lambda i: (_mod_row(i, tm, nb), k_shift, 0, 0)),
                  pl.BlockSpec((1, 1, 1, d), lambda i: (_mod_row(i, tm, nb), k_scale, 0, 0))],
        out_specs=pl.BlockSpec((tm, d), lambda i: (i, 0)),
        out_shape=jax.ShapeDtypeStruct((n, d), BF16),
        compiler_params=_cp(("parallel",)),
        name="norm_mod",
    )(s, g.reshape(1, d), mod, mod)


def _final_norm_kernel(x_ref, g_ref, o_ref):
    x = x_ref[...]
    o_ref[0] = x * lax.rsqrt(jnp.mean(x * x, axis=-1, keepdims=True) + EPS) * g_ref[...]


def final_norm(s, g, nb):
    d = s.shape[1]
    tm = PROJ_TM
    per = SEQ // tm
    return pl.pallas_call(
        _final_norm_kernel,
        grid=(nb * per,),
        in_specs=[pl.BlockSpec((tm, d), lambda i: (i, 0)),
                  pl.BlockSpec((1, d), lambda i: (0, 0))],
        out_specs=pl.BlockSpec((1, tm, d), lambda i: (i // per, i % per, 0)),
        out_shape=jax.ShapeDtypeStruct((nb, SEQ, d), F32),
        compiler_params=_cp(("parallel",)),
        name="final_norm",
    )(s, g.reshape(1, d))


def _mm_kernel(a_ref, w_ref, o_ref):
    o_ref[...] = jnp.dot(a_ref[...], w_ref[...], preferred_element_type=F32).astype(o_ref.dtype)


def matmul(a, w, out_dtype, tn=1024):
    n, k = a.shape
    tm = _tile(n, PROJ_TM)
    nout = w.shape[1]
    tn = min(tn, nout)
    return pl.pallas_call(
        _mm_kernel,
        grid=(nout // tn, n // tm),
        in_specs=[pl.BlockSpec((tm, k), lambda j, i: (i, 0)),
                  pl.BlockSpec((k, tn), lambda j, i: (0, j))],
        out_specs=pl.BlockSpec((tm, tn), lambda j, i: (i, j)),
        out_shape=jax.ShapeDtypeStruct((n, nout), out_dtype),
        compiler_params=_cp(("parallel", "parallel")),
        name="matmul",
    )(a, w)


def _mm_res_kernel(a_ref, w_ref, r_ref, g_ref, o_ref):
    acc = jnp.dot(a_ref[...], w_ref[...], preferred_element_type=F32)
    o_ref[...] = r_ref[...] + g_ref[0, 0] * acc


def matmul_residual(a, w, s, mod, k_gate, nb):
    n, k = a.shape
    d = w.shape[1]
    tm = _tile(n, PROJ_TM)
    return pl.pallas_call(
        _mm_res_kernel,
        grid=(n // tm,),
        in_specs=[pl.BlockSpec((tm, k), lambda i: (i, 0)),
                  pl.BlockSpec((k, d), lambda i: (0, 0)),
                  pl.BlockSpec((tm, d), lambda i: (i, 0)),
                  pl.BlockSpec((1, 1, 1, d), lambda i: (_mod_row(i, tm, nb), k_gate, 0, 0))],
        out_specs=pl.BlockSpec((tm, d), lambda i: (i, 0)),
        out_shape=jax.ShapeDtypeStruct((n, d), F32),
        input_output_aliases={2: 0},
        compiler_params=_cp(("parallel",)),
        name="matmul_residual",
    )(a, w, s, mod)


def _na_bias_tables(rpb):
    o = np.arange(NA_WIN_H)[:, None]
    j = np.arange(NA_WIN_H)[None, :]
    dy = j - o + NA_WIN_H - 1
    qc = np.arange(GRID_W)[:, None]
    kc = np.arange(GRID_W)[None, :]
    dx = np.clip(kc - qc, -(NA_WIN_W - 1), NA_WIN_W - 1) + NA_WIN_W - 1
    w_start = np.clip(qc - NA_WIN_W // 2, 0, GRID_W - NA_WIN_W)
    valid = (kc >= w_start) & (kc < w_start + NA_WIN_W)
    tbl = rpb.astype(F32)[:, dy][:, :, :, dx]
    tbl = jnp.where(jnp.asarray(valid)[None, None, None], tbl, NEG)
    tbl = tbl.reshape(NA_HEADS // 2, 2, NA_WIN_H, NA_WIN_H, GRID_W, GRID_W)
    tbl = tbl.transpose(2, 0, 3, 5, 1, 4)
    return tbl.reshape(NA_WIN_H, NA_HEADS // 2, NA_WIN_H * GRID_W, 2 * GRID_W)


def _softmax_cols(s):
    m = jnp.max(s, axis=0, keepdims=True)
    p = jnp.exp(s - m)
    return (p * (1.0 / jnp.sum(p, axis=0, keepdims=True))).astype(BF16)


def _na_x_kernel(q_ref, kx_ref, vx_ref, kc_ref, vc_ref, bias_ref, o_ref, s_sc, p_sc, va_sc):
    t = pl.program_id(2)
    lane = lax.broadcasted_iota(jnp.int32, (1, LANES), 1)
    nwin = NA_WIN_H * GRID_W

    @pl.when(t == 0)
    def _():
        va_sc[0:SEQ, 0:LANES] = vx_ref[...]
        va_sc[SEQ:, 0:LANES] = vc_ref[...]
        va_sc[:, LANES:] = jnp.ones((SEQ + CTX_LEN, LANES), BF16)

    starts = []
    for i in range(NA_ROWS):
        r = t * NA_ROWS + i
        rs = jnp.clip(r - NA_WIN_H // 2, 0, SEQ // GRID_W - NA_WIN_H)
        start = pl.multiple_of(rs * GRID_W, GRID_W)
        starts.append(start)
        qq = _head_rows(q_ref[i * GRID_W:(i + 1) * GRID_W, :] * 0.125, lane)
        s_sc[i, 0:nwin, :] = _nt(kx_ref[pl.ds(start, nwin), :], qq) + bias_ref[r - rs, 0]
        s_sc[i, nwin:, :] = _nt(kc_ref[...], qq)
    for i in range(NA_ROWS):
        s = s_sc[i]
        p_sc[i] = jnp.exp(s - jnp.max(s, axis=0, keepdims=True)).astype(BF16)
    for i in range(NA_ROWS):
        oa = (_tn(p_sc[i, 0:nwin, :], va_sc[pl.ds(starts[i], nwin), :])
              + _tn(p_sc[i, nwin:, :], va_sc[SEQ:, :]))
        o = oa[:, 0:LANES] / oa[:, LANES:]
        o = jnp.where(lane < 64, o[0:GRID_W], o[GRID_W:])
        o_ref[i * GRID_W:(i + 1) * GRID_W, :] = o.astype(o_ref.dtype)


def _na_ctx_kernel(q_ref, kc_ref, vc_ref, o_in_hbm, o_ref):
    del o_in_hbm
    lane = lax.broadcasted_iota(jnp.int32, (1, LANES), 1)
    q = q_ref[...] * 0.125
    outs = []
    for hh in range(2):
        qm = jnp.where((lane // 64) == hh, q, jnp.zeros_like(q))
        p = _softmax_cols(_nt(kc_ref[...], qm))
        outs.append(_tn(p, vc_ref[...]))
    o_ref[...] = jnp.where(lane < 64, outs[0], outs[1]).astype(o_ref.dtype)


def na_attention(qkv, bias, nb):
    n = qkv.shape[0]
    d = D_MODEL
    ncb = d // LANES
    xc = nb * X_CHUNKS
    nkeys = NA_WIN_H * GRID_W + CTX_LEN
    tq = NA_ROWS * GRID_W
    steps = SEQ // tq
    o = pl.pallas_call(
        _na_x_kernel,
        grid=(nb, ncb, steps),
        in_specs=[pl.BlockSpec((tq, LANES), lambda b, h, t: (b * steps + t, h)),
                  pl.BlockSpec((SEQ, LANES), lambda b, h, t: (b, ncb + h)),
                  pl.BlockSpec((SEQ, LANES), lambda b, h, t: (b, 2 * ncb + h)),
                  pl.BlockSpec((CHUNK, LANES), lambda b, h, t: (xc + b, ncb + h)),
                  pl.BlockSpec((CHUNK, LANES), lambda b, h, t: (xc + b, 2 * ncb + h)),
                  pl.BlockSpec((NA_WIN_H, 1, NA_WIN_H * GRID_W, LANES), lambda b, h, t: (0, h, 0, 0))],
        out_specs=pl.BlockSpec((tq, LANES), lambda b, h, t: (b * steps + t, h)),
        out_shape=jax.ShapeDtypeStruct((n, d), BF16),
        scratch_shapes=[pltpu.VMEM((NA_ROWS, nkeys, LANES), F32),
                        pltpu.VMEM((NA_ROWS, nkeys, LANES), BF16),
                        pltpu.VMEM((SEQ + CTX_LEN, 2 * LANES), BF16)],
        compiler_params=_cp(("parallel", "parallel", "arbitrary")),
        name="na_attention_x",
    )(qkv, qkv, qkv, qkv, qkv, bias)
    return pl.pallas_call(
        _na_ctx_kernel,
        grid=(nb, ncb),
        in_specs=[pl.BlockSpec((CHUNK, LANES), lambda b, h: (xc + b, h)),
                  pl.BlockSpec((CHUNK, LANES), lambda b, h: (xc + b, ncb + h)),
                  pl.BlockSpec((CHUNK, LANES), lambda b, h: (xc + b, 2 * ncb + h)),
                  pl.BlockSpec(memory_space=pl.ANY)],
        out_specs=pl.BlockSpec((CHUNK, LANES), lambda b, h: (xc + b, h)),
        out_shape=jax.ShapeDtypeStruct((n, d), BF16),
        input_output_aliases={3: 0},
        compiler_params=_cp(("parallel", "parallel")),
        name="na_attention_ctx",
    )(qkv, qkv, qkv, o)


def _rope_tables(tm):
    half = DA_HEAD_DIM // 2
    freqs = ROPE_BASE ** (-np.arange(0, half, 2, dtype=np.float32) / half)
    t = np.arange(SEQ)
    row, col = t // GRID_W, t % GRID_W
    lane = np.arange(LANES)
    l64 = lane % DA_HEAD_DIM
    use_col = (l64 // half) == 1
    l32 = l64 % half
    fi = l32 % (half // 2)
    second = l32 >= half // 2
    pos = np.where(use_col[None, :], col[:, None], row[:, None]).astype(np.float32)
    ang = pos * freqs[fi][None, :]
    cos = np.cos(ang).astype(np.float32)
    sin = np.sin(ang).astype(np.float32)
    sa = np.where(second[None, :], 0.0, -sin).astype(np.float32)
    sb = np.where(second[None, :], sin, 0.0).astype(np.float32)
    ident = np.ones((tm, LANES), np.float32)
    zero = np.zeros((tm, LANES), np.float32)
    return (jnp.asarray(np.concatenate([cos, ident])),
            jnp.asarray(np.concatenate([sa, zero])),
            jnp.asarray(np.concatenate([sb, zero])))


def _mm_rope_kernel(a_ref, w_ref, c_ref, sa_ref, sb_ref, o_ref):
    j = pl.program_id(0)
    acc = jnp.dot(a_ref[...], w_ref[...], preferred_element_type=F32)

    @pl.when(j < 2)
    def _():
        scale = jnp.where(j == 0, DA_HEAD_DIM ** -0.5 * math.log2(math.e), 1.0)
        cos = c_ref[...] * scale
        sa = sa_ref[...] * scale
        sb = sb_ref[...] * scale
        for g in range(acc.shape[1] // LANES):
            sl = slice(g * LANES, (g + 1) * LANES)
            x = acc[:, sl]
            y = x * cos + pltpu.roll(x, LANES - 16, 1) * sa + pltpu.roll(x, 16, 1) * sb
            o_ref[:, sl] = y.astype(o_ref.dtype)

    @pl.when(j >= 2)
    def _():
        o_ref[...] = acc.astype(o_ref.dtype)


def matmul_rope(a, w, nb):
    n, k = a.shape
    d = D_MODEL
    tm = _tile(n, PROJ_TM)
    cos, sa, sb = _rope_tables(tm)
    xt = nb * (SEQ // tm)

    def tab(j, i):
        return (jnp.where(i < xt, i % (SEQ // tm), SEQ // tm), 0)

    tspec = pl.BlockSpec((tm, LANES), tab)
    return pl.pallas_call(
        _mm_rope_kernel,
        grid=(3, n // tm),
        in_specs=[pl.BlockSpec((tm, k), lambda j, i: (i, 0)),
                  pl.BlockSpec((k, d), lambda j, i: (0, j)),
                  tspec, tspec, tspec],
        out_specs=pl.BlockSpec((tm, d), lambda j, i: (i, j)),
        out_shape=jax.ShapeDtypeStruct((n, 3 * d), BF16),
        compiler_params=_cp(("parallel", "parallel")),
        name="matmul_rope",
    )(a, w, cos, sa, sb)


def _da_body(q_ref, lam_ref, g_ref, o_ref, s_sc, acc_sc, m_sc, lambda_init, run_chunks):
    tq = q_ref.shape[0]
    lane = lax.broadcasted_iota(jnp.int32, (1, LANES), 1)
    qq = _head_rows(q_ref[...], lane)
    m_sc[...] = jnp.full_like(m_sc, NEG)
    acc_sc[...] = jnp.zeros_like(acc_sc)

    def scores(slot, k):
        s_sc[slot] = _nt(k, qq)

    def fold(slot, vt):
        for g in range(2 * tq // CHUNK):
            sl = slice(g * CHUNK, (g + 1) * CHUNK)
            s = s_sc[slot, :, sl]
            m_old = m_sc[:, sl]
            m_new = jnp.maximum(m_old, jnp.max(s, axis=0, keepdims=True))
            alpha = jnp.exp2(m_old - m_new)
            p = jnp.exp2(s - m_new)
            m_sc[:, sl] = m_new
            acc_sc[:, sl] = acc_sc[:, sl] * alpha + jnp.dot(vt, p.astype(BF16), preferred_element_type=F32)

    run_chunks(scores, fold)

    ot = acc_sc[0:LANES, :] / acc_sc[LANES:LANES + 1, :]
    lam = lam_ref[...]
    lam_full = (jnp.exp(jnp.sum(lam[0:1] * lam[1:2], axis=1, keepdims=True))
                - jnp.exp(jnp.sum(lam[2:3] * lam[3:4], axis=1, keepdims=True)) + lambda_init)
    od = ot[:, :tq] - lam_full * ot[:, tq:]
    y = od * lax.rsqrt(jnp.mean(od * od, axis=0, keepdims=True) + EPS) * g_ref[...]
    o_ref[...] = (y * (1.0 - lambda_init)).T.astype(o_ref.dtype)


def _da_x_kernel(q_ref, kx_ref, kc_ref, vtx_ref, vtc_ref, lam_ref, g_ref, o_ref,
                 s_sc, acc_sc, m_sc, *, lambda_init):
    def kx(c):
        return kx_ref[pl.ds(pl.multiple_of(c * CHUNK, CHUNK), CHUNK), :]

    def run_chunks(scores, fold):
        scores(0, kx(0))

        def body(i, carry):
            scores(1, kx(2 * i + 1))
            fold(0, vtx_ref[0, 2 * i])
            scores(0, kx(2 * i + 2))
            fold(1, vtx_ref[0, 2 * i + 1])
            return carry

        lax.fori_loop(0, X_CHUNKS // 2 - 1, body, 0)
        scores(1, kx(X_CHUNKS - 1))
        fold(0, vtx_ref[0, X_CHUNKS - 2])
        scores(0, kc_ref[...])
        fold(1, vtx_ref[0, X_CHUNKS - 1])
        fold(0, vtc_ref[0, 0])

    _da_body(q_ref, lam_ref, g_ref, o_ref, s_sc, acc_sc, m_sc, lambda_init, run_chunks)


def _da_ctx_kernel(q_ref, kc_ref, vtc_ref, lam_ref, g_ref, o_in_hbm, o_ref,
                   s_sc, acc_sc, m_sc, *, lambda_init):
    del o_in_hbm

    def run_chunks(scores, fold):
        scores(0, kc_ref[...])
        fold(0, vtc_ref[0, 0])

    _da_body(q_ref, lam_ref, g_ref, o_ref, s_sc, acc_sc, m_sc, lambda_init, run_chunks)


def _da_scratch(tq):
    return [pltpu.VMEM((2, CHUNK, 2 * tq), F32), pltpu.VMEM((DA_VT_ROWS, 2 * tq), F32),
            pltpu.VMEM((1, 2 * tq), F32)]


def da_attention(qkv, lam, subln_g, lambda_init, nb):
    n = qkv.shape[0]
    d = D_MODEL
    ncb = d // LANES
    xc = nb * X_CHUNKS
    tq = DA_TQ
    vt = qkv[:, 2 * d:].reshape(n // CHUNK, CHUNK, ncb, LANES).transpose(2, 0, 3, 1)
    extra = jnp.zeros((ncb, n // CHUNK, DA_VT_ROWS - LANES, CHUNK), BF16).at[:, :, 0, :].set(1.0)
    vt = jnp.concatenate([vt, extra], axis=2)
    lam = lam.astype(F32)
    gcol = subln_g.astype(F32).reshape(LANES, 1)
    o = pl.pallas_call(
        functools.partial(_da_x_kernel, lambda_init=lambda_init),
        grid=(nb, ncb, SEQ // tq),
        in_specs=[pl.BlockSpec((tq, LANES), lambda b, h, t: (b * (SEQ // tq) + t, h)),
                  pl.BlockSpec((SEQ, LANES), lambda b, h, t: (b, ncb + h)),
                  pl.BlockSpec((CHUNK, LANES), lambda b, h, t: (xc + b, ncb + h)),
                  pl.BlockSpec((1, X_CHUNKS, DA_VT_ROWS, CHUNK), lambda b, h, t: (h, b, 0, 0)),
                  pl.BlockSpec((1, 1, DA_VT_ROWS, CHUNK), lambda b, h, t: (h, xc + b, 0, 0)),
                  pl.BlockSpec((4, DA_HEAD_DIM), lambda b, h, t: (0, 0)),
                  pl.BlockSpec((LANES, 1), lambda b, h, t: (0, 0))],
        out_specs=pl.BlockSpec((tq, LANES), lambda b, h, t: (b * (SEQ // tq) + t, h)),
        out_shape=jax.ShapeDtypeStruct((n, d), BF16),
        scratch_shapes=_da_scratch(tq),
        compiler_params=_cp(("parallel", "parallel", "arbitrary")),
        name="da_attention_x",
    )(qkv, qkv, qkv, vt, vt, lam, gcol)
    return pl.pallas_call(
        functools.partial(_da_ctx_kernel, lambda_init=lambda_init),
        grid=(nb, ncb),
        in_specs=[pl.BlockSpec((CHUNK, LANES), lambda b, h: (xc + b, h)),
                  pl.BlockSpec((CHUNK, LANES), lambda b, h: (xc + b, ncb + h)),
                  pl.BlockSpec((1, 1, DA_VT_ROWS, CHUNK), lambda b, h: (h, xc + b, 0, 0)),
                  pl.BlockSpec((4, DA_HEAD_DIM), lambda b, h: (0, 0)),
                  pl.BlockSpec((LANES, 1), lambda b, h: (0, 0)),
                  pl.BlockSpec(memory_space=pl.ANY)],
        out_specs=pl.BlockSpec((CHUNK, LANES), lambda b, h: (xc + b, h)),
        out_shape=jax.ShapeDtypeStruct((n, d), BF16),
        scratch_shapes=_da_scratch(CHUNK),
        input_output_aliases={5: 0},
        compiler_params=_cp(("parallel", "parallel")),
        name="da_attention_ctx",
    )(qkv, qkv, vt, lam, gcol, o)


def _block_diag(w):
    per = LANES // ML_BLOCK
    wr = w.reshape(ML_INNER // LANES, per, ML_BLOCK, ML_BLOCK)
    eye = jnp.eye(per, dtype=w.dtype)
    return jnp.einsum('cgio,gh->cgiho', wr, eye).reshape(ML_INNER // LANES, LANES, LANES).astype(BF16)


def _ml_conv_kernel(p_ref, c_ref, n_ref, cw_ref, cb_ref, wq_ref, wk_ref, wv_ref, wg_ref, bg_ref,
                    xc_ref, q_ref, k_ref, v_ref, g_ref, *, n_xchunks):
    i = pl.program_id(0)
    is_x = i < n_xchunks
    j = i % X_CHUNKS
    halo = 16
    cur = c_ref[...].astype(F32)
    prev = p_ref[CHUNK - halo:CHUNK, :].astype(F32)
    nxt = n_ref[0:halo, :].astype(F32)
    prev = jnp.where(jnp.logical_and(is_x, j > 0), prev, jnp.zeros_like(prev))
    nxt = jnp.where(jnp.logical_and(is_x, j < X_CHUNKS - 1), nxt, jnp.zeros_like(nxt))
    xp = jnp.concatenate([prev, cur, nxt], axis=0)
    rows = CHUNK + 2 * halo
    y = cb_ref[...] + cw_ref[ML_CONV_K // 2:ML_CONV_K // 2 + 1, :] * cur
    for tap in range(ML_CONV_K):
        dlt = tap - ML_CONV_K // 2
        if dlt == 0:
            continue
        shifted = pltpu.roll(xp, (-dlt) % rows, 0)[halo:halo + CHUNK]
        y = y + cw_ref[tap:tap + 1, :] * shifted
    xcb = _silu(y).astype(BF16)
    xc_ref[...] = xcb
    xm = c_ref[...]
    for s in range(xcb.shape[1] // LANES):
        sl = slice(s * LANES, (s + 1) * LANES)
        q_ref[:, sl] = jnp.dot(xcb[:, sl], wq_ref[s], preferred_element_type=F32).astype(BF16)
        k_ref[:, sl] = jnp.dot(xcb[:, sl], wk_ref[s], preferred_element_type=F32).astype(BF16)
        v_ref[:, sl] = jnp.dot(xm[:, sl], wv_ref[s], preferred_element_type=F32).astype(BF16)

    part = (jnp.dot(q_ref[...], wg_ref[0], preferred_element_type=F32)
            + jnp.dot(k_ref[...], wg_ref[1], preferred_element_type=F32)
            + jnp.dot(v_ref[...], wg_ref[2], preferred_element_type=F32))
    c = pl.program_id(1)

    @pl.when(c == 0)
    def _():
        g_ref[...] = bg_ref[...] + part

    @pl.when(c > 0)
    def _():
        g_ref[...] += part


def ml_conv_qkv(up, conv_w, conv_b, wq, wk, wv, wg, bg, nb):
    n = up.shape[0]
    nchunks = n // CHUNK
    cw = 1024
    ncb = ML_INNER // cw
    sub = cw // LANES
    blk = pl.BlockSpec((CHUNK, cw), lambda i, c: (i, c))
    wspec = pl.BlockSpec((sub, LANES, LANES), lambda i, c: (c, 0, 0))
    out = jax.ShapeDtypeStruct((n, ML_INNER), BF16)
    return pl.pallas_call(
        functools.partial(_ml_conv_kernel, n_xchunks=nb * X_CHUNKS),
        grid=(nchunks, ncb),
        in_specs=[pl.BlockSpec((CHUNK, cw), lambda i, c: (jnp.maximum(i - 1, 0), c)),
                  blk,
                  pl.BlockSpec((CHUNK, cw), lambda i, c: (jnp.minimum(i + 1, nchunks - 1), c)),
                  pl.BlockSpec((ML_CONV_K, cw), lambda i, c: (0, c)),
                  pl.BlockSpec((1, cw), lambda i, c: (0, c)),
                  wspec, wspec, wspec,
                  pl.BlockSpec((3, cw, LANES), lambda i, c: (0, c, 0)),
                  pl.BlockSpec((1, LANES), lambda i, c: (0, 0))],
        out_specs=[blk, blk, blk, blk, pl.BlockSpec((CHUNK, LANES), lambda i, c: (i, 0))],
        out_shape=[out, out, out, out, jax.ShapeDtypeStruct((n, LANES), F32)],
        compiler_params=_cp(("parallel", "arbitrary")),
        name="ml_conv_qkv",
    )(up, up, up, conv_w.astype(F32), conv_b.astype(F32).reshape(1, ML_INNER), wq, wk, wv, wg, bg)


def _log_sigmoid(x):
    return jnp.minimum(x, 0.0) - jnp.log(1.0 + jnp.exp(-jnp.abs(x)))


def _ml_scan_dir(d, h, p, q_ref, k_ref, v_ref, g_ref, gt_ref, o_ref, c_sc, n_sc, m_sc):
    ninf = -jnp.inf

    @pl.when(p == 0)
    def _():
        c_sc[...] = jnp.zeros_like(c_sc)
        n_sc[...] = jnp.zeros_like(n_sc)
        m_sc[...] = jnp.full_like(m_sc, ninf)

    L = CHUNK
    col_i = d * 2 * ML_HEADS + h
    col_f = col_i + ML_HEADS
    lane = lax.broadcasted_iota(jnp.int32, (1, LANES), 1)
    g = g_ref[...]
    i_col = jnp.sum(jnp.where(lane == col_i, g, 0.0), axis=1, keepdims=True)
    f_col = _log_sigmoid(jnp.sum(jnp.where(lane == col_f, g, 0.0), axis=1, keepdims=True))
    i_row = gt_ref[pl.ds(col_i, 1), :]
    f_row = _log_sigmoid(gt_ref[pl.ds(col_f, 1), :])

    jj = lax.broadcasted_iota(jnp.int32, (L, L), 0)
    ss = lax.broadcasted_iota(jnp.int32, (L, L), 1)
    valid = (ss <= jj) if d == 0 else (ss >= jj)
    valid_t = (jj <= ss) if d == 0 else (jj >= ss)
    bcum_col = jnp.sum(jnp.where(valid, f_row, 0.0), axis=1, keepdims=True)
    bcum_row = jnp.sum(jnp.where(valid_t, f_col, 0.0), axis=0, keepdims=True)
    b_last = jnp.sum(f_row, axis=1, keepdims=True)
    m_prev = m_sc[...]

    logd = jnp.where(valid, bcum_col - bcum_row + i_row, ninf)
    log_inter = bcum_col + m_prev
    m_j = jnp.maximum(log_inter, jnp.max(logd, axis=1, keepdims=True))
    dmat = jnp.exp(logd - m_j)
    inter = jnp.exp(log_inter - m_j)

    q = q_ref[...]
    k = k_ref[...]
    v = v_ref[...]
    scale = ML_HEAD_DIM ** -0.5
    sc = _nt(q, k) * scale * dmat
    c_prev = c_sc[...]
    n_prev = n_sc[...]
    qc = jnp.dot(q, c_prev.astype(BF16), preferred_element_type=F32) * scale
    num = inter * qc + jnp.dot(sc.astype(BF16), v, preferred_element_type=F32)
    qn = jnp.sum(q.astype(F32) * n_prev, axis=1, keepdims=True) * scale
    den = inter * qn + jnp.sum(sc, axis=1, keepdims=True)
    o_ref[...] = num / jnp.maximum(jnp.abs(den), jnp.exp(-m_j))

    ls = b_last - bcum_col + i_col
    m_new = jnp.maximum(b_last + m_prev, jnp.max(ls, axis=0, keepdims=True))
    w = jnp.exp(ls - m_new)
    decay = jnp.exp(b_last + m_prev - m_new)
    kw = k.astype(F32) * w
    c_sc[...] = decay * c_prev + _tn(kw.astype(BF16), v)
    n_sc[...] = decay * n_prev + jnp.sum(kw, axis=0, keepdims=True)
    m_sc[...] = m_new


def _ml_scan_kernel(qf_ref, kf_ref, vf_ref, gf_ref, gtf_ref, qb_ref, kb_ref, vb_ref, gb_ref, gtb_ref,
                    of_ref, ob_ref, c_sc, n_sc, m_sc):
    h = pl.program_id(1)
    p = pl.program_id(2)
    _ml_scan_dir(0, h, p, qf_ref, kf_ref, vf_ref, gf_ref, gtf_ref, of_ref, c_sc.at[0], n_sc.at[0], m_sc.at[0])
    _ml_scan_dir(1, h, p, qb_ref, kb_ref, vb_ref, gb_ref, gtb_ref, ob_ref, c_sc.at[1], n_sc.at[1], m_sc.at[1])


def ml_scan(q, k, v, g, gt, nb):
    n = q.shape[0]
    xc = nb * X_CHUNKS

    def cidx(b, d, p):
        xi = p - 1 if d == 0 else X_CHUNKS - p
        return jnp.where(p == 0, xc + b, b * X_CHUNKS + xi)

    def specs(d):
        blk = pl.BlockSpec((CHUNK, ML_HEAD_DIM), lambda b, h, p: (cidx(b, d, p), h))
        return [blk, blk, blk,
                pl.BlockSpec((CHUNK, LANES), lambda b, h, p: (cidx(b, d, p), 0)),
                pl.BlockSpec((16, CHUNK), lambda b, h, p: (0, cidx(b, d, p)))]

    out = jax.ShapeDtypeStruct((n, ML_INNER), F32)
    return pl.pallas_call(
        _ml_scan_kernel,
        grid=(nb, ML_HEADS, X_CHUNKS + 1),
        in_specs=specs(0) + specs(1),
        out_specs=[pl.BlockSpec((CHUNK, ML_HEAD_DIM), lambda b, h, p: (cidx(b, 0, p), h)),
                   pl.BlockSpec((CHUNK, ML_HEAD_DIM), lambda b, h, p: (cidx(b, 1, p), h))],
        out_shape=[out, out],
        scratch_shapes=[pltpu.VMEM((2, ML_HEAD_DIM, ML_HEAD_DIM), F32),
                        pltpu.VMEM((2, 1, ML_HEAD_DIM), F32),
                        pltpu.VMEM((2, 1, 1), F32)],
        compiler_params=_cp(("parallel", "parallel", "arbitrary")),
        name="ml_scan",
    )(q, k, v, g, gt, q, k, v, g, gt)


def _ml_finish_kernel(hf_ref, hb_ref, xc_ref, z_ref, gn_ref, sk_ref, o_ref):
    hh = hf_ref[...] + hb_ref[...]
    z = z_ref[...].astype(F32)
    gate = _silu(z)
    for hd in range(ML_HEADS):
        sl = slice(hd * ML_HEAD_DIM, (hd + 1) * ML_HEAD_DIM)
        seg = hh[:, sl]
        mu = jnp.mean(seg, axis=-1, keepdims=True)
        cen = seg - mu
        var = jnp.mean(cen * cen, axis=-1, keepdims=True)
        hn = cen * lax.rsqrt(var + EPS) * gn_ref[:, sl]
        a = (hn + sk_ref[:, sl] * xc_ref[:, sl].astype(F32)) * gate[:, sl]
        o_ref[:, sl] = a.astype(o_ref.dtype)


def ml_finish(hf, hb, xc, up, gn_w, skip):
    n = xc.shape[0]
    tm = CHUNK
    vec = pl.BlockSpec((1, ML_INNER), lambda i: (0, 0))
    return pl.pallas_call(
        _ml_finish_kernel,
        grid=(n // tm,),
        in_specs=[pl.BlockSpec((tm, ML_INNER), lambda i: (i, 0)),
                  pl.BlockSpec((tm, ML_INNER), lambda i: (i, 0)),
                  pl.BlockSpec((tm, ML_INNER), lambda i: (i, 0)),
                  pl.BlockSpec((tm, ML_INNER), lambda i: (i, 1)),
                  vec, vec],
        out_specs=pl.BlockSpec((tm, ML_INNER), lambda i: (i, 0)),
        out_shape=jax.ShapeDtypeStruct((n, ML_INNER), BF16),
        compiler_params=_cp(("parallel",)),
        name="ml_finish",
    )(hf, hb, xc, up, gn_w.astype(F32).reshape(1, ML_INNER), skip.astype(F32).reshape(1, ML_INNER))


def _router_kernel(x_ref, g_ref, sh_ref, sc_ref, w_ref, b_ref, h_ref, o_ref):
    x = x_ref[...]
    y = x * lax.rsqrt(jnp.mean(x * x, axis=-1, keepdims=True) + EPS) * g_ref[...]
    hf = y * (1.0 + sc_ref[0, 0]) + sh_ref[0, 0]
    h_hi = hf.astype(BF16)
    h_ref[...] = hf
    h_lo = (hf - h_hi.astype(F32)).astype(BF16)
    logits = (jnp.dot(h_hi, w_ref[0], preferred_element_type=F32)
              + jnp.dot(h_hi, w_ref[1], preferred_element_type=F32)
              + jnp.dot(h_lo, w_ref[0], preferred_element_type=F32)) + b_ref[...]
    lane = lax.broadcasted_iota(jnp.int32, (1, LANES), 1).astype(F32)
    big = 1e9
    ninf = -jnp.inf
    is_g = jnp.logical_and(lane >= MOE_E, lane < MOE_E + MOE_GROUPS)
    gl = jnp.where(is_g, logits, ninf)
    gmax = jnp.max(gl, axis=1, keepdims=True)
    g_val = 1.0 / jnp.sum(jnp.exp(gl - gmax), axis=1, keepdims=True)
    g_idx = jnp.min(jnp.where(gl == gmax, lane, big), axis=1, keepdims=True) - MOE_E
    lo = g_idx * MOE_EPG
    sel = jnp.logical_and(lane >= lo, lane < lo + MOE_EPG)
    el = jnp.where(sel, logits, ninf)
    e1 = jnp.max(el, axis=1, keepdims=True)
    esum = jnp.sum(jnp.exp(el - e1), axis=1, keepdims=True)
    i1 = jnp.min(jnp.where(el == e1, lane, big), axis=1, keepdims=True)
    el2 = jnp.where(lane == i1, ninf, el)
    e2 = jnp.max(el2, axis=1, keepdims=True)
    i2 = jnp.min(jnp.where(el2 == e2, lane, big), axis=1, keepdims=True)
    p1 = 1.0 / esum
    p2 = jnp.exp(e2 - e1) / esum
    w1 = g_val * p1 / (p1 + p2)
    w2 = g_val * p2 / (p1 + p2)
    gates = jnp.where(lane == i1, w1, 0.0) + jnp.where(lane == i2, w2, 0.0)
    marks = jnp.where(jnp.logical_or(lane == i1 + SEL_LANE, lane == i2 + SEL_LANE), 1.0, 0.0)
    o_ref[...] = gates + marks


def moe_router(s, g, mod, k_shift, k_scale, wr, br, nb):
    n, d = s.shape
    tm = _tile(n, PROJ_TM)
    return pl.pallas_call(
        _router_kernel,
        grid=(n // tm,),
        in_specs=[pl.BlockSpec((tm, d), lambda i: (i, 0)),
                  pl.BlockSpec((1, d), lambda i: (0, 0)),
                  pl.BlockSpec((1, 1, 1, d), lambda i: (_mod_row(i, tm, nb), k_shift, 0, 0)),
                  pl.BlockSpec((1, 1, 1, d), lambda i: (_mod_row(i, tm, nb), k_scale, 0, 0)),
                  pl.BlockSpec((2, d, LANES), lambda i: (0, 0, 0)),
                  pl.BlockSpec((1, LANES), lambda i: (0, 0))],
        out_specs=[pl.BlockSpec((tm, d), lambda i: (i, 0)),
                   pl.BlockSpec((tm, LANES), lambda i: (i, 0))],
        out_shape=[jax.ShapeDtypeStruct((n, d), F32),
                   jax.ShapeDtypeStruct((n, LANES), F32)],
        compiler_params=_cp(("parallel",)),
        name="moe_router",
    )(s, g.reshape(1, d), mod, mod, wr, br)


def _rank_kernel(route_ref, rank_ref, cnt_ref, carry_ref):
    i = pl.program_id(0)

    @pl.when(i == 0)
    def _():
        carry_ref[...] = jnp.zeros_like(carry_ref)

    tm = route_ref.shape[0]
    lane = lax.broadcasted_iota(jnp.int32, (1, LANES), 1)
    marks = jnp.where(lane >= SEL_LANE, route_ref[...], 0.0)
    rr = lax.broadcasted_iota(jnp.int32, (tm, tm), 0)
    cc = lax.broadcasted_iota(jnp.int32, (tm, tm), 1)
    below = jnp.where(cc < rr, 1.0, 0.0).astype(BF16)
    rank_ref[...] = jnp.dot(below, marks.astype(BF16), preferred_element_type=F32) + carry_ref[...]
    carry_ref[...] += jnp.sum(marks, axis=0, keepdims=True)
    cnt_ref[...] = carry_ref[...]


def moe_rank(route):
    n = route.shape[0]
    tm = _tile(n, PROJ_TM)
    return pl.pallas_call(
        _rank_kernel,
        grid=(n // tm,),
        in_specs=[pl.BlockSpec((tm, LANES), lambda i: (i, 0))],
        out_specs=[pl.BlockSpec((tm, LANES), lambda i: (i, 0)),
                   pl.BlockSpec((1, LANES), lambda i: (0, 0))],
        out_shape=[jax.ShapeDtypeStruct((n, LANES), F32),
                   jax.ShapeDtypeStruct((1, LANES), F32)],
        scratch_shapes=[pltpu.VMEM((1, LANES), F32)],
        compiler_params=_cp(("arbitrary",)),
        name="moe_rank",
    )(route)


def _pos_kernel(route_ref, rank_ref, off_ref, o_ref):
    lane = lax.broadcasted_iota(jnp.int32, (1, LANES), 1).astype(F32)
    route = route_ref[...]
    marked = jnp.logical_and(lane >= SEL_LANE, route > 0.5)
    p = rank_ref[...] + off_ref[...]
    lane_a = jnp.min(jnp.where(marked, lane, 1e9), axis=1, keepdims=True)
    lane_b = jnp.max(jnp.where(marked, lane, -1.0), axis=1, keepdims=True)

    def pick(src, at):
        return jnp.sum(jnp.where(lane == at, src, 0.0), axis=1, keepdims=True)

    out = jnp.where(lane == 0.0, pick(p, lane_a), 0.0)
    out = out + jnp.where(lane == 1.0, pick(p, lane_b), 0.0)
    out = out + jnp.where(lane == 2.0, pick(route, lane_a - SEL_LANE), 0.0)
    out = out + jnp.where(lane == 3.0, pick(route, lane_b - SEL_LANE), 0.0)
    o_ref[...] = out


def moe_positions(route, rank, off):
    n = route.shape[0]
    tm = _tile(n, PROJ_TM)
    blk = pl.BlockSpec((tm, LANES), lambda i: (i, 0))
    return pl.pallas_call(
        _pos_kernel,
        grid=(n // tm,),
        in_specs=[blk, blk, pl.BlockSpec((1, LANES), lambda i: (0, 0))],
        out_specs=blk,
        out_shape=jax.ShapeDtypeStruct((n, LANES), F32),
        compiler_params=_cp(("parallel",)),
        name="moe_positions",
    )(route, rank, off)


def _row_copy(src, src_row, dst, dst_row, sem):
    return pltpu.make_async_copy(src.at[pl.ds(src_row, 1)], dst.at[pl.ds(dst_row, 1)], sem)


def _dispatch_kernel(last_ref, pos_ref, h_ref, xs_hbm, zero_sc, sem):
    @pl.when(pl.program_id(0) == 0)
    def _():
        zero_sc[...] = jnp.zeros_like(zero_sc)

        def fill(e, carry):
            start = pl.multiple_of(last_ref[e], MOE_TM)
            pltpu.make_async_copy(zero_sc, xs_hbm.at[pl.ds(start, MOE_TM)], sem).start()
            return carry

        lax.fori_loop(0, MOE_E, fill, 0)

        def drain(e, carry):
            pltpu.make_async_copy(zero_sc, xs_hbm.at[pl.ds(0, MOE_TM)], sem).wait()
            return carry

        lax.fori_loop(0, MOE_E, drain, 0)

    tok = h_ref.shape[0]

    def issue(r, carry):
        _row_copy(h_ref, r, xs_hbm, pos_ref[0, 0, r], sem).start()
        _row_copy(h_ref, r, xs_hbm, pos_ref[0, 0, tok + r], sem).start()
        return carry

    lax.fori_loop(0, tok, issue, 0, unroll=8)
    for _ in range(2):
        pltpu.make_async_copy(h_ref, xs_hbm.at[pl.ds(0, tok)], sem).wait()


def moe_dispatch(last_tile_start, pos, h, n_sorted):
    n, d = h.shape
    tok = pos.shape[2] // 2
    return pl.pallas_call(
        _dispatch_kernel,
        grid_spec=pltpu.PrefetchScalarGridSpec(
            num_scalar_prefetch=1,
            grid=(n // tok,),
            in_specs=[pl.BlockSpec((1, 1, 2 * tok), lambda i, last: (i, 0, 0), memory_space=pltpu.SMEM),
                      pl.BlockSpec((tok, d), lambda i, last: (i, 0))],
            out_specs=pl.BlockSpec(memory_space=pl.ANY),
            scratch_shapes=[pltpu.VMEM((MOE_TM, d), h.dtype), pltpu.SemaphoreType.DMA]),
        out_shape=jax.ShapeDtypeStruct((n_sorted, d), h.dtype),
        compiler_params=_cp(("arbitrary",)),
        name="moe_dispatch",
    )(last_tile_start, pos, h)


def _experts_kernel(te_ref, nt_ref, x_ref, w1_ref, w3_ref, w2_ref, o_ref, w1_sc, w3_sc, w2_sc):
    i = pl.program_id(0)
    active = i < nt_ref[0]

    @pl.when(jnp.logical_and(active, jnp.logical_or(i == 0, te_ref[i] != te_ref[jnp.maximum(i - 1, 0)])))
    def _():
        w1_sc[...] = w1_ref[0, 0].astype(BF16)
        w3_sc[...] = w3_ref[0, 0].astype(BF16)
        w2_sc[...] = w2_ref[0, 0].astype(BF16)

    @pl.when(active)
    def _():
        x = x_ref[...].astype(BF16)
        a = jnp.dot(x, w1_sc[...], preferred_element_type=F32)
        b = jnp.dot(x, w3_sc[...], preferred_element_type=F32)
        he = (_silu(a) * b).astype(BF16)
        o_ref[...] = jnp.dot(he, w2_sc[...], preferred_element_type=F32)

    @pl.when(jnp.logical_not(active))
    def _():
        o_ref[...] = jnp.zeros_like(o_ref)


def moe_experts(tile_expert, n_tiles, xs, w1, w3, w2, layer):
    n_sorted, d = xs.shape
    return pl.pallas_call(
        _experts_kernel,
        grid_spec=pltpu.PrefetchScalarGridSpec(
            num_scalar_prefetch=2,
            grid=(n_sorted // MOE_TM,),
            in_specs=[pl.BlockSpec((MOE_TM, d), lambda i, te, nt: (jnp.minimum(i, nt[0] - 1), 0)),
                      pl.BlockSpec((1, 1, d, MOE_HIDDEN), lambda i, te, nt: (layer, te[i], 0, 0)),
                      pl.BlockSpec((1, 1, d, MOE_HIDDEN), lambda i, te, nt: (layer, te[i], 0, 0)),
                      pl.BlockSpec((1, 1, MOE_HIDDEN, d), lambda i, te, nt: (layer, te[i], 0, 0))],
            out_specs=pl.BlockSpec((MOE_TM, d), lambda i, te, nt: (i, 0)),
            scratch_shapes=[pltpu.VMEM((d, MOE_HIDDEN), BF16), pltpu.VMEM((d, MOE_HIDDEN), BF16),
                            pltpu.VMEM((MOE_HIDDEN, d), BF16)]),
        out_shape=jax.ShapeDtypeStruct((n_sorted, d), F32),
        compiler_params=_cp(("arbitrary",)),
        name="moe_experts",
    )(tile_expert, n_tiles, xs, w1, w3, w2)


def _combine_kernel(pos_ref, nxt_ref, meta_ref, s_ref, g_ref, ys_hbm, o_ref, buf_ref, sem):
    i = pl.program_id(0)
    slot = i % 2
    tok = s_ref.shape[0]

    def gather(p_ref, dst_slot):
        def issue(r, carry):
            _row_copy(ys_hbm, p_ref[0, 0, r], buf_ref.at[dst_slot, 0], r, sem.at[dst_slot]).start()
            _row_copy(ys_hbm, p_ref[0, 0, tok + r], buf_ref.at[dst_slot, 1], r, sem.at[dst_slot]).start()
            return carry

        lax.fori_loop(0, tok, issue, 0, unroll=8)

    @pl.when(i == 0)
    def _():
        gather(pos_ref, 0)

    @pl.when(i + 1 < pl.num_programs(0))
    def _():
        gather(nxt_ref, 1 - slot)

    for k in range(2):
        pltpu.make_async_copy(ys_hbm.at[pl.ds(0, tok)], buf_ref.at[slot, k], sem.at[slot]).wait()
    meta = meta_ref[...]
    y = meta[:, 2:3] * buf_ref[slot, 0] + meta[:, 3:4] * buf_ref[slot, 1]
    o_ref[...] = s_ref[...] + g_ref[0, 0] * y


def moe_combine(pos, meta, s, mod, k_gate, ys, nb):
    n, d = s.shape
    tm = pos.shape[2] // 2
    last = n // tm - 1
    return pl.pallas_call(
        _combine_kernel,
        grid=(n // tm,),
        in_specs=[pl.BlockSpec((1, 1, 2 * tm), lambda i: (i, 0, 0), memory_space=pltpu.SMEM),
                  pl.BlockSpec((1, 1, 2 * tm), lambda i: (jnp.minimum(i + 1, last), 0, 0),
                               memory_space=pltpu.SMEM),
                  pl.BlockSpec((tm, LANES), lambda i: (i, 0)),
                  pl.BlockSpec((tm, d), lambda i: (i, 0)),
                  pl.BlockSpec((1, 1, 1, d), lambda i: (_mod_row(i, tm, nb), k_gate, 0, 0)),
                  pl.BlockSpec(memory_space=pl.ANY)],
        out_specs=pl.BlockSpec((tm, d), lambda i: (i, 0)),
        out_shape=jax.ShapeDtypeStruct((n, d), F32),
        scratch_shapes=[pltpu.VMEM((2, 2, tm, d), F32), pltpu.SemaphoreType.DMA((2,))],
        input_output_aliases={3: 0},
        compiler_params=_cp(("arbitrary",)),
        name="moe_combine",
    )(pos, pos, meta, s, mod, ys)


def _na_layer(h, w_qkv, w_o, rpb, s, mod, nb):
    qkv = matmul(h, w_qkv.astype(BF16), BF16)
    o = na_attention(qkv, _na_bias_tables(rpb), nb)
    return matmul_residual(o, w_o.astype(BF16), s, mod, 2, nb)


def _da_layer(h, w_qkv, lam, subln_g, w_o, lambda_init, s, mod, nb):
    qkv = matmul_rope(h, w_qkv.astype(BF16), nb)
    o = da_attention(qkv, lam, subln_g, lambda_init, nb)
    return matmul_residual(o, w_o.astype(BF16), s, mod, 2, nb)


def _ml_layer(h, w_up, conv_w, conv_b, w_q, w_k, w_v, w_gate, b_gate, gn_w, skip, w_down, s, mod, nb):
    up = matmul(h, w_up.astype(BF16), BF16)
    ng = w_gate.shape[1]
    wg = jnp.pad(w_gate, ((0, 0), (0, LANES - ng))).reshape(3, ML_INNER, LANES).astype(BF16)
    bg = jnp.pad(b_gate.astype(F32), (0, LANES - ng)).reshape(1, LANES)
    xc, q, k, v, g = ml_conv_qkv(up, conv_w, conv_b, _block_diag(w_q), _block_diag(w_k), _block_diag(w_v),
                                 wg, bg, nb)
    gt = g[:, :ng].T
    hf, hb = ml_scan(q, k, v, g, gt, nb)
    a = ml_finish(hf, hb, xc, up, gn_w, skip)
    return matmul_residual(a, w_down.astype(BF16), s, mod, 2, nb)


def _moe_layer(norm_g, w_group, b_group, w_router, b_router, w1, w3, w2, layer, s, mod, nb):
    d = s.shape[1]
    pad = LANES - MOE_E - MOE_GROUPS
    wr = jnp.concatenate([w_router, w_group, jnp.zeros((d, pad), w_router.dtype)], axis=1).astype(F32)
    wr_hi = wr.astype(BF16)
    wr_lo = (wr - wr_hi.astype(F32)).astype(BF16)
    br = jnp.concatenate([b_router, b_group, jnp.zeros((pad,), b_router.dtype)]).astype(F32).reshape(1, LANES)
    h, route = moe_router(s, norm_g.astype(F32), mod, 3, 4, jnp.stack([wr_hi, wr_lo]), br, nb)
    rank, cnt = moe_rank(route)
    n = s.shape[0]
    cnt_e = cnt[0, SEL_LANE:SEL_LANE + MOE_E].astype(jnp.int32)
    gsz = ((cnt_e + MOE_TM - 1) // MOE_TM) * MOE_TM
    ends = jnp.cumsum(gsz)
    off_row = jnp.zeros((1, LANES), F32).at[0, SEL_LANE:SEL_LANE + MOE_E].set((ends - gsz).astype(F32))
    meta = moe_positions(route, rank, off_row)
    n_sorted = 2 * n + MOE_E * MOE_TM
    n_tiles = (ends[-1:] // MOE_TM).astype(jnp.int32)
    tile_ids = jnp.arange(n_sorted // MOE_TM, dtype=jnp.int32)
    tile_expert = jnp.sum((tile_ids[:, None] >= (ends // MOE_TM)[None, :]).astype(jnp.int32), axis=1)
    tile_expert = jnp.minimum(tile_expert, MOE_E - 1)
    tok = _tile(n, MOE_TOK)
    pos = meta[:, :2].astype(jnp.int32).reshape(n // tok, tok, 2)
    pos = pos.transpose(0, 2, 1).reshape(n // tok, 1, 2 * tok)
    last_tile_start = jnp.maximum(ends - MOE_TM, 0).astype(jnp.int32)
    xs = moe_dispatch(last_tile_start, pos, h, n_sorted)
    ys = moe_experts(tile_expert, n_tiles, xs, w1.astype(F32), w3.astype(F32), w2.astype(F32), layer)
    return moe_combine(pos, meta, s, mod, 5, ys, nb)


def kernel(x, c, ctx, c_ctx, mod_w, mod_b, norm_g, final_g, na_w_qkv, na_w_o, na_rpb, ml_w_up, ml_conv_w, ml_conv_b, ml_w_q, ml_w_k, ml_w_v, ml_w_gate, ml_b_gate, ml_gn_w, ml_skip, ml_w_down, da_w_qkv, da_lambda, da_subln_g, da_w_o, moe_w_group, moe_b_group, moe_w_router, moe_b_router, moe_w1, moe_w3, moe_w2):
    nb, seq, d = x.shape
    assert (seq, d, ctx.shape[1]) == (SEQ, D_MODEL, CTX_LEN) and nb < 16
    depth = mod_w.shape[0]
    nx = nb * seq
    s = jnp.concatenate([x.reshape(nx, d), ctx.reshape(nb * CTX_LEN, d)], axis=0).astype(F32)
    cvec = jnp.concatenate([c, c_ctx[None, :], jnp.zeros((16 - nb - 1, d), c.dtype)], axis=0).astype(F32)
    mods = mod_vectors(cvec, mod_w, mod_b).reshape(depth, 16, 6, 1, d)

    for i in range(depth):
        kind, j = i % N_MIXERS, i // N_MIXERS
        mod = mods[i]
        h = norm_mod(s, norm_g[i, 0].astype(F32), mod, 0, 1, nb)
        if kind == 0:
            s = _na_layer(h, na_w_qkv[j], na_w_o[j], na_rpb[j], s, mod, nb)
        elif kind == 1:
            s = _ml_layer(h, ml_w_up[j], ml_conv_w[j], ml_conv_b[j], ml_w_q[j], ml_w_k[j], ml_w_v[j],
                          ml_w_gate[j], ml_b_gate[j], ml_gn_w[j], ml_skip[j], ml_w_down[j], s, mod, nb)
        else:
            lambda_init = 0.8 - 0.6 * math.exp(-0.3 * i)
            s = _da_layer(h, da_w_qkv[j], da_lambda[j], da_subln_g[j], da_w_o[j], lambda_init, s, mod, nb)
        s = _moe_layer(norm_g[i, 1], moe_w_group[i], moe_b_group[i], moe_w_router[i], moe_b_router[i],
                       moe_w1, moe_w3, moe_w2, i, s, mod, nb)
    return final_norm(s, final_g.astype(F32), nb)
```

```python
import functools
import math

import numpy as np
import jax
import jax.numpy as jnp
from jax import lax
from jax.experimental import pallas as pl
from jax.experimental.pallas import tpu as pltpu

F32 = jnp.float32
BF16 = jnp.bfloat16

D_MODEL = 1024
SEQ = 4096
CTX_LEN = 256
GRID_W = 64
N_MIXERS = 3
EPS = 1e-6

NA_HEADS = 16
NA_WIN_H = 8
NA_WIN_W = 16

ML_HEADS = 4
ML_INNER = 2 * D_MODEL
ML_HEAD_DIM = ML_INNER // ML_HEADS
ML_BLOCK = 4
ML_CONV_K = 5

DA_HEADS = 8
DA_HEAD_DIM = 64
ROPE_BASE = 10000.0

MOE_GROUPS = 4
MOE_EPG = 8
MOE_E = MOE_GROUPS * MOE_EPG
MOE_HIDDEN = 512
MOE_TM = 512
MOE_TOK = 1024
SEL_LANE = 64

LANES = 128
CHUNK = 256
X_CHUNKS = SEQ // CHUNK
PROJ_TM = 1024
DA_TQ = 512
DA_VT_ROWS = 128 + 16
NA_ROWS = 16
NEG = -1e30
VMEM_LIMIT = 56 * 1024 * 1024


def _cp(sem, vmem=VMEM_LIMIT):
    return pltpu.CompilerParams(dimension_semantics=sem, vmem_limit_bytes=vmem)


def _tile(n, pref):
    tm = pref
    while n % tm:
        tm //= 2
    assert tm >= CHUNK
    return tm


def _mod_row(i, tm, nb):
    return jnp.where(i < nb * (SEQ // tm), i // (SEQ // tm), nb)


def _nt(a, b):
    return lax.dot_general(a, b, (((1,), (1,)), ((), ())), preferred_element_type=F32)


def _tn(a, b):
    return lax.dot_general(a, b, (((0,), (0,)), ((), ())), preferred_element_type=F32)


def _silu(x):
    return x * jax.nn.sigmoid(x)


def _head_rows(q, lane):
    zero = jnp.zeros_like(q)
    return jnp.concatenate([jnp.where(lane < 64, q, zero), jnp.where(lane >= 64, q, zero)], axis=0)


def _mod_kernel(c_ref, w_ref, b_ref, o_ref):
    a = _silu(c_ref[...]).astype(BF16)
    o_ref[0] = jnp.dot(a, w_ref[0].astype(BF16), preferred_element_type=F32) + b_ref[0]


def mod_vectors(cvec, mod_w, mod_b):
    depth, d, n6 = mod_w.shape
    tn = 1024
    return pl.pallas_call(
        _mod_kernel,
        grid=(depth, n6 // tn),
        in_specs=[pl.BlockSpec((16, d), lambda l, j: (0, 0)),
                  pl.BlockSpec((1, d, tn), lambda l, j: (l, 0, j)),
                  pl.BlockSpec((1, 1, tn), lambda l, j: (l, 0, j))],
        out_specs=pl.BlockSpec((1, 16, tn), lambda l, j: (l, 0, j)),
        out_shape=jax.ShapeDtypeStruct((depth, 16, n6), F32),
        compiler_params=_cp(("parallel", "parallel")),
        name="mod_vectors",
    )(cvec, mod_w, mod_b.reshape(depth, 1, n6))


def _norm_mod_kernel(x_ref, g_ref, sh_ref, sc_ref, o_ref):
    x = x_ref[...]
    y = x * lax.rsqrt(jnp.mean(x * x, axis=-1, keepdims=True) + EPS) * g_ref[...]
    o_ref[...] = (y * (1.0 + sc_ref[0, 0]) + sh_ref[0, 0]).astype(o_ref.dtype)


def norm_mod(s, g, mod, k_shift, k_scale, nb):
    n, d = s.shape
    tm = _tile(n, PROJ_TM)
    return pl.pallas_call(
        _norm_mod_kernel,
        grid=(n // tm,),
        in_specs=[pl.BlockSpec((tm, d), lambda i: (i, 0)),
                  pl.BlockSpec((1, d), lambda i: (0, 0)),
                  pl.BlockSpec((1, 1, 1, d), lambda i: (_mod_row(i, tm, nb), k_shift, 0, 0)),
                  pl.BlockSpec((1, 1, 1, d), lambda i: (_mod_row(i, tm, nb), k_scale, 0, 0))],
        out_specs=pl.BlockSpec((tm, d), lambda i: (i, 0)),
        out_shape=jax.ShapeDtypeStruct((n, d), BF16),
        compiler_params=_cp(("parallel",)),
        name="norm_mod",
    )(s, g.reshape(1, d), mod, mod)


def _final_norm_kernel(x_ref, g_ref, o_ref):
    x = x_ref[...]
    o_ref[0] = x * lax.rsqrt(jnp.mean(x * x, axis=-1, keepdims=True) + EPS) * g_ref[...]


def final_norm(s, g, nb):
    d = s.shape[1]
    tm = PROJ_TM
    per = SEQ // tm
    return pl.pallas_call(
        _final_norm_kernel,
        grid=(nb * per,),
        in_specs=[pl.BlockSpec((tm, d), lambda i: (i, 0)),
                  pl.BlockSpec((1, d), lambda i: (0, 0))],
        out_specs=pl.BlockSpec((1, tm, d), lambda i: (i // per, i % per, 0)),
        out_shape=jax.ShapeDtypeStruct((nb, SEQ, d), F32),
        compiler_params=_cp(("parallel",)),
        name="final_norm",
    )(s, g.reshape(1, d))


def _mm_kernel(a_ref, w_ref, o_ref):
    o_ref[...] = jnp.dot(a_ref[...], w_ref[...], preferred_element_type=F32).astype(o_ref.dtype)


def matmul(a, w, out_dtype, tn=1024):
    n, k = a.shape
    tm = _tile(n, PROJ_TM)
    nout = w.shape[1]
    tn = min(tn, nout)
    return pl.pallas_call(
        _mm_kernel,
        grid=(nout // tn, n // tm),
        in_specs=[pl.BlockSpec((tm, k), lambda j, i: (i, 0)),
                  pl.BlockSpec((k, tn), lambda j, i: (0, j))],
        out_specs=pl.BlockSpec((tm, tn), lambda j, i: (i, j)),
        out_shape=jax.ShapeDtypeStruct((n, nout), out_dtype),
        compiler_params=_cp(("parallel", "parallel")),
        name="matmul",
    )(a, w)


def _mm_res_kernel(a_ref, w_ref, r_ref, g_ref, o_ref):
    acc = jnp.dot(a_ref[...], w_ref[...], preferred_element_type=F32)
    o_ref[...] = r_ref[...] + g_ref[0, 0] * acc


def matmul_residual(a, w, s, mod, k_gate, nb):
    n, k = a.shape
    d = w.shape[1]
    tm = _tile(n, PROJ_TM)
    return pl.pallas_call(
        _mm_res_kernel,
        grid=(n // tm,),
        in_specs=[pl.BlockSpec((tm, k), lambda i: (i, 0)),
                  pl.BlockSpec((k, d), lambda i: (0, 0)),
                  pl.BlockSpec((tm, d), lambda i: (i, 0)),
                  pl.BlockSpec((1, 1, 1, d), lambda i: (_mod_row(i, tm, nb), k_gate, 0, 0))],
        out_specs=pl.BlockSpec((tm, d), lambda i: (i, 0)),
        out_shape=jax.ShapeDtypeStruct((n, d), F32),
        input_output_aliases={2: 0},
        compiler_params=_cp(("parallel",)),
        name="matmul_residual",
    )(a, w, s, mod)


def _na_bias_tables(rpb):
    o = np.arange(NA_WIN_H)[:, None]
    j = np.arange(NA_WIN_H)[None, :]
    dy = j - o + NA_WIN_H - 1
    qc = np.arange(GRID_W)[:, None]
    kc = np.arange(GRID_W)[None, :]
    dx = np.clip(kc - qc, -(NA_WIN_W - 1), NA_WIN_W - 1) + NA_WIN_W - 1
    w_start = np.clip(qc - NA_WIN_W // 2, 0, GRID_W - NA_WIN_W)
    valid = (kc >= w_start) & (kc < w_start + NA_WIN_W)
    tbl = rpb.astype(F32)[:, dy][:, :, :, dx]
    tbl = jnp.where(jnp.asarray(valid)[None, None, None], tbl, NEG)
    tbl = tbl.reshape(NA_HEADS // 2, 2, NA_WIN_H, NA_WIN_H, GRID_W, GRID_W)
    tbl = tbl.transpose(2, 0, 3, 5, 1, 4)
    return tbl.reshape(NA_WIN_H, NA_HEADS // 2, NA_WIN_H * GRID_W, 2 * GRID_W)


def _softmax_cols(s):
    m = jnp.max(s, axis=0, keepdims=True)
    p = jnp.exp(s - m)
    return (p * (1.0 / jnp.sum(p, axis=0, keepdims=True))).astype(BF16)


def _na_x_kernel(q_ref, kx_ref, vx_ref, kc_ref, vc_ref, bias_ref, o_ref, s_sc, p_sc, va_sc):
    t = pl.program_id(2)
    lane = lax.broadcasted_iota(jnp.int32, (1, LANES), 1)
    nwin = NA_WIN_H * GRID_W

    @pl.when(t == 0)
    def _():
        va_sc[0:SEQ, 0:LANES] = vx_ref[...]
        va_sc[SEQ:, 0:LANES] = vc_ref[...]
        va_sc[:, LANES:] = jnp.ones((SEQ + CTX_LEN, LANES), BF16)

    starts = []
    for i in range(NA_ROWS):
        r = t * NA_ROWS + i
        rs = jnp.clip(r - NA_WIN_H // 2, 0, SEQ // GRID_W - NA_WIN_H)
        start = pl.multiple_of(rs * GRID_W, GRID_W)
        starts.append(start)
        qq = _head_rows(q_ref[i * GRID_W:(i + 1) * GRID_W, :] * 0.125, lane)
        s_sc[i, 0:nwin, :] = _nt(kx_ref[pl.ds(start, nwin), :], qq) + bias_ref[r - rs, 0]
        s_sc[i, nwin:, :] = _nt(kc_ref[...], qq)
    for i in range(NA_ROWS):
        s = s_sc[i]
        p_sc[i] = jnp.exp(s - jnp.max(s, axis=0, keepdims=True)).astype(BF16)
    for i in range(NA_ROWS):
        oa = (_tn(p_sc[i, 0:nwin, :], va_sc[pl.ds(starts[i], nwin), :])
              + _tn(p_sc[i, nwin:, :], va_sc[SEQ:, :]))
        o = oa[:, 0:LANES] / oa[:, LANES:]
        o = jnp.where(lane < 64, o[0:GRID_W], o[GRID_W:])
        o_ref[i * GRID_W:(i + 1) * GRID_W, :] = o.astype(o_ref.dtype)


def _na_ctx_kernel(q_ref, kc_ref, vc_ref, o_in_hbm, o_ref):
    del o_in_hbm
    lane = lax.broadcasted_iota(jnp.int32, (1, LANES), 1)
    q = q_ref[...] * 0.125
    outs = []
    for hh in range(2):
        qm = jnp.where((lane // 64) == hh, q, jnp.zeros_like(q))
        p = _softmax_cols(_nt(kc_ref[...], qm))
        outs.append(_tn(p, vc_ref[...]))
    o_ref[...] = jnp.where(lane < 64, outs[0], outs[1]).astype(o_ref.dtype)


def na_attention(qkv, bias, nb):
    n = qkv.shape[0]
    d = D_MODEL
    ncb = d // LANES
    xc = nb * X_CHUNKS
    nkeys = NA_WIN_H * GRID_W + CTX_LEN
    tq = NA_ROWS * GRID_W
    steps = SEQ // tq
    o = pl.pallas_call(
        _na_x_kernel,
        grid=(nb, ncb, steps),
        in_specs=[pl.BlockSpec((tq, LANES), lambda b, h, t: (b * steps + t, h)),
                  pl.BlockSpec((SEQ, LANES), lambda b, h, t: (b, ncb + h)),
                  pl.BlockSpec((SEQ, LANES), lambda b, h, t: (b, 2 * ncb + h)),
                  pl.BlockSpec((CHUNK, LANES), lambda b, h, t: (xc + b, ncb + h)),
                  pl.BlockSpec((CHUNK, LANES), lambda b, h, t: (xc + b, 2 * ncb + h)),
                  pl.BlockSpec((NA_WIN_H, 1, NA_WIN_H * GRID_W, LANES), lambda b, h, t: (0, h, 0, 0))],
        out_specs=pl.BlockSpec((tq, LANES), lambda b, h, t: (b * steps + t, h)),
        out_shape=jax.ShapeDtypeStruct((n, d), BF16),
        scratch_shapes=[pltpu.VMEM((NA_ROWS, nkeys, LANES), F32),
                        pltpu.VMEM((NA_ROWS, nkeys, LANES), BF16),
                        pltpu.VMEM((SEQ + CTX_LEN, 2 * LANES), BF16)],
        compiler_params=_cp(("parallel", "parallel", "arbitrary")),
        name="na_attention_x",
    )(qkv, qkv, qkv, qkv, qkv, bias)
    return pl.pallas_call(
        _na_ctx_kernel,
        grid=(nb, ncb),
        in_specs=[pl.BlockSpec((CHUNK, LANES), lambda b, h: (xc + b, h)),
                  pl.BlockSpec((CHUNK, LANES), lambda b, h: (xc + b, ncb + h)),
                  pl.BlockSpec((CHUNK, LANES), lambda b, h: (xc + b, 2 * ncb + h)),
                  pl.BlockSpec(memory_space=pl.ANY)],
        out_specs=pl.BlockSpec((CHUNK, LANES), lambda b, h: (xc + b, h)),
        out_shape=jax.ShapeDtypeStruct((n, d), BF16),
        input_output_aliases={3: 0},
        compiler_params=_cp(("parallel", "parallel")),
        name="na_attention_ctx",
    )(qkv, qkv, qkv, o)


def _rope_tables(tm):
    half = DA_HEAD_DIM // 2
    freqs = ROPE_BASE ** (-np.arange(0, half, 2, dtype=np.float32) / half)
    t = np.arange(SEQ)
    row, col = t // GRID_W, t % GRID_W
    lane = np.arange(LANES)
    l64 = lane % DA_HEAD_DIM
    use_col = (l64 // half) == 1
    l32 = l64 % half
    fi = l32 % (half // 2)
    second = l32 >= half // 2
    pos = np.where(use_col[None, :], col[:, None], row[:, None]).astype(np.float32)
    ang = pos * freqs[fi][None, :]
    cos = np.cos(ang).astype(np.float32)
    sin = np.sin(ang).astype(np.float32)
    sa = np.where(second[None, :], 0.0, -sin).astype(np.float32)
    sb = np.where(second[None, :], sin, 0.0).astype(np.float32)
    ident = np.ones((tm, LANES), np.float32)
    zero = np.zeros((tm, LANES), np.float32)
    return (jnp.asarray(np.concatenate([cos, ident])),
            jnp.asarray(np.concatenate([sa, zero])),
            jnp.asarray(np.concatenate([sb, zero])))


def _mm_rope_kernel(a_ref, w_ref, c_ref, sa_ref, sb_ref, o_ref):
    j = pl.program_id(0)
    acc = jnp.dot(a_ref[...], w_ref[...], preferred_element_type=F32)

    @pl.when(j < 2)
    def _():
        scale = jnp.where(j == 0, DA_HEAD_DIM ** -0.5 * math.log2(math.e), 1.0)
        cos = c_ref[...] * scale
        sa = sa_ref[...] * scale
        sb = sb_ref[...] * scale
        for g in range(acc.shape[1] // LANES):
            sl = slice(g * LANES, (g + 1) * LANES)
            x = acc[:, sl]
            y = x * cos + pltpu.roll(x, LANES - 16, 1) * sa + pltpu.roll(x, 16, 1) * sb
            o_ref[:, sl] = y.astype(o_ref.dtype)

    @pl.when(j >= 2)
    def _():
        o_ref[...] = acc.astype(o_ref.dtype)


def matmul_rope(a, w, nb):
    n, k = a.shape
    d = D_MODEL
    tm = _tile(n, PROJ_TM)
    cos, sa, sb = _rope_tables(tm)
    xt = nb * (SEQ // tm)

    def tab(j, i):
        return (jnp.where(i < xt, i % (SEQ // tm), SEQ // tm), 0)

    tspec = pl.BlockSpec((tm, LANES), tab)
    return pl.pallas_call(
        _mm_rope_kernel,
        grid=(3, n // tm),
        in_specs=[pl.BlockSpec((tm, k), lambda j, i: (i, 0)),
                  pl.BlockSpec((k, d), lambda j, i: (0, j)),
                  tspec, tspec, tspec],
        out_specs=pl.BlockSpec((tm, d), lambda j, i: (i, j)),
        out_shape=jax.ShapeDtypeStruct((n, 3 * d), BF16),
        compiler_params=_cp(("parallel", "parallel")),
        name="matmul_rope",
    )(a, w, cos, sa, sb)


def _da_body(q_ref, lam_ref, g_ref, o_ref, s_sc, acc_sc, m_sc, lambda_init, run_chunks):
    tq = q_ref.shape[0]
    lane = lax.broadcasted_iota(jnp.int32, (1, LANES), 1)
    qq = _head_rows(q_ref[...], lane)
    m_sc[...] = jnp.full_like(m_sc, NEG)
    acc_sc[...] = jnp.zeros_like(acc_sc)

    def scores(slot, k):
        s_sc[slot] = _nt(k, qq)

    def fold(slot, vt):
        for g in range(2 * tq // CHUNK):
            sl = slice(g * CHUNK, (g + 1) * CHUNK)
            s = s_sc[slot, :, sl]
            m_old = m_sc[:, sl]
            m_new = jnp.maximum(m_old, jnp.max(s, axis=0, keepdims=True))
            alpha = jnp.exp2(m_old - m_new)
            p = jnp.exp2(s - m_new)
            m_sc[:, sl] = m_new
            acc_sc[:, sl] = acc_sc[:, sl] * alpha + jnp.dot(vt, p.astype(BF16), preferred_element_type=F32)

    run_chunks(scores, fold)

    ot = acc_sc[0:LANES, :] / acc_sc[LANES:LANES + 1, :]
    lam = lam_ref[...]
    lam_full = (jnp.exp(jnp.sum(lam[0:1] * lam[1:2], axis=1, keepdims=True))
                - jnp.exp(jnp.sum(lam[2:3] * lam[3:4], axis=1, keepdims=True)) + lambda_init)
    od = ot[:, :tq] - lam_full * ot[:, tq:]
    y = od * lax.rsqrt(jnp.mean(od * od, axis=0, keepdims=True) + EPS) * g_ref[...]
    o_ref[...] = (y * (1.0 - lambda_init)).T.astype(o_ref.dtype)


def _da_x_kernel(q_ref, kx_ref, kc_ref, vtx_ref, vtc_ref, lam_ref, g_ref, o_ref,
                 s_sc, acc_sc, m_sc, *, lambda_init):
    def kx(c):
        return kx_ref[pl.ds(pl.multiple_of(c * CHUNK, CHUNK), CHUNK), :]

    def run_chunks(scores, fold):
        scores(0, kx(0))

        def body(i, carry):
            scores(1, kx(2 * i + 1))
            fold(0, vtx_ref[0, 2 * i])
            scores(0, kx(2 * i + 2))
            fold(1, vtx_ref[0, 2 * i + 1])
            return carry

        lax.fori_loop(0, X_CHUNKS // 2 - 1, body, 0)
        scores(1, kx(X_CHUNKS - 1))
        fold(0, vtx_ref[0, X_CHUNKS - 2])
        scores(0, kc_ref[...])
        fold(1, vtx_ref[0, X_CHUNKS - 1])
        fold(0, vtc_ref[0, 0])

    _da_body(q_ref, lam_ref, g_ref, o_ref, s_sc, acc_sc, m_sc, lambda_init, run_chunks)


def _da_ctx_kernel(q_ref, kc_ref, vtc_ref, lam_ref, g_ref, o_in_hbm, o_ref,
                   s_sc, acc_sc, m_sc, *, lambda_init):
    del o_in_hbm

    def run_chunks(scores, fold):
        scores(0, kc_ref[...])
        fold(0, vtc_ref[0, 0])

    _da_body(q_ref, lam_ref, g_ref, o_ref, s_sc, acc_sc, m_sc, lambda_init, run_chunks)


def _da_scratch(tq):
    return [pltpu.VMEM((2, CHUNK, 2 * tq), F32), pltpu.VMEM((DA_VT_ROWS, 2 * tq), F32),
            pltpu.VMEM((1, 2 * tq), F32)]


def da_attention(qkv, lam, subln_g, lambda_init, nb):
    n = qkv.shape[0]
    d = D_MODEL
    ncb = d // LANES
    xc = nb * X_CHUNKS
    tq = DA_TQ
    vt = qkv[:, 2 * d:].reshape(n // CHUNK, CHUNK, ncb, LANES).transpose(2, 0, 3, 1)
    extra = jnp.zeros((ncb, n // CHUNK, DA_VT_ROWS - LANES, CHUNK), BF16).at[:, :, 0, :].set(1.0)
    vt = jnp.concatenate([vt, extra], axis=2)
    lam = lam.astype(F32)
    gcol = subln_g.astype(F32).reshape(LANES, 1)
    o = pl.pallas_call(
        functools.partial(_da_x_kernel, lambda_init=lambda_init),
        grid=(nb, ncb, SEQ // tq),
        in_specs=[pl.BlockSpec((tq, LANES), lambda b, h, t: (b * (SEQ // tq) + t, h)),
                  pl.BlockSpec((SEQ, LANES), lambda b, h, t: (b, ncb + h)),
                  pl.BlockSpec((CHUNK, LANES), lambda b, h, t: (xc + b, ncb + h)),
                  pl.BlockSpec((1, X_CHUNKS, DA_VT_ROWS, CHUNK), lambda b, h, t: (h, b, 0, 0)),
                  pl.BlockSpec((1, 1, DA_VT_ROWS, CHUNK), lambda b, h, t: (h, xc + b, 0, 0)),
                  pl.BlockSpec((4, DA_HEAD_DIM), lambda b, h, t: (0, 0)),
                  pl.BlockSpec((LANES, 1), lambda b, h, t: (0, 0))],
        out_specs=pl.BlockSpec((tq, LANES), lambda b, h, t: (b * (SEQ // tq) + t, h)),
        out_shape=jax.ShapeDtypeStruct((n, d), BF16),
        scratch_shapes=_da_scratch(tq),
        compiler_params=_cp(("parallel", "parallel", "arbitrary")),
        name="da_attention_x",
    )(qkv, qkv, qkv, vt, vt, lam, gcol)
    return pl.pallas_call(
        functools.partial(_da_ctx_kernel, lambda_init=lambda_init),
        grid=(nb, ncb),
        in_specs=[pl.BlockSpec((CHUNK, LANES), lambda b, h: (xc + b, h)),
                  pl.BlockSpec((CHUNK, LANES), lambda b, h: (xc + b, ncb + h)),
                  pl.BlockSpec((1, 1, DA_VT_ROWS, CHUNK), lambda b, h: (h, xc + b, 0, 0)),
                  pl.BlockSpec((4, DA_HEAD_DIM), lambda b, h: (0, 0)),
                  pl.BlockSpec((LANES, 1), lambda b, h: (0, 0)),
                  pl.BlockSpec(memory_space=pl.ANY)],
        out_specs=pl.BlockSpec((CHUNK, LANES), lambda b, h: (xc + b, h)),
        out_shape=jax.ShapeDtypeStruct((n, d), BF16),
        scratch_shapes=_da_scratch(CHUNK),
        input_output_aliases={5: 0},
        compiler_params=_cp(("parallel", "parallel")),
        name="da_attention_ctx",
    )(qkv, qkv, vt, lam, gcol, o)


def _block_diag(w):
    per = LANES // ML_BLOCK
    wr = w.reshape(ML_INNER // LANES, per, ML_BLOCK, ML_BLOCK)
    eye = jnp.eye(per, dtype=w.dtype)
    return jnp.einsum('cgio,gh->cgiho', wr, eye).reshape(ML_INNER // LANES, LANES, LANES).astype(BF16)


def _ml_conv_kernel(p_ref, c_ref, n_ref, cw_ref, cb_ref, wq_ref, wk_ref, wv_ref, wg_ref, bg_ref,
                    xc_ref, q_ref, k_ref, v_ref, g_ref, *, n_xchunks):
    i = pl.program_id(0)
    is_x = i < n_xchunks
    j = i % X_CHUNKS
    halo = 16
    cur = c_ref[...].astype(F32)
    prev = p_ref[CHUNK - halo:CHUNK, :].astype(F32)
    nxt = n_ref[0:halo, :].astype(F32)
    prev = jnp.where(jnp.logical_and(is_x, j > 0), prev, jnp.zeros_like(prev))
    nxt = jnp.where(jnp.logical_and(is_x, j < X_CHUNKS - 1), nxt, jnp.zeros_like(nxt))
    xp = jnp.concatenate([prev, cur, nxt], axis=0)
    rows = CHUNK + 2 * halo
    y = cb_ref[...] + cw_ref[ML_CONV_K // 2:ML_CONV_K // 2 + 1, :] * cur
    for tap in range(ML_CONV_K):
        dlt = tap - ML_CONV_K // 2
        if dlt == 0:
            continue
        shifted = pltpu.roll(xp, (-dlt) % rows, 0)[halo:halo + CHUNK]
        y = y + cw_ref[tap:tap + 1, :] * shifted
    xcb = _silu(y).astype(BF16)
    xc_ref[...] = xcb
    xm = c_ref[...]
    for s in range(xcb.shape[1] // LANES):
        sl = slice(s * LANES, (s + 1) * LANES)
        q_ref[:, sl] = jnp.dot(xcb[:, sl], wq_ref[s], preferred_element_type=F32).astype(BF16)
        k_ref[:, sl] = jnp.dot(xcb[:, sl], wk_ref[s], preferred_element_type=F32).astype(BF16)
        v_ref[:, sl] = jnp.dot(xm[:, sl], wv_ref[s], preferred_element_type=F32).astype(BF16)

    part = (jnp.dot(q_ref[...], wg_ref[0], preferred_element_type=F32)
            + jnp.dot(k_ref[...], wg_ref[1], preferred_element_type=F32)
            + jnp.dot(v_ref[...], wg_ref[2], preferred_element_type=F32))
    c = pl.program_id(1)

    @pl.when(c == 0)
    def _():
        g_ref[...] = bg_ref[...] + part

    @pl.when(c > 0)
    def _():
        g_ref[...] += part


def ml_conv_qkv(up, conv_w, conv_b, wq, wk, wv, wg, bg, nb):
    n = up.shape[0]
    nchunks = n // CHUNK
    cw = 1024
    ncb = ML_INNER // cw
    sub = cw // LANES
    blk = pl.BlockSpec((CHUNK, cw), lambda i, c: (i, c))
    wspec = pl.BlockSpec((sub, LANES, LANES), lambda i, c: (c, 0, 0))
    out = jax.ShapeDtypeStruct((n, ML_INNER), BF16)
    return pl.pallas_call(
        functools.partial(_ml_conv_kernel, n_xchunks=nb * X_CHUNKS),
        grid=(nchunks, ncb),
        in_specs=[pl.BlockSpec((CHUNK, cw), lambda i, c: (jnp.maximum(i - 1, 0), c)),
                  blk,
                  pl.BlockSpec((CHUNK, cw), lambda i, c: (jnp.minimum(i + 1, nchunks - 1), c)),
                  pl.BlockSpec((ML_CONV_K, cw), lambda i, c: (0, c)),
                  pl.BlockSpec((1, cw), lambda i, c: (0, c)),
                  wspec, wspec, wspec,
                  pl.BlockSpec((3, cw, LANES), lambda i, c: (0, c, 0)),
                  pl.BlockSpec((1, LANES), lambda i, c: (0, 0))],
        out_specs=[blk, blk, blk, blk, pl.BlockSpec((CHUNK, LANES), lambda i, c: (i, 0))],
        out_shape=[out, out, out, out, jax.ShapeDtypeStruct((n, LANES), F32)],
        compiler_params=_cp(("parallel", "arbitrary")),
        name="ml_conv_qkv",
    )(up, up, up, conv_w.astype(F32), conv_b.astype(F32).reshape(1, ML_INNER), wq, wk, wv, wg, bg)


def _log_sigmoid(x):
    return jnp.minimum(x, 0.0) - jnp.log(1.0 + jnp.exp(-jnp.abs(x)))


def _ml_scan_dir(d, h, p, q_ref, k_ref, v_ref, g_ref, gt_ref, o_ref, c_sc, n_sc, m_sc):
    ninf = -jnp.inf

    @pl.when(p == 0)
    def _():
        c_sc[...] = jnp.zeros_like(c_sc)
        n_sc[...] = jnp.zeros_like(n_sc)
        m_sc[...] = jnp.full_like(m_sc, ninf)

    L = CHUNK
    col_i = d * 2 * ML_HEADS + h
    col_f = col_i + ML_HEADS
    lane = lax.broadcasted_iota(jnp.int32, (1, LANES), 1)
    g = g_ref[...]
    i_col = jnp.sum(jnp.where(lane == col_i, g, 0.0), axis=1, keepdims=True)
    f_col = _log_sigmoid(jnp.sum(jnp.where(lane == col_f, g, 0.0), axis=1, keepdims=True))
    i_row = gt_ref[pl.ds(col_i, 1), :]
    f_row = _log_sigmoid(gt_ref[pl.ds(col_f, 1), :])

    jj = lax.broadcasted_iota(jnp.int32, (L, L), 0)
    ss = lax.broadcasted_iota(jnp.int32, (L, L), 1)
    valid = (ss <= jj) if d == 0 else (ss >= jj)
    valid_t = (jj <= ss) if d == 0 else (jj >= ss)
    bcum_col = jnp.sum(jnp.where(valid, f_row, 0.0), axis=1, keepdims=True)
    bcum_row = jnp.sum(jnp.where(valid_t, f_col, 0.0), axis=0, keepdims=True)
    b_last = jnp.sum(f_row, axis=1, keepdims=True)
    m_prev = m_sc[...]

    logd = jnp.where(valid, bcum_col - bcum_row + i_row, ninf)
    log_inter = bcum_col + m_prev
    m_j = jnp.maximum(log_inter, jnp.max(logd, axis=1, keepdims=True))
    dmat = jnp.exp(logd - m_j)
    inter = jnp.exp(log_inter - m_j)

    q = q_ref[...]
    k = k_ref[...]
    v = v_ref[...]
    scale = ML_HEAD_DIM ** -0.5
    sc = _nt(q, k) * scale * dmat
    c_prev = c_sc[...]
    n_prev = n_sc[...]
    qc = jnp.dot(q, c_prev.astype(BF16), preferred_element_type=F32) * scale
    num = inter * qc + jnp.dot(sc.astype(BF16), v, preferred_element_type=F32)
    qn = jnp.sum(q.astype(F32) * n_prev, axis=1, keepdims=True) * scale
    den = inter * qn + jnp.sum(sc, axis=1, keepdims=True)
    o_ref[...] = num / jnp.maximum(jnp.abs(den), jnp.exp(-m_j))

    ls = b_last - bcum_col + i_col
    m_new = jnp.maximum(b_last + m_prev, jnp.max(ls, axis=0, keepdims=True))
    w = jnp.exp(ls - m_new)
    decay = jnp.exp(b_last + m_prev - m_new)
    kw = k.astype(F32) * w
    c_sc[...] = decay * c_prev + _tn(kw.astype(BF16), v)
    n_sc[...] = decay * n_prev + jnp.sum(kw, axis=0, keepdims=True)
    m_sc[...] = m_new


def _ml_scan_kernel(qf_ref, kf_ref, vf_ref, gf_ref, gtf_ref, qb_ref, kb_ref, vb_ref, gb_ref, gtb_ref,
                    of_ref, ob_ref, c_sc, n_sc, m_sc):
    h = pl.program_id(1)
    p = pl.program_id(2)
    _ml_scan_dir(0, h, p, qf_ref, kf_ref, vf_ref, gf_ref, gtf_ref, of_ref, c_sc.at[0], n_sc.at[0], m_sc.at[0])
    _ml_scan_dir(1, h, p, qb_ref, kb_ref, vb_ref, gb_ref, gtb_ref, ob_ref, c_sc.at[1], n_sc.at[1], m_sc.at[1])


def ml_scan(q, k, v, g, gt, nb):
    n = q.shape[0]
    xc = nb * X_CHUNKS

    def cidx(b, d, p):
        xi = p - 1 if d == 0 else X_CHUNKS - p
        return jnp.where(p == 0, xc + b, b * X_CHUNKS + xi)

    def specs(d):
        blk = pl.BlockSpec((CHUNK, ML_HEAD_DIM), lambda b, h, p: (cidx(b, d, p), h))
        return [blk, blk, blk,
                pl.BlockSpec((CHUNK, LANES), lambda b, h, p: (cidx(b, d, p), 0)),
                pl.BlockSpec((16, CHUNK), lambda b, h, p: (0, cidx(b, d, p)))]

    out = jax.ShapeDtypeStruct((n, ML_INNER), F32)
    return pl.pallas_call(
        _ml_scan_kernel,
        grid=(nb, ML_HEADS, X_CHUNKS + 1),
        in_specs=specs(0) + specs(1),
        out_specs=[pl.BlockSpec((CHUNK, ML_HEAD_DIM), lambda b, h, p: (cidx(b, 0, p), h)),
                   pl.BlockSpec((CHUNK, ML_HEAD_DIM), lambda b, h, p: (cidx(b, 1, p), h))],
        out_shape=[out, out],
        scratch_shapes=[pltpu.VMEM((2, ML_HEAD_DIM, ML_HEAD_DIM), F32),
                        pltpu.VMEM((2, 1, ML_HEAD_DIM), F32),
                        pltpu.VMEM((2, 1, 1), F32)],
        compiler_params=_cp(("parallel", "parallel", "arbitrary")),
        name="ml_scan",
    )(q, k, v, g, gt, q, k, v, g, gt)


def _ml_finish_kernel(hf_ref, hb_ref, xc_ref, z_ref, gn_ref, sk_ref, o_ref):
    hh = hf_ref[...] + hb_ref[...]
    z = z_ref[...].astype(F32)
    gate = _silu(z)
    for hd in range(ML_HEADS):
        sl = slice(hd * ML_HEAD_DIM, (hd + 1) * ML_HEAD_DIM)
        seg = hh[:, sl]
        mu = jnp.mean(seg, axis=-1, keepdims=True)
        cen = seg - mu
        var = jnp.mean(cen * cen, axis=-1, keepdims=True)
        hn = cen * lax.rsqrt(var + EPS) * gn_ref[:, sl]
        a = (hn + sk_ref[:, sl] * xc_ref[:, sl].astype(F32)) * gate[:, sl]
        o_ref[:, sl] = a.astype(o_ref.dtype)


def ml_finish(hf, hb, xc, up, gn_w, skip):
    n = xc.shape[0]
    tm = CHUNK
    vec = pl.BlockSpec((1, ML_INNER), lambda i: (0, 0))
    return pl.pallas_call(
        _ml_finish_kernel,
        grid=(n // tm,),
        in_specs=[pl.BlockSpec((tm, ML_INNER), lambda i: (i, 0)),
                  pl.BlockSpec((tm, ML_INNER), lambda i: (i, 0)),
                  pl.BlockSpec((tm, ML_INNER), lambda i: (i, 0)),
                  pl.BlockSpec((tm, ML_INNER), lambda i: (i, 1)),
                  vec, vec],
        out_specs=pl.BlockSpec((tm, ML_INNER), lambda i: (i, 0)),
        out_shape=jax.ShapeDtypeStruct((n, ML_INNER), BF16),
        compiler_params=_cp(("parallel",)),
        name="ml_finish",
    )(hf, hb, xc, up, gn_w.astype(F32).reshape(1, ML_INNER), skip.astype(F32).reshape(1, ML_INNER))


def _router_kernel(x_ref, g_ref, sh_ref, sc_ref, w_ref, b_ref, h_ref, o_ref):
    x = x_ref[...]
    y = x * lax.rsqrt(jnp.mean(x * x, axis=-1, keepdims=True) + EPS) * g_ref[...]
    hf = y * (1.0 + sc_ref[0, 0]) + sh_ref[0, 0]
    h_hi = hf.astype(BF16)
    h_ref[...] = hf
    h_lo = (hf - h_hi.astype(F32)).astype(BF16)
    logits = (jnp.dot(h_hi, w_ref[0], preferred_element_type=F32)
              + jnp.dot(h_hi, w_ref[1], preferred_element_type=F32)
              + jnp.dot(h_lo, w_ref[0], preferred_element_type=F32)) + b_ref[...]
    lane = lax.broadcasted_iota(jnp.int32, (1, LANES), 1).astype(F32)
    big = 1e9
    ninf = -jnp.inf
    is_g = jnp.logical_and(lane >= MOE_E, lane < MOE_E + MOE_GROUPS)
    gl = jnp.where(is_g, logits, ninf)
    gmax = jnp.max(gl, axis=1, keepdims=True)
    g_val = 1.0 / jnp.sum(jnp.exp(gl - gmax), axis=1, keepdims=True)
    g_idx = jnp.min(jnp.where(gl == gmax, lane, big), axis=1, keepdims=True) - MOE_E
    lo = g_idx * MOE_EPG
    sel = jnp.logical_and(lane >= lo, lane < lo + MOE_EPG)
    el = jnp.where(sel, logits, ninf)
    e1 = jnp.max(el, axis=1, keepdims=True)
    esum = jnp.sum(jnp.exp(el - e1), axis=1, keepdims=True)
    i1 = jnp.min(jnp.where(el == e1, lane, big), axis=1, keepdims=True)
    el2 = jnp.where(lane == i1, ninf, el)
    e2 = jnp.max(el2, axis=1, keepdims=True)
    i2 = jnp.min(jnp.where(el2 == e2, lane, big), axis=1, keepdims=True)
    p1 = 1.0 / esum
    p2 = jnp.exp(e2 - e1) / esum
    w1 = g_val * p1 / (p1 + p2)
    w2 = g_val * p2 / (p1 + p2)
    gates = jnp.where(lane == i1, w1, 0.0) + jnp.where(lane == i2, w2, 0.0)
    marks = jnp.where(jnp.logical_or(lane == i1 + SEL_LANE, lane == i2 + SEL_LANE), 1.0, 0.0)
    o_ref[...] = gates + marks


def moe_router(s, g, mod, k_shift, k_scale, wr, br, nb):
    n, d = s.shape
    tm = _tile(n, PROJ_TM)
    return pl.pallas_call(
        _router_kernel,
        grid=(n // tm,),
        in_specs=[pl.BlockSpec((tm, d), lambda i: (i, 0)),
                  pl.BlockSpec((1, d), lambda i: (0, 0)),
                  pl.BlockSpec((1, 1, 1, d), lambda i: (_mod_row(i, tm, nb), k_shift, 0, 0)),
                  pl.BlockSpec((1, 1, 1, d), lambda i: (_mod_row(i, tm, nb), k_scale, 0, 0)),
                  pl.BlockSpec((2, d, LANES), lambda i: (0, 0, 0)),
                  pl.BlockSpec((1, LANES), lambda i: (0, 0))],
        out_specs=[pl.BlockSpec((tm, d), lambda i: (i, 0)),
                   pl.BlockSpec((tm, LANES), lambda i: (i, 0))],
        out_shape=[jax.ShapeDtypeStruct((n, d), F32),
                   jax.ShapeDtypeStruct((n, LANES), F32)],
        compiler_params=_cp(("parallel",)),
        name="moe_router",
    )(s, g.reshape(1, d), mod, mod, wr, br)


def _rank_kernel(route_ref, rank_ref, cnt_ref, carry_ref):
    i = pl.program_id(0)

    @pl.when(i == 0)
    def _():
        carry_ref[...] = jnp.zeros_like(carry_ref)

    tm = route_ref.shape[0]
    lane = lax.broadcasted_iota(jnp.int32, (1, LANES), 1)
    marks = jnp.where(lane >= SEL_LANE, route_ref[...], 0.0)
    rr = lax.broadcasted_iota(jnp.int32, (tm, tm), 0)
    cc = lax.broadcasted_iota(jnp.int32, (tm, tm), 1)
    below = jnp.where(cc < rr, 1.0, 0.0).astype(BF16)
    rank_ref[...] = jnp.dot(below, marks.astype(BF16), preferred_element_type=F32) + carry_ref[...]
    carry_ref[...] += jnp.sum(marks, axis=0, keepdims=True)
    cnt_ref[...] = carry_ref[...]


def moe_rank(route):
    n = route.shape[0]
    tm = _tile(n, PROJ_TM)
    return pl.pallas_call(
        _rank_kernel,
        grid=(n // tm,),
        in_specs=[pl.BlockSpec((tm, LANES), lambda i: (i, 0))],
        out_specs=[pl.BlockSpec((tm, LANES), lambda i: (i, 0)),
                   pl.BlockSpec((1, LANES), lambda i: (0, 0))],
        out_shape=[jax.ShapeDtypeStruct((n, LANES), F32),
                   jax.ShapeDtypeStruct((1, LANES), F32)],
        scratch_shapes=[pltpu.VMEM((1, LANES), F32)],
        compiler_params=_cp(("arbitrary",)),
        name="moe_rank",
    )(route)


def _pos_kernel(route_ref, rank_ref, off_ref, o_ref):
    lane = lax.broadcasted_iota(jnp.int32, (1, LANES), 1).astype(F32)
    route = route_ref[...]
    marked = jnp.logical_and(lane >= SEL_LANE, route > 0.5)
    p = rank_ref[...] + off_ref[...]
    lane_a = jnp.min(jnp.where(marked, lane, 1e9), axis=1, keepdims=True)
    lane_b = jnp.max(jnp.where(marked, lane, -1.0), axis=1, keepdims=True)

    def pick(src, at):
        return jnp.sum(jnp.where(lane == at, src, 0.0), axis=1, keepdims=True)

    out = jnp.where(lane == 0.0, pick(p, lane_a), 0.0)
    out = out + jnp.where(lane == 1.0, pick(p, lane_b), 0.0)
    out = out + jnp.where(lane == 2.0, pick(route, lane_a - SEL_LANE), 0.0)
    out = out + jnp.where(lane == 3.0, pick(route, lane_b - SEL_LANE), 0.0)
    o_ref[...] = out


def moe_positions(route, rank, off):
    n = route.shape[0]
    tm = _tile(n, PROJ_TM)
    blk = pl.BlockSpec((tm, LANES), lambda i: (i, 0))
    return pl.pallas_call(
        _pos_kernel,
        grid=(n // tm,),
        in_specs=[blk, blk, pl.BlockSpec((1, LANES), lambda i: (0, 0))],
        out_specs=blk,
        out_shape=jax.ShapeDtypeStruct((n, LANES), F32),
        compiler_params=_cp(("parallel",)),
        name="moe_positions",
    )(route, rank, off)


def _row_copy(src, src_row, dst, dst_row, sem):
    return pltpu.make_async_copy(src.at[pl.ds(src_row, 1)], dst.at[pl.ds(dst_row, 1)], sem)


def _dispatch_kernel(last_ref, pos_ref, h_ref, xs_hbm, zero_sc, sem):
    @pl.when(pl.program_id(0) == 0)
    def _():
        zero_sc[...] = jnp.zeros_like(zero_sc)

        def fill(e, carry):
            start = pl.multiple_of(last_ref[e], MOE_TM)
            pltpu.make_async_copy(zero_sc, xs_hbm.at[pl.ds(start, MOE_TM)], sem).start()
            return carry

        lax.fori_loop(0, MOE_E, fill, 0)

        def drain(e, carry):
            pltpu.make_async_copy(zero_sc, xs_hbm.at[pl.ds(0, MOE_TM)], sem).wait()
            return carry

        lax.fori_loop(0, MOE_E, drain, 0)

    tok = h_ref.shape[0]

    def issue(r, carry):
        _row_copy(h_ref, r, xs_hbm, pos_ref[0, 0, r], sem).start()
        _row_copy(h_ref, r, xs_hbm, pos_ref[0, 0, tok + r], sem).start()
        return carry

    lax.fori_loop(0, tok, issue, 0, unroll=8)
    for _ in range(2):
        pltpu.make_async_copy(h_ref, xs_hbm.at[pl.ds(0, tok)], sem).wait()


def moe_dispatch(last_tile_start, pos, h, n_sorted):
    n, d = h.shape
    tok = pos.shape[2] // 2
    return pl.pallas_call(
        _dispatch_kernel,
        grid_spec=pltpu.PrefetchScalarGridSpec(
            num_scalar_prefetch=1,
            grid=(n // tok,),
            in_specs=[pl.BlockSpec((1, 1, 2 * tok), lambda i, last: (i, 0, 0), memory_space=pltpu.SMEM),
                      pl.BlockSpec((tok, d), lambda i, last: (i, 0))],
            out_specs=pl.BlockSpec(memory_space=pl.ANY),
            scratch_shapes=[pltpu.VMEM((MOE_TM, d), h.dtype), pltpu.SemaphoreType.DMA]),
        out_shape=jax.ShapeDtypeStruct((n_sorted, d), h.dtype),
        compiler_params=_cp(("arbitrary",)),
        name="moe_dispatch",
    )(last_tile_start, pos, h)


def _experts_kernel(te_ref, nt_ref, x_ref, w1_ref, w3_ref, w2_ref, o_ref, w1_sc, w3_sc, w2_sc):
    i = pl.program_id(0)
    active = i < nt_ref[0]

    @pl.when(jnp.logical_and(active, jnp.logical_or(i == 0, te_ref[i] != te_ref[jnp.maximum(i - 1, 0)])))
    def _():
        w1_sc[...] = w1_ref[0, 0].astype(BF16)
        w3_sc[...] = w3_ref[0, 0].astype(BF16)
        w2_sc[...] = w2_ref[0, 0].astype(BF16)

    @pl.when(active)
    def _():
        x = x_ref[...].astype(BF16)
        a = jnp.dot(x, w1_sc[...], preferred_element_type=F32)
        b = jnp.dot(x, w3_sc[...], preferred_element_type=F32)
        he = (_silu(a) * b).astype(BF16)
        o_ref[...] = jnp.dot(he, w2_sc[...], preferred_element_type=F32)

    @pl.when(jnp.logical_not(active))
    def _():
        o_ref[...] = jnp.zeros_like(o_ref)


def moe_experts(tile_expert, n_tiles, xs, w1, w3, w2, layer):
    n_sorted, d = xs.shape
    return pl.pallas_call(
        _experts_kernel,
        grid_spec=pltpu.PrefetchScalarGridSpec(
            num_scalar_prefetch=2,
            grid=(n_sorted // MOE_TM,),
            in_specs=[pl.BlockSpec((MOE_TM, d), lambda i, te, nt: (jnp.minimum(i, nt[0] - 1), 0)),
                      pl.BlockSpec((1, 1, d, MOE_HIDDEN), lambda i, te, nt: (layer, te[i], 0, 0)),
                      pl.BlockSpec((1, 1, d, MOE_HIDDEN), lambda i, te, nt: (layer, te[i], 0, 0)),
                      pl.BlockSpec((1, 1, MOE_HIDDEN, d), lambda i, te, nt: (layer, te[i], 0, 0))],
            out_specs=pl.BlockSpec((MOE_TM, d), lambda i, te, nt: (i, 0)),
            scratch_shapes=[pltpu.VMEM((d, MOE_HIDDEN), BF16), pltpu.VMEM((d, MOE_HIDDEN), BF16),
                            pltpu.VMEM((MOE_HIDDEN, d), BF16)]),
        out_shape=jax.ShapeDtypeStruct((n_sorted, d), F32),
        compiler_params=_cp(("arbitrary",)),
        name="moe_experts",
    )(tile_expert, n_tiles, xs, w1, w3, w2)


def _combine_kernel(pos_ref, nxt_ref, meta_ref, s_ref, g_ref, ys_hbm, o_ref, buf_ref, sem):
    i = pl.program_id(0)
    slot = i % 2
    tok = s_ref.shape[0]

    def gather(p_ref, dst_slot):
        def issue(r, carry):
            _row_copy(ys_hbm, p_ref[0, 0, r], buf_ref.at[dst_slot, 0], r, sem.at[dst_slot]).start()
            _row_copy(ys_hbm, p_ref[0, 0, tok + r], buf_ref.at[dst_slot, 1], r, sem.at[dst_slot]).start()
            return carry

        lax.fori_loop(0, tok, issue, 0, unroll=8)

    @pl.when(i == 0)
    def _():
        gather(pos_ref, 0)

    @pl.when(i + 1 < pl.num_programs(0))
    def _():
        gather(nxt_ref, 1 - slot)

    for k in range(2):
        pltpu.make_async_copy(ys_hbm.at[pl.ds(0, tok)], buf_ref.at[slot, k], sem.at[slot]).wait()
    meta = meta_ref[...]
    y = meta[:, 2:3] * buf_ref[slot, 0] + meta[:, 3:4] * buf_ref[slot, 1]
    o_ref[...] = s_ref[...] + g_ref[0, 0] * y


def moe_combine(pos, meta, s, mod, k_gate, ys, nb):
    n, d = s.shape
    tm = pos.shape[2] // 2
    last = n // tm - 1
    return pl.pallas_call(
        _combine_kernel,
        grid=(n // tm,),
        in_specs=[pl.BlockSpec((1, 1, 2 * tm), lambda i: (i, 0, 0), memory_space=pltpu.SMEM),
                  pl.BlockSpec((1, 1, 2 * tm), lambda i: (jnp.minimum(i + 1, last), 0, 0),
                               memory_space=pltpu.SMEM),
                  pl.BlockSpec((tm, LANES), lambda i: (i, 0)),
                  pl.BlockSpec((tm, d), lambda i: (i, 0)),
                  pl.BlockSpec((1, 1, 1, d), lambda i: (_mod_row(i, tm, nb), k_gate, 0, 0)),
                  pl.BlockSpec(memory_space=pl.ANY)],
        out_specs=pl.BlockSpec((tm, d), lambda i: (i, 0)),
        out_shape=jax.ShapeDtypeStruct((n, d), F32),
        scratch_shapes=[pltpu.VMEM((2, 2, tm, d), F32), pltpu.SemaphoreType.DMA((2,))],
        input_output_aliases={3: 0},
        compiler_params=_cp(("arbitrary",)),
        name="moe_combine",
    )(pos, pos, meta, s, mod, ys)


def _na_layer(h, w_qkv, w_o, rpb, s, mod, nb):
    qkv = matmul(h, w_qkv.astype(BF16), BF16)
    o = na_attention(qkv, _na_bias_tables(rpb), nb)
    return matmul_residual(o, w_o.astype(BF16), s, mod, 2, nb)


def _da_layer(h, w_qkv, lam, subln_g, w_o, lambda_init, s, mod, nb):
    qkv = matmul_rope(h, w_qkv.astype(BF16), nb)
    o = da_attention(qkv, lam, subln_g, lambda_init, nb)
    return matmul_residual(o, w_o.astype(BF16), s, mod, 2, nb)


def _ml_layer(h, w_up, conv_w, conv_b, w_q, w_k, w_v, w_gate, b_gate, gn_w, skip, w_down, s, mod, nb):
    up = matmul(h, w_up.astype(BF16), BF16)
    ng = w_gate.shape[1]
    wg = jnp.pad(w_gate, ((0, 0), (0, LANES - ng))).reshape(3, ML_INNER, LANES).astype(BF16)
    bg = jnp.pad(b_gate.astype(F32), (0, LANES - ng)).reshape(1, LANES)
    xc, q, k, v, g = ml_conv_qkv(up, conv_w, conv_b, _block_diag(w_q), _block_diag(w_k), _block_diag(w_v),
                                 wg, bg, nb)
    gt = g[:, :ng].T
    hf, hb = ml_scan(q, k, v, g, gt, nb)
    a = ml_finish(hf, hb, xc, up, gn_w, skip)
    return matmul_residual(a, w_down.astype(BF16), s, mod, 2, nb)


def _moe_layer(norm_g, w_group, b_group, w_router, b_router, w1, w3, w2, layer, s, mod, nb):
    d = s.shape[1]
    pad = LANES - MOE_E - MOE_GROUPS
    wr = jnp.concatenate([w_router, w_group, jnp.zeros((d, pad), w_router.dtype)], axis=1).astype(F32)
    wr_hi = wr.astype(BF16)
    wr_lo = (wr - wr_hi.astype(F32)).astype(BF16)
    br = jnp.concatenate([b_router, b_group, jnp.zeros((pad,), b_router.dtype)]).astype(F32).reshape(1, LANES)
    h, route = moe_router(s, norm_g.astype(F32), mod, 3, 4, jnp.stack([wr_hi, wr_lo]), br, nb)
    rank, cnt = moe_rank(route)
    n = s.shape[0]
    cnt_e = cnt[0, SEL_LANE:SEL_LANE + MOE_E].astype(jnp.int32)
    gsz = ((cnt_e + MOE_TM - 1) // MOE_TM) * MOE_TM
    ends = jnp.cumsum(gsz)
    off_row = jnp.zeros((1, LANES), F32).at[0, SEL_LANE:SEL_LANE + MOE_E].set((ends - gsz).astype(F32))
    meta = moe_positions(route, rank, off_row)
    n_sorted = 2 * n + MOE_E * MOE_TM
    n_tiles = (ends[-1:] // MOE_TM).astype(jnp.int32)
    tile_ids = jnp.arange(n_sorted // MOE_TM, dtype=jnp.int32)
    tile_expert = jnp.sum((tile_ids[:, None] >= (ends // MOE_TM)[None, :]).astype(jnp.int32), axis=1)
    tile_expert = jnp.minimum(tile_expert, MOE_E - 1)
    tok = _tile(n, MOE_TOK)
    pos = meta[:, :2].astype(jnp.int32).reshape(n // tok, tok, 2)
    pos = pos.transpose(0, 2, 1).reshape(n // tok, 1, 2 * tok)
    last_tile_start = jnp.maximum(ends - MOE_TM, 0).astype(jnp.int32)
    xs = moe_dispatch(last_tile_start, pos, h, n_sorted)
    ys = moe_experts(tile_expert, n_tiles, xs, w1.astype(F32), w3.astype(F32), w2.astype(F32), layer)
    return moe_combine(pos, meta, s, mod, 5, ys, nb)


def kernel(x, c, ctx, c_ctx, mod_w, mod_b, norm_g, final_g, na_w_qkv, na_w_o, na_rpb, ml_w_up, ml_conv_w, ml_conv_b, ml_w_q, ml_w_k, ml_w_v, ml_w_gate, ml_b_gate, ml_gn_w, ml_skip, ml_w_down, da_w_qkv, da_lambda, da_subln_g, da_w_o, moe_w_group, moe_b_group, moe_w_router, moe_b_router, moe_w1, moe_w3, moe_w2):
    nb, seq, d = x.shape
    assert (seq, d, ctx.shape[1]) == (SEQ, D_MODEL, CTX_LEN) and nb < 16
    depth = mod_w.shape[0]
    nx = nb * seq
    s = jnp.concatenate([x.reshape(nx, d), ctx.reshape(nb * CTX_LEN, d)], axis=0).astype(F32)
    cvec = jnp.concatenate([c, c_ctx[None, :], jnp.zeros((16 - nb - 1, d), c.dtype)], axis=0).astype(F32)
    mods = mod_vectors(cvec, mod_w, mod_b).reshape(depth, 16, 6, 1, d)

    for i in range(depth):
        kind, j = i % N_MIXERS, i // N_MIXERS
        mod = mods[i]
        h = norm_mod(s, norm_g[i, 0].astype(F32), mod, 0, 1, nb)
        if kind == 0:
            s = _na_layer(h, na_w_qkv[j], na_w_o[j], na_rpb[j], s, mod, nb)
        elif kind == 1:
            s = _ml_layer(h, ml_w_up[j], ml_conv_w[j], ml_conv_b[j], ml_w_q[j], ml_w_k[j], ml_w_v[j],
                          ml_w_gate[j], ml_b_gate[j], ml_gn_w[j], ml_skip[j], ml_w_down[j], s, mod, nb)
        else:
            lambda_init = 0.8 - 0.6 * math.exp(-0.3 * i)
            s = _da_layer(h, da_w_qkv[j], da_lambda[j], da_subln_g[j], da_w_o[j], lambda_init, s, mod, nb)
        s = _moe_layer(norm_g[i, 1], moe_w_group[i], moe_b_group[i], moe_w_router[i], moe_b_router[i],
                       moe_w1, moe_w3, moe_w2, i, s, mod, nb)
    return final_norm(s, final_g.astype(F32), nb)
```

```python
import functools
import math

import numpy as np
import jax
import jax.numpy as jnp
from jax import lax
from jax.experimental import pallas as pl
from jax.experimental.pallas import tpu as pltpu

F32 = jnp.float32
BF16 = jnp.bfloat16

D_MODEL = 1024
SEQ = 4096
CTX_LEN = 256
GRID_W = 64
N_MIXERS = 3
EPS = 1e-6

NA_HEADS = 16
NA_WIN_H = 8
NA_WIN_W = 16

ML_HEADS = 4
ML_INNER = 2 * D_MODEL
ML_HEAD_DIM = ML_INNER // ML_HEADS
ML_BLOCK = 4
ML_CONV_K = 5

DA_HEADS = 8
DA_HEAD_DIM = 64
ROPE_BASE = 10000.0

MOE_GROUPS = 4
MOE_EPG = 8
MOE_E = MOE_GROUPS * MOE_EPG
MOE_HIDDEN = 512
MOE_TM = 512
MOE_TOK = 1024
SEL_LANE = 64

LANES = 128
CHUNK = 256
X_CHUNKS = SEQ // CHUNK
PROJ_TM = 1024
DA_TQ = 512
DA_VT_ROWS = 128 + 16
NA_ROWS = 16
NEG = -1e30
VMEM_LIMIT = 56 * 1024 * 1024


def _cp(sem, vmem=VMEM_LIMIT):
    return pltpu.CompilerParams(dimension_semantics=sem, vmem_limit_bytes=vmem)


def _tile(n, pref):
    tm = pref
    while n % tm:
        tm //= 2
    assert tm >= CHUNK
    return tm


def _mod_row(i, tm, nb):
    return jnp.where(i < nb * (SEQ // tm), i // (SEQ // tm), nb)


def _nt(a, b):
    return lax.dot_general(a, b, (((1,), (1,)), ((), ())), preferred_element_type=F32)


def _tn(a, b):
    return lax.dot_general(a, b, (((0,), (0,)), ((), ())), preferred_element_type=F32)


def _silu(x):
    return x * jax.nn.sigmoid(x)


def _head_rows(q, lane):
    zero = jnp.zeros_like(q)
    return jnp.concatenate([jnp.where(lane < 64, q, zero), jnp.where(lane >= 64, q, zero)], axis=0)


def _mod_kernel(c_ref, w_ref, b_ref, o_ref):
    a = _silu(c_ref[...]).astype(BF16)
    o_ref[0] = jnp.dot(a, w_ref[0].astype(BF16), preferred_element_type=F32) + b_ref[0]


def mod_vectors(cvec, mod_w, mod_b):
    depth, d, n6 = mod_w.shape
    tn = 1024
    return pl.pallas_call(
        _mod_kernel,
        grid=(depth, n6 // tn),
        in_specs=[pl.BlockSpec((16, d), lambda l, j: (0, 0)),
                  pl.BlockSpec((1, d, tn), lambda l, j: (l, 0, j)),
                  pl.BlockSpec((1, 1, tn), lambda l, j: (l, 0, j))],
        out_specs=pl.BlockSpec((1, 16, tn), lambda l, j: (l, 0, j)),
        out_shape=jax.ShapeDtypeStruct((depth, 16, n6), F32),
        compiler_params=_cp(("parallel", "parallel")),
        name="mod_vectors",
    )(cvec, mod_w, mod_b.reshape(depth, 1, n6))


def _norm_mod_kernel(x_ref, g_ref, sh_ref, sc_ref, o_ref):
    x = x_ref[...]
    y = x * lax.rsqrt(jnp.mean(x * x, axis=-1, keepdims=True) + EPS) * g_ref[...]
    o_ref[...] = (y * (1.0 + sc_ref[0, 0]) + sh_ref[0, 0]).astype(o_ref.dtype)


def norm_mod(s, g, mod, k_shift, k_scale, nb):
    n, d = s.shape
    tm = _tile(n, PROJ_TM)
    return pl.pallas_call(
        _norm_mod_kernel,
        grid=(n // tm,),
        in_specs=[pl.BlockSpec((tm, d), lambda i: (i, 0)),
                  pl.BlockSpec((1, d), lambda i: (0, 0)),
                  pl.BlockSpec((1, 1, 1, d), lambda i: (_mod_row(i, tm, nb), k_shift, 0, 0)),
                  pl.BlockSpec((1, 1, 1, d), lambda i: (_mod_row(i, tm, nb), k_scale, 0, 0))],
        out_specs=pl.BlockSpec((tm, d), lambda i: (i, 0)),
        out_shape=jax.ShapeDtypeStruct((n, d), BF16),
        compiler_params=_cp(("parallel",)),
        name="norm_mod",
    )(s, g.reshape(1, d), mod, mod)


def _final_norm_kernel(x_ref, g_ref, o_ref):
    x = x_ref[...]
    o_ref[0] = x * lax.rsqrt(jnp.mean(x * x, axis=-1, keepdims=True) + EPS) * g_ref[...]


def final_norm(s, g, nb):
    d = s.shape[1]
    tm = PROJ_TM
    per = SEQ // tm
    return pl.pallas_call(
        _final_norm_kernel,
        grid=(nb * per,),
        in_specs=[pl.BlockSpec((tm, d), lambda i: (i, 0)),
                  pl.BlockSpec((1, d), lambda i: (0, 0))],
        out_specs=pl.BlockSpec((1, tm, d), lambda i: (i // per, i % per, 0)),
        out_shape=jax.ShapeDtypeStruct((nb, SEQ, d), F32),
        compiler_params=_cp(("parallel",)),
        name="final_norm",
    )(s, g.reshape(1, d))


def _mm_kernel(a_ref, w_ref, o_ref):
    o_ref[...] = jnp.dot(a_ref[...], w_ref[...], preferred_element_type=F32).astype(o_ref.dtype)


def matmul(a, w, out_dtype, tn=1024):
    n, k = a.shape
    tm = _tile(n, PROJ_TM)
    nout = w.shape[1]
    tn = min(tn, nout)
    return pl.pallas_call(
        _mm_kernel,
        grid=(nout // tn, n // tm),
        in_specs=[pl.BlockSpec((tm, k), lambda j, i: (i, 0)),
                  pl.BlockSpec((k, tn), lambda j, i: (0, j))],
        out_specs=pl.BlockSpec((tm, tn), lambda j, i: (i, j)),
        out_shape=jax.ShapeDtypeStruct((n, nout), out_dtype),
        compiler_params=_cp(("parallel", "parallel")),
        name="matmul",
    )(a, w)


def _mm_res_kernel(a_ref, w_ref, r_ref, g_ref, o_ref):
    acc = jnp.dot(a_ref[...], w_ref[...], preferred_element_type=F32)
    o_ref[...] = r_ref[...] + g_ref[0, 0] * acc


def matmul_residual(a, w, s, mod, k_gate, nb):
    n, k = a.shape
    d = w.shape[1]
    tm = _tile(n, PROJ_TM)
    return pl.pallas_call(
        _mm_res_kernel,
        grid=(n // tm,),
        in_specs=[pl.BlockSpec((tm, k), lambda i: (i, 0)),
                  pl.BlockSpec((k, d), lambda i: (0, 0)),
                  pl.BlockSpec((tm, d), lambda i: (i, 0)),
                  pl.BlockSpec((1, 1, 1, d), lambda i: (_mod_row(i, tm, nb), k_gate, 0, 0))],
        out_specs=pl.BlockSpec((tm, d), lambda i: (i, 0)),
        out_shape=jax.ShapeDtypeStruct((n, d), F32),
        input_output_aliases={2: 0},
        compiler_params=_cp(("parallel",)),
        name="matmul_residual",
    )(a, w, s, mod)


def _na_bias_tables(rpb):
    o = np.arange(NA_WIN_H)[:, None]
    j = np.arange(NA_WIN_H)[None, :]
    dy = j - o + NA_WIN_H - 1
    qc = np.arange(GRID_W)[:, None]
    kc = np.arange(GRID_W)[None, :]
    dx = np.clip(kc - qc, -(NA_WIN_W - 1), NA_WIN_W - 1) + NA_WIN_W - 1
    w_start = np.clip(qc - NA_WIN_W // 2, 0, GRID_W - NA_WIN_W)
    valid = (kc >= w_start) & (kc < w_start + NA_WIN_W)
    tbl = rpb.astype(F32)[:, dy][:, :, :, dx]
    tbl = jnp.where(jnp.asarray(valid)[None, None, None], tbl, NEG)
    tbl = tbl.reshape(NA_HEADS // 2, 2, NA_WIN_H, NA_WIN_H, GRID_W, GRID_W)
    tbl = tbl.transpose(2, 0, 3, 5, 1, 4)
    return tbl.reshape(NA_WIN_H, NA_HEADS // 2, NA_WIN_H * GRID_W, 2 * GRID_W)


def _softmax_cols(s):
    m = jnp.max(s, axis=0, keepdims=True)
    p = jnp.exp(s - m)
    return (p * (1.0 / jnp.sum(p, axis=0, keepdims=True))).astype(BF16)


def _na_x_kernel(q_ref, kx_ref, vx_ref, kc_ref, vc_ref, bias_ref, o_ref, s_sc, p_sc, va_sc):
    t = pl.program_id(2)
    lane = lax.broadcasted_iota(jnp.int32, (1, LANES), 1)
    nwin = NA_WIN_H * GRID_W

    @pl.when(t == 0)
    def _():
        va_sc[0:SEQ, 0:LANES] = vx_ref[...]
        va_sc[SEQ:, 0:LANES] = vc_ref[...]
        va_sc[:, LANES:] = jnp.ones((SEQ + CTX_LEN, LANES), BF16)

    starts = []
    for i in range(NA_ROWS):
        r = t * NA_ROWS + i
        rs = jnp.clip(r - NA_WIN_H // 2, 0, SEQ // GRID_W - NA_WIN_H)
        start = pl.multiple_of(rs * GRID_W, GRID_W)
        starts.append(start)
        qq = _head_rows(q_ref[i * GRID_W:(i + 1) * GRID_W, :] * 0.125, lane)
        s_sc[i, 0:nwin, :] = _nt(kx_ref[pl.ds(start, nwin), :], qq) + bias_ref[r - rs, 0]
        s_sc[i, nwin:, :] = _nt(kc_ref[...], qq)
    for i in range(NA_ROWS):
        s = s_sc[i]
        p_sc[i] = jnp.exp(s - jnp.max(s, axis=0, keepdims=True)).astype(BF16)
    for i in range(NA_ROWS):
        oa = (_tn(p_sc[i, 0:nwin, :], va_sc[pl.ds(starts[i], nwin), :])
              + _tn(p_sc[i, nwin:, :], va_sc[SEQ:, :]))
        o = oa[:, 0:LANES] / oa[:, LANES:]
        o = jnp.where(lane < 64, o[0:GRID_W], o[GRID_W:])
        o_ref[i * GRID_W:(i + 1) * GRID_W, :] = o.astype(o_ref.dtype)


def _na_ctx_kernel(q_ref, kc_ref, vc_ref, o_in_hbm, o_ref):
    del o_in_hbm
    lane = lax.broadcasted_iota(jnp.int32, (1, LANES), 1)
    q = q_ref[...] * 0.125
    outs = []
    for hh in range(2):
        qm = jnp.where((lane // 64) == hh, q, jnp.zeros_like(q))
        p = _softmax_cols(_nt(kc_ref[...], qm))
        outs.append(_tn(p, vc_ref[...]))
    o_ref[...] = jnp.where(lane < 64, outs[0], outs[1]).astype(o_ref.dtype)


def na_attention(qkv, bias, nb):
    n = qkv.shape[0]
    d = D_MODEL
    ncb = d // LANES
    xc = nb * X_CHUNKS
    nkeys = NA_WIN_H * GRID_W + CTX_LEN
    tq = NA_ROWS * GRID_W
    steps = SEQ // tq
    o = pl.pallas_call(
        _na_x_kernel,
        grid=(nb, ncb, steps),
        in_specs=[pl.BlockSpec((tq, LANES), lambda b, h, t: (b * steps + t, h)),
                  pl.BlockSpec((SEQ, LANES), lambda b, h, t: (b, ncb + h)),
                  pl.BlockSpec((SEQ, LANES), lambda b, h, t: (b, 2 * ncb + h)),
                  pl.BlockSpec((CHUNK, LANES), lambda b, h, t: (xc + b, ncb + h)),
                  pl.BlockSpec((CHUNK, LANES), lambda b, h, t: (xc + b, 2 * ncb + h)),
                  pl.BlockSpec((NA_WIN_H, 1, NA_WIN_H * GRID_W, LANES), lambda b, h, t: (0, h, 0, 0))],
        out_specs=pl.BlockSpec((tq, LANES), lambda b, h, t: (b * steps + t, h)),
        out_shape=jax.ShapeDtypeStruct((n, d), BF16),
        scratch_shapes=[pltpu.VMEM((NA_ROWS, nkeys, LANES), F32),
                        pltpu.VMEM((NA_ROWS, nkeys, LANES), BF16),
                        pltpu.VMEM((SEQ + CTX_LEN, 2 * LANES), BF16)],
        compiler_params=_cp(("parallel", "parallel", "arbitrary")),
        name="na_attention_x",
    )(qkv, qkv, qkv, qkv, qkv, bias)
    return pl.pallas_call(
        _na_ctx_kernel,
        grid=(nb, ncb),
        in_specs=[pl.BlockSpec((CHUNK, LANES), lambda b, h: (xc + b, h)),
                  pl.BlockSpec((CHUNK, LANES), lambda b, h: (xc + b, ncb + h)),
                  pl.BlockSpec((CHUNK, LANES), lambda b, h: (xc + b, 2 * ncb + h)),
                  pl.BlockSpec(memory_space=pl.ANY)],
        out_specs=pl.BlockSpec((CHUNK, LANES), lambda b, h: (xc + b, h)),
        out_shape=jax.ShapeDtypeStruct((n, d), BF16),
        input_output_aliases={3: 0},
        compiler_params=_cp(("parallel", "parallel")),
        name="na_attention_ctx",
    )(qkv, qkv, qkv, o)


def _rope_tables(tm):
    half = DA_HEAD_DIM // 2
    freqs = ROPE_BASE ** (-np.arange(0, half, 2, dtype=np.float32) / half)
    t = np.arange(SEQ)
    row, col = t // GRID_W, t % GRID_W
    lane = np.arange(LANES)
    l64 = lane % DA_HEAD_DIM
    use_col = (l64 // half) == 1
    l32 = l64 % half
    fi = l32 % (half // 2)
    second = l32 >= half // 2
    pos = np.where(use_col[None, :], col[:, None], row[:, None]).astype(np.float32)
    ang = pos * freqs[fi][None, :]
    cos = np.cos(ang).astype(np.float32)
    sin = np.sin(ang).astype(np.float32)
    sa = np.where(second[None, :], 0.0, -sin).astype(np.float32)
    sb = np.where(second[None, :], sin, 0.0).astype(np.float32)
    ident = np.ones((tm, LANES), np.float32)
    zero = np.zeros((tm, LANES), np.float32)
    return (jnp.asarray(np.concatenate([cos, ident])),
            jnp.asarray(np.concatenate([sa, zero])),
            jnp.asarray(np.concatenate([sb, zero])))


def _mm_rope_kernel(a_ref, w_ref, c_ref, sa_ref, sb_ref, o_ref):
    j = pl.program_id(0)
    acc = jnp.dot(a_ref[...], w_ref[...], preferred_element_type=F32)

    @pl.when(j < 2)
    def _():
        scale = jnp.where(j == 0, DA_HEAD_DIM ** -0.5 * math.log2(math.e), 1.0)
        cos = c_ref[...] * scale
        sa = sa_ref[...] * scale
        sb = sb_ref[...] * scale
        for g in range(acc.shape[1] // LANES):
            sl = slice(g * LANES, (g + 1) * LANES)
            x = acc[:, sl]
            y = x * cos + pltpu.roll(x, LANES - 16, 1) * sa + pltpu.roll(x, 16, 1) * sb
            o_ref[:, sl] = y.astype(o_ref.dtype)

    @pl.when(j >= 2)
    def _():
        o_ref[...] = acc.astype(o_ref.dtype)


def matmul_rope(a, w, nb):
    n, k = a.shape
    d = D_MODEL
    tm = _tile(n, PROJ_TM)
    cos, sa, sb = _rope_tables(tm)
    xt = nb * (SEQ // tm)

    def tab(j, i):
        return (jnp.where(i < xt, i % (SEQ // tm), SEQ // tm), 0)

    tspec = pl.BlockSpec((tm, LANES), tab)
    return pl.pallas_call(
        _mm_rope_kernel,
        grid=(3, n // tm),
        in_specs=[pl.BlockSpec((tm, k), lambda j, i: (i, 0)),
                  pl.BlockSpec((k, d), lambda j, i: (0, j)),
                  tspec, tspec, tspec],
        out_specs=pl.BlockSpec((tm, d), lambda j, i: (i, j)),
        out_shape=jax.ShapeDtypeStruct((n, 3 * d), BF16),
        compiler_params=_cp(("parallel", "parallel")),
        name="matmul_rope",
    )(a, w, cos, sa, sb)


def _da_body(q_ref, lam_ref, g_ref, o_ref, s_sc, acc_sc, m_sc, lambda_init, run_chunks):
    tq = q_ref.shape[0]
    lane = lax.broadcasted_iota(jnp.int32, (1, LANES), 1)
    qq = _head_rows(q_ref[...], lane)
    m_sc[...] = jnp.full_like(m_sc, NEG)
    acc_sc[...] = jnp.zeros_like(acc_sc)

    def scores(slot, k):
        s_sc[slot] = _nt(k, qq)

    def fold(slot, vt):
        for g in range(2 * tq // CHUNK):
            sl = slice(g * CHUNK, (g + 1) * CHUNK)
            s = s_sc[slot, :, sl]
            m_old = m_sc[:, sl]
            m_new = jnp.maximum(m_old, jnp.max(s, axis=0, keepdims=True))
            alpha = jnp.exp2(m_old - m_new)
            p = jnp.exp2(s - m_new)
            m_sc[:, sl] = m_new
            acc_sc[:, sl] = acc_sc[:, sl] * alpha + jnp.dot(vt, p.astype(BF16), preferred_element_type=F32)

    run_chunks(scores, fold)

    ot = acc_sc[0:LANES, :] / acc_sc[LANES:LANES + 1, :]
    lam = lam_ref[...]
    lam_full = (jnp.exp(jnp.sum(lam[0:1] * lam[1:2], axis=1, keepdims=True))
                - jnp.exp(jnp.sum(lam[2:3] * lam[3:4], axis=1, keepdims=True)) + lambda_init)
    od = ot[:, :tq] - lam_full * ot[:, tq:]
    y = od * lax.rsqrt(jnp.mean(od * od, axis=0, keepdims=True) + EPS) * g_ref[...]
    o_ref[...] = (y * (1.0 - lambda_init)).T.astype(o_ref.dtype)


def _da_x_kernel(q_ref, kx_ref, kc_ref, vtx_ref, vtc_ref, lam_ref, g_ref, o_ref,
                 s_sc, acc_sc, m_sc, *, lambda_init):
    def kx(c):
        return kx_ref[pl.ds(pl.multiple_of(c * CHUNK, CHUNK), CHUNK), :]

    def run_chunks(scores, fold):
        scores(0, kx(0))

        def body(i, carry):
            scores(1, kx(2 * i + 1))
            fold(0, vtx_ref[0, 2 * i])
            scores(0, kx(2 * i + 2))
            fold(1, vtx_ref[0, 2 * i + 1])
            return carry

        lax.fori_loop(0, X_CHUNKS // 2 - 1, body, 0)
        scores(1, kx(X_CHUNKS - 1))
        fold(0, vtx_ref[0, X_CHUNKS - 2])
        scores(0, kc_ref[...])
        fold(1, vtx_ref[0, X_CHUNKS - 1])
        fold(0, vtc_ref[0, 0])

    _da_body(q_ref, lam_ref, g_ref, o_ref, s_sc, acc_sc, m_sc, lambda_init, run_chunks)


def _da_ctx_kernel(q_ref, kc_ref, vtc_ref, lam_ref, g_ref, o_in_hbm, o_ref,
                   s_sc, acc_sc, m_sc, *, lambda_init):
    del o_in_hbm

    def run_chunks(scores, fold):
        scores(0, kc_ref[...])
        fold(0, vtc_ref[0, 0])

    _da_body(q_ref, lam_ref, g_ref, o_ref, s_sc, acc_sc, m_sc, lambda_init, run_chunks)


def _da_scratch(tq):
    return [pltpu.VMEM((2, CHUNK, 2 * tq), F32), pltpu.VMEM((DA_VT_ROWS, 2 * tq), F32),
            pltpu.VMEM((1, 2 * tq), F32)]


def da_attention(qkv, lam, subln_g, lambda_init, nb):
    n = qkv.shape[0]
    d = D_MODEL
    ncb = d // LANES
    xc = nb * X_CHUNKS
    tq = DA_TQ
    vt = qkv[:, 2 * d:].reshape(n // CHUNK, CHUNK, ncb, LANES).transpose(2, 0, 3, 1)
    extra = jnp.zeros((ncb, n // CHUNK, DA_VT_ROWS - LANES, CHUNK), BF16).at[:, :, 0, :].set(1.0)
    vt = jnp.concatenate([vt, extra], axis=2)
    lam = lam.astype(F32)
    gcol = subln_g.astype(F32).reshape(LANES, 1)
    o = pl.pallas_call(
        functools.partial(_da_x_kernel, lambda_init=lambda_init),
        grid=(nb, ncb, SEQ // tq),
        in_specs=[pl.BlockSpec((tq, LANES), lambda b, h, t: (b * (SEQ // tq) + t, h)),
                  pl.BlockSpec((SEQ, LANES), lambda b, h, t: (b, ncb + h)),
                  pl.BlockSpec((CHUNK, LANES), lambda b, h, t: (xc + b, ncb + h)),
                  pl.BlockSpec((1, X_CHUNKS, DA_VT_ROWS, CHUNK), lambda b, h, t: (h, b, 0, 0)),
                  pl.BlockSpec((1, 1, DA_VT_ROWS, CHUNK), lambda b, h, t: (h, xc + b, 0, 0)),
                  pl.BlockSpec((4, DA_HEAD_DIM), lambda b, h, t: (0, 0)),
                  pl.BlockSpec((LANES, 1), lambda b, h, t: (0, 0))],
        out_specs=pl.BlockSpec((tq, LANES), lambda b, h, t: (b * (SEQ // tq) + t, h)),
        out_shape=jax.ShapeDtypeStruct((n, d), BF16),
        scratch_shapes=_da_scratch(tq),
        compiler_params=_cp(("parallel", "parallel", "arbitrary")),
        name="da_attention_x",
    )(qkv, qkv, qkv, vt, vt, lam, gcol)
    return pl.pallas_call(
        functools.partial(_da_ctx_kernel, lambda_init=lambda_init),
        grid=(nb, ncb),
        in_specs=[pl.BlockSpec((CHUNK, LANES), lambda b, h: (xc + b, h)),
                  pl.BlockSpec((CHUNK, LANES), lambda b, h: (xc + b, ncb + h)),
                  pl.BlockSpec((1, 1, DA_VT_ROWS, CHUNK), lambda b, h: (h, xc + b, 0, 0)),
                  pl.BlockSpec((4, DA_HEAD_DIM), lambda b, h: (0, 0)),
                  pl.BlockSpec((LANES, 1), lambda b, h: (0, 0)),
                  pl.BlockSpec(memory_space=pl.ANY)],
        out_specs=pl.BlockSpec((CHUNK, LANES), lambda b, h: (xc + b, h)),
        out_shape=jax.ShapeDtypeStruct((n, d), BF16),
        scratch_shapes=_da_scratch(CHUNK),
        input_output_aliases={5: 0},
        compiler_params=_cp(("parallel", "parallel")),
        name="da_attention_ctx",
    )(qkv, qkv, vt, lam, gcol, o)


def _block_diag(w):
    per = LANES // ML_BLOCK
    wr = w.reshape(ML_INNER // LANES, per, ML_BLOCK, ML_BLOCK)
    eye = jnp.eye(per, dtype=w.dtype)
    return jnp.einsum('cgio,gh->cgiho', wr, eye).reshape(ML_INNER // LANES, LANES, LANES).astype(BF16)


def _ml_conv_kernel(p_ref, c_ref, n_ref, cw_ref, cb_ref, wq_ref, wk_ref, wv_ref, wg_ref, bg_ref,
                    xc_ref, q_ref, k_ref, v_ref, g_ref, *, n_xchunks):
    i = pl.program_id(0)
    is_x = i < n_xchunks
    j = i % X_CHUNKS
    halo = 16
    cur = c_ref[...].astype(F32)
    prev = p_ref[CHUNK - halo:CHUNK, :].astype(F32)
    nxt = n_ref[0:halo, :].astype(F32)
    prev = jnp.where(jnp.logical_and(is_x, j > 0), prev, jnp.zeros_like(prev))
    nxt = jnp.where(jnp.logical_and(is_x, j < X_CHUNKS - 1), nxt, jnp.zeros_like(nxt))
    xp = jnp.concatenate([prev, cur, nxt], axis=0)
    rows = CHUNK + 2 * halo
    y = cb_ref[...] + cw_ref[ML_CONV_K // 2:ML_CONV_K // 2 + 1, :] * cur
    for tap in range(ML_CONV_K):
        dlt = tap - ML_CONV_K // 2
        if dlt == 0:
            continue
        shifted = pltpu.roll(xp, (-dlt) % rows, 0)[halo:halo + CHUNK]
        y = y + cw_ref[tap:tap + 1, :] * shifted
    xcb = _silu(y).astype(BF16)
    xc_ref[...] = xcb
    xm = c_ref[...]
    for s in range(xcb.shape[1] // LANES):
        sl = slice(s * LANES, (s + 1) * LANES)
        q_ref[:, sl] = jnp.dot(xcb[:, sl], wq_ref[s], preferred_element_type=F32).astype(BF16)
        k_ref[:, sl] = jnp.dot(xcb[:, sl], wk_ref[s], preferred_element_type=F32).astype(BF16)
        v_ref[:, sl] = jnp.dot(xm[:, sl], wv_ref[s], preferred_element_type=F32).astype(BF16)

    part = (jnp.dot(q_ref[...], wg_ref[0], preferred_element_type=F32)
            + jnp.dot(k_ref[...], wg_ref[1], preferred_element_type=F32)
            + jnp.dot(v_ref[...], wg_ref[2], preferred_element_type=F32))
    c = pl.program_id(1)

    @pl.when(c == 0)
    def _():
        g_ref[...] = bg_ref[...] + part

    @pl.when(c > 0)
    def _():
        g_ref[...] += part


def ml_conv_qkv(up, conv_w, conv_b, wq, wk, wv, wg, bg, nb):
    n = up.shape[0]
    nchunks = n // CHUNK
    cw = 1024
    ncb = ML_INNER // cw
    sub = cw // LANES
    blk = pl.BlockSpec((CHUNK, cw), lambda i, c: (i, c))
    wspec = pl.BlockSpec((sub, LANES, LANES), lambda i, c: (c, 0, 0))
    out = jax.ShapeDtypeStruct((n, ML_INNER), BF16)
    return pl.pallas_call(
        functools.partial(_ml_conv_kernel, n_xchunks=nb * X_CHUNKS),
        grid=(nchunks, ncb),
        in_specs=[pl.BlockSpec((CHUNK, cw), lambda i, c: (jnp.maximum(i - 1, 0), c)),
                  blk,
                  pl.BlockSpec((CHUNK, cw), lambda i, c: (jnp.minimum(i + 1, nchunks - 1), c)),
                  pl.BlockSpec((ML_CONV_K, cw), lambda i, c: (0, c)),
                  pl.BlockSpec((1, cw), lambda i, c: (0, c)),
                  wspec, wspec, wspec,
                  pl.BlockSpec((3, cw, LANES), lambda i, c: (0, c, 0)),
                  pl.BlockSpec((1, LANES), lambda i, c: (0, 0))],
        out_specs=[blk, blk, blk, blk, pl.BlockSpec((CHUNK, LANES), lambda i, c: (i, 0))],
        out_shape=[out, out, out, out, jax.ShapeDtypeStruct((n, LANES), F32)],
        compiler_params=_cp(("parallel", "arbitrary")),
        name="ml_conv_qkv",
    )(up, up, up, conv_w.astype(F32), conv_b.astype(F32).reshape(1, ML_INNER), wq, wk, wv, wg, bg)


def _log_sigmoid(x):
    return jnp.minimum(x, 0.0) - jnp.log(1.0 + jnp.exp(-jnp.abs(x)))


def _ml_scan_dir(d, h, p, q_ref, k_ref, v_ref, g_ref, gt_ref, o_ref, c_sc, n_sc, m_sc):
    ninf = -jnp.inf

    @pl.when(p == 0)
    def _():
        c_sc[...] = jnp.zeros_like(c_sc)
        n_sc[...] = jnp.zeros_like(n_sc)
        m_sc[...] = jnp.full_like(m_sc, ninf)

    L = CHUNK
    col_i = d * 2 * ML_HEADS + h
    col_f = col_i + ML_HEADS
    lane = lax.broadcasted_iota(jnp.int32, (1, LANES), 1)
    g = g_ref[...]
    i_col = jnp.sum(jnp.where(lane == col_i, g, 0.0), axis=1, keepdims=True)
    f_col = _log_sigmoid(jnp.sum(jnp.where(lane == col_f, g, 0.0), axis=1, keepdims=True))
    i_row = gt_ref[pl.ds(col_i, 1), :]
    f_row = _log_sigmoid(gt_ref[pl.ds(col_f, 1), :])

    jj = lax.broadcasted_iota(jnp.int32, (L, L), 0)
    ss = lax.broadcasted_iota(jnp.int32, (L, L), 1)
    valid = (ss <= jj) if d == 0 else (ss >= jj)
    valid_t = (jj <= ss) if d == 0 else (jj >= ss)
    bcum_col = jnp.sum(jnp.where(valid, f_row, 0.0), axis=1, keepdims=True)
    bcum_row = jnp.sum(jnp.where(valid_t, f_col, 0.0), axis=0, keepdims=True)
    b_last = jnp.sum(f_row, axis=1, keepdims=True)
    m_prev = m_sc[...]

    logd = jnp.where(valid, bcum_col - bcum_row + i_row, ninf)
    log_inter = bcum_col + m_prev
    m_j = jnp.maximum(log_inter, jnp.max(logd, axis=1, keepdims=True))
    dmat = jnp.exp(logd - m_j)
    inter = jnp.exp(log_inter - m_j)

    q = q_ref[...]
    k = k_ref[...]
    v = v_ref[...]
    scale = ML_HEAD_DIM ** -0.5
    sc = _nt(q, k) * scale * dmat
    c_prev = c_sc[...]
    n_prev = n_sc[...]
    qc = jnp.dot(q, c_prev.astype(BF16), preferred_element_type=F32) * scale
    num = inter * qc + jnp.dot(sc.astype(BF16), v, preferred_element_type=F32)
    qn = jnp.sum(q.astype(F32) * n_prev, axis=1, keepdims=True) * scale
    den = inter * qn + jnp.sum(sc, axis=1, keepdims=True)
    o_ref[...] = num / jnp.maximum(jnp.abs(den), jnp.exp(-m_j))

    ls = b_last - bcum_col + i_col
    m_new = jnp.maximum(b_last + m_prev, jnp.max(ls, axis=0, keepdims=True))
    w = jnp.exp(ls - m_new)
    decay = jnp.exp(b_last + m_prev - m_new)
    kw = k.astype(F32) * w
    c_sc[...] = decay * c_prev + _tn(kw.astype(BF16), v)
    n_sc[...] = decay * n_prev + jnp.sum(kw, axis=0, keepdims=True)
    m_sc[...] = m_new


def _ml_scan_kernel(qf_ref, kf_ref, vf_ref, gf_ref, gtf_ref, qb_ref, kb_ref, vb_ref, gb_ref, gtb_ref,
                    of_ref, ob_ref, c_sc, n_sc, m_sc):
    h = pl.program_id(1)
    p = pl.program_id(2)
    _ml_scan_dir(0, h, p, qf_ref, kf_ref, vf_ref, gf_ref, gtf_ref, of_ref, c_sc.at[0], n_sc.at[0], m_sc.at[0])
    _ml_scan_dir(1, h, p, qb_ref, kb_ref, vb_ref, gb_ref, gtb_ref, ob_ref, c_sc.at[1], n_sc.at[1], m_sc.at[1])


def ml_scan(q, k, v, g, gt, nb):
    n = q.shape[0]
    xc = nb * X_CHUNKS

    def cidx(b, d, p):
        xi = p - 1 if d == 0 else X_CHUNKS - p
        return jnp.where(p == 0, xc + b, b * X_CHUNKS + xi)

    def specs(d):
        blk = pl.BlockSpec((CHUNK, ML_HEAD_DIM), lambda b, h, p: (cidx(b, d, p), h))
        return [blk, blk, blk,
                pl.BlockSpec((CHUNK, LANES), lambda b, h, p: (cidx(b, d, p), 0)),
                pl.BlockSpec((16, CHUNK), lambda b, h, p: (0, cidx(b, d, p)))]

    out = jax.ShapeDtypeStruct((n, ML_INNER), F32)
    return pl.pallas_call(
        _ml_scan_kernel,
        grid=(nb, ML_HEADS, X_CHUNKS + 1),
        in_specs=specs(0) + specs(1),
        out_specs=[pl.BlockSpec((CHUNK, ML_HEAD_DIM), lambda b, h, p: (cidx(b, 0, p), h)),
                   pl.BlockSpec((CHUNK, ML_HEAD_DIM), lambda b, h, p: (cidx(b, 1, p), h))],
        out_shape=[out, out],
        scratch_shapes=[pltpu.VMEM((2, ML_HEAD_DIM, ML_HEAD_DIM), F32),
                        pltpu.VMEM((2, 1, ML_HEAD_DIM), F32),
                        pltpu.VMEM((2, 1, 1), F32)],
        compiler_params=_cp(("parallel", "parallel", "arbitrary")),
        name="ml_scan",
    )(q, k, v, g, gt, q, k, v, g, gt)


def _ml_finish_kernel(hf_ref, hb_ref, xc_ref, z_ref, gn_ref, sk_ref, o_ref):
    hh = hf_ref[...] + hb_ref[...]
    z = z_ref[...].astype(F32)
    gate = _silu(z)
    for hd in range(ML_HEADS):
        sl = slice(hd * ML_HEAD_DIM, (hd + 1) * ML_HEAD_DIM)
        seg = hh[:, sl]
        mu = jnp.mean(seg, axis=-1, keepdims=True)
        cen = seg - mu
        var = jnp.mean(cen * cen, axis=-1, keepdims=True)
        hn = cen * lax.rsqrt(var + EPS) * gn_ref[:, sl]
        a = (hn + sk_ref[:, sl] * xc_ref[:, sl].astype(F32)) * gate[:, sl]
        o_ref[:, sl] = a.astype(o_ref.dtype)


def ml_finish(hf, hb, xc, up, gn_w, skip):
    n = xc.shape[0]
    tm = CHUNK
    vec = pl.BlockSpec((1, ML_INNER), lambda i: (0, 0))
    return pl.pallas_call(
        _ml_finish_kernel,
        grid=(n // tm,),
        in_specs=[pl.BlockSpec((tm, ML_INNER), lambda i: (i, 0)),
                  pl.BlockSpec((tm, ML_INNER), lambda i: (i, 0)),
                  pl.BlockSpec((tm, ML_INNER), lambda i: (i, 0)),
                  pl.BlockSpec((tm, ML_INNER), lambda i: (i, 1)),
                  vec, vec],
        out_specs=pl.BlockSpec((tm, ML_INNER), lambda i: (i, 0)),
        out_shape=jax.ShapeDtypeStruct((n, ML_INNER), BF16),
        compiler_params=_cp(("parallel",)),
        name="ml_finish",
    )(hf, hb, xc, up, gn_w.astype(F32).reshape(1, ML_INNER), skip.astype(F32).reshape(1, ML_INNER))


def _router_kernel(x_ref, g_ref, sh_ref, sc_ref, w_ref, b_ref, h_ref, o_ref):
    x = x_ref[...]
    y = x * lax.rsqrt(jnp.mean(x * x, axis=-1, keepdims=True) + EPS) * g_ref[...]
    hf = y * (1.0 + sc_ref[0, 0]) + sh_ref[0, 0]
    h_hi = hf.astype(BF16)
    h_ref[...] = hf
    h_lo = (hf - h_hi.astype(F32)).astype(BF16)
    logits = (jnp.dot(h_hi, w_ref[0], preferred_element_type=F32)
              + jnp.dot(h_hi, w_ref[1], preferred_element_type=F32)
              + jnp.dot(h_lo, w_ref[0], preferred_element_type=F32)) + b_ref[...]
    lane = lax.broadcasted_iota(jnp.int32, (1, LANES), 1).astype(F32)
    big = 1e9
    ninf = -jnp.inf
    is_g = jnp.logical_and(lane >= MOE_E, lane < MOE_E + MOE_GROUPS)
    gl = jnp.where(is_g, logits, ninf)
    gmax = jnp.max(gl, axis=1, keepdims=True)
    g_val = 1.0 / jnp.sum(jnp.exp(gl - gmax), axis=1, keepdims=True)
    g_idx = jnp.min(jnp.where(gl == gmax, lane, big), axis=1, keepdims=True) - MOE_E
    lo = g_idx * MOE_EPG
    sel = jnp.logical_and(lane >= lo, lane < lo + MOE_EPG)
    el = jnp.where(sel, logits, ninf)
    e1 = jnp.max(el, axis=1, keepdims=True)
    esum = jnp.sum(jnp.exp(el - e1), axis=1, keepdims=True)
    i1 = jnp.min(jnp.where(el == e1, lane, big), axis=1, keepdims=True)
    el2 = jnp.where(lane == i1, ninf, el)
    e2 = jnp.max(el2, axis=1, keepdims=True)
    i2 = jnp.min(jnp.where(el2 == e2, lane, big), axis=1, keepdims=True)
    p1 = 1.0 / esum
    p2 = jnp.exp(e2 - e1) / esum
    w1 = g_val * p1 / (p1 + p2)
    w2 = g_val * p2 / (p1 + p2)
    gates = jnp.where(lane == i1, w1, 0.0) + jnp.where(lane == i2, w2, 0.0)
    marks = jnp.where(jnp.logical_or(lane == i1 + SEL_LANE, lane == i2 + SEL_LANE), 1.0, 0.0)
    o_ref[...] = gates + marks


def moe_router(s, g, mod, k_shift, k_scale, wr, br, nb):
    n, d = s.shape
    tm = _tile(n, PROJ_TM)
    return pl.pallas_call(
        _router_kernel,
        grid=(n // tm,),
        in_specs=[pl.BlockSpec((tm, d), lambda i: (i, 0)),
                  pl.BlockSpec((1, d), lambda i: (0, 0)),
                  pl.BlockSpec((1, 1, 1, d), lambda i: (_mod_row(i, tm, nb), k_shift, 0, 0)),
                  pl.BlockSpec((1, 1, 1, d), lambda i: (_mod_row(i, tm, nb), k_scale, 0, 0)),
                  pl.BlockSpec((2, d, LANES), lambda i: (0, 0, 0)),
                  pl.BlockSpec((1, LANES), lambda i: (0, 0))],
        out_specs=[pl.BlockSpec((tm, d), lambda i: (i, 0)),
                   pl.BlockSpec((tm, LANES), lambda i: (i, 0))],
        out_shape=[jax.ShapeDtypeStruct((n, d), F32),
                   jax.ShapeDtypeStruct((n, LANES), F32)],
        compiler_params=_cp(("parallel",)),
        name="moe_router",
    )(s, g.reshape(1, d), mod, mod, wr, br)


def _rank_kernel(route_ref, rank_ref, cnt_ref, carry_ref):
    i = pl.program_id(0)

    @pl.when(i == 0)
    def _():
        carry_ref[...] = jnp.zeros_like(carry_ref)

    tm = route_ref.shape[0]
    lane = lax.broadcasted_iota(jnp.int32, (1, LANES), 1)
    marks = jnp.where(lane >= SEL_LANE, route_ref[...], 0.0)
    rr = lax.broadcasted_iota(jnp.int32, (tm, tm), 0)
    cc = lax.broadcasted_iota(jnp.int32, (tm, tm), 1)
    below = jnp.where(cc < rr, 1.0, 0.0).astype(BF16)
    rank_ref[...] = jnp.dot(below, marks.astype(BF16), preferred_element_type=F32) + carry_ref[...]
    carry_ref[...] += jnp.sum(marks, axis=0, keepdims=True)
    cnt_ref[...] = carry_ref[...]


def moe_rank(route):
    n = route.shape[0]
    tm = _tile(n, PROJ_TM)
    return pl.pallas_call(
        _rank_kernel,
        grid=(n // tm,),
        in_specs=[pl.BlockSpec((tm, LANES), lambda i: (i, 0))],
        out_specs=[pl.BlockSpec((tm, LANES), lambda i: (i, 0)),
                   pl.BlockSpec((1, LANES), lambda i: (0, 0))],
        out_shape=[jax.ShapeDtypeStruct((n, LANES), F32),
                   jax.ShapeDtypeStruct((1, LANES), F32)],
        scratch_shapes=[pltpu.VMEM((1, LANES), F32)],
        compiler_params=_cp(("arbitrary",)),
        name="moe_rank",
    )(route)


def _pos_kernel(route_ref, rank_ref, off_ref, o_ref):
    lane = lax.broadcasted_iota(jnp.int32, (1, LANES), 1).astype(F32)
    route = route_ref[...]
    marked = jnp.logical_and(lane >= SEL_LANE, route > 0.5)
    p = rank_ref[...] + off_ref[...]
    lane_a = jnp.min(jnp.where(marked, lane, 1e9), axis=1, keepdims=True)
    lane_b = jnp.max(jnp.where(marked, lane, -1.0), axis=1, keepdims=True)

    def pick(src, at):
        return jnp.sum(jnp.where(lane == at, src, 0.0), axis=1, keepdims=True)

    out = jnp.where(lane == 0.0, pick(p, lane_a), 0.0)
    out = out + jnp.where(lane == 1.0, pick(p, lane_b), 0.0)
    out = out + jnp.where(lane == 2.0, pick(route, lane_a - SEL_LANE), 0.0)
    out = out + jnp.where(lane == 3.0, pick(route, lane_b - SEL_LANE), 0.0)
    o_ref[...] = out


def moe_positions(route, rank, off):
    n = route.shape[0]
    tm = _tile(n, PROJ_TM)
    blk = pl.BlockSpec((tm, LANES), lambda i: (i, 0))
    return pl.pallas_call(
        _pos_kernel,
        grid=(n // tm,),
        in_specs=[blk, blk, pl.BlockSpec((1, LANES), lambda i: (0, 0))],
        out_specs=blk,
        out_shape=jax.ShapeDtypeStruct((n, LANES), F32),
        compiler_params=_cp(("parallel",)),
        name="moe_positions",
    )(route, rank, off)


def _row_copy(src, src_row, dst, dst_row, sem):
    return pltpu.make_async_copy(src.at[pl.ds(src_row, 1)], dst.at[pl.ds(dst_row, 1)], sem)


def _dispatch_kernel(last_ref, pos_ref, h_ref, xs_hbm, zero_sc, sem):
    @pl.when(pl.program_id(0) == 0)
    def _():
        zero_sc[...] = jnp.zeros_like(zero_sc)

        def fill(e, carry):
            start = pl.multiple_of(last_ref[e], MOE_TM)
            pltpu.make_async_copy(zero_sc, xs_hbm.at[pl.ds(start, MOE_TM)], sem).start()
            return carry

        lax.fori_loop(0, MOE_E, fill, 0)

        def drain(e, carry):
            pltpu.make_async_copy(zero_sc, xs_hbm.at[pl.ds(0, MOE_TM)], sem).wait()
            return carry

        lax.fori_loop(0, MOE_E, drain, 0)

    tok = h_ref.shape[0]

    def issue(r, carry):
        _row_copy(h_ref, r, xs_hbm, pos_ref[0, 0, r], sem).start(priority=0)
        _row_copy(h_ref, r, xs_hbm, pos_ref[0, 0, tok + r], sem).start(priority=1)
        return carry

    lax.fori_loop(0, tok, issue, 0, unroll=8)
    for _ in range(2):
        pltpu.make_async_copy(h_ref, xs_hbm.at[pl.ds(0, tok)], sem).wait()


def moe_dispatch(last_tile_start, pos, h, n_sorted):
    n, d = h.shape
    tok = pos.shape[2] // 2
    return pl.pallas_call(
        _dispatch_kernel,
        grid_spec=pltpu.PrefetchScalarGridSpec(
            num_scalar_prefetch=1,
            grid=(n // tok,),
            in_specs=[pl.BlockSpec((1, 1, 2 * tok), lambda i, last: (i, 0, 0), memory_space=pltpu.SMEM),
                      pl.BlockSpec((tok, d), lambda i, last: (i, 0))],
            out_specs=pl.BlockSpec(memory_space=pl.ANY),
            scratch_shapes=[pltpu.VMEM((MOE_TM, d), h.dtype), pltpu.SemaphoreType.DMA]),
        out_shape=jax.ShapeDtypeStruct((n_sorted, d), h.dtype),
        compiler_params=_cp(("arbitrary",)),
        name="moe_dispatch",
    )(last_tile_start, pos, h)


def _experts_kernel(te_ref, nt_ref, x_ref, w1_ref, w3_ref, w2_ref, o_ref, w1_sc, w3_sc, w2_sc):
    i = pl.program_id(0)
    active = i < nt_ref[0]

    @pl.when(jnp.logical_and(active, jnp.logical_or(i == 0, te_ref[i] != te_ref[jnp.maximum(i - 1, 0)])))
    def _():
        w1_sc[...] = w1_ref[0, 0].astype(BF16)
        w3_sc[...] = w3_ref[0, 0].astype(BF16)
        w2_sc[...] = w2_ref[0, 0].astype(BF16)

    @pl.when(active)
    def _():
        x = x_ref[...].astype(BF16)
        a = jnp.dot(x, w1_sc[...], preferred_element_type=F32)
        b = jnp.dot(x, w3_sc[...], preferred_element_type=F32)
        he = (_silu(a) * b).astype(BF16)
        o_ref[...] = jnp.dot(he, w2_sc[...], preferred_element_type=F32)

    @pl.when(jnp.logical_not(active))
    def _():
        o_ref[...] = jnp.zeros_like(o_ref)


def moe_experts(tile_expert, n_tiles, xs, w1, w3, w2, layer):
    n_sorted, d = xs.shape
    return pl.pallas_call(
        _experts_kernel,
        grid_spec=pltpu.PrefetchScalarGridSpec(
            num_scalar_prefetch=2,
            grid=(n_sorted // MOE_TM,),
            in_specs=[pl.BlockSpec((MOE_TM, d), lambda i, te, nt: (jnp.minimum(i, nt[0] - 1), 0)),
                      pl.BlockSpec((1, 1, d, MOE_HIDDEN), lambda i, te, nt: (layer, te[i], 0, 0)),
                      pl.BlockSpec((1, 1, d, MOE_HIDDEN), lambda i, te, nt: (layer, te[i], 0, 0)),
                      pl.BlockSpec((1, 1, MOE_HIDDEN, d), lambda i, te, nt: (layer, te[i], 0, 0))],
            out_specs=pl.BlockSpec((MOE_TM, d), lambda i, te, nt: (i, 0)),
            scratch_shapes=[pltpu.VMEM((d, MOE_HIDDEN), BF16), pltpu.VMEM((d, MOE_HIDDEN), BF16),
                            pltpu.VMEM((MOE_HIDDEN, d), BF16)]),
        out_shape=jax.ShapeDtypeStruct((n_sorted, d), F32),
        compiler_params=_cp(("arbitrary",)),
        name="moe_experts",
    )(tile_expert, n_tiles, xs, w1, w3, w2)


def _combine_kernel(pos_ref, nxt_ref, meta_ref, s_ref, g_ref, ys_hbm, o_ref, buf_ref, sem):
    i = pl.program_id(0)
    slot = i % 2
    tok = s_ref.shape[0]

    def gather(p_ref, dst_slot):
        def issue(r, carry):
            _row_copy(ys_hbm, p_ref[0, 0, r], buf_ref.at[dst_slot, 0], r, sem.at[dst_slot]).start(priority=0)
            _row_copy(ys_hbm, p_ref[0, 0, tok + r], buf_ref.at[dst_slot, 1], r, sem.at[dst_slot]).start(priority=1)
            return carry

        lax.fori_loop(0, tok, issue, 0, unroll=8)

    @pl.when(i == 0)
    def _():
        gather(pos_ref, 0)

    @pl.when(i + 1 < pl.num_programs(0))
    def _():
        gather(nxt_ref, 1 - slot)

    for k in range(2):
        pltpu.make_async_copy(ys_hbm.at[pl.ds(0, tok)], buf_ref.at[slot, k], sem.at[slot]).wait()
    meta = meta_ref[...]
    y = meta[:, 2:3] * buf_ref[slot, 0] + meta[:, 3:4] * buf_ref[slot, 1]
    o_ref[...] = s_ref[...] + g_ref[0, 0] * y


def moe_combine(pos, meta, s, mod, k_gate, ys, nb):
    n, d = s.shape
    tm = pos.shape[2] // 2
    last = n // tm - 1
    return pl.pallas_call(
        _combine_kernel,
        grid=(n // tm,),
        in_specs=[pl.BlockSpec((1, 1, 2 * tm), lambda i: (i, 0, 0), memory_space=pltpu.SMEM),
                  pl.BlockSpec((1, 1, 2 * tm), lambda i: (jnp.minimum(i + 1, last), 0, 0),
                               memory_space=pltpu.SMEM),
                  pl.BlockSpec((tm, LANES), lambda i: (i, 0)),
                  pl.BlockSpec((tm, d), lambda i: (i, 0)),
                  pl.BlockSpec((1, 1, 1, d), lambda i: (_mod_row(i, tm, nb), k_gate, 0, 0)),
                  pl.BlockSpec(memory_space=pl.ANY)],
        out_specs=pl.BlockSpec((tm, d), lambda i: (i, 0)),
        out_shape=jax.ShapeDtypeStruct((n, d), F32),
        scratch_shapes=[pltpu.VMEM((2, 2, tm, d), F32), pltpu.SemaphoreType.DMA((2,))],
        input_output_aliases={3: 0},
        compiler_params=_cp(("arbitrary",)),
        name="moe_combine",
    )(pos, pos, meta, s, mod, ys)


def _na_layer(h, w_qkv, w_o, rpb, s, mod, nb):
    qkv = matmul(h, w_qkv.astype(BF16), BF16)
    o = na_attention(qkv, _na_bias_tables(rpb), nb)
    return matmul_residual(o, w_o.astype(BF16), s, mod, 2, nb)


def _da_layer(h, w_qkv, lam, subln_g, w_o, lambda_init, s, mod, nb):
    qkv = matmul_rope(h, w_qkv.astype(BF16), nb)
    o = da_attention(qkv, lam, subln_g, lambda_init, nb)
    return matmul_residual(o, w_o.astype(BF16), s, mod, 2, nb)


def _ml_layer(h, w_up, conv_w, conv_b, w_q, w_k, w_v, w_gate, b_gate, gn_w, skip, w_down, s, mod, nb):
    up = matmul(h, w_up.astype(BF16), BF16)
    ng = w_gate.shape[1]
    wg = jnp.pad(w_gate, ((0, 0), (0, LANES - ng))).reshape(3, ML_INNER, LANES).astype(BF16)
    bg = jnp.pad(b_gate.astype(F32), (0, LANES - ng)).reshape(1, LANES)
    xc, q, k, v, g = ml_conv_qkv(up, conv_w, conv_b, _block_diag(w_q), _block_diag(w_k), _block_diag(w_v),
                                 wg, bg, nb)
    gt = g[:, :ng].T
    hf, hb = ml_scan(q, k, v, g, gt, nb)
    a = ml_finish(hf, hb, xc, up, gn_w, skip)
    return matmul_residual(a, w_down.astype(BF16), s, mod, 2, nb)


def _moe_layer(norm_g, w_group, b_group, w_router, b_router, w1, w3, w2, layer, s, mod, nb):
    d = s.shape[1]
    pad = LANES - MOE_E - MOE_GROUPS
    wr = jnp.concatenate([w_router, w_group, jnp.zeros((d, pad), w_router.dtype)], axis=1).astype(F32)
    wr_hi = wr.astype(BF16)
    wr_lo = (wr - wr_hi.astype(F32)).astype(BF16)
    br = jnp.concatenate([b_router, b_group, jnp.zeros((pad,), b_router.dtype)]).astype(F32).reshape(1, LANES)
    h, route = moe_router(s, norm_g.astype(F32), mod, 3, 4, jnp.stack([wr_hi, wr_lo]), br, nb)
    rank, cnt = moe_rank(route)
    n = s.shape[0]
    cnt_e = cnt[0, SEL_LANE:SEL_LANE + MOE_E].astype(jnp.int32)
    gsz = ((cnt_e + MOE_TM - 1) // MOE_TM) * MOE_TM
    ends = jnp.cumsum(gsz)
    off_row = jnp.zeros((1, LANES), F32).at[0, SEL_LANE:SEL_LANE + MOE_E].set((ends - gsz).astype(F32))
    meta = moe_positions(route, rank, off_row)
    n_sorted = 2 * n + MOE_E * MOE_TM
    n_tiles = (ends[-1:] // MOE_TM).astype(jnp.int32)
    tile_ids = jnp.arange(n_sorted // MOE_TM, dtype=jnp.int32)
    tile_expert = jnp.sum((tile_ids[:, None] >= (ends // MOE_TM)[None, :]).astype(jnp.int32), axis=1)
    tile_expert = jnp.minimum(tile_expert, MOE_E - 1)
    tok = _tile(n, MOE_TOK)
    pos = meta[:, :2].astype(jnp.int32).reshape(n // tok, tok, 2)
    pos = pos.transpose(0, 2, 1).reshape(n // tok, 1, 2 * tok)
    last_tile_start = jnp.maximum(ends - MOE_TM, 0).astype(jnp.int32)
    xs = moe_dispatch(last_tile_start, pos, h, n_sorted)
    ys = moe_experts(tile_expert, n_tiles, xs, w1.astype(F32), w3.astype(F32), w2.astype(F32), layer)
    return moe_combine(pos, meta, s, mod, 5, ys, nb)


def kernel(x, c, ctx, c_ctx, mod_w, mod_b, norm_g, final_g, na_w_qkv, na_w_o, na_rpb, ml_w_up, ml_conv_w, ml_conv_b, ml_w_q, ml_w_k, ml_w_v, ml_w_gate, ml_b_gate, ml_gn_w, ml_skip, ml_w_down, da_w_qkv, da_lambda, da_subln_g, da_w_o, moe_w_group, moe_b_group, moe_w_router, moe_b_router, moe_w1, moe_w3, moe_w2):
    nb, seq, d = x.shape
    assert (seq, d, ctx.shape[1]) == (SEQ, D_MODEL, CTX_LEN) and nb < 16
    depth = mod_w.shape[0]
    nx = nb * seq
    s = jnp.concatenate([x.reshape(nx, d), ctx.reshape(nb * CTX_LEN, d)], axis=0).astype(F32)
    cvec = jnp.concatenate([c, c_ctx[None, :], jnp.zeros((16 - nb - 1, d), c.dtype)], axis=0).astype(F32)
    mods = mod_vectors(cvec, mod_w, mod_b).reshape(depth, 16, 6, 1, d)

    for i in range(depth):
        kind, j = i % N_MIXERS, i // N_MIXERS
        mod = mods[i]
        h = norm_mod(s, norm_g[i, 0].astype(F32), mod, 0, 1, nb)
        if kind == 0:
            s = _na_layer(h, na_w_qkv[j], na_w_o[j], na_rpb[j], s, mod, nb)
        elif kind == 1:
            s = _ml_layer(h, ml_w_up[j], ml_conv_w[j], ml_conv_b[j], ml_w_q[j], ml_w_k[j], ml_w_v[j],
                          ml_w_gate[j], ml_b_gate[j], ml_gn_w[j], ml_skip[j], ml_w_down[j], s, mod, nb)
        else:
            lambda_init = 0.8 - 0.6 * math.exp(-0.3 * i)
            s = _da_layer(h, da_w_qkv[j], da_lambda[j], da_subln_g[j], da_w_o[j], lambda_init, s, mod, nb)
        s = _moe_layer(norm_g[i, 1], moe_w_group[i], moe_b_group[i], moe_w_router[i], moe_b_router[i],
                       moe_w1, moe_w3, moe_w2, i, s, mod, nb)
    return final_norm(s, final_g.astype(F32), nb)
```
